```python
import math
import jax, jax.numpy as jnp
from jax import lax
import numpy as np

D_MODEL = 2048
BATCH = 8
SEQ = 2048
DEPTH = 1

N_META = 16
MIX_WIDTH = D_MODEL
RWKV_WIDTH = MIX_WIDTH // 2
RWKV_HEAD_DIM = 64
RWKV_HEADS = RWKV_WIDTH // RWKV_HEAD_DIM
RWKV_LORA_W = 64
RWKV_LORA_A = 64
RWKV_SHIFT_COLS = 3 * RWKV_WIDTH + RWKV_LORA_W + RWKV_LORA_A
RWKV_COLS = RWKV_SHIFT_COLS + RWKV_WIDTH
DIFF_WIDTH = MIX_WIDTH - RWKV_WIDTH
DIFF_HEAD_DIM = 64
DIFF_V_DIM = 2 * DIFF_HEAD_DIM
DIFF_HEADS = DIFF_WIDTH // DIFF_V_DIM
DIFF_COLS = 4 * DIFF_WIDTH
IN_COLS = RWKV_COLS + DIFF_COLS

ROPE_THETA = 10000.0
Q_BLOCK = 128
NORM_EPS = 1e-6
GN_EPS = 64e-5
SUBLN_EPS = 1e-5

kernel_name = "hymba_rwkv7_diffattn_hybrid"


def rms_norm(x, g, eps=NORM_EPS):
    xf = x.astype(jnp.float32)
    y = xf * lax.rsqrt(jnp.mean(xf * xf, axis=-1, keepdims=True) + eps)
    return (y * g.astype(jnp.float32)).astype(x.dtype)


def rope_tables(L):
    pos = jnp.arange(L, dtype=jnp.float32)
    inv_freq = ROPE_THETA ** (-jnp.arange(0, DIFF_HEAD_DIM, 2, dtype=jnp.float32) / DIFF_HEAD_DIM)
    ang = pos[:, None] * inv_freq[None, :]
    ang = jnp.concatenate([ang, ang], axis=-1)
    return jnp.cos(ang), jnp.sin(ang)


def apply_rope(x, cos, sin):
    half = x.shape[-1] // 2
    x1, x2 = x[..., :half], x[..., half:]
    rot = jnp.concatenate([-x2, x1], axis=-1)
    c = cos[None, :, None, :].astype(x.dtype)
    s = sin[None, :, None, :].astype(x.dtype)
    return x * c + rot * s


def rwkv7_branch(u, mu, w0, w_up, a0, a_up, k_k, k_a, r_k, gn_w, gn_b):
    Bn, L, _ = u.shape
    H, Dh, C = RWKV_HEADS, RWKV_HEAD_DIM, RWKV_WIDTH
    u_prev = jnp.pad(u, ((0, 0), (1, 0), (0, 0)))[:, :-1]
    u = u + (u_prev - u) * mu.astype(u.dtype)
    r = u[..., :C]
    k = u[..., C:2 * C]
    v = u[..., 2 * C:3 * C]
    wd = u[..., 3 * C:3 * C + RWKV_LORA_W]
    ad = u[..., 3 * C + RWKV_LORA_W:]
    w = -jax.nn.softplus(-(w0 + jnp.tanh(wd) @ w_up).astype(jnp.float32)) - 0.5
    decay = jnp.exp(-jnp.exp(w))
    a = jax.nn.sigmoid((a0 + ad @ a_up).astype(jnp.float32))
    hs = lambda t: t.reshape(Bn, L, H, Dh).astype(jnp.float32)
    r, k, v, decay, a = hs(r), hs(k), hs(v), hs(decay), hs(a)
    kk = k * k_k.reshape(H, Dh).astype(jnp.float32)
    kk = kk / jnp.maximum(jnp.sqrt(jnp.sum(kk * kk, axis=-1, keepdims=True)), 1e-12)
    k = k * (1.0 + (a - 1.0) * k_a.reshape(H, Dh).astype(jnp.float32))
    b = kk * a

    def step(S, inp):
        r_t, w_t, k_t, v_t, kk_t, b_t = inp
        sa = jnp.einsum('bhvk,bhk->bhv', S, -kk_t)
        S = S * w_t[:, :, None, :] + sa[..., None] * b_t[:, :, None, :] + v_t[..., None] * k_t[:, :, None, :]
        y = jnp.einsum('bhvk,bhk->bhv', S, r_t)
        return S, y

    tm = lambda t: jnp.moveaxis(t, 1, 0)
    S0 = jnp.zeros((Bn, H, Dh, Dh), jnp.float32)
    _, ys = lax.scan(step, S0, (tm(r), tm(decay), tm(k), tm(v), tm(kk), tm(b)))
    y = jnp.moveaxis(ys, 0, 1)
    mean = jnp.mean(y, axis=-1, keepdims=True)
    var = jnp.mean(jnp.square(y - mean), axis=-1, keepdims=True)
    y = ((y - mean) * lax.rsqrt(var + GN_EPS)).reshape(Bn, L, C)
    y = y * gn_w.astype(jnp.float32) + gn_b.astype(jnp.float32)
    bonus = jnp.sum(r * k * r_k.astype(jnp.float32)[None, None], axis=-1, keepdims=True) * v
    y = y + bonus.reshape(Bn, L, C)
    return y.astype(u.dtype)


def diff_attn_branch(q, k, v, cos, sin, lam_q1, lam_k1, lam_q2, lam_k2, subln_w, lambda_init):
    Bn, L, _ = q.shape
    H, Dh = DIFF_HEADS, DIFF_HEAD_DIM
    q = apply_rope(q.reshape(Bn, L, 2 * H, Dh), cos, sin).reshape(Bn, L, H, 2, Dh)
    k = apply_rope(k.reshape(Bn, L, 2 * H, Dh), cos, sin).reshape(Bn, L, H, 2, Dh)
    v = v.reshape(Bn, L, H, DIFF_V_DIM)
    lam = (jnp.exp(jnp.sum(lam_q1.astype(jnp.float32) * lam_k1.astype(jnp.float32)))
           - jnp.exp(jnp.sum(lam_q2.astype(jnp.float32) * lam_k2.astype(jnp.float32)))
           + lambda_init)
    scale = Dh ** -0.5
    bounds = [0] + list(range(N_META, L, Q_BLOCK)) + [L]
    outs = []
    for s, e in zip(bounds[:-1], bounds[1:]):
        qb = q[:, s:e]
        kb = k[:, :e]
        vb = v[:, :e]
        sc = jnp.einsum('bqhcd,bkhcd->bhcqk', qb, kb).astype(jnp.float32) * scale
        mask = jnp.arange(e)[None, :] <= jnp.arange(s, e)[:, None]
        sc = jnp.where(mask, sc, -jnp.inf)
        p = jax.nn.softmax(sc, axis=-1)
        attn = p[:, :, 0] - lam * p[:, :, 1]
        outs.append(jnp.einsum('bhqk,bkhd->bqhd', attn.astype(vb.dtype), vb))
    o = jnp.concatenate(outs, axis=1)
    o = rms_norm(o, subln_w, SUBLN_EPS) * (1.0 - lambda_init)
    return o.reshape(Bn, L, DIFF_WIDTH)


def setup_inputs(seed: int = 0) -> dict:
    key = jax.random.key(seed)
    ks = jax.random.split(key, 24)
    f32 = jnp.float32
    nrm = lambda k, shape, s: jax.random.normal(k, shape, f32) * s
    H, Dh = RWKV_HEADS, RWKV_HEAD_DIM
    return {
        "x": nrm(ks[0], (BATCH, SEQ, D_MODEL), 1.0),
        "meta_tokens": nrm(ks[1], (N_META, D_MODEL), 1.0),
        "pre_norm_w": 1.0 + nrm(ks[2], (DEPTH, D_MODEL), 0.05),
        "w_in": nrm(ks[3], (DEPTH, D_MODEL, IN_COLS), D_MODEL ** -0.5),
        "rwkv_mu": jax.random.uniform(ks[4], (DEPTH, RWKV_SHIFT_COLS), f32, 0.0, 1.0),
        "rwkv_w0": jax.random.uniform(ks[5], (DEPTH, RWKV_WIDTH), f32, -3.0, 1.0),
        "rwkv_w_up": nrm(ks[6], (DEPTH, RWKV_LORA_W, RWKV_WIDTH), 0.1),
        "rwkv_a0": nrm(ks[7], (DEPTH, RWKV_WIDTH), 0.1),
        "rwkv_a_up": nrm(ks[8], (DEPTH, RWKV_LORA_A, RWKV_WIDTH), 0.1),
        "rwkv_k_k": 0.85 + nrm(ks[9], (DEPTH, RWKV_WIDTH), 0.05),
        "rwkv_k_a": 1.0 + nrm(ks[10], (DEPTH, RWKV_WIDTH), 0.05),
        "rwkv_r_k": nrm(ks[11], (DEPTH, H, Dh), 0.1),
        "rwkv_gn_w": 1.0 + nrm(ks[12], (DEPTH, RWKV_WIDTH), 0.05),
        "rwkv_gn_b": nrm(ks[13], (DEPTH, RWKV_WIDTH), 0.01),
        "diff_lam_q1": nrm(ks[14], (DEPTH, DIFF_HEAD_DIM), 0.1),
        "diff_lam_k1": nrm(ks[15], (DEPTH, DIFF_HEAD_DIM), 0.1),
        "diff_lam_q2": nrm(ks[16], (DEPTH, DIFF_HEAD_DIM), 0.1),
        "diff_lam_k2": nrm(ks[17], (DEPTH, DIFF_HEAD_DIM), 0.1),
        "diff_subln_w": 1.0 + nrm(ks[18], (DEPTH, DIFF_V_DIM), 0.05),
        "w_out": nrm(ks[19], (DEPTH, MIX_WIDTH, D_MODEL), MIX_WIDTH ** -0.5),
        "post_norm_w": 1.0 + nrm(ks[20], (DEPTH, D_MODEL), 0.05),
    }


def reference(x, meta_tokens, pre_norm_w, w_in, rwkv_mu, rwkv_w0, rwkv_w_up, rwkv_a0, rwkv_a_up,
              rwkv_k_k, rwkv_k_a, rwkv_r_k, rwkv_gn_w, rwkv_gn_b, diff_lam_q1, diff_lam_k1,
              diff_lam_q2, diff_lam_k2, diff_subln_w, w_out, post_norm_w):
    Bn = x.shape[0]
    meta = jnp.broadcast_to(meta_tokens[None].astype(x.dtype), (Bn, N_META, D_MODEL))
    h = jnp.concatenate([meta, x], axis=1)
    L = h.shape[1]
    cos, sin = rope_tables(L)
    for layer in range(DEPTH):
        lambda_init = 0.8 - 0.6 * math.exp(-0.3 * layer)
        hn = rms_norm(h, pre_norm_w[layer])
        proj = hn @ w_in[layer]
        u_r = proj[..., :RWKV_SHIFT_COLS]
        g_r = proj[..., RWKV_SHIFT_COLS:RWKV_COLS]
        o = RWKV_COLS
        q_d = proj[..., o:o + DIFF_WIDTH]
        k_d = proj[..., o + DIFF_WIDTH:o + 2 * DIFF_WIDTH]
        v_d = proj[..., o + 2 * DIFF_WIDTH:o + 3 * DIFF_WIDTH]
        g_d = proj[..., o + 3 * DIFF_WIDTH:o + 4 * DIFF_WIDTH]
        y_r = rwkv7_branch(u_r, rwkv_mu[layer], rwkv_w0[layer], rwkv_w_up[layer], rwkv_a0[layer],
                           rwkv_a_up[layer], rwkv_k_k[layer], rwkv_k_a[layer], rwkv_r_k[layer],
                           rwkv_gn_w[layer], rwkv_gn_b[layer]) * jax.nn.silu(g_r)
        y_d = diff_attn_branch(q_d, k_d, v_d, cos, sin, diff_lam_q1[layer], diff_lam_k1[layer],
                               diff_lam_q2[layer], diff_lam_k2[layer], diff_subln_w[layer],
                               lambda_init) * jax.nn.silu(g_d)
        y = jnp.concatenate([y_r, y_d], axis=-1) @ w_out[layer]
        h = h + rms_norm(y, post_norm_w[layer])
    return h[:, N_META:]
```

```python
import functools
import math

import jax
import jax.numpy as jnp
from jax import lax
from jax.experimental import pallas as pl
from jax.experimental.pallas import tpu as pltpu

F32 = jnp.float32
BF16 = jnp.bfloat16

D_MODEL = 2048
N_META = 16
HEAD = 64
LANES = 128
RWKV_WIDTH = 1024
N_PAIRS = RWKV_WIDTH // LANES
DIFF_WIDTH = 1024
DIFF_HEADS = DIFF_WIDTH // LANES
LORA = 64
ROPE_THETA = 10000.0
NORM_EPS = 1e-6
GN_EPS = 64e-5
SUBLN_EPS = 1e-5
LAMBDA_INIT = 0.8 - 0.6 * math.exp(-0.3 * 0)

COL_RKV = 0
COL_GR = 3 * RWKV_WIDTH
COL_Q = 4 * RWKV_WIDTH
COL_K = COL_Q + DIFF_WIDTH
COL_V = COL_K + DIFF_WIDTH
COL_GD = COL_V + DIFF_WIDTH
P_COLS = COL_GD + DIFF_WIDTH

PROJ_TN = 1024
RWKV_CHUNK = 64
ATT_TQ = 256
ATT_TK = 256
OUT_TM = 256
VMEM_LIMIT = 48 * 1024 * 1024
MASK_VALUE = -1e30


def _dot(a, b):
    return jnp.dot(a, b, preferred_element_type=F32)


def _dot_nt(a, b):
    return lax.dot_general(a, b, (((1,), (1,)), ((), ())), preferred_element_type=F32)


def _dot_tn(a, b):
    return lax.dot_general(a, b, (((0,), (0,)), ((), ())), preferred_element_type=F32)


def _split2(x):
    hi = x.astype(BF16)
    lo = (x - hi.astype(F32)).astype(BF16)
    return hi, lo


def _inproj_kernel(x_ref, g_ref, w_ref, wl_ref, cos_ref, sin_ref, o_ref, ol_ref, hn_ref, *, tn):
    j = pl.program_id(1)

    @pl.when(j == 0)
    def _():
        x = x_ref[...]
        ms = jnp.mean(x * x, axis=-1, keepdims=True)
        hn = (x * lax.rsqrt(ms + NORM_EPS) * g_ref[...]).astype(BF16)
        hn_ref[...] = hn
        ol_ref[...] = _dot(hn, wl_ref[...])

    acc = _dot(hn_ref[...], w_ref[...])
    q_tile = COL_Q // tn
    k_tile = COL_K // tn
    is_rope = jnp.logical_or(j == q_tile, j == k_tile)

    @pl.when(is_rope)
    def _():
        cos = cos_ref[...]
        sin = sin_ref[...]
        lane = lax.broadcasted_iota(jnp.int32, cos.shape, 1)
        first_half = (lane % HEAD) < (HEAD // 2)
        scale = jnp.where(j == q_tile, HEAD ** -0.5, 1.0).astype(F32)
        for c in range(tn // LANES):
            blk = acc[:, c * LANES:(c + 1) * LANES]
            swapped = jnp.where(first_half, pltpu.roll(blk, LANES - HEAD // 2, 1),
                                pltpu.roll(blk, HEAD // 2, 1))
            o_ref[:, c * LANES:(c + 1) * LANES] = ((blk * cos + swapped * sin) * scale).astype(o_ref.dtype)

    @pl.when(jnp.logical_not(is_rope))
    def _():
        o_ref[...] = acc.astype(o_ref.dtype)


def _inproj(x2, g, w_main, w_lora, cos, sin, *, tm):
    m = x2.shape[0]
    tn = PROJ_TN
    n_pos_tiles = cos.shape[0] // tm
    return pl.pallas_call(
        functools.partial(_inproj_kernel, tn=tn),
        out_shape=(jax.ShapeDtypeStruct((m, P_COLS), BF16), jax.ShapeDtypeStruct((m, LANES), F32)),
        grid=(m // tm, P_COLS // tn),
        in_specs=[
            pl.BlockSpec((tm, D_MODEL), lambda i, j: (i, 0)),
            pl.BlockSpec((1, D_MODEL), lambda i, j: (0, 0)),
            pl.BlockSpec((D_MODEL, tn), lambda i, j: (0, j)),
            pl.BlockSpec((D_MODEL, LANES), lambda i, j: (0, 0)),
            pl.BlockSpec((tm, LANES), lambda i, j: (i % n_pos_tiles, 0)),
            pl.BlockSpec((tm, LANES), lambda i, j: (i % n_pos_tiles, 0)),
        ],
        out_specs=(
            pl.BlockSpec((tm, tn), lambda i, j: (i, j)),
            pl.BlockSpec((tm, LANES), lambda i, j: (i, 0)),
        ),
        scratch_shapes=[pltpu.VMEM((tm, D_MODEL), BF16)],
        compiler_params=pltpu.CompilerParams(
            dimension_semantics=("arbitrary", "arbitrary"), vmem_limit_bytes=VMEM_LIMIT),
        name="inproj",
    )(x2, g, w_main, w_lora, cos, sin)


def _rwkv_kernel(rkv_ref, g_ref, lora_ref, pinit_rkv_ref, pinit_lora_ref, hinit_ref,
                 mu_rkv_ref, mu_lora_ref, w0_ref, a0_ref, wl_hi_ref, wl_lo_ref,
                 kk_ref, ka_ref, rk_ref, gnw_ref, gnb_ref,
                 y_ref, hout_ref, h_scr, prev_rkv, prev_lora, *, C, n_chunks):
    c = pl.program_id(1)

    @pl.when(c == 0)
    def _():
        h_scr[...] = hinit_ref[0]
        prev_rkv[...] = pinit_rkv_ref[...]
        prev_lora[...] = pinit_lora_ref[...]

    row = lax.broadcasted_iota(jnp.int32, (C, 1), 0)

    def token_shift(x, prev8, mu):
        xp = jnp.where(row == 0, prev8[7:8, :], pltpu.roll(x, 1, 0))
        return x + (xp - x) * mu

    x = rkv_ref[...].astype(F32)
    xl = lora_ref[...]
    u = token_shift(x, prev_rkv[...], mu_rkv_ref[...])
    ul = token_shift(xl, prev_lora[...], mu_lora_ref[...])
    prev_rkv[...] = x[C - 8:C]
    prev_lora[...] = xl[C - 8:C]

    r = u[:, 0:RWKV_WIDTH]
    k = u[:, RWKV_WIDTH:2 * RWKV_WIDTH]
    v = u[:, 2 * RWKV_WIDTH:3 * RWKV_WIDTH]

    lane = lax.broadcasted_iota(jnp.int32, (C, LANES), 1)
    head0 = lane < HEAD
    tl_hi, tl_lo = _split2(jnp.where(head0, jnp.tanh(ul), ul))
    wl_hi = wl_hi_ref[...]
    lo_out = _dot(tl_hi, wl_hi) + _dot(tl_lo, wl_hi) + _dot(tl_hi, wl_lo_ref[...])
    z = w0_ref[...] + lo_out[:, 0:RWKV_WIDTH]
    softplus_neg_z = jnp.maximum(-z, 0.0) + jnp.log(1.0 + jnp.exp(-jnp.abs(z)))
    logdec = -jnp.exp(-softplus_neg_z - 0.5)
    a_lr = jax.nn.sigmoid(a0_ref[...] + lo_out[:, RWKV_WIDTH:2 * RWKV_WIDTH])

    ti = lax.broadcasted_iota(jnp.int32, (C, C), 0)
    si = lax.broadcasted_iota(jnp.int32, (C, C), 1)
    tri = jnp.where(ti >= si, 1.0, 0.0).astype(BF16)
    ld_hi = logdec.astype(BF16)
    rem = logdec - ld_hi.astype(F32)
    ld_mid = rem.astype(BF16)
    ld_lo = (rem - ld_mid.astype(F32)).astype(BF16)
    cum = _dot(tri, ld_hi) + _dot(tri, ld_mid) + _dot(tri, ld_lo)
    cum_last = cum[C - 1:C, :]
    e_excl = jnp.exp(cum - logdec)
    e_incl = jnp.exp(cum)
    e_neg = jnp.exp(-cum)
    e_hat = jnp.exp(cum_last - cum)
    gamma = jnp.exp(cum_last)

    kk_all = k * kk_ref[...]
    k2_all = k * (1.0 + (a_lr - 1.0) * ka_ref[...])
    rk_all = r * k2_all * rk_ref[...]
    g_all = g_ref[...].astype(F32)

    li = lax.broadcasted_iota(jnp.int32, (LANES, LANES), 0)
    lj = lax.broadcasted_iota(jnp.int32, (LANES, LANES), 1)
    ones_bd = jnp.where((li // HEAD) == (lj // HEAD), 1.0, 0.0).astype(BF16)

    def head_sum(t):
        hi, lo = _split2(t)
        return _dot(hi, ones_bd) + _dot(lo, ones_bd)

    def stack(t):
        return jnp.concatenate([jnp.where(head0, t, 0.0), jnp.where(head0, 0.0, t)], axis=0).astype(BF16)

    ri = lax.broadcasted_iota(jnp.int32, (2 * C, 2 * C), 0)
    ci = lax.broadcasted_iota(jnp.int32, (2 * C, 2 * C), 1)
    same_head = (ri // C) == (ci // C)
    strict = jnp.logical_and(same_head, (ri % C) > (ci % C))
    incl = jnp.logical_and(same_head, (ri % C) >= (ci % C))
    eye = jnp.where(ri == ci, 1.0, 0.0).astype(F32)

    for p in range(N_PAIRS):
        sl = slice(p * LANES, (p + 1) * LANES)
        kk = kk_all[:, sl]
        kk = kk / jnp.maximum(jnp.sqrt(head_sum(kk * kk)), 1e-12)
        a_p = a_lr[:, sl]
        k2 = k2_all[:, sl]
        b_p = kk * a_p
        v_p = v[:, sl]
        xa = stack(-kk * e_excl[:, sl])
        xr = stack(r[:, sl] * e_incl[:, sl])
        yb = stack(b_p * e_neg[:, sl])
        yk = stack(k2 * e_neg[:, sl])
        kh = stack(k2 * e_hat[:, sl])
        bh = stack(b_p * e_hat[:, sl])
        vs = stack(v_p)

        a_ab = jnp.where(strict, _dot_nt(xa, yb), 0.0)
        a_ak = jnp.where(strict, _dot_nt(xa, yk), 0.0)
        a_rb = jnp.where(incl, _dot_nt(xr, yb), 0.0)
        a_rk = jnp.where(incl, _dot_nt(xr, yk), 0.0)

        inv = eye + a_ab
        apow = a_ab
        n = 1
        while 2 * n < C:
            apow_b = apow.astype(BF16)
            apow = _dot(apow_b, apow_b)
            inv = inv + _dot(inv.astype(BF16), apow.astype(BF16))
            n *= 2

        s_old = h_scr[p]
        s_b = s_old.astype(BF16)
        rhs = _dot_nt(xa, s_b) + _dot(a_ak.astype(BF16), vs)
        us = _dot(inv.astype(BF16), rhs.astype(BF16)).astype(BF16)
        ys = _dot_nt(xr, s_b) + _dot(a_rb.astype(BF16), us) + _dot(a_rk.astype(BF16), vs)
        h_scr[p] = s_old * gamma[:, sl] + _dot_tn(vs, kh) + _dot_tn(us, bh)

        y = ys[0:C] + ys[C:2 * C]
        mean = head_sum(y) * (1.0 / HEAD)
        yc = y - mean
        var = head_sum(yc * yc) * (1.0 / HEAD)
        y = yc * lax.rsqrt(var + GN_EPS) * gnw_ref[:, sl] + gnb_ref[:, sl]
        y = y + head_sum(rk_all[:, sl]) * v_p
        g_p = g_all[:, sl]
        y_ref[:, sl] = (y * (g_p * jax.nn.sigmoid(g_p))).astype(y_ref.dtype)

    @pl.when(c == n_chunks - 1)
    def _():
        hout_ref[0] = h_scr[...]


def _rwkv(p_arr, lora_arr, pinit_rkv, pinit_lora, hinit, params, *, batch, C):
    rows = p_arr.shape[0]
    n_chunks = rows // (batch * C)
    const2 = lambda b, c: (0, 0)
    vec = lambda width: pl.BlockSpec((1, width), const2)
    return pl.pallas_call(
        functools.partial(_rwkv_kernel, C=C, n_chunks=n_chunks),
        out_shape=(jax.ShapeDtypeStruct((rows, RWKV_WIDTH), BF16),
                   jax.ShapeDtypeStruct((batch, N_PAIRS, LANES, LANES), F32)),
        grid=(batch, n_chunks),
        in_specs=[
            pl.BlockSpec((C, 3 * RWKV_WIDTH), lambda b, c: (b * n_chunks + c, COL_RKV // (3 * RWKV_WIDTH))),
            pl.BlockSpec((C, RWKV_WIDTH), lambda b, c: (b * n_chunks + c, COL_GR // RWKV_WIDTH)),
            pl.BlockSpec((C, LANES), lambda b, c: (b * n_chunks + c, 0)),
            pl.BlockSpec((8, 3 * RWKV_WIDTH), const2),
            pl.BlockSpec((8, LANES), const2),
            pl.BlockSpec((1, N_PAIRS, LANES, LANES), lambda b, c: (0, 0, 0, 0)),
            vec(3 * RWKV_WIDTH), vec(LANES), vec(RWKV_WIDTH), vec(RWKV_WIDTH),
            pl.BlockSpec((LANES, 2 * RWKV_WIDTH), const2),
            pl.BlockSpec((LANES, 2 * RWKV_WIDTH), const2),
            vec(RWKV_WIDTH), vec(RWKV_WIDTH), vec(RWKV_WIDTH), vec(RWKV_WIDTH), vec(RWKV_WIDTH),
        ],
        out_specs=(
            pl.BlockSpec((C, RWKV_WIDTH), lambda b, c: (b * n_chunks + c, 0)),
            pl.BlockSpec((1, N_PAIRS, LANES, LANES), lambda b, c: (b, 0, 0, 0)),
        ),
        scratch_shapes=[
            pltpu.VMEM((N_PAIRS, LANES, LANES), F32),
            pltpu.VMEM((8, 3 * RWKV_WIDTH), F32),
            pltpu.VMEM((8, LANES), F32),
        ],
        compiler_params=pltpu.CompilerParams(
            dimension_semantics=("arbitrary", "arbitrary"), vmem_limit_bytes=VMEM_LIMIT),
        name="rwkv7_chunk%d" % C,
    )(p_arr, p_arr, lora_arr, pinit_rkv, pinit_lora, hinit, *params)


def _attn_kernel(q_ref, k_ref, v_ref, g_ref, km_ref, vm_ref, lq1_ref, lk1_ref, lq2_ref, lk2_ref, sw_ref,
                 o_ref, m_scr, l_scr, acc_scr, *, tq, tk):
    i = pl.program_id(2)
    q = q_ref[...]
    lane = lax.broadcasted_iota(jnp.int32, (tq, LANES), 1)
    comp0 = lane < HEAD
    zero = jnp.zeros_like(q)
    qs = jnp.concatenate([jnp.where(comp0, q, zero), jnp.where(comp0, zero, q)], axis=0)

    s = _dot_nt(qs, km_ref[...])
    m = jnp.max(s, axis=-1, keepdims=True)
    p = jnp.exp(s - m)
    m_scr[...] = m
    l_scr[...] = jnp.sum(p, axis=-1, keepdims=True)
    acc_scr[...] = _dot(p.astype(BF16), vm_ref[...])

    def block(j, masked):
        start = pl.multiple_of(j * tk, tk)
        s = _dot_nt(qs, k_ref[pl.ds(start, tk), :])
        if masked:
            qpos = lax.broadcasted_iota(jnp.int32, (2 * tq, tk), 0) % tq + i * tq
            kpos = lax.broadcasted_iota(jnp.int32, (2 * tq, tk), 1) + j * tk
            s = jnp.where(kpos <= qpos, s, MASK_VALUE)
        m_old = m_scr[...]
        m_new = jnp.maximum(m_old, jnp.max(s, axis=-1, keepdims=True))
        alpha = jnp.exp(m_old - m_new)
        p = jnp.exp(s - m_new)
        m_scr[...] = m_new
        l_scr[...] = alpha * l_scr[...] + jnp.sum(p, axis=-1, keepdims=True)
        acc_scr[...] = alpha * acc_scr[...] + _dot(p.astype(BF16), v_ref[pl.ds(start, tk), :])

    n_full = i * (tq // tk)

    def body(j, carry):
        block(j, False)
        return carry

    lax.fori_loop(0, n_full, body, 0)
    for d in range(tq // tk):
        block(n_full + d, True)

    lam = (jnp.exp(jnp.sum(lq1_ref[...] * lk1_ref[...], axis=-1, keepdims=True))
           - jnp.exp(jnp.sum(lq2_ref[...] * lk2_ref[...], axis=-1, keepdims=True)) + LAMBDA_INIT)
    on = acc_scr[...] / l_scr[...]
    o = on[0:tq] - lam * on[tq:2 * tq]
    o = o * lax.rsqrt(jnp.mean(o * o, axis=-1, keepdims=True) + SUBLN_EPS) * sw_ref[...] * (1.0 - LAMBDA_INIT)
    g = g_ref[...].astype(F32)
    o_ref[...] = (o * (g * jax.nn.sigmoid(g))).astype(o_ref.dtype)


def _attention(p_arr, p_meta, lam_vecs, subln_w, *, batch, seq):
    tq, tk = ATT_TQ, ATT_TK
    nq = seq // tq
    lanes_blk = lambda col: col // LANES
    small = pl.BlockSpec((1, HEAD), lambda b, h, i: (0, 0))
    return pl.pallas_call(
        functools.partial(_attn_kernel, tq=tq, tk=tk),
        out_shape=jax.ShapeDtypeStruct((batch * seq, DIFF_WIDTH), BF16),
        grid=(batch, DIFF_HEADS, nq),
        in_specs=[
            pl.BlockSpec((tq, LANES), lambda b, h, i: (b * nq + i, lanes_blk(COL_Q) + h)),
            pl.BlockSpec((seq, LANES), lambda b, h, i: (b, lanes_blk(COL_K) + h)),
            pl.BlockSpec((seq, LANES), lambda b, h, i: (b, lanes_blk(COL_V) + h)),
            pl.BlockSpec((tq, LANES), lambda b, h, i: (b * nq + i, lanes_blk(COL_GD) + h)),
            pl.BlockSpec((N_META, LANES), lambda b, h, i: (0, lanes_blk(COL_K) + h)),
            pl.BlockSpec((N_META, LANES), lambda b, h, i: (0, lanes_blk(COL_V) + h)),
            small, small, small, small,
            pl.BlockSpec((1, LANES), lambda b, h, i: (0, 0)),
        ],
        out_specs=pl.BlockSpec((tq, LANES), lambda b, h, i: (b * nq + i, h)),
        scratch_shapes=[
            pltpu.VMEM((2 * tq, 1), F32),
            pltpu.VMEM((2 * tq, 1), F32),
            pltpu.VMEM((2 * tq, LANES), F32),
        ],
        compiler_params=pltpu.CompilerParams(
            dimension_semantics=("arbitrary", "arbitrary", "arbitrary"), vmem_limit_bytes=VMEM_LIMIT),
        name="diff_attn",
    )(p_arr, p_arr, p_arr, p_arr, p_meta, p_meta, *lam_vecs, subln_w)


def _outproj_kernel(yr_ref, yd_ref, w1_ref, w2_ref, x_ref, g_ref, o_ref):
    y = _dot(yr_ref[...], w1_ref[...]) + _dot(yd_ref[...], w2_ref[...])
    ms = jnp.mean(y * y, axis=-1, keepdims=True)
    o_ref[...] = x_ref[...] + y * lax.rsqrt(ms + NORM_EPS) * g_ref[...]


def _outproj(y_r, y_d, w1, w2, x2, g):
    m = x2.shape[0]
    tm = OUT_TM
    return pl.pallas_call(
        _outproj_kernel,
        out_shape=jax.ShapeDtypeStruct((m, D_MODEL), F32),
        grid=(m // tm,),
        in_specs=[
            pl.BlockSpec((tm, RWKV_WIDTH), lambda i: (i, 0)),
            pl.BlockSpec((tm, DIFF_WIDTH), lambda i: (i, 0)),
            pl.BlockSpec((RWKV_WIDTH, D_MODEL), lambda i: (0, 0)),
            pl.BlockSpec((DIFF_WIDTH, D_MODEL), lambda i: (0, 0)),
            pl.BlockSpec((tm, D_MODEL), lambda i: (i, 0)),
            pl.BlockSpec((1, D_MODEL), lambda i: (0, 0)),
        ],
        out_specs=pl.BlockSpec((tm, D_MODEL), lambda i: (i, 0)),
        compiler_params=pltpu.CompilerParams(
            dimension_semantics=("arbitrary",), vmem_limit_bytes=VMEM_LIMIT),
        name="outproj",
    )(y_r, y_d, w1, w2, x2, g)


def _rope_tables(first_pos, n_pos):
    pos = jnp.arange(first_pos, first_pos + n_pos, dtype=F32)
    inv_freq = ROPE_THETA ** (-jnp.arange(0, HEAD, 2, dtype=F32) / HEAD)
    ang = pos[:, None] * inv_freq[None, :]
    cos = jnp.cos(ang)
    sin = jnp.sin(ang)
    cos = jnp.concatenate([cos, cos, cos, cos], axis=-1)
    sin = jnp.concatenate([-sin, sin, -sin, sin], axis=-1)
    return cos, sin


def kernel(x, meta_tokens, pre_norm_w, w_in, rwkv_mu, rwkv_w0, rwkv_w_up, rwkv_a0, rwkv_a_up, rwkv_k_k, rwkv_k_a, rwkv_r_k, rwkv_gn_w, rwkv_gn_b, diff_lam_q1, diff_lam_k1, diff_lam_q2, diff_lam_k2, diff_subln_w, w_out, post_norm_w):
    batch, seq, d = x.shape
    assert d == D_MODEL and meta_tokens.shape == (N_META, D_MODEL)
    assert seq % RWKV_CHUNK == 0 and seq % ATT_TQ == 0 and ATT_TQ % ATT_TK == 0
    layer = 0
    x2 = x.reshape(batch * seq, D_MODEL)

    w = w_in[layer]
    rkv_end = 3 * RWKV_WIDTH
    lora_end = rkv_end + 2 * LORA
    w_main = jnp.concatenate([w[:, :rkv_end], w[:, lora_end:]], axis=1).astype(BF16)
    w_lora_in = w[:, rkv_end:lora_end].astype(BF16)
    g_pre = pre_norm_w[layer].reshape(1, D_MODEL)

    cos_m, sin_m = _rope_tables(0, N_META)
    cos_x, sin_x = _rope_tables(N_META, seq)
    proj_tm = 512
    p_meta, lora_meta = _inproj(meta_tokens.astype(x.dtype), g_pre, w_main, w_lora_in, cos_m, sin_m, tm=N_META)
    p_x, lora_x = _inproj(x2, g_pre, w_main, w_lora_in, cos_x, sin_x, tm=proj_tm)

    mu = rwkv_mu[layer]
    zeros = jnp.zeros((LORA, RWKV_WIDTH), F32)
    w_lora = jnp.concatenate([
        jnp.concatenate([rwkv_w_up[layer], zeros], axis=1),
        jnp.concatenate([zeros, rwkv_a_up[layer]], axis=1)], axis=0)
    w_lora_hi, w_lora_lo = _split2(w_lora)
    row = lambda t, n: t.reshape(1, n)
    rwkv_params = (
        row(mu[:rkv_end], rkv_end), row(mu[rkv_end:lora_end], LANES),
        row(rwkv_w0[layer], RWKV_WIDTH), row(rwkv_a0[layer], RWKV_WIDTH),
        w_lora_hi, w_lora_lo,
        row(rwkv_k_k[layer], RWKV_WIDTH), row(rwkv_k_a[layer], RWKV_WIDTH),
        row(rwkv_r_k[layer], RWKV_WIDTH), row(rwkv_gn_w[layer], RWKV_WIDTH), row(rwkv_gn_b[layer], RWKV_WIDTH),
    )
    _, h_meta = _rwkv(p_meta, lora_meta, jnp.zeros((8, rkv_end), F32), jnp.zeros((8, LANES), F32),
                      jnp.zeros((1, N_PAIRS, LANES, LANES), F32), rwkv_params, batch=1, C=N_META)
    y_r, _ = _rwkv(p_x, lora_x, p_meta[N_META - 8:, :rkv_end].astype(F32), lora_meta[N_META - 8:],
                   h_meta, rwkv_params, batch=batch, C=RWKV_CHUNK)

    lam_vecs = tuple(t[layer].reshape(1, HEAD) for t in (diff_lam_q1, diff_lam_k1, diff_lam_q2, diff_lam_k2))
    y_d = _attention(p_x, p_meta, lam_vecs, diff_subln_w[layer].reshape(1, LANES), batch=batch, seq=seq)

    wo = w_out[layer].astype(BF16)
    out = _outproj(y_r, y_d, wo[:RWKV_WIDTH], wo[RWKV_WIDTH:], x2, post_norm_w[layer].reshape(1, D_MODEL))
    return out.reshape(batch, seq, D_MODEL)
```

```python
import functools
import math

import jax
import jax.numpy as jnp
from jax import lax
from jax.experimental import pallas as pl
from jax.experimental.pallas import tpu as pltpu

F32 = jnp.float32
BF16 = jnp.bfloat16

D_MODEL = 2048
N_META = 16
HEAD = 64
LANES = 128
RWKV_WIDTH = 1024
N_PAIRS = RWKV_WIDTH // LANES
DIFF_WIDTH = 1024
DIFF_HEADS = DIFF_WIDTH // LANES
LORA = 64
ROPE_THETA = 10000.0
NORM_EPS = 1e-6
GN_EPS = 64e-5
SUBLN_EPS = 1e-5
LAMBDA_INIT = 0.8 - 0.6 * math.exp(-0.3 * 0)

COL_RKV = 0
COL_GR = 3 * RWKV_WIDTH
COL_Q = 4 * RWKV_WIDTH
COL_K = COL_Q + DIFF_WIDTH
COL_V = COL_K + DIFF_WIDTH
COL_GD = COL_V + DIFF_WIDTH
P_COLS = COL_GD + DIFF_WIDTH

PROJ_TM = 512
PROJ_TN = 1024
RWKV_CHUNK = 64
ATT_TQ = 256
ATT_TK = 256
OUT_TM = 256
VMEM_LIMIT = 48 * 1024 * 1024
MASK_VALUE = -1e30


def _dot(a, b):
    return jnp.dot(a, b, preferred_element_type=F32)


def _dot_nt(a, b):
    return lax.dot_general(a, b, (((1,), (1,)), ((), ())), preferred_element_type=F32)


def _dot_tn(a, b):
    return lax.dot_general(a, b, (((0,), (0,)), ((), ())), preferred_element_type=F32)


def _split2(x):
    hi = x.astype(BF16)
    lo = (x - hi.astype(F32)).astype(BF16)
    return hi, lo


def _inproj_kernel(x_ref, g_ref, w_ref, wl_ref, cos_ref, sin_ref, o_ref, ol_ref, hn_ref, *, tn):
    j = pl.program_id(1)

    @pl.when(j == 0)
    def _():
        x = x_ref[...]
        ms = jnp.mean(x * x, axis=-1, keepdims=True)
        hn = (x * lax.rsqrt(ms + NORM_EPS) * g_ref[...]).astype(BF16)
        hn_ref[...] = hn
        ol_ref[...] = _dot(hn, wl_ref[...])

    acc = _dot(hn_ref[...], w_ref[...])
    q_tile = COL_Q // tn
    k_tile = COL_K // tn
    is_rope = jnp.logical_or(j == q_tile, j == k_tile)

    @pl.when(is_rope)
    def _():
        cos = cos_ref[...]
        sin = sin_ref[...]
        lane = lax.broadcasted_iota(jnp.int32, cos.shape, 1)
        first_half = (lane % HEAD) < (HEAD // 2)
        scale = jnp.where(j == q_tile, HEAD ** -0.5, 1.0).astype(F32)
        for c in range(tn // LANES):
            blk = acc[:, c * LANES:(c + 1) * LANES]
            swapped = jnp.where(first_half, pltpu.roll(blk, LANES - HEAD // 2, 1),
                                pltpu.roll(blk, HEAD // 2, 1))
            o_ref[:, c * LANES:(c + 1) * LANES] = ((blk * cos + swapped * sin) * scale).astype(o_ref.dtype)

    @pl.when(jnp.logical_not(is_rope))
    def _():
        o_ref[...] = acc.astype(o_ref.dtype)


def _inproj(x2, g, w_main, w_lora, cos, sin, *, tm):
    m = x2.shape[0]
    tn = PROJ_TN
    n_pos_tiles = cos.shape[0] // tm
    return pl.pallas_call(
        functools.partial(_inproj_kernel, tn=tn),
        out_shape=(jax.ShapeDtypeStruct((m, P_COLS), BF16), jax.ShapeDtypeStruct((m, LANES), F32)),
        grid=(m // tm, P_COLS // tn),
        in_specs=[
            pl.BlockSpec((tm, D_MODEL), lambda i, j: (i, 0)),
            pl.BlockSpec((1, D_MODEL), lambda i, j: (0, 0)),
            pl.BlockSpec((D_MODEL, tn), lambda i, j: (0, j)),
            pl.BlockSpec((D_MODEL, LANES), lambda i, j: (0, 0)),
            pl.BlockSpec((tm, LANES), lambda i, j: (i % n_pos_tiles, 0)),
            pl.BlockSpec((tm, LANES), lambda i, j: (i % n_pos_tiles, 0)),
        ],
        out_specs=(
            pl.BlockSpec((tm, tn), lambda i, j: (i, j)),
            pl.BlockSpec((tm, LANES), lambda i, j: (i, 0)),
        ),
        scratch_shapes=[pltpu.VMEM((tm, D_MODEL), BF16)],
        compiler_params=pltpu.CompilerParams(
            dimension_semantics=("arbitrary", "arbitrary"), vmem_limit_bytes=VMEM_LIMIT),
        name="inproj",
    )(x2, g, w_main, w_lora, cos, sin)


def _rwkv_kernel(rkv_ref, g_ref, lora_ref, pinit_rkv_ref, pinit_lora_ref, hinit_ref,
                 mu_rkv_ref, mu_lora_ref, w0_ref, a0_ref, wl_hi_ref, wl_lo_ref,
                 kk_ref, ka_ref, rk_ref, gnw_ref, gnb_ref,
                 y_ref, hout_ref, h_scr, prev_rkv, prev_lora, *, n_chunks):
    C = RWKV_CHUNK
    c = pl.program_id(1)

    @pl.when(c == 0)
    def _():
        h_scr[...] = hinit_ref[0]
        prev_rkv[...] = pinit_rkv_ref[...]
        prev_lora[...] = pinit_lora_ref[...]

    row = lax.broadcasted_iota(jnp.int32, (C, 1), 0)

    def token_shift(x, prev8, mu):
        xp = jnp.where(row == 0, prev8[7:8, :], pltpu.roll(x, 1, 0))
        return x + (xp - x) * mu

    x = rkv_ref[...].astype(F32)
    xl = lora_ref[...]
    u = token_shift(x, prev_rkv[...], mu_rkv_ref[...])
    ul = token_shift(xl, prev_lora[...], mu_lora_ref[...])
    prev_rkv[...] = x[C - 8:C]
    prev_lora[...] = xl[C - 8:C]

    r = u[:, 0:RWKV_WIDTH]
    k = u[:, RWKV_WIDTH:2 * RWKV_WIDTH]
    v = u[:, 2 * RWKV_WIDTH:3 * RWKV_WIDTH]

    lane = lax.broadcasted_iota(jnp.int32, (C, LANES), 1)
    head0 = lane < HEAD
    tl_hi, tl_lo = _split2(jnp.where(head0, jnp.tanh(ul), ul))
    wl_hi = wl_hi_ref[...]
    lo_out = _dot(tl_hi, wl_hi) + _dot(tl_lo, wl_hi) + _dot(tl_hi, wl_lo_ref[...])
    z = w0_ref[...] + lo_out[:, 0:RWKV_WIDTH]
    softplus_neg_z = jnp.maximum(-z, 0.0) + jnp.log(1.0 + jnp.exp(-jnp.abs(z)))
    logdec = -jnp.exp(-softplus_neg_z - 0.5)
    a_lr = jax.nn.sigmoid(a0_ref[...] + lo_out[:, RWKV_WIDTH:2 * RWKV_WIDTH])

    ti = lax.broadcasted_iota(jnp.int32, (C, C), 0)
    si = lax.broadcasted_iota(jnp.int32, (C, C), 1)
    tri = jnp.where(ti >= si, 1.0, 0.0).astype(BF16)
    ld_hi = logdec.astype(BF16)
    rem = logdec - ld_hi.astype(F32)
    ld_mid = rem.astype(BF16)
    ld_lo = (rem - ld_mid.astype(F32)).astype(BF16)
    cum = _dot(tri, ld_hi) + _dot(tri, ld_mid) + _dot(tri, ld_lo)
    cum_last = cum[C - 1:C, :]
    e_excl = jnp.exp(cum - logdec)
    e_incl = jnp.exp(cum)
    e_neg = jnp.exp(-cum)
    e_hat = jnp.exp(cum_last - cum)
    gamma = jnp.exp(cum_last)

    li = lax.broadcasted_iota(jnp.int32, (2 * LANES, LANES), 0)
    lj = lax.broadcasted_iota(jnp.int32, (2 * LANES, LANES), 1)
    ones_bd2 = jnp.where(((li % LANES) // HEAD) == (lj // HEAD), 1.0, 0.0).astype(BF16)

    def head_sum(t):
        rows = jnp.concatenate([t[:, p * LANES:(p + 1) * LANES] for p in range(N_PAIRS)], axis=0)
        hi, lo = _split2(rows)
        s = _dot(jnp.concatenate([hi, lo], axis=1), ones_bd2)
        return jnp.concatenate([s[p * C:(p + 1) * C] for p in range(N_PAIRS)], axis=1)

    kk = k * kk_ref[...]
    kk = kk / jnp.maximum(jnp.sqrt(head_sum(kk * kk)), 1e-12)
    k2 = k * (1.0 + (a_lr - 1.0) * ka_ref[...])
    b = kk * a_lr
    bonus = head_sum(r * k2 * rk_ref[...]) * v
    xa_all = -kk * e_excl
    xr_all = r * e_incl
    yb_all = b * e_neg
    yk_all = k2 * e_neg
    kh_all = k2 * e_hat
    bh_all = b * e_hat

    def stack(ts, p):
        parts = []
        for t in ts:
            tp = t[:, p * LANES:(p + 1) * LANES]
            parts += [jnp.where(head0, tp, 0.0), jnp.where(head0, 0.0, tp)]
        return jnp.concatenate(parts, axis=0).astype(BF16)

    S = 2 * C
    ri = lax.broadcasted_iota(jnp.int32, (2 * S, 2 * S), 0)
    ci = lax.broadcasted_iota(jnp.int32, (2 * S, 2 * S), 1)
    same_head = ((ri % S) // C) == ((ci % S) // C)
    tpos = ri % C
    spos = ci % C
    causal = tpos >= spos + jnp.where(ri < S, 1, 0)
    a_mask = jnp.logical_and(same_head, causal)
    ei = lax.broadcasted_iota(jnp.int32, (S, S), 0)
    ej = lax.broadcasted_iota(jnp.int32, (S, S), 1)
    eye = jnp.where(ei == ej, 1.0, 0.0).astype(F32)

    pairs = range(N_PAIRS)
    lhs_ar = [stack((xa_all, xr_all), p) for p in pairs]
    rhs_bk = [stack((yb_all, yk_all), p) for p in pairs]
    hat_kb = [stack((kh_all, bh_all), p) for p in pairs]
    vs = [stack((v,), p) for p in pairs]

    a_all = [jnp.where(a_mask, _dot_nt(lhs_ar[p], rhs_bk[p]), 0.0) for p in pairs]
    a_ab = [a_all[p][0:S, 0:S] for p in pairs]
    a_x_v = [_dot(a_all[p][:, S:2 * S].astype(BF16), vs[p]) for p in pairs]

    inv = [eye + a_ab[p] for p in pairs]
    apow = []
    for p in pairs:
        ab = a_ab[p].astype(BF16)
        apow.append(_dot(ab, ab))
    n = 2
    while 2 * n < C:
        for p in pairs:
            ab = apow[p].astype(BF16)
            res = _dot(jnp.concatenate([inv[p].astype(BF16), ab], axis=0), ab)
            inv[p] = inv[p] + res[0:S]
            apow[p] = res[S:2 * S]
        n *= 2
    for p in pairs:
        inv[p] = inv[p] + _dot(inv[p].astype(BF16), apow[p].astype(BF16))

    s_old = [h_scr[p] for p in pairs]
    x_s = [_dot_nt(lhs_ar[p], s_old[p].astype(BF16)) for p in pairs]
    us = [_dot(inv[p].astype(BF16), (x_s[p][0:S] + a_x_v[p][0:S]).astype(BF16)).astype(BF16) for p in pairs]
    ys = [x_s[p][S:2 * S] + a_x_v[p][S:2 * S] + _dot(a_all[p][S:2 * S, 0:S].astype(BF16), us[p])
          for p in pairs]
    for p in pairs:
        sl = slice(p * LANES, (p + 1) * LANES)
        h_scr[p] = s_old[p] * gamma[:, sl] + _dot_tn(jnp.concatenate([vs[p], us[p]], axis=0), hat_kb[p])

    y = jnp.concatenate([ys[p][0:C] + ys[p][C:S] for p in pairs], axis=1)
    yc = y - head_sum(y) * (1.0 / HEAD)
    var = head_sum(yc * yc) * (1.0 / HEAD)
    y = yc * lax.rsqrt(var + GN_EPS) * gnw_ref[...] + gnb_ref[...] + bonus
    g = g_ref[...].astype(F32)
    y_ref[...] = (y * (g * jax.nn.sigmoid(g))).astype(y_ref.dtype)

    @pl.when(c == n_chunks - 1)
    def _():
        hout_ref[0] = h_scr[...]


def _rwkv(p_arr, lora_arr, pinit_rkv, pinit_lora, hinit, params, *, batch):
    C = RWKV_CHUNK
    rows = p_arr.shape[0]
    n_chunks = rows // (batch * C)
    const2 = lambda b, c: (0, 0)
    vec = lambda width: pl.BlockSpec((1, width), const2)
    return pl.pallas_call(
        functools.partial(_rwkv_kernel, n_chunks=n_chunks),
        out_shape=(jax.ShapeDtypeStruct((rows, RWKV_WIDTH), BF16),
                   jax.ShapeDtypeStruct((batch, N_PAIRS, LANES, LANES), F32)),
        grid=(batch, n_chunks),
        in_specs=[
            pl.BlockSpec((C, 3 * RWKV_WIDTH), lambda b, c: (b * n_chunks + c, COL_RKV // (3 * RWKV_WIDTH))),
            pl.BlockSpec((C, RWKV_WIDTH), lambda b, c: (b * n_chunks + c, COL_GR // RWKV_WIDTH)),
            pl.BlockSpec((C, LANES), lambda b, c: (b * n_chunks + c, 0)),
            pl.BlockSpec((8, 3 * RWKV_WIDTH), const2),
            pl.BlockSpec((8, LANES), const2),
            pl.BlockSpec((1, N_PAIRS, LANES, LANES), lambda b, c: (0, 0, 0, 0)),
            vec(3 * RWKV_WIDTH), vec(LANES), vec(RWKV_WIDTH), vec(RWKV_WIDTH),
            pl.BlockSpec((LANES, 2 * RWKV_WIDTH), const2),
            pl.BlockSpec((LANES, 2 * RWKV_WIDTH), const2),
            vec(RWKV_WIDTH), vec(RWKV_WIDTH), vec(RWKV_WIDTH), vec(RWKV_WIDTH), vec(RWKV_WIDTH),
        ],
        out_specs=(
            pl.BlockSpec((C, RWKV_WIDTH), lambda b, c: (b * n_chunks + c, 0)),
            pl.BlockSpec((1, N_PAIRS, LANES, LANES), lambda b, c: (b, 0, 0, 0)),
        ),
        scratch_shapes=[
            pltpu.VMEM((N_PAIRS, LANES, LANES), F32),
            pltpu.VMEM((8, 3 * RWKV_WIDTH), F32),
            pltpu.VMEM((8, LANES), F32),
        ],
        compiler_params=pltpu.CompilerParams(
            dimension_semantics=("arbitrary", "arbitrary"), vmem_limit_bytes=VMEM_LIMIT),
        name="rwkv7_chunk",
    )(p_arr, p_arr, lora_arr, pinit_rkv, pinit_lora, hinit, *params)


def _attn_kernel(q_ref, k_ref, v_ref, g_ref, km_ref, vmt_ref, lq1_ref, lk1_ref, lq2_ref, lk2_ref, sw_ref,
                 o_ref, vt_scr, m_scr, l_scr, acc_scr, *, tq, tk, n_kv):
    i = pl.program_id(2)

    @pl.when(i == 0)
    def _():
        for c in range(n_kv):
            vt_scr[c] = v_ref[c * tk:(c + 1) * tk, :].astype(F32).T.astype(BF16)

    qt = q_ref[...].astype(F32).T
    comp0 = lax.broadcasted_iota(jnp.int32, (LANES, tq), 0) < HEAD
    qst = jnp.concatenate([jnp.where(comp0, qt, 0.0), jnp.where(comp0, 0.0, qt)], axis=1).astype(BF16)

    s = _dot(km_ref[...], qst)
    m = jnp.max(s, axis=0, keepdims=True)
    p = jnp.exp(s - m)
    m_scr[...] = m
    l_scr[...] = jnp.sum(p, axis=0, keepdims=True)
    acc_scr[...] = _dot(vmt_ref[0], p.astype(BF16))

    def block(j, masked):
        start = pl.multiple_of(j * tk, tk)
        s = _dot(k_ref[pl.ds(start, tk), :], qst)
        if masked:
            kpos = lax.broadcasted_iota(jnp.int32, (tk, 2 * tq), 0) + j * tk
            qpos = lax.broadcasted_iota(jnp.int32, (tk, 2 * tq), 1) % tq + i * tq
            s = jnp.where(kpos <= qpos, s, MASK_VALUE)
        m_old = m_scr[...]
        m_new = jnp.maximum(m_old, jnp.max(s, axis=0, keepdims=True))
        alpha = jnp.exp(m_old - m_new)
        p = jnp.exp(s - m_new)
        m_scr[...] = m_new
        l_scr[...] = alpha * l_scr[...] + jnp.sum(p, axis=0, keepdims=True)
        acc_scr[...] = alpha * acc_scr[...] + _dot(vt_scr[j], p.astype(BF16))

    n_full = i * (tq // tk)

    def body(j, carry):
        block(j, False)
        return carry

    lax.fori_loop(0, n_full, body, 0)
    for d in range(tq // tk):
        block(n_full + d, True)

    lam = (jnp.exp(jnp.sum(lq1_ref[...] * lk1_ref[...], axis=-1, keepdims=True))
           - jnp.exp(jnp.sum(lq2_ref[...] * lk2_ref[...], axis=-1, keepdims=True)) + LAMBDA_INIT)
    on = acc_scr[...] / l_scr[...]
    ot = on[:, 0:tq] - lam * on[:, tq:2 * tq]
    ot = ot * lax.rsqrt(jnp.mean(ot * ot, axis=0, keepdims=True) + SUBLN_EPS)
    g = g_ref[...].astype(F32)
    o_ref[...] = (ot.T * sw_ref[...] * (1.0 - LAMBDA_INIT) * (g * jax.nn.sigmoid(g))).astype(o_ref.dtype)


def _attention(p_arr, p_meta, vm_t, lam_vecs, subln_w, *, batch, seq):
    tq, tk = ATT_TQ, ATT_TK
    nq = seq // tq
    n_kv = seq // tk
    lanes_blk = lambda col: col // LANES
    small = pl.BlockSpec((1, HEAD), lambda b, h, i: (0, 0))
    return pl.pallas_call(
        functools.partial(_attn_kernel, tq=tq, tk=tk, n_kv=n_kv),
        out_shape=jax.ShapeDtypeStruct((batch * seq, DIFF_WIDTH), BF16),
        grid=(batch, DIFF_HEADS, nq),
        in_specs=[
            pl.BlockSpec((tq, LANES), lambda b, h, i: (b * nq + i, lanes_blk(COL_Q) + h)),
            pl.BlockSpec((seq, LANES), lambda b, h, i: (b, lanes_blk(COL_K) + h)),
            pl.BlockSpec((seq, LANES), lambda b, h, i: (b, lanes_blk(COL_V) + h)),
            pl.BlockSpec((tq, LANES), lambda b, h, i: (b * nq + i, lanes_blk(COL_GD) + h)),
            pl.BlockSpec((N_META, LANES), lambda b, h, i: (0, lanes_blk(COL_K) + h)),
            pl.BlockSpec((1, LANES, N_META), lambda b, h, i: (h, 0, 0)),
            small, small, small, small,
            pl.BlockSpec((1, LANES), lambda b, h, i: (0, 0)),
        ],
        out_specs=pl.BlockSpec((tq, LANES), lambda b, h, i: (b * nq + i, h)),
        scratch_shapes=[
            pltpu.VMEM((n_kv, LANES, tk), BF16),
            pltpu.VMEM((1, 2 * tq), F32),
            pltpu.VMEM((1, 2 * tq), F32),
            pltpu.VMEM((LANES, 2 * tq), F32),
        ],
        compiler_params=pltpu.CompilerParams(
            dimension_semantics=("arbitrary", "arbitrary", "arbitrary"), vmem_limit_bytes=VMEM_LIMIT),
        name="diff_attn",
    )(p_arr, p_arr, p_arr, p_arr, p_meta, vm_t, *lam_vecs, subln_w)


def _outproj_kernel(yr_ref, yd_ref, w1_ref, w2_ref, x_ref, g_ref, o_ref):
    y = _dot(yr_ref[...], w1_ref[...]) + _dot(yd_ref[...], w2_ref[...])
    ms = jnp.mean(y * y, axis=-1, keepdims=True)
    o_ref[...] = x_ref[...] + y * lax.rsqrt(ms + NORM_EPS) * g_ref[...]


def _outproj(y_r, y_d, w1, w2, x2, g):
    m = x2.shape[0]
    tm = OUT_TM
    return pl.pallas_call(
        _outproj_kernel,
        out_shape=jax.ShapeDtypeStruct((m, D_MODEL), F32),
        grid=(m // tm,),
        in_specs=[
            pl.BlockSpec((tm, RWKV_WIDTH), lambda i: (i, 0)),
            pl.BlockSpec((tm, DIFF_WIDTH), lambda i: (i, 0)),
            pl.BlockSpec((RWKV_WIDTH, D_MODEL), lambda i: (0, 0)),
            pl.BlockSpec((DIFF_WIDTH, D_MODEL), lambda i: (0, 0)),
            pl.BlockSpec((tm, D_MODEL), lambda i: (i, 0)),
            pl.BlockSpec((1, D_MODEL), lambda i: (0, 0)),
        ],
        out_specs=pl.BlockSpec((tm, D_MODEL), lambda i: (i, 0)),
        compiler_params=pltpu.CompilerParams(
            dimension_semantics=("arbitrary",), vmem_limit_bytes=VMEM_LIMIT),
        name="outproj",
    )(y_r, y_d, w1, w2, x2, g)


def _rope_tables(first_pos, n_pos):
    pos = jnp.arange(first_pos, first_pos + n_pos, dtype=F32)
    inv_freq = ROPE_THETA ** (-jnp.arange(0, HEAD, 2, dtype=F32) / HEAD)
    ang = pos[:, None] * inv_freq[None, :]
    cos = jnp.cos(ang)
    sin = jnp.sin(ang)
    cos = jnp.concatenate([cos, cos, cos, cos], axis=-1)
    sin = jnp.concatenate([-sin, sin, -sin, sin], axis=-1)
    return cos, sin


def kernel(x, meta_tokens, pre_norm_w, w_in, rwkv_mu, rwkv_w0, rwkv_w_up, rwkv_a0, rwkv_a_up, rwkv_k_k, rwkv_k_a, rwkv_r_k, rwkv_gn_w, rwkv_gn_b, diff_lam_q1, diff_lam_k1, diff_lam_q2, diff_lam_k2, diff_subln_w, w_out, post_norm_w):
    batch, seq, d = x.shape
    assert d == D_MODEL and meta_tokens.shape == (N_META, D_MODEL)
    assert seq % RWKV_CHUNK == 0 and seq % ATT_TQ == 0 and ATT_TQ % ATT_TK == 0
    assert (batch * seq) % PROJ_TM == 0 and seq % PROJ_TM == 0
    layer = 0
    x2 = x.reshape(batch * seq, D_MODEL)

    w = w_in[layer]
    rkv_end = 3 * RWKV_WIDTH
    lora_end = rkv_end + 2 * LORA
    w_main = jnp.concatenate([w[:, :rkv_end], w[:, lora_end:]], axis=1).astype(BF16)
    w_lora_in = w[:, rkv_end:lora_end].astype(BF16)
    g_pre = pre_norm_w[layer].reshape(1, D_MODEL)

    cos_m, sin_m = _rope_tables(0, N_META)
    cos_x, sin_x = _rope_tables(N_META, seq)
    p_meta, lora_meta = _inproj(meta_tokens.astype(x.dtype), g_pre, w_main, w_lora_in, cos_m, sin_m, tm=N_META)
    p_x, lora_x = _inproj(x2, g_pre, w_main, w_lora_in, cos_x, sin_x, tm=PROJ_TM)

    mu = rwkv_mu[layer]
    zeros = jnp.zeros((LORA, RWKV_WIDTH), F32)
    w_lora = jnp.concatenate([
        jnp.concatenate([rwkv_w_up[layer], zeros], axis=1),
        jnp.concatenate([zeros, rwkv_a_up[layer]], axis=1)], axis=0)
    w_lora_hi, w_lora_lo = _split2(w_lora)
    row = lambda t, n: t.reshape(1, n)
    rwkv_params = (
        row(mu[:rkv_end], rkv_end), row(mu[rkv_end:lora_end], LANES),
        row(rwkv_w0[layer], RWKV_WIDTH), row(rwkv_a0[layer], RWKV_WIDTH),
        w_lora_hi, w_lora_lo,
        row(rwkv_k_k[layer], RWKV_WIDTH), row(rwkv_k_a[layer], RWKV_WIDTH),
        row(rwkv_r_k[layer], RWKV_WIDTH), row(rwkv_gn_w[layer], RWKV_WIDTH), row(rwkv_gn_b[layer], RWKV_WIDTH),
    )
    pad = RWKV_CHUNK - N_META
    _, h_meta = _rwkv(jnp.pad(p_meta, ((pad, 0), (0, 0))), jnp.pad(lora_meta, ((pad, 0), (0, 0))),
                      jnp.zeros((8, rkv_end), F32), jnp.zeros((8, LANES), F32),
                      jnp.zeros((1, N_PAIRS, LANES, LANES), F32), rwkv_params, batch=1)
    y_r, _ = _rwkv(p_x, lora_x, p_meta[N_META - 8:, :rkv_end].astype(F32), lora_meta[N_META - 8:],
                   h_meta, rwkv_params, batch=batch)

    lam_vecs = tuple(t[layer].reshape(1, HEAD) for t in (diff_lam_q1, diff_lam_k1, diff_lam_q2, diff_lam_k2))
    vm_t = p_meta[:, COL_V:COL_V + DIFF_WIDTH].reshape(N_META, DIFF_HEADS, LANES).transpose(1, 2, 0)
    y_d = _attention(p_x, p_meta, vm_t, lam_vecs, diff_subln_w[layer].reshape(1, LANES), batch=batch, seq=seq)

    wo = w_out[layer].astype(BF16)
    out = _outproj(y_r, y_d, wo[:RWKV_WIDTH], wo[RWKV_WIDTH:], x2, post_norm_w[layer].reshape(1, D_MODEL))
    return out.reshape(batch, seq, D_MODEL)
```

```python
import functools
import math

import jax
import jax.numpy as jnp
from jax import lax
from jax.experimental import pallas as pl
from jax.experimental.pallas import tpu as pltpu

F32 = jnp.float32
BF16 = jnp.bfloat16

D_MODEL = 2048
N_META = 16
HEAD = 64
LANES = 128
RWKV_WIDTH = 1024
N_PAIRS = RWKV_WIDTH // LANES
DIFF_WIDTH = 1024
DIFF_HEADS = DIFF_WIDTH // LANES
LORA = 64
ROPE_THETA = 10000.0
NORM_EPS = 1e-6
GN_EPS = 64e-5
SUBLN_EPS = 1e-5
LAMBDA_INIT = 0.8 - 0.6 * math.exp(-0.3 * 0)

COL_RKV = 0
COL_GR = 3 * RWKV_WIDTH
COL_Q = 4 * RWKV_WIDTH
COL_K = COL_Q + DIFF_WIDTH
COL_V = COL_K + DIFF_WIDTH
COL_GD = COL_V + DIFF_WIDTH
P_COLS = COL_GD + DIFF_WIDTH

PROJ_TM = 512
PROJ_TN = 1024
RWKV_CHUNK = 64
ATT_TQ = 512
ATT_TK = 512
OUT_TM = 256
VMEM_LIMIT = 48 * 1024 * 1024
MASK_VALUE = -1e30
ONES_ROWS = 16
Q_SCALE = HEAD ** -0.5 * math.log2(math.e)


def _dot(a, b):
    return jnp.dot(a, b, preferred_element_type=F32)


def _dot_nt(a, b):
    return lax.dot_general(a, b, (((1,), (1,)), ((), ())), preferred_element_type=F32)


def _dot_tn(a, b):
    return lax.dot_general(a, b, (((0,), (0,)), ((), ())), preferred_element_type=F32)


def _split2(x):
    hi = x.astype(BF16)
    lo = (x - hi.astype(F32)).astype(BF16)
    return hi, lo


def _sigmoid(x):
    return 1.0 / (1.0 + jnp.exp(-x))


def _silu(x):
    return x * _sigmoid(x)


def _inproj_kernel(x_ref, g_ref, w_ref, wl_ref, cos_ref, sin_ref, o_ref, ol_ref, hn_ref, *, tn):
    j = pl.program_id(1)

    @pl.when(j == 0)
    def _():
        x = x_ref[...]
        ms = jnp.mean(x * x, axis=-1, keepdims=True)
        hn = (x * lax.rsqrt(ms + NORM_EPS) * g_ref[...]).astype(BF16)
        hn_ref[...] = hn
        ol_ref[...] = _dot(hn, wl_ref[...])

    acc = _dot(hn_ref[...], w_ref[...])
    q_tile = COL_Q // tn
    k_tile = COL_K // tn
    is_rope = jnp.logical_or(j == q_tile, j == k_tile)

    @pl.when(is_rope)
    def _():
        cos = cos_ref[...]
        sin = sin_ref[...]
        lane = lax.broadcasted_iota(jnp.int32, cos.shape, 1)
        first_half = (lane % HEAD) < (HEAD // 2)
        scale = jnp.where(j == q_tile, Q_SCALE, 1.0).astype(F32)
        for c in range(tn // LANES):
            blk = acc[:, c * LANES:(c + 1) * LANES]
            swapped = jnp.where(first_half, pltpu.roll(blk, LANES - HEAD // 2, 1),
                                pltpu.roll(blk, HEAD // 2, 1))
            o_ref[:, c * LANES:(c + 1) * LANES] = ((blk * cos + swapped * sin) * scale).astype(o_ref.dtype)

    @pl.when(jnp.logical_not(is_rope))
    def _():
        o_ref[...] = acc.astype(o_ref.dtype)


def _inproj(x2, g, w_main, w_lora, cos, sin, *, tm):
    m = x2.shape[0]
    tn = PROJ_TN
    n_pos_tiles = cos.shape[0] // tm
    return pl.pallas_call(
        functools.partial(_inproj_kernel, tn=tn),
        out_shape=(jax.ShapeDtypeStruct((m, P_COLS), BF16), jax.ShapeDtypeStruct((m, LANES), F32)),
        grid=(m // tm, P_COLS // tn),
        in_specs=[
            pl.BlockSpec((tm, D_MODEL), lambda i, j: (i, 0)),
            pl.BlockSpec((1, D_MODEL), lambda i, j: (0, 0)),
            pl.BlockSpec((D_MODEL, tn), lambda i, j: (0, j)),
            pl.BlockSpec((D_MODEL, LANES), lambda i, j: (0, 0)),
            pl.BlockSpec((tm, LANES), lambda i, j: (i % n_pos_tiles, 0)),
            pl.BlockSpec((tm, LANES), lambda i, j: (i % n_pos_tiles, 0)),
        ],
        out_specs=(
            pl.BlockSpec((tm, tn), lambda i, j: (i, j)),
            pl.BlockSpec((tm, LANES), lambda i, j: (i, 0)),
        ),
        scratch_shapes=[pltpu.VMEM((tm, D_MODEL), BF16)],
        compiler_params=pltpu.CompilerParams(
            dimension_semantics=("arbitrary", "arbitrary"), vmem_limit_bytes=VMEM_LIMIT),
        name="inproj",
    )(x2, g, w_main, w_lora, cos, sin)


def _rwkv_kernel(rkv_ref, g_ref, lora_ref, pinit_rkv_ref, pinit_lora_ref, hinit_ref,
                 mu_rkv_ref, mu_lora_ref, w0_ref, a0_ref, wl_hi_ref, wl_lo_ref,
                 kk_ref, ka_ref, rk_ref, gnw_ref, gnb_ref,
                 y_ref, hout_ref, h_scr, prev_rkv, prev_lora, *, n_chunks):
    C = RWKV_CHUNK
    c = pl.program_id(1)

    @pl.when(c == 0)
    def _():
        h_scr[...] = hinit_ref[0]
        prev_rkv[...] = pinit_rkv_ref[...]
        prev_lora[...] = pinit_lora_ref[...]

    row = lax.broadcasted_iota(jnp.int32, (C, 1), 0)

    def token_shift(x, prev8, mu):
        xp = jnp.where(row == 0, prev8[7:8, :], pltpu.roll(x, 1, 0))
        return x + (xp - x) * mu

    x = rkv_ref[...].astype(F32)
    xl = lora_ref[...]
    u = token_shift(x, prev_rkv[...], mu_rkv_ref[...])
    ul = token_shift(xl, prev_lora[...], mu_lora_ref[...])
    prev_rkv[...] = x[C - 8:C]
    prev_lora[...] = xl[C - 8:C]

    r = u[:, 0:RWKV_WIDTH]
    k = u[:, RWKV_WIDTH:2 * RWKV_WIDTH]
    v = u[:, 2 * RWKV_WIDTH:3 * RWKV_WIDTH]

    lane = lax.broadcasted_iota(jnp.int32, (C, LANES), 1)
    head0 = lane < HEAD
    tl_hi, tl_lo = _split2(jnp.where(head0, jnp.tanh(ul), ul))
    wl_hi = wl_hi_ref[...]
    lo_out = _dot(tl_hi, wl_hi) + _dot(tl_lo, wl_hi) + _dot(tl_hi, wl_lo_ref[...])
    z = w0_ref[...] + lo_out[:, 0:RWKV_WIDTH]
    logdec = -math.exp(-0.5) * _sigmoid(z)
    a_lr = _sigmoid(a0_ref[...] + lo_out[:, RWKV_WIDTH:2 * RWKV_WIDTH])

    ti = lax.broadcasted_iota(jnp.int32, (C, C), 0)
    si = lax.broadcasted_iota(jnp.int32, (C, C), 1)
    tri = jnp.where(ti >= si, 1.0, 0.0).astype(BF16)
    ld_hi = logdec.astype(BF16)
    rem = logdec - ld_hi.astype(F32)
    ld_mid = rem.astype(BF16)
    ld_lo = (rem - ld_mid.astype(F32)).astype(BF16)
    cum = _dot(tri, ld_hi) + _dot(tri, ld_mid) + _dot(tri, ld_lo)
    cum_last = cum[C - 1:C, :]
    e_excl = jnp.exp(cum - logdec)
    e_incl = jnp.exp(cum)
    e_neg = jnp.exp(-cum)
    e_hat = jnp.exp(cum_last - cum)
    gamma = jnp.exp(cum_last)

    li = lax.broadcasted_iota(jnp.int32, (2 * LANES, LANES), 0)
    lj = lax.broadcasted_iota(jnp.int32, (2 * LANES, LANES), 1)
    ones_bd2 = jnp.where(((li % LANES) // HEAD) == (lj // HEAD), 1.0, 0.0).astype(BF16)

    def head_sum(t):
        rows = jnp.concatenate([t[:, p * LANES:(p + 1) * LANES] for p in range(N_PAIRS)], axis=0)
        hi, lo = _split2(rows)
        s = _dot(jnp.concatenate([hi, lo], axis=1), ones_bd2)
        return jnp.concatenate([s[p * C:(p + 1) * C] for p in range(N_PAIRS)], axis=1)

    kk = k * kk_ref[...]
    kk = kk * lax.rsqrt(jnp.maximum(head_sum(kk * kk), 1e-24))
    k2 = k * (1.0 + (a_lr - 1.0) * ka_ref[...])
    b = kk * a_lr
    bonus = head_sum(r * k2 * rk_ref[...]) * v
    xa_all = -kk * e_excl
    xr_all = r * e_incl
    yb_all = b * e_neg
    yk_all = k2 * e_neg
    kh_all = k2 * e_hat
    bh_all = b * e_hat

    S = 2 * C
    assert S == LANES

    def stack(t):
        return jnp.concatenate([jnp.where(head0, t, 0.0), jnp.where(head0, 0.0, t)], axis=0).astype(BF16)

    def pair(t, p):
        return t[:, p * LANES:(p + 1) * LANES]

    ri = lax.broadcasted_iota(jnp.int32, (S, 2 * S), 0)
    ci = lax.broadcasted_iota(jnp.int32, (S, 2 * S), 1)
    a_mask = (ri % C) >= (ci % C) + jnp.where(ri < C, 1, 0)
    ei = lax.broadcasted_iota(jnp.int32, (C, S), 0)
    ej = lax.broadcasted_iota(jnp.int32, (C, S), 1)
    eye = jnp.where(ei == ej % C, 1.0, 0.0).astype(F32)
    bi = lax.broadcasted_iota(jnp.int32, (LANES, LANES), 0)
    bj = lax.broadcasted_iota(jnp.int32, (LANES, LANES), 1)
    same_head = (bi // HEAD) == (bj // HEAD)

    pairs = range(N_PAIRS)
    lhs_ar = [jnp.concatenate([pair(xa_all, p), pair(xr_all, p)], axis=0).astype(BF16) for p in pairs]
    rhs_bk = [jnp.concatenate([stack(pair(yb_all, p)), stack(pair(yk_all, p))], axis=0) for p in pairs]
    vs = [stack(pair(v, p)) for p in pairs]

    a_all = [jnp.where(a_mask, _dot_nt(lhs_ar[p], rhs_bk[p]), 0.0) for p in pairs]
    a_ab = [a_all[p][0:C, 0:S] for p in pairs]
    a_x_v = [_dot(a_all[p][:, S:2 * S].astype(BF16), vs[p]) for p in pairs]

    inv = [eye + a_ab[p] for p in pairs]
    apow = [_dot(a_ab[p].astype(BF16), stack(a_ab[p])) for p in pairs]
    n = 2
    while 2 * n < C:
        for p in pairs:
            res = _dot(jnp.concatenate([inv[p], apow[p]], axis=0).astype(BF16), stack(apow[p]))
            inv[p] = inv[p] + res[0:C]
            apow[p] = res[C:S]
        n *= 2
    for p in pairs:
        inv[p] = inv[p] + _dot(inv[p].astype(BF16), stack(apow[p]))

    s_old = [h_scr[p] for p in pairs]
    x_s = [_dot_nt(lhs_ar[p], s_old[p].astype(BF16)) for p in pairs]
    us = [_dot(inv[p].astype(BF16), stack(x_s[p][0:C] + a_x_v[p][0:C])) for p in pairs]
    ys = [x_s[p][C:S] + a_x_v[p][C:S] + _dot(a_all[p][C:S, 0:S].astype(BF16), stack(us[p])) for p in pairs]
    for p in pairs:
        vu = jnp.concatenate([pair(v, p), us[p]], axis=0).astype(BF16)
        khb = jnp.concatenate([pair(kh_all, p), pair(bh_all, p)], axis=0).astype(BF16)
        h_scr[p] = s_old[p] * pair(gamma, p) + jnp.where(same_head, _dot_tn(vu, khb), 0.0)

    y = jnp.concatenate(ys, axis=1)
    yc = y - head_sum(y) * (1.0 / HEAD)
    var = head_sum(yc * yc) * (1.0 / HEAD)
    y = yc * lax.rsqrt(var + GN_EPS) * gnw_ref[...] + gnb_ref[...] + bonus
    g = g_ref[...].astype(F32)
    y_ref[...] = (y * _silu(g)).astype(y_ref.dtype)

    @pl.when(c == n_chunks - 1)
    def _():
        hout_ref[0] = h_scr[...]


def _rwkv(p_arr, lora_arr, pinit_rkv, pinit_lora, hinit, params, *, batch):
    C = RWKV_CHUNK
    rows = p_arr.shape[0]
    n_chunks = rows // (batch * C)
    const2 = lambda b, c: (0, 0)
    vec = lambda width: pl.BlockSpec((1, width), const2)
    return pl.pallas_call(
        functools.partial(_rwkv_kernel, n_chunks=n_chunks),
        out_shape=(jax.ShapeDtypeStruct((rows, RWKV_WIDTH), BF16),
                   jax.ShapeDtypeStruct((batch, N_PAIRS, LANES, LANES), F32)),
        grid=(batch, n_chunks),
        in_specs=[
            pl.BlockSpec((C, 3 * RWKV_WIDTH), lambda b, c: (b * n_chunks + c, COL_RKV // (3 * RWKV_WIDTH))),
            pl.BlockSpec((C, RWKV_WIDTH), lambda b, c: (b * n_chunks + c, COL_GR // RWKV_WIDTH)),
            pl.BlockSpec((C, LANES), lambda b, c: (b * n_chunks + c, 0)),
            pl.BlockSpec((8, 3 * RWKV_WIDTH), const2),
            pl.BlockSpec((8, LANES), const2),
            pl.BlockSpec((1, N_PAIRS, LANES, LANES), lambda b, c: (0, 0, 0, 0)),
            vec(3 * RWKV_WIDTH), vec(LANES), vec(RWKV_WIDTH), vec(RWKV_WIDTH),
            pl.BlockSpec((LANES, 2 * RWKV_WIDTH), const2),
            pl.BlockSpec((LANES, 2 * RWKV_WIDTH), const2),
            vec(RWKV_WIDTH), vec(RWKV_WIDTH), vec(RWKV_WIDTH), vec(RWKV_WIDTH), vec(RWKV_WIDTH),
        ],
        out_specs=(
            pl.BlockSpec((C, RWKV_WIDTH), lambda b, c: (b * n_chunks + c, 0)),
            pl.BlockSpec((1, N_PAIRS, LANES, LANES), lambda b, c: (b, 0, 0, 0)),
        ),
        scratch_shapes=[
            pltpu.VMEM((N_PAIRS, LANES, LANES), F32),
            pltpu.VMEM((8, 3 * RWKV_WIDTH), F32),
            pltpu.VMEM((8, LANES), F32),
        ],
        compiler_params=pltpu.CompilerParams(
            dimension_semantics=("arbitrary", "arbitrary"), vmem_limit_bytes=VMEM_LIMIT),
        name="rwkv7_chunk",
    )(p_arr, p_arr, lora_arr, pinit_rkv, pinit_lora, hinit, *params)


def _attn_kernel(q_ref, k_ref, v_ref, g_ref, km_ref, vmt_ref, lq1_ref, lk1_ref, lq2_ref, lk2_ref, sw_ref,
                 o_ref, vt_scr, m_scr, acc_scr, s_scr, *, tq, tk, n_kv):
    i = pl.program_id(2)

    @pl.when(i == 0)
    def _():
        for c in range(n_kv):
            vt_scr[c, 0:LANES, :] = v_ref[c * tk:(c + 1) * tk, :].astype(F32).T.astype(BF16)
            vt_scr[c, LANES:LANES + ONES_ROWS, :] = jnp.ones((ONES_ROWS, tk), BF16)

    qt = q_ref[...].astype(F32).T
    comp0 = lax.broadcasted_iota(jnp.int32, (LANES, tq), 0) < HEAD
    qst = jnp.concatenate([jnp.where(comp0, qt, 0.0), jnp.where(comp0, 0.0, qt)], axis=1).astype(BF16)

    s = _dot(km_ref[...], qst)
    m = jnp.max(s, axis=0, keepdims=True)
    m_scr[...] = m
    acc_scr[...] = _dot(vmt_ref[0], jnp.exp2(s - m).astype(BF16))

    def scores(j):
        start = pl.multiple_of(j * tk, tk)
        return _dot(k_ref[pl.ds(start, tk), :], qst)

    def softmax_pv(j, slot, masked):
        s = s_scr[slot]
        if masked:
            kpos = lax.broadcasted_iota(jnp.int32, (tk, 2 * tq), 0) + j * tk
            qpos = lax.broadcasted_iota(jnp.int32, (tk, 2 * tq), 1) % tq + i * tq
            s = jnp.where(kpos <= qpos, s, MASK_VALUE)
        m_old = m_scr[...]
        m_new = jnp.maximum(m_old, jnp.max(s, axis=0, keepdims=True))
        alpha = jnp.exp2(m_old - m_new)
        m_scr[...] = m_new
        acc_scr[...] = alpha * acc_scr[...] + _dot(vt_scr[j], jnp.exp2(s - m_new).astype(BF16))

    s_scr[0] = scores(0)

    def body(jj, carry):
        j = 2 * jj
        s_scr[1] = scores(j + 1)
        softmax_pv(j, 0, False)
        s_scr[0] = scores(j + 2)
        softmax_pv(j + 1, 1, False)
        return carry

    lax.fori_loop(0, lax.shift_right_logical(i, 1), body, 0)
    odd = jnp.bitwise_and(i, 1) == 1

    @pl.when(odd)
    def _():
        s_scr[1] = scores(i)
        softmax_pv(i - 1, 0, False)
        softmax_pv(i, 1, True)

    @pl.when(jnp.logical_not(odd))
    def _():
        softmax_pv(i, 0, True)

    lam = (jnp.exp(jnp.sum(lq1_ref[...] * lk1_ref[...], axis=-1, keepdims=True))
           - jnp.exp(jnp.sum(lq2_ref[...] * lk2_ref[...], axis=-1, keepdims=True)) + LAMBDA_INIT)
    on = acc_scr[0:LANES, :] / acc_scr[LANES:LANES + 1, :]
    ot = on[:, 0:tq] - lam * on[:, tq:2 * tq]
    ot = ot * lax.rsqrt(jnp.mean(ot * ot, axis=0, keepdims=True) + SUBLN_EPS)
    g = g_ref[...].astype(F32)
    o_ref[...] = (ot.T * sw_ref[...] * (1.0 - LAMBDA_INIT) * _silu(g)).astype(o_ref.dtype)


def _attention(p_arr, p_meta, vm_t, lam_vecs, subln_w, *, batch, seq):
    tq, tk = ATT_TQ, ATT_TK
    nq = seq // tq
    n_kv = seq // tk
    lanes_blk = lambda col: col // LANES
    small = pl.BlockSpec((1, HEAD), lambda b, h, i: (0, 0))
    return pl.pallas_call(
        functools.partial(_attn_kernel, tq=tq, tk=tk, n_kv=n_kv),
        out_shape=jax.ShapeDtypeStruct((batch * seq, DIFF_WIDTH), BF16),
        grid=(batch, DIFF_HEADS, nq),
        in_specs=[
            pl.BlockSpec((tq, LANES), lambda b, h, i: (b * nq + i, lanes_blk(COL_Q) + h)),
            pl.BlockSpec((seq, LANES), lambda b, h, i: (b, lanes_blk(COL_K) + h)),
            pl.BlockSpec((seq, LANES), lambda b, h, i: (b, lanes_blk(COL_V) + h)),
            pl.BlockSpec((tq, LANES), lambda b, h, i: (b * nq + i, lanes_blk(COL_GD) + h)),
            pl.BlockSpec((N_META, LANES), lambda b, h, i: (0, lanes_blk(COL_K) + h)),
            pl.BlockSpec((1, LANES + ONES_ROWS, N_META), lambda b, h, i: (h, 0, 0)),
            small, small, small, small,
            pl.BlockSpec((1, LANES), lambda b, h, i: (0, 0)),
        ],
        out_specs=pl.BlockSpec((tq, LANES), lambda b, h, i: (b * nq + i, h)),
        scratch_shapes=[
            pltpu.VMEM((n_kv, LANES + ONES_ROWS, tk), BF16),
            pltpu.VMEM((1, 2 * tq), F32),
            pltpu.VMEM((LANES + ONES_ROWS, 2 * tq), F32),
            pltpu.VMEM((2, tk, 2 * tq), F32),
        ],
        compiler_params=pltpu.CompilerParams(
            dimension_semantics=("arbitrary", "arbitrary", "arbitrary"), vmem_limit_bytes=VMEM_LIMIT),
        name="diff_attn",
    )(p_arr, p_arr, p_arr, p_arr, p_meta, vm_t, *lam_vecs, subln_w)


def _outproj_kernel(yr_ref, yd_ref, w1_ref, w2_ref, x_ref, g_ref, o_ref):
    y = _dot(yr_ref[...], w1_ref[...]) + _dot(yd_ref[...], w2_ref[...])
    ms = jnp.mean(y * y, axis=-1, keepdims=True)
    o_ref[...] = x_ref[...] + y * lax.rsqrt(ms + NORM_EPS) * g_ref[...]


def _outproj(y_r, y_d, w1, w2, x2, g):
    m = x2.shape[0]
    tm = OUT_TM
    return pl.pallas_call(
        _outproj_kernel,
        out_shape=jax.ShapeDtypeStruct((m, D_MODEL), F32),
        grid=(m // tm,),
        in_specs=[
            pl.BlockSpec((tm, RWKV_WIDTH), lambda i: (i, 0)),
            pl.BlockSpec((tm, DIFF_WIDTH), lambda i: (i, 0)),
            pl.BlockSpec((RWKV_WIDTH, D_MODEL), lambda i: (0, 0)),
            pl.BlockSpec((DIFF_WIDTH, D_MODEL), lambda i: (0, 0)),
            pl.BlockSpec((tm, D_MODEL), lambda i: (i, 0)),
            pl.BlockSpec((1, D_MODEL), lambda i: (0, 0)),
        ],
        out_specs=pl.BlockSpec((tm, D_MODEL), lambda i: (i, 0)),
        compiler_params=pltpu.CompilerParams(
            dimension_semantics=("arbitrary",), vmem_limit_bytes=VMEM_LIMIT),
        name="outproj",
    )(y_r, y_d, w1, w2, x2, g)


def _rope_tables(first_pos, n_pos):
    pos = jnp.arange(first_pos, first_pos + n_pos, dtype=F32)
    inv_freq = ROPE_THETA ** (-jnp.arange(0, HEAD, 2, dtype=F32) / HEAD)
    ang = pos[:, None] * inv_freq[None, :]
    cos = jnp.cos(ang)
    sin = jnp.sin(ang)
    cos = jnp.concatenate([cos, cos, cos, cos], axis=-1)
    sin = jnp.concatenate([-sin, sin, -sin, sin], axis=-1)
    return cos, sin


def kernel(x, meta_tokens, pre_norm_w, w_in, rwkv_mu, rwkv_w0, rwkv_w_up, rwkv_a0, rwkv_a_up, rwkv_k_k, rwkv_k_a, rwkv_r_k, rwkv_gn_w, rwkv_gn_b, diff_lam_q1, diff_lam_k1, diff_lam_q2, diff_lam_k2, diff_subln_w, w_out, post_norm_w):
    batch, seq, d = x.shape
    assert d == D_MODEL and meta_tokens.shape == (N_META, D_MODEL)
    assert seq % RWKV_CHUNK == 0 and seq % ATT_TQ == 0 and ATT_TQ == ATT_TK
    assert (batch * seq) % PROJ_TM == 0 and seq % PROJ_TM == 0
    layer = 0
    x2 = x.reshape(batch * seq, D_MODEL)

    w = w_in[layer]
    rkv_end = 3 * RWKV_WIDTH
    lora_end = rkv_end + 2 * LORA
    w_main = jnp.concatenate([w[:, :rkv_end], w[:, lora_end:]], axis=1).astype(BF16)
    w_lora_in = w[:, rkv_end:lora_end].astype(BF16)
    g_pre = pre_norm_w[layer].reshape(1, D_MODEL)

    cos_m, sin_m = _rope_tables(0, N_META)
    cos_x, sin_x = _rope_tables(N_META, seq)
    p_meta, lora_meta = _inproj(meta_tokens.astype(x.dtype), g_pre, w_main, w_lora_in, cos_m, sin_m, tm=N_META)
    p_x, lora_x = _inproj(x2, g_pre, w_main, w_lora_in, cos_x, sin_x, tm=PROJ_TM)

    mu = rwkv_mu[layer]
    zeros = jnp.zeros((LORA, RWKV_WIDTH), F32)
    w_lora = jnp.concatenate([
        jnp.concatenate([rwkv_w_up[layer], zeros], axis=1),
        jnp.concatenate([zeros, rwkv_a_up[layer]], axis=1)], axis=0)
    w_lora_hi, w_lora_lo = _split2(w_lora)
    row = lambda t, n: t.reshape(1, n)
    rwkv_params = (
        row(mu[:rkv_end], rkv_end), row(mu[rkv_end:lora_end], LANES),
        row(rwkv_w0[layer], RWKV_WIDTH), row(rwkv_a0[layer], RWKV_WIDTH),
        w_lora_hi, w_lora_lo,
        row(rwkv_k_k[layer], RWKV_WIDTH), row(rwkv_k_a[layer], RWKV_WIDTH),
        row(rwkv_r_k[layer], RWKV_WIDTH), row(rwkv_gn_w[layer], RWKV_WIDTH), row(rwkv_gn_b[layer], RWKV_WIDTH),
    )
    pad = RWKV_CHUNK - N_META
    _, h_meta = _rwkv(jnp.pad(p_meta, ((pad, 0), (0, 0))), jnp.pad(lora_meta, ((pad, 0), (0, 0))),
                      jnp.zeros((8, rkv_end), F32), jnp.zeros((8, LANES), F32),
                      jnp.zeros((1, N_PAIRS, LANES, LANES), F32), rwkv_params, batch=1)
    y_r, _ = _rwkv(p_x, lora_x, p_meta[N_META - 8:, :rkv_end].astype(F32), lora_meta[N_META - 8:],
                   h_meta, rwkv_params, batch=batch)

    lam_vecs = tuple(t[layer].reshape(1, HEAD) for t in (diff_lam_q1, diff_lam_k1, diff_lam_q2, diff_lam_k2))
    vm_t = p_meta[:, COL_V:COL_V + DIFF_WIDTH].reshape(N_META, DIFF_HEADS, LANES).transpose(1, 2, 0)
    vm_t = jnp.concatenate([vm_t, jnp.ones((DIFF_HEADS, ONES_ROWS, N_META), BF16)], axis=1)
    y_d = _attention(p_x, p_meta, vm_t, lam_vecs, diff_subln_w[layer].reshape(1, LANES), batch=batch, seq=seq)

    wo = w_out[layer].astype(BF16)
    out = _outproj(y_r, y_d, wo[:RWKV_WIDTH], wo[RWKV_WIDTH:], x2, post_norm_w[layer].reshape(1, D_MODEL))
    return out.reshape(batch, seq, D_MODEL)
```

```python
import functools
import math

import jax
import jax.numpy as jnp
from jax import lax
from jax.experimental import pallas as pl
from jax.experimental.pallas import tpu as pltpu

F32 = jnp.float32
BF16 = jnp.bfloat16

D_MODEL = 2048
N_META = 16
HEAD = 64
LANES = 128
RWKV_WIDTH = 1024
N_PAIRS = RWKV_WIDTH // LANES
DIFF_WIDTH = 1024
DIFF_HEADS = DIFF_WIDTH // LANES
LORA = 64
ROPE_THETA = 10000.0
NORM_EPS = 1e-6
GN_EPS = 64e-5
SUBLN_EPS = 1e-5
LAMBDA_INIT = 0.8 - 0.6 * math.exp(-0.3 * 0)

COL_RKV = 0
COL_GR = 3 * RWKV_WIDTH
COL_Q = 4 * RWKV_WIDTH
COL_K = COL_Q + DIFF_WIDTH
COL_V = COL_K + DIFF_WIDTH
COL_GD = COL_V + DIFF_WIDTH
P_COLS = COL_GD + DIFF_WIDTH

PROJ_TM = 512
PROJ_TN = 1024
RWKV_CHUNK = 64
ATT_BLOCK = 512
ATT_LOOKAHEAD = 1
OUT_TM = 256
VMEM_LIMIT = 48 * 1024 * 1024
MASK_VALUE = -1e30
ONES_ROWS = 16
Q_SCALE = HEAD ** -0.5 * math.log2(math.e)


def _dot(a, b):
    return jnp.dot(a, b, preferred_element_type=F32)


def _dot_nt(a, b):
    return lax.dot_general(a, b, (((1,), (1,)), ((), ())), preferred_element_type=F32)


def _dot_tn(a, b):
    return lax.dot_general(a, b, (((0,), (0,)), ((), ())), preferred_element_type=F32)


def _split2(x):
    hi = x.astype(BF16)
    lo = (x - hi.astype(F32)).astype(BF16)
    return hi, lo


def _sigmoid(x):
    return 1.0 / (1.0 + jnp.exp(-x))


def _silu(x):
    return x * _sigmoid(x)


def _inproj_kernel(x_ref, g_ref, w_ref, wl_ref, cos_ref, sin_ref, o_ref, ol_ref, hn_ref, *, tn):
    j = pl.program_id(1)

    @pl.when(j == 0)
    def _():
        x = x_ref[...]
        ms = jnp.mean(x * x, axis=-1, keepdims=True)
        hn = (x * lax.rsqrt(ms + NORM_EPS) * g_ref[...]).astype(BF16)
        hn_ref[...] = hn
        ol_ref[...] = _dot(hn, wl_ref[...])

    acc = _dot(hn_ref[...], w_ref[...])
    q_tile = COL_Q // tn
    k_tile = COL_K // tn
    is_rope = jnp.logical_or(j == q_tile, j == k_tile)

    @pl.when(is_rope)
    def _():
        cos = cos_ref[...]
        sin = sin_ref[...]
        lane = lax.broadcasted_iota(jnp.int32, cos.shape, 1)
        first_half = (lane % HEAD) < (HEAD // 2)
        scale = jnp.where(j == q_tile, Q_SCALE, 1.0).astype(F32)
        for c in range(tn // LANES):
            blk = acc[:, c * LANES:(c + 1) * LANES]
            swapped = jnp.where(first_half, pltpu.roll(blk, LANES - HEAD // 2, 1),
                                pltpu.roll(blk, HEAD // 2, 1))
            o_ref[:, c * LANES:(c + 1) * LANES] = ((blk * cos + swapped * sin) * scale).astype(o_ref.dtype)

    @pl.when(jnp.logical_not(is_rope))
    def _():
        o_ref[...] = acc.astype(o_ref.dtype)


def _inproj(x2, g, w_main, w_lora, cos, sin, *, tm):
    m = x2.shape[0]
    tn = PROJ_TN
    n_pos_tiles = cos.shape[0] // tm
    return pl.pallas_call(
        functools.partial(_inproj_kernel, tn=tn),
        out_shape=(jax.ShapeDtypeStruct((m, P_COLS), BF16), jax.ShapeDtypeStruct((m, LANES), F32)),
        grid=(m // tm, P_COLS // tn),
        in_specs=[
            pl.BlockSpec((tm, D_MODEL), lambda i, j: (i, 0)),
            pl.BlockSpec((1, D_MODEL), lambda i, j: (0, 0)),
            pl.BlockSpec((D_MODEL, tn), lambda i, j: (0, j)),
            pl.BlockSpec((D_MODEL, LANES), lambda i, j: (0, 0)),
            pl.BlockSpec((tm, LANES), lambda i, j: (i % n_pos_tiles, 0)),
            pl.BlockSpec((tm, LANES), lambda i, j: (i % n_pos_tiles, 0)),
        ],
        out_specs=(
            pl.BlockSpec((tm, tn), lambda i, j: (i, j)),
            pl.BlockSpec((tm, LANES), lambda i, j: (i, 0)),
        ),
        scratch_shapes=[pltpu.VMEM((tm, D_MODEL), BF16)],
        compiler_params=pltpu.CompilerParams(
            dimension_semantics=("arbitrary", "arbitrary"), vmem_limit_bytes=VMEM_LIMIT),
        name="inproj",
    )(x2, g, w_main, w_lora, cos, sin)


def _rwkv_kernel(rkv_ref, g_ref, lora_ref, pinit_rkv_ref, pinit_lora_ref, hinit_ref,
                 mu_rkv_ref, mu_lora_ref, w0_ref, a0_ref, wl_hi_ref, wl_lo_ref,
                 kk_ref, ka_ref, rk_ref, gnw_ref, gnb_ref,
                 y_ref, hout_ref, h_scr, prev_rkv, prev_lora, *, n_chunks):
    C = RWKV_CHUNK
    c = pl.program_id(1)

    @pl.when(c == 0)
    def _():
        h_scr[...] = hinit_ref[0]
        prev_rkv[...] = pinit_rkv_ref[...]
        prev_lora[...] = pinit_lora_ref[...]

    row = lax.broadcasted_iota(jnp.int32, (C, 1), 0)

    def token_shift(x, prev8, mu):
        xp = jnp.where(row == 0, prev8[7:8, :], pltpu.roll(x, 1, 0))
        return x + (xp - x) * mu

    x = rkv_ref[...].astype(F32)
    xl = lora_ref[...]
    u = token_shift(x, prev_rkv[...], mu_rkv_ref[...])
    ul = token_shift(xl, prev_lora[...], mu_lora_ref[...])
    prev_rkv[...] = x[C - 8:C]
    prev_lora[...] = xl[C - 8:C]

    r = u[:, 0:RWKV_WIDTH]
    k = u[:, RWKV_WIDTH:2 * RWKV_WIDTH]
    v = u[:, 2 * RWKV_WIDTH:3 * RWKV_WIDTH]

    lane = lax.broadcasted_iota(jnp.int32, (C, LANES), 1)
    head0 = lane < HEAD
    tl_hi, tl_lo = _split2(jnp.where(head0, jnp.tanh(ul), ul))
    wl_hi = wl_hi_ref[...]
    lo_out = _dot(tl_hi, wl_hi) + _dot(tl_lo, wl_hi) + _dot(tl_hi, wl_lo_ref[...])
    z = w0_ref[...] + lo_out[:, 0:RWKV_WIDTH]
    logdec = -math.exp(-0.5) * _sigmoid(z)
    a_lr = _sigmoid(a0_ref[...] + lo_out[:, RWKV_WIDTH:2 * RWKV_WIDTH])

    ti = lax.broadcasted_iota(jnp.int32, (C, C), 0)
    si = lax.broadcasted_iota(jnp.int32, (C, C), 1)
    tri = jnp.where(ti >= si, 1.0, 0.0).astype(BF16)
    ld_hi = logdec.astype(BF16)
    rem = logdec - ld_hi.astype(F32)
    ld_mid = rem.astype(BF16)
    ld_lo = (rem - ld_mid.astype(F32)).astype(BF16)
    cum = _dot(tri, ld_hi) + _dot(tri, ld_mid) + _dot(tri, ld_lo)
    cum_last = cum[C - 1:C, :]
    e_excl = jnp.exp(cum - logdec)
    e_incl = jnp.exp(cum)
    e_neg = jnp.exp(-cum)
    e_hat = jnp.exp(cum_last - cum)
    gamma = jnp.exp(cum_last)

    li = lax.broadcasted_iota(jnp.int32, (2 * LANES, LANES), 0)
    lj = lax.broadcasted_iota(jnp.int32, (2 * LANES, LANES), 1)
    ones_bd2 = jnp.where(((li % LANES) // HEAD) == (lj // HEAD), 1.0, 0.0).astype(BF16)

    def head_sum(t):
        rows = jnp.concatenate([t[:, p * LANES:(p + 1) * LANES] for p in range(N_PAIRS)], axis=0)
        hi, lo = _split2(rows)
        s = _dot(jnp.concatenate([hi, lo], axis=1), ones_bd2)
        return jnp.concatenate([s[p * C:(p + 1) * C] for p in range(N_PAIRS)], axis=1)

    kk = k * kk_ref[...]
    kk = kk * lax.rsqrt(jnp.maximum(head_sum(kk * kk), 1e-24))
    k2 = k * (1.0 + (a_lr - 1.0) * ka_ref[...])
    b = kk * a_lr
    bonus = head_sum(r * k2 * rk_ref[...]) * v
    xa_all = -kk * e_excl
    xr_all = r * e_incl
    yb_all = b * e_neg
    yk_all = k2 * e_neg
    kh_all = k2 * e_hat
    bh_all = b * e_hat

    S = 2 * C
    assert S == LANES

    def stack(t):
        return jnp.concatenate([jnp.where(head0, t, 0.0), jnp.where(head0, 0.0, t)], axis=0).astype(BF16)

    def pair(t, p):
        return t[:, p * LANES:(p + 1) * LANES]

    ri = lax.broadcasted_iota(jnp.int32, (S, 2 * S), 0)
    ci = lax.broadcasted_iota(jnp.int32, (S, 2 * S), 1)
    a_mask = (ri % C) >= (ci % C) + jnp.where(ri < C, 1, 0)
    ei = lax.broadcasted_iota(jnp.int32, (C, S), 0)
    ej = lax.broadcasted_iota(jnp.int32, (C, S), 1)
    eye = jnp.where(ei == ej % C, 1.0, 0.0).astype(F32)
    bi = lax.broadcasted_iota(jnp.int32, (LANES, LANES), 0)
    bj = lax.broadcasted_iota(jnp.int32, (LANES, LANES), 1)
    same_head = (bi // HEAD) == (bj // HEAD)

    pairs = range(N_PAIRS)
    lhs_ar = [jnp.concatenate([pair(xa_all, p), pair(xr_all, p)], axis=0).astype(BF16) for p in pairs]
    rhs_bk = [jnp.concatenate([stack(pair(yb_all, p)), stack(pair(yk_all, p))], axis=0) for p in pairs]
    vs = [stack(pair(v, p)) for p in pairs]

    a_all = [jnp.where(a_mask, _dot_nt(lhs_ar[p], rhs_bk[p]), 0.0) for p in pairs]
    a_ab = [a_all[p][0:C, 0:S] for p in pairs]
    a_x_v = [_dot(a_all[p][:, S:2 * S].astype(BF16), vs[p]) for p in pairs]

    inv = [eye + a_ab[p] for p in pairs]
    apow = [_dot(a_ab[p].astype(BF16), stack(a_ab[p])) for p in pairs]
    n = 2
    while 2 * n < C:
        for p in pairs:
            res = _dot(jnp.concatenate([inv[p], apow[p]], axis=0).astype(BF16), stack(apow[p]))
            inv[p] = inv[p] + res[0:C]
            apow[p] = res[C:S]
        n *= 2
    for p in pairs:
        inv[p] = inv[p] + _dot(inv[p].astype(BF16), stack(apow[p]))

    s_old = [h_scr[p] for p in pairs]
    x_s = [_dot_nt(lhs_ar[p], s_old[p].astype(BF16)) for p in pairs]
    us = [_dot(inv[p].astype(BF16), stack(x_s[p][0:C] + a_x_v[p][0:C])) for p in pairs]
    ys = [x_s[p][C:S] + a_x_v[p][C:S] + _dot(a_all[p][C:S, 0:S].astype(BF16), stack(us[p])) for p in pairs]
    for p in pairs:
        vu = jnp.concatenate([pair(v, p), us[p]], axis=0).astype(BF16)
        khb = jnp.concatenate([pair(kh_all, p), pair(bh_all, p)], axis=0).astype(BF16)
        h_scr[p] = s_old[p] * pair(gamma, p) + jnp.where(same_head, _dot_tn(vu, khb), 0.0)

    y = jnp.concatenate(ys, axis=1)
    yc = y - head_sum(y) * (1.0 / HEAD)
    var = head_sum(yc * yc) * (1.0 / HEAD)
    y = yc * lax.rsqrt(var + GN_EPS) * gnw_ref[...] + gnb_ref[...] + bonus
    g = g_ref[...].astype(F32)
    y_ref[...] = (y * _silu(g)).astype(y_ref.dtype)

    @pl.when(c == n_chunks - 1)
    def _():
        hout_ref[0] = h_scr[...]


def _rwkv(p_arr, lora_arr, pinit_rkv, pinit_lora, hinit, params, *, batch):
    C = RWKV_CHUNK
    rows = p_arr.shape[0]
    n_chunks = rows // (batch * C)
    const2 = lambda b, c: (0, 0)
    vec = lambda width: pl.BlockSpec((1, width), const2)
    return pl.pallas_call(
        functools.partial(_rwkv_kernel, n_chunks=n_chunks),
        out_shape=(jax.ShapeDtypeStruct((rows, RWKV_WIDTH), BF16),
                   jax.ShapeDtypeStruct((batch, N_PAIRS, LANES, LANES), F32)),
        grid=(batch, n_chunks),
        in_specs=[
            pl.BlockSpec((C, 3 * RWKV_WIDTH), lambda b, c: (b * n_chunks + c, COL_RKV // (3 * RWKV_WIDTH))),
            pl.BlockSpec((C, RWKV_WIDTH), lambda b, c: (b * n_chunks + c, COL_GR // RWKV_WIDTH)),
            pl.BlockSpec((C, LANES), lambda b, c: (b * n_chunks + c, 0)),
            pl.BlockSpec((8, 3 * RWKV_WIDTH), const2),
            pl.BlockSpec((8, LANES), const2),
            pl.BlockSpec((1, N_PAIRS, LANES, LANES), lambda b, c: (0, 0, 0, 0)),
            vec(3 * RWKV_WIDTH), vec(LANES), vec(RWKV_WIDTH), vec(RWKV_WIDTH),
            pl.BlockSpec((LANES, 2 * RWKV_WIDTH), const2),
            pl.BlockSpec((LANES, 2 * RWKV_WIDTH), const2),
            vec(RWKV_WIDTH), vec(RWKV_WIDTH), vec(RWKV_WIDTH), vec(RWKV_WIDTH), vec(RWKV_WIDTH),
        ],
        out_specs=(
            pl.BlockSpec((C, RWKV_WIDTH), lambda b, c: (b * n_chunks + c, 0)),
            pl.BlockSpec((1, N_PAIRS, LANES, LANES), lambda b, c: (b, 0, 0, 0)),
        ),
        scratch_shapes=[
            pltpu.VMEM((N_PAIRS, LANES, LANES), F32),
            pltpu.VMEM((8, 3 * RWKV_WIDTH), F32),
            pltpu.VMEM((8, LANES), F32),
        ],
        compiler_params=pltpu.CompilerParams(
            dimension_semantics=("arbitrary", "arbitrary"), vmem_limit_bytes=VMEM_LIMIT),
        name="rwkv7_chunk",
    )(p_arr, p_arr, lora_arr, pinit_rkv, pinit_lora, hinit, *params)


def _attn_kernel(q_ref, k_ref, v_ref, g_ref, km_ref, vmt_ref, lq1_ref, lk1_ref, lq2_ref, lk2_ref, sw_ref,
                 o_ref, vt_scr, acc_scr, s_scr, *, t, n_blk):
    for c in range(n_blk):
        vt_scr[c, 0:LANES, :] = v_ref[c * t:(c + 1) * t, :].astype(F32).T.astype(BF16)
        vt_scr[c, LANES:LANES + ONES_ROWS, :] = jnp.ones((ONES_ROWS, t), BF16)

    comp0 = lax.broadcasted_iota(jnp.int32, (LANES, t), 0) < HEAD
    qst = []
    for i in range(n_blk):
        qt = q_ref[i * t:(i + 1) * t, :].astype(F32).T
        qst.append(jnp.concatenate([jnp.where(comp0, qt, 0.0), jnp.where(comp0, 0.0, qt)], axis=1).astype(BF16))

    m = []
    for i in range(n_blk):
        s = _dot(km_ref[...], qst[i])
        m.append(jnp.max(s, axis=0, keepdims=True))
        acc_scr[i] = _dot(vmt_ref[0], jnp.exp2(s - m[i]).astype(BF16))

    order = sorted(((i, j) for i in range(n_blk) for j in range(i + 1)), key=lambda ij: (ij[1], ij[0]))
    n_slots = s_scr.shape[0]

    def issue(n):
        i, j = order[n]
        s_scr[n % n_slots] = _dot(k_ref[j * t:(j + 1) * t, :], qst[i])

    causal = (lax.broadcasted_iota(jnp.int32, (t, 2 * t), 0)
              <= lax.broadcasted_iota(jnp.int32, (t, 2 * t), 1) % t)
    for n in range(min(ATT_LOOKAHEAD, len(order))):
        issue(n)
    for n, (i, j) in enumerate(order):
        if n + ATT_LOOKAHEAD < len(order):
            issue(n + ATT_LOOKAHEAD)
        s = s_scr[n % n_slots]
        if i == j:
            s = jnp.where(causal, s, MASK_VALUE)
        m_new = jnp.maximum(m[i], jnp.max(s, axis=0, keepdims=True))
        alpha = jnp.exp2(m[i] - m_new)
        m[i] = m_new
        acc_scr[i] = alpha * acc_scr[i] + _dot(vt_scr[j], jnp.exp2(s - m_new).astype(BF16))

    lam = (jnp.exp(jnp.sum(lq1_ref[...] * lk1_ref[...], axis=-1, keepdims=True))
           - jnp.exp(jnp.sum(lq2_ref[...] * lk2_ref[...], axis=-1, keepdims=True)) + LAMBDA_INIT)
    for i in range(n_blk):
        on = acc_scr[i, 0:LANES, :] / acc_scr[i, LANES:LANES + 1, :]
        ot = on[:, 0:t] - lam * on[:, t:2 * t]
        ot = ot * lax.rsqrt(jnp.mean(ot * ot, axis=0, keepdims=True) + SUBLN_EPS)
        g = g_ref[i * t:(i + 1) * t, :].astype(F32)
        o_ref[i * t:(i + 1) * t, :] = (ot.T * sw_ref[...] * (1.0 - LAMBDA_INIT) * _silu(g)).astype(o_ref.dtype)


def _attention(p_arr, p_meta, vm_t, lam_vecs, subln_w, *, batch, seq):
    t = ATT_BLOCK
    n_blk = seq // t
    lanes_blk = lambda col: col // LANES
    small = pl.BlockSpec((1, HEAD), lambda b, h: (0, 0))
    head_cols = lambda col: pl.BlockSpec((seq, LANES), lambda b, h: (b, lanes_blk(col) + h))
    return pl.pallas_call(
        functools.partial(_attn_kernel, t=t, n_blk=n_blk),
        out_shape=jax.ShapeDtypeStruct((batch * seq, DIFF_WIDTH), BF16),
        grid=(batch, DIFF_HEADS),
        in_specs=[
            head_cols(COL_Q), head_cols(COL_K), head_cols(COL_V), head_cols(COL_GD),
            pl.BlockSpec((N_META, LANES), lambda b, h: (0, lanes_blk(COL_K) + h)),
            pl.BlockSpec((1, LANES + ONES_ROWS, N_META), lambda b, h: (h, 0, 0)),
            small, small, small, small,
            pl.BlockSpec((1, LANES), lambda b, h: (0, 0)),
        ],
        out_specs=pl.BlockSpec((seq, LANES), lambda b, h: (b, h)),
        scratch_shapes=[
            pltpu.VMEM((n_blk, LANES + ONES_ROWS, t), BF16),
            pltpu.VMEM((n_blk, LANES + ONES_ROWS, 2 * t), F32),
            pltpu.VMEM((ATT_LOOKAHEAD + 1, t, 2 * t), F32),
        ],
        compiler_params=pltpu.CompilerParams(
            dimension_semantics=("arbitrary", "arbitrary"), vmem_limit_bytes=VMEM_LIMIT),
        name="diff_attn",
    )(p_arr, p_arr, p_arr, p_arr, p_meta, vm_t, *lam_vecs, subln_w)


def _outproj_kernel(yr_ref, yd_ref, w1_ref, w2_ref, x_ref, g_ref, o_ref):
    y = _dot(yr_ref[...], w1_ref[...]) + _dot(yd_ref[...], w2_ref[...])
    ms = jnp.mean(y * y, axis=-1, keepdims=True)
    o_ref[...] = x_ref[...] + y * lax.rsqrt(ms + NORM_EPS) * g_ref[...]


def _outproj(y_r, y_d, w1, w2, x2, g):
    m = x2.shape[0]
    tm = OUT_TM
    return pl.pallas_call(
        _outproj_kernel,
        out_shape=jax.ShapeDtypeStruct((m, D_MODEL), F32),
        grid=(m // tm,),
        in_specs=[
            pl.BlockSpec((tm, RWKV_WIDTH), lambda i: (i, 0)),
            pl.BlockSpec((tm, DIFF_WIDTH), lambda i: (i, 0)),
            pl.BlockSpec((RWKV_WIDTH, D_MODEL), lambda i: (0, 0)),
            pl.BlockSpec((DIFF_WIDTH, D_MODEL), lambda i: (0, 0)),
            pl.BlockSpec((tm, D_MODEL), lambda i: (i, 0)),
            pl.BlockSpec((1, D_MODEL), lambda i: (0, 0)),
        ],
        out_specs=pl.BlockSpec((tm, D_MODEL), lambda i: (i, 0)),
        compiler_params=pltpu.CompilerParams(
            dimension_semantics=("arbitrary",), vmem_limit_bytes=VMEM_LIMIT),
        name="outproj",
    )(y_r, y_d, w1, w2, x2, g)


def _rope_tables(first_pos, n_pos):
    pos = jnp.arange(first_pos, first_pos + n_pos, dtype=F32)
    inv_freq = ROPE_THETA ** (-jnp.arange(0, HEAD, 2, dtype=F32) / HEAD)
    ang = pos[:, None] * inv_freq[None, :]
    cos = jnp.cos(ang)
    sin = jnp.sin(ang)
    cos = jnp.concatenate([cos, cos, cos, cos], axis=-1)
    sin = jnp.concatenate([-sin, sin, -sin, sin], axis=-1)
    return cos, sin


def kernel(x, meta_tokens, pre_norm_w, w_in, rwkv_mu, rwkv_w0, rwkv_w_up, rwkv_a0, rwkv_a_up, rwkv_k_k, rwkv_k_a, rwkv_r_k, rwkv_gn_w, rwkv_gn_b, diff_lam_q1, diff_lam_k1, diff_lam_q2, diff_lam_k2, diff_subln_w, w_out, post_norm_w):
    batch, seq, d = x.shape
    assert d == D_MODEL and meta_tokens.shape == (N_META, D_MODEL)
    assert seq % RWKV_CHUNK == 0 and seq % ATT_BLOCK == 0
    assert (batch * seq) % PROJ_TM == 0 and seq % PROJ_TM == 0
    layer = 0
    x2 = x.reshape(batch * seq, D_MODEL)

    w = w_in[layer]
    rkv_end = 3 * RWKV_WIDTH
    lora_end = rkv_end + 2 * LORA
    w_main = jnp.concatenate([w[:, :rkv_end], w[:, lora_end:]], axis=1).astype(BF16)
    w_lora_in = w[:, rkv_end:lora_end].astype(BF16)
    g_pre = pre_norm_w[layer].reshape(1, D_MODEL)

    cos_m, sin_m = _rope_tables(0, N_META)
    cos_x, sin_x = _rope_tables(N_META, seq)
    p_meta, lora_meta = _inproj(meta_tokens.astype(x.dtype), g_pre, w_main, w_lora_in, cos_m, sin_m, tm=N_META)
    p_x, lora_x = _inproj(x2, g_pre, w_main, w_lora_in, cos_x, sin_x, tm=PROJ_TM)

    mu = rwkv_mu[layer]
    zeros = jnp.zeros((LORA, RWKV_WIDTH), F32)
    w_lora = jnp.concatenate([
        jnp.concatenate([rwkv_w_up[layer], zeros], axis=1),
        jnp.concatenate([zeros, rwkv_a_up[layer]], axis=1)], axis=0)
    w_lora_hi, w_lora_lo = _split2(w_lora)
    row = lambda t, n: t.reshape(1, n)
    rwkv_params = (
        row(mu[:rkv_end], rkv_end), row(mu[rkv_end:lora_end], LANES),
        row(rwkv_w0[layer], RWKV_WIDTH), row(rwkv_a0[layer], RWKV_WIDTH),
        w_lora_hi, w_lora_lo,
        row(rwkv_k_k[layer], RWKV_WIDTH), row(rwkv_k_a[layer], RWKV_WIDTH),
        row(rwkv_r_k[layer], RWKV_WIDTH), row(rwkv_gn_w[layer], RWKV_WIDTH), row(rwkv_gn_b[layer], RWKV_WIDTH),
    )
    pad = RWKV_CHUNK - N_META
    _, h_meta = _rwkv(jnp.pad(p_meta, ((pad, 0), (0, 0))), jnp.pad(lora_meta, ((pad, 0), (0, 0))),
                      jnp.zeros((8, rkv_end), F32), jnp.zeros((8, LANES), F32),
                      jnp.zeros((1, N_PAIRS, LANES, LANES), F32), rwkv_params, batch=1)
    y_r, _ = _rwkv(p_x, lora_x, p_meta[N_META - 8:, :rkv_end].astype(F32), lora_meta[N_META - 8:],
                   h_meta, rwkv_params, batch=batch)

    lam_vecs = tuple(t[layer].reshape(1, HEAD) for t in (diff_lam_q1, diff_lam_k1, diff_lam_q2, diff_lam_k2))
    vm_t = p_meta[:, COL_V:COL_V + DIFF_WIDTH].reshape(N_META, DIFF_HEADS, LANES).transpose(1, 2, 0)
    vm_t = jnp.concatenate([vm_t, jnp.ones((DIFF_HEADS, ONES_ROWS, N_META), BF16)], axis=1)
    y_d = _attention(p_x, p_meta, vm_t, lam_vecs, diff_subln_w[layer].reshape(1, LANES), batch=batch, seq=seq)

    wo = w_out[layer].astype(BF16)
    out = _outproj(y_r, y_d, wo[:RWKV_WIDTH], wo[RWKV_WIDTH:], x2, post_norm_w[layer].reshape(1, D_MODEL))
    return out.reshape(batch, seq, D_MODEL)
```

```python
import functools
import math

import jax
import jax.numpy as jnp
from jax import lax
from jax.experimental import pallas as pl
from jax.experimental.pallas import tpu as pltpu

F32 = jnp.float32
BF16 = jnp.bfloat16

D_MODEL = 2048
N_META = 16
HEAD = 64
LANES = 128
RWKV_WIDTH = 1024
N_PAIRS = RWKV_WIDTH // LANES
DIFF_WIDTH = 1024
DIFF_HEADS = DIFF_WIDTH // LANES
LORA = 64
ROPE_THETA = 10000.0
NORM_EPS = 1e-6
GN_EPS = 64e-5
SUBLN_EPS = 1e-5
LAMBDA_INIT = 0.8 - 0.6 * math.exp(-0.3 * 0)

COL_RKV = 0
COL_GR = 3 * RWKV_WIDTH
COL_Q = 4 * RWKV_WIDTH
COL_K = COL_Q + DIFF_WIDTH
COL_V = COL_K + DIFF_WIDTH
COL_GD = COL_V + DIFF_WIDTH
P_COLS = COL_GD + DIFF_WIDTH

PROJ_TM = 512
PROJ_TN = 1024
RWKV_CHUNK = 64
RWKV_INTERLEAVE = 4
ATT_BLOCK = 512
ATT_LOOKAHEAD = 1
OUT_TM = 256
VMEM_LIMIT = 48 * 1024 * 1024
MASK_VALUE = -1e30
ONES_ROWS = 16
Q_SCALE = HEAD ** -0.5 * math.log2(math.e)


def _dot(a, b):
    return jnp.dot(a, b, preferred_element_type=F32)


def _dot_nt(a, b):
    return lax.dot_general(a, b, (((1,), (1,)), ((), ())), preferred_element_type=F32)


def _dot_tn(a, b):
    return lax.dot_general(a, b, (((0,), (0,)), ((), ())), preferred_element_type=F32)


def _split2(x):
    hi = x.astype(BF16)
    lo = (x - hi.astype(F32)).astype(BF16)
    return hi, lo


def _sigmoid(x):
    return 1.0 / (1.0 + jnp.exp(-x))


def _silu(x):
    return x * _sigmoid(x)


def _inproj_kernel(x_ref, g_ref, w_ref, wl_ref, cos_ref, sin_ref, o_ref, ol_ref, hn_ref, *, tn):
    j = pl.program_id(1)

    @pl.when(j == 0)
    def _():
        x = x_ref[...]
        ms = jnp.mean(x * x, axis=-1, keepdims=True)
        hn = (x * lax.rsqrt(ms + NORM_EPS) * g_ref[...]).astype(BF16)
        hn_ref[...] = hn
        ol_ref[...] = _dot(hn, wl_ref[...])

    acc = _dot(hn_ref[...], w_ref[...])
    q_tile = COL_Q // tn
    k_tile = COL_K // tn
    is_rope = jnp.logical_or(j == q_tile, j == k_tile)

    @pl.when(is_rope)
    def _():
        cos = cos_ref[...]
        sin = sin_ref[...]
        lane = lax.broadcasted_iota(jnp.int32, cos.shape, 1)
        first_half = (lane % HEAD) < (HEAD // 2)
        scale = jnp.where(j == q_tile, Q_SCALE, 1.0).astype(F32)
        for c in range(tn // LANES):
            blk = acc[:, c * LANES:(c + 1) * LANES]
            swapped = jnp.where(first_half, pltpu.roll(blk, LANES - HEAD // 2, 1),
                                pltpu.roll(blk, HEAD // 2, 1))
            o_ref[:, c * LANES:(c + 1) * LANES] = ((blk * cos + swapped * sin) * scale).astype(o_ref.dtype)

    @pl.when(jnp.logical_not(is_rope))
    def _():
        o_ref[...] = acc.astype(o_ref.dtype)


def _inproj(x2, g, w_main, w_lora, cos, sin, *, tm):
    m = x2.shape[0]
    tn = PROJ_TN
    n_pos_tiles = cos.shape[0] // tm
    return pl.pallas_call(
        functools.partial(_inproj_kernel, tn=tn),
        out_shape=(jax.ShapeDtypeStruct((m, P_COLS), BF16), jax.ShapeDtypeStruct((m, LANES), F32)),
        grid=(m // tm, P_COLS // tn),
        in_specs=[
            pl.BlockSpec((tm, D_MODEL), lambda i, j: (i, 0)),
            pl.BlockSpec((1, D_MODEL), lambda i, j: (0, 0)),
            pl.BlockSpec((D_MODEL, tn), lambda i, j: (0, j)),
            pl.BlockSpec((D_MODEL, LANES), lambda i, j: (0, 0)),
            pl.BlockSpec((tm, LANES), lambda i, j: (i % n_pos_tiles, 0)),
            pl.BlockSpec((tm, LANES), lambda i, j: (i % n_pos_tiles, 0)),
        ],
        out_specs=(
            pl.BlockSpec((tm, tn), lambda i, j: (i, j)),
            pl.BlockSpec((tm, LANES), lambda i, j: (i, 0)),
        ),
        scratch_shapes=[pltpu.VMEM((tm, D_MODEL), BF16)],
        compiler_params=pltpu.CompilerParams(
            dimension_semantics=("arbitrary", "arbitrary"), vmem_limit_bytes=VMEM_LIMIT),
        name="inproj",
    )(x2, g, w_main, w_lora, cos, sin)


def _rwkv_kernel(rkv_ref, g_ref, lora_ref, pinit_rkv_ref, pinit_lora_ref, hinit_ref,
                 mu_rkv_ref, mu_lora_ref, w0_ref, a0_ref, wl_hi_ref, wl_lo_ref,
                 kk_ref, ka_ref, rk_ref, gnw_ref, gnb_ref,
                 y_ref, hout_ref,
                 h_scr, prev_rkv, prev_lora,
                 st_lhs, st_rhs, st_vs, st_v, st_kh, st_bh, st_gamma, st_bonus, *, n_chunks):
    C = RWKV_CHUNK
    S = 2 * C
    assert S == LANES
    b_id = pl.program_id(0)
    c = pl.program_id(1)
    pairs = range(N_PAIRS)

    @pl.when(c == 0)
    def _():
        prev_rkv[...] = pinit_rkv_ref[...]
        prev_lora[...] = pinit_lora_ref[...]

    @pl.when(c <= 1)
    def _():
        h_scr[...] = hinit_ref[0]

    @pl.when(jnp.logical_and(b_id == 0, c == 0))
    def _():
        for ref in (st_lhs, st_rhs, st_vs, st_v, st_kh, st_bh, st_gamma, st_bonus):
            ref[1] = jnp.zeros(ref.shape[1:], ref.dtype)

    def cols(p, base=0):
        return slice(base + p * LANES, base + (p + 1) * LANES)

    def make_stack():
        head0 = lax.broadcasted_iota(jnp.int32, (C, LANES), 1) < HEAD

        def stack(t):
            return jnp.concatenate([jnp.where(head0, t, 0.0), jnp.where(head0, 0.0, t)], axis=0).astype(BF16)
        return head0, stack

    def make_head_sum():
        li = lax.broadcasted_iota(jnp.int32, (2 * LANES, LANES), 0)
        lj = lax.broadcasted_iota(jnp.int32, (2 * LANES, LANES), 1)
        ones_bd2 = jnp.where(((li % LANES) // HEAD) == (lj // HEAD), 1.0, 0.0).astype(BF16)

        def head_sum(t):
            rows = jnp.concatenate([t[:, cols(p)] for p in pairs], axis=0)
            hi, lo = _split2(rows)
            s = _dot(jnp.concatenate([hi, lo], axis=1), ones_bd2)
            return jnp.concatenate([s[p * C:(p + 1) * C] for p in pairs], axis=1)
        return head_sum

    def prepare(slot):
        head0, stack = make_stack()
        head_sum = make_head_sum()
        row = lax.broadcasted_iota(jnp.int32, (C, 1), 0)

        def token_shift(x, prev8, mu):
            xp = jnp.where(row == 0, prev8[7:8, :], pltpu.roll(x, 1, 0))
            return x + (xp - x) * mu

        xl = lora_ref[...]
        ul = token_shift(xl, prev_lora[...], mu_lora_ref[...])
        prev_lora[...] = xl[C - 8:C]
        tl_hi, tl_lo = _split2(jnp.where(head0, jnp.tanh(ul), ul))
        wl_hi = wl_hi_ref[...]
        lo_out = _dot(tl_hi, wl_hi) + _dot(tl_lo, wl_hi) + _dot(tl_hi, wl_lo_ref[...])
        yield

        def shifted(base):
            sl = slice(base, base + RWKV_WIDTH)
            xs = rkv_ref[:, sl].astype(F32)
            out = token_shift(xs, prev_rkv[:, sl], mu_rkv_ref[:, sl])
            prev_rkv[:, sl] = xs[C - 8:C]
            return out

        r = shifted(0)
        yield
        k = shifted(RWKV_WIDTH)
        yield
        v = shifted(2 * RWKV_WIDTH)
        yield

        logdec = -math.exp(-0.5) * _sigmoid(w0_ref[...] + lo_out[:, 0:RWKV_WIDTH])
        yield
        a_lr = _sigmoid(a0_ref[...] + lo_out[:, RWKV_WIDTH:2 * RWKV_WIDTH])
        yield

        ti = lax.broadcasted_iota(jnp.int32, (C, C), 0)
        si = lax.broadcasted_iota(jnp.int32, (C, C), 1)
        tri = jnp.where(ti >= si, 1.0, 0.0).astype(BF16)
        ld_hi = logdec.astype(BF16)
        rem = logdec - ld_hi.astype(F32)
        ld_mid = rem.astype(BF16)
        ld_lo = (rem - ld_mid.astype(F32)).astype(BF16)
        cum = _dot(tri, ld_hi) + _dot(tri, ld_mid) + _dot(tri, ld_lo)
        cum_last = cum[C - 1:C, :]
        yield
        e_excl = jnp.exp(cum - logdec)
        e_incl = jnp.exp(cum)
        yield
        e_neg = jnp.exp(-cum)
        e_hat = jnp.exp(cum_last - cum)
        yield

        kk = k * kk_ref[...]
        kk = kk * lax.rsqrt(jnp.maximum(head_sum(kk * kk), 1e-24))
        yield
        k2 = k * (1.0 + (a_lr - 1.0) * ka_ref[...])
        b = kk * a_lr
        yield
        xa_all = -kk * e_excl
        xr_all = r * e_incl
        yield
        yb_all = b * e_neg
        yk_all = k2 * e_neg
        yield

        st_gamma[slot] = jnp.exp(cum_last)
        st_bonus[slot] = head_sum(r * k2 * rk_ref[...]) * v
        yield
        st_v[slot] = v.astype(BF16)
        st_kh[slot] = (k2 * e_hat).astype(BF16)
        st_bh[slot] = (b * e_hat).astype(BF16)
        yield
        for p in pairs:
            sl = cols(p)
            st_lhs[slot, p] = jnp.concatenate([xa_all[:, sl], xr_all[:, sl]], axis=0).astype(BF16)
            st_rhs[slot, p] = jnp.concatenate([stack(yb_all[:, sl]), stack(yk_all[:, sl])], axis=0)
            st_vs[slot, p] = stack(v[:, sl])
            yield

    def recur(slot):
        _, stack = make_stack()
        head_sum = make_head_sum()
        ri = lax.broadcasted_iota(jnp.int32, (S, 2 * S), 0)
        ci = lax.broadcasted_iota(jnp.int32, (S, 2 * S), 1)
        a_mask = (ri % C) >= (ci % C) + jnp.where(ri < C, 1, 0)
        ei = lax.broadcasted_iota(jnp.int32, (C, S), 0)
        ej = lax.broadcasted_iota(jnp.int32, (C, S), 1)
        eye = jnp.where(ei == ej % C, 1.0, 0.0).astype(F32)
        bi = lax.broadcasted_iota(jnp.int32, (LANES, LANES), 0)
        bj = lax.broadcasted_iota(jnp.int32, (LANES, LANES), 1)
        same_head = (bi // HEAD) == (bj // HEAD)

        def stage(fn):
            out = []
            for p in pairs:
                out.append(fn(p))
                yield
            return out

        lhs_ar = [st_lhs[slot, p] for p in pairs]
        vs = [st_vs[slot, p] for p in pairs]

        a_all = yield from stage(lambda p: jnp.where(a_mask, _dot_nt(lhs_ar[p], st_rhs[slot, p]), 0.0))
        a_ab = [a_all[p][0:C, 0:S] for p in pairs]
        a_x_v = yield from stage(lambda p: _dot(a_all[p][:, S:2 * S].astype(BF16), vs[p]))

        inv = [eye + a_ab[p] for p in pairs]
        apow = yield from stage(lambda p: _dot(a_ab[p].astype(BF16), stack(a_ab[p])))
        n = 2
        while 2 * n < C:
            res = yield from stage(
                lambda p: _dot(jnp.concatenate([inv[p], apow[p]], axis=0).astype(BF16), stack(apow[p])))
            inv = [inv[p] + res[p][0:C] for p in pairs]
            apow = [res[p][C:S] for p in pairs]
            n *= 2
        inv = yield from stage(lambda p: inv[p] + _dot(inv[p].astype(BF16), stack(apow[p])))

        s_old = [h_scr[p] for p in pairs]
        x_s = yield from stage(lambda p: _dot_nt(lhs_ar[p], s_old[p].astype(BF16)))
        us = yield from stage(lambda p: _dot(inv[p].astype(BF16), stack(x_s[p][0:C] + a_x_v[p][0:C])))
        ys = yield from stage(
            lambda p: x_s[p][C:S] + a_x_v[p][C:S] + _dot(a_all[p][C:S, 0:S].astype(BF16), stack(us[p])))

        def update(p):
            sl = cols(p)
            vu = jnp.concatenate([st_v[slot, :, sl], us[p].astype(BF16)], axis=0)
            khb = jnp.concatenate([st_kh[slot, :, sl], st_bh[slot, :, sl]], axis=0)
            h_scr[p] = s_old[p] * st_gamma[slot, :, sl] + jnp.where(same_head, _dot_tn(vu, khb), 0.0)

        yield from stage(update)

        y = jnp.concatenate(ys, axis=1)
        yc = y - head_sum(y) * (1.0 / HEAD)
        yield
        var = head_sum(yc * yc) * (1.0 / HEAD)
        y = yc * lax.rsqrt(var + GN_EPS) * gnw_ref[...] + gnb_ref[...] + st_bonus[slot]
        yield
        y_ref[...] = (y * _silu(g_ref[...].astype(F32))).astype(y_ref.dtype)

    def interleave(chain, filler, chain_steps_per_filler_step):
        active = [chain, filler]
        while active:
            for gen in list(active):
                for _ in range(chain_steps_per_filler_step if gen is chain else 1):
                    if next(gen, "done") == "done":
                        active.remove(gen)
                        break

    odd = jnp.bitwise_and(c, 1) == 1

    @pl.when(jnp.logical_not(odd))
    def _():
        interleave(recur(1), prepare(0), RWKV_INTERLEAVE)

    @pl.when(odd)
    def _():
        interleave(recur(0), prepare(1), RWKV_INTERLEAVE)

    @pl.when(c == n_chunks)
    def _():
        hout_ref[0] = h_scr[...]


def _rwkv(p_arr, lora_arr, pinit_rkv, pinit_lora, hinit, params, *, batch):
    C = RWKV_CHUNK
    rows = p_arr.shape[0]
    n_chunks = rows // (batch * C)
    const2 = lambda b, c: (0, 0)
    vec = lambda width: pl.BlockSpec((1, width), const2)
    prep_chunk = lambda b, c: b * n_chunks + jnp.minimum(c, n_chunks - 1)
    recur_chunk = lambda b, c: b * n_chunks + jnp.maximum(c - 1, 0)
    slots = lambda shape, dtype: pltpu.VMEM((2,) + shape, dtype)
    return pl.pallas_call(
        functools.partial(_rwkv_kernel, n_chunks=n_chunks),
        out_shape=(jax.ShapeDtypeStruct((rows, RWKV_WIDTH), BF16),
                   jax.ShapeDtypeStruct((batch, N_PAIRS, LANES, LANES), F32)),
        grid=(batch, n_chunks + 1),
        in_specs=[
            pl.BlockSpec((C, 3 * RWKV_WIDTH), lambda b, c: (prep_chunk(b, c), COL_RKV // (3 * RWKV_WIDTH))),
            pl.BlockSpec((C, RWKV_WIDTH), lambda b, c: (recur_chunk(b, c), COL_GR // RWKV_WIDTH)),
            pl.BlockSpec((C, LANES), lambda b, c: (prep_chunk(b, c), 0)),
            pl.BlockSpec((8, 3 * RWKV_WIDTH), const2),
            pl.BlockSpec((8, LANES), const2),
            pl.BlockSpec((1, N_PAIRS, LANES, LANES), lambda b, c: (0, 0, 0, 0)),
            vec(3 * RWKV_WIDTH), vec(LANES), vec(RWKV_WIDTH), vec(RWKV_WIDTH),
            pl.BlockSpec((LANES, 2 * RWKV_WIDTH), const2),
            pl.BlockSpec((LANES, 2 * RWKV_WIDTH), const2),
            vec(RWKV_WIDTH), vec(RWKV_WIDTH), vec(RWKV_WIDTH), vec(RWKV_WIDTH), vec(RWKV_WIDTH),
        ],
        out_specs=(
            pl.BlockSpec((C, RWKV_WIDTH), lambda b, c: (recur_chunk(b, c), 0)),
            pl.BlockSpec((1, N_PAIRS, LANES, LANES), lambda b, c: (b, 0, 0, 0)),
        ),
        scratch_shapes=[
            pltpu.VMEM((N_PAIRS, LANES, LANES), F32),
            pltpu.VMEM((8, 3 * RWKV_WIDTH), F32),
            pltpu.VMEM((8, LANES), F32),
            slots((N_PAIRS, 2 * C, LANES), BF16),
            slots((N_PAIRS, 4 * C, LANES), BF16),
            slots((N_PAIRS, 2 * C, LANES), BF16),
            slots((C, RWKV_WIDTH), BF16),
            slots((C, RWKV_WIDTH), BF16),
            slots((C, RWKV_WIDTH), BF16),
            slots((1, RWKV_WIDTH), F32),
            slots((C, RWKV_WIDTH), F32),
        ],
        compiler_params=pltpu.CompilerParams(
            dimension_semantics=("arbitrary", "arbitrary"), vmem_limit_bytes=VMEM_LIMIT),
        name="rwkv7_chunk",
    )(p_arr, p_arr, lora_arr, pinit_rkv, pinit_lora, hinit, *params)


def _attn_kernel(q_ref, k_ref, v_ref, g_ref, km_ref, vmt_ref, lq1_ref, lk1_ref, lq2_ref, lk2_ref, sw_ref,
                 o_ref, vt_scr, acc_scr, s_scr, *, t, n_blk):
    for c in range(n_blk):
        vt_scr[c, 0:LANES, :] = v_ref[c * t:(c + 1) * t, :].astype(F32).T.astype(BF16)
        vt_scr[c, LANES:LANES + ONES_ROWS, :] = jnp.ones((ONES_ROWS, t), BF16)

    comp0 = lax.broadcasted_iota(jnp.int32, (LANES, t), 0) < HEAD
    qst = []
    for i in range(n_blk):
        qt = q_ref[i * t:(i + 1) * t, :].astype(F32).T
        qst.append(jnp.concatenate([jnp.where(comp0, qt, 0.0), jnp.where(comp0, 0.0, qt)], axis=1).astype(BF16))

    m = []
    for i in range(n_blk):
        s = _dot(km_ref[...], qst[i])
        m.append(jnp.max(s, axis=0, keepdims=True))
        acc_scr[i] = _dot(vmt_ref[0], jnp.exp2(s - m[i]).astype(BF16))

    order = sorted(((i, j) for i in range(n_blk) for j in range(i + 1)), key=lambda ij: (ij[1], ij[0]))
    n_slots = s_scr.shape[0]

    def issue(n):
        i, j = order[n]
        s_scr[n % n_slots] = _dot(k_ref[j * t:(j + 1) * t, :], qst[i])

    causal = (lax.broadcasted_iota(jnp.int32, (t, 2 * t), 0)
              <= lax.broadcasted_iota(jnp.int32, (t, 2 * t), 1) % t)
    for n in range(min(ATT_LOOKAHEAD, len(order))):
        issue(n)
    for n, (i, j) in enumerate(order):
        if n + ATT_LOOKAHEAD < len(order):
            issue(n + ATT_LOOKAHEAD)
        s = s_scr[n % n_slots]
        if i == j:
            s = jnp.where(causal, s, MASK_VALUE)
        m_new = jnp.maximum(m[i], jnp.max(s, axis=0, keepdims=True))
        alpha = jnp.exp2(m[i] - m_new)
        m[i] = m_new
        acc_scr[i] = alpha * acc_scr[i] + _dot(vt_scr[j], jnp.exp2(s - m_new).astype(BF16))

    lam = (jnp.exp(jnp.sum(lq1_ref[...] * lk1_ref[...], axis=-1, keepdims=True))
           - jnp.exp(jnp.sum(lq2_ref[...] * lk2_ref[...], axis=-1, keepdims=True)) + LAMBDA_INIT)
    for i in range(n_blk):
        on = acc_scr[i, 0:LANES, :] / acc_scr[i, LANES:LANES + 1, :]
        ot = on[:, 0:t] - lam * on[:, t:2 * t]
        ot = ot * lax.rsqrt(jnp.mean(ot * ot, axis=0, keepdims=True) + SUBLN_EPS)
        g = g_ref[i * t:(i + 1) * t, :].astype(F32)
        o_ref[i * t:(i + 1) * t, :] = (ot.T * sw_ref[...] * (1.0 - LAMBDA_INIT) * _silu(g)).astype(o_ref.dtype)


def _attention(p_arr, p_meta, vm_t, lam_vecs, subln_w, *, batch, seq):
    t = ATT_BLOCK
    n_blk = seq // t
    lanes_blk = lambda col: col // LANES
    small = pl.BlockSpec((1, HEAD), lambda b, h: (0, 0))
    head_cols = lambda col: pl.BlockSpec((seq, LANES), lambda b, h: (b, lanes_blk(col) + h))
    return pl.pallas_call(
        functools.partial(_attn_kernel, t=t, n_blk=n_blk),
        out_shape=jax.ShapeDtypeStruct((batch * seq, DIFF_WIDTH), BF16),
        grid=(batch, DIFF_HEADS),
        in_specs=[
            head_cols(COL_Q), head_cols(COL_K), head_cols(COL_V), head_cols(COL_GD),
            pl.BlockSpec((N_META, LANES), lambda b, h: (0, lanes_blk(COL_K) + h)),
            pl.BlockSpec((1, LANES + ONES_ROWS, N_META), lambda b, h: (h, 0, 0)),
            small, small, small, small,
            pl.BlockSpec((1, LANES), lambda b, h: (0, 0)),
        ],
        out_specs=pl.BlockSpec((seq, LANES), lambda b, h: (b, h)),
        scratch_shapes=[
            pltpu.VMEM((n_blk, LANES + ONES_ROWS, t), BF16),
            pltpu.VMEM((n_blk, LANES + ONES_ROWS, 2 * t), F32),
            pltpu.VMEM((ATT_LOOKAHEAD + 1, t, 2 * t), F32),
        ],
        compiler_params=pltpu.CompilerParams(
            dimension_semantics=("arbitrary", "arbitrary"), vmem_limit_bytes=VMEM_LIMIT),
        name="diff_attn",
    )(p_arr, p_arr, p_arr, p_arr, p_meta, vm_t, *lam_vecs, subln_w)


def _outproj_kernel(yr_ref, yd_ref, w1_ref, w2_ref, x_ref, g_ref, o_ref):
    y = _dot(yr_ref[...], w1_ref[...]) + _dot(yd_ref[...], w2_ref[...])
    ms = jnp.mean(y * y, axis=-1, keepdims=True)
    o_ref[...] = x_ref[...] + y * lax.rsqrt(ms + NORM_EPS) * g_ref[...]


def _outproj(y_r, y_d, w1, w2, x2, g):
    m = x2.shape[0]
    tm = OUT_TM
    return pl.pallas_call(
        _outproj_kernel,
        out_shape=jax.ShapeDtypeStruct((m, D_MODEL), F32),
        grid=(m // tm,),
        in_specs=[
            pl.BlockSpec((tm, RWKV_WIDTH), lambda i: (i, 0)),
            pl.BlockSpec((tm, DIFF_WIDTH), lambda i: (i, 0)),
            pl.BlockSpec((RWKV_WIDTH, D_MODEL), lambda i: (0, 0)),
            pl.BlockSpec((DIFF_WIDTH, D_MODEL), lambda i: (0, 0)),
            pl.BlockSpec((tm, D_MODEL), lambda i: (i, 0)),
            pl.BlockSpec((1, D_MODEL), lambda i: (0, 0)),
        ],
        out_specs=pl.BlockSpec((tm, D_MODEL), lambda i: (i, 0)),
        compiler_params=pltpu.CompilerParams(
            dimension_semantics=("arbitrary",), vmem_limit_bytes=VMEM_LIMIT),
        name="outproj",
    )(y_r, y_d, w1, w2, x2, g)


def _rope_tables(first_pos, n_pos):
    pos = jnp.arange(first_pos, first_pos + n_pos, dtype=F32)
    inv_freq = ROPE_THETA ** (-jnp.arange(0, HEAD, 2, dtype=F32) / HEAD)
    ang = pos[:, None] * inv_freq[None, :]
    cos = jnp.cos(ang)
    sin = jnp.sin(ang)
    cos = jnp.concatenate([cos, cos, cos, cos], axis=-1)
    sin = jnp.concatenate([-sin, sin, -sin, sin], axis=-1)
    return cos, sin


def kernel(x, meta_tokens, pre_norm_w, w_in, rwkv_mu, rwkv_w0, rwkv_w_up, rwkv_a0, rwkv_a_up, rwkv_k_k, rwkv_k_a, rwkv_r_k, rwkv_gn_w, rwkv_gn_b, diff_lam_q1, diff_lam_k1, diff_lam_q2, diff_lam_k2, diff_subln_w, w_out, post_norm_w):
    batch, seq, d = x.shape
    assert d == D_MODEL and meta_tokens.shape == (N_META, D_MODEL)
    assert seq % RWKV_CHUNK == 0 and seq % ATT_BLOCK == 0
    assert (batch * seq) % PROJ_TM == 0 and seq % PROJ_TM == 0
    layer = 0
    x2 = x.reshape(batch * seq, D_MODEL)

    w = w_in[layer]
    rkv_end = 3 * RWKV_WIDTH
    lora_end = rkv_end + 2 * LORA
    w_main = jnp.concatenate([w[:, :rkv_end], w[:, lora_end:]], axis=1).astype(BF16)
    w_lora_in = w[:, rkv_end:lora_end].astype(BF16)
    g_pre = pre_norm_w[layer].reshape(1, D_MODEL)

    cos_m, sin_m = _rope_tables(0, N_META)
    cos_x, sin_x = _rope_tables(N_META, seq)
    p_meta, lora_meta = _inproj(meta_tokens.astype(x.dtype), g_pre, w_main, w_lora_in, cos_m, sin_m, tm=N_META)
    p_x, lora_x = _inproj(x2, g_pre, w_main, w_lora_in, cos_x, sin_x, tm=PROJ_TM)

    mu = rwkv_mu[layer]
    zeros = jnp.zeros((LORA, RWKV_WIDTH), F32)
    w_lora = jnp.concatenate([
        jnp.concatenate([rwkv_w_up[layer], zeros], axis=1),
        jnp.concatenate([zeros, rwkv_a_up[layer]], axis=1)], axis=0)
    w_lora_hi, w_lora_lo = _split2(w_lora)
    row = lambda t, n: t.reshape(1, n)
    rwkv_params = (
        row(mu[:rkv_end], rkv_end), row(mu[rkv_end:lora_end], LANES),
        row(rwkv_w0[layer], RWKV_WIDTH), row(rwkv_a0[layer], RWKV_WIDTH),
        w_lora_hi, w_lora_lo,
        row(rwkv_k_k[layer], RWKV_WIDTH), row(rwkv_k_a[layer], RWKV_WIDTH),
        row(rwkv_r_k[layer], RWKV_WIDTH), row(rwkv_gn_w[layer], RWKV_WIDTH), row(rwkv_gn_b[layer], RWKV_WIDTH),
    )
    pad = RWKV_CHUNK - N_META
    _, h_meta = _rwkv(jnp.pad(p_meta, ((pad, 0), (0, 0))), jnp.pad(lora_meta, ((pad, 0), (0, 0))),
                      jnp.zeros((8, rkv_end), F32), jnp.zeros((8, LANES), F32),
                      jnp.zeros((1, N_PAIRS, LANES, LANES), F32), rwkv_params, batch=1)
    y_r, _ = _rwkv(p_x, lora_x, p_meta[N_META - 8:, :rkv_end].astype(F32), lora_meta[N_META - 8:],
                   h_meta, rwkv_params, batch=batch)

    lam_vecs = tuple(t[layer].reshape(1, HEAD) for t in (diff_lam_q1, diff_lam_k1, diff_lam_q2, diff_lam_k2))
    vm_t = p_meta[:, COL_V:COL_V + DIFF_WIDTH].reshape(N_META, DIFF_HEADS, LANES).transpose(1, 2, 0)
    vm_t = jnp.concatenate([vm_t, jnp.ones((DIFF_HEADS, ONES_ROWS, N_META), BF16)], axis=1)
    y_d = _attention(p_x, p_meta, vm_t, lam_vecs, diff_subln_w[layer].reshape(1, LANES), batch=batch, seq=seq)

    wo = w_out[layer].astype(BF16)
    out = _outproj(y_r, y_d, wo[:RWKV_WIDTH], wo[RWKV_WIDTH:], x2, post_norm_w[layer].reshape(1, D_MODEL))
    return out.reshape(batch, seq, D_MODEL)
```

```python
import functools
import math

import jax
import jax.numpy as jnp
from jax import lax
from jax.experimental import pallas as pl
from jax.experimental.pallas import tpu as pltpu

F32 = jnp.float32
BF16 = jnp.bfloat16

D_MODEL = 2048
N_META = 16
HEAD = 64
LANES = 128
RWKV_WIDTH = 1024
N_PAIRS = RWKV_WIDTH // LANES
DIFF_WIDTH = 1024
DIFF_HEADS = DIFF_WIDTH // LANES
LORA = 64
ROPE_THETA = 10000.0
NORM_EPS = 1e-6
GN_EPS = 64e-5
SUBLN_EPS = 1e-5
LAMBDA_INIT = 0.8 - 0.6 * math.exp(-0.3 * 0)

COL_RKV = 0
COL_GR = 3 * RWKV_WIDTH
COL_Q = 4 * RWKV_WIDTH
COL_K = COL_Q + DIFF_WIDTH
COL_V = COL_K + DIFF_WIDTH
COL_GD = COL_V + DIFF_WIDTH
P_COLS = COL_GD + DIFF_WIDTH

PROJ_TM = 1024
PROJ_TN = 1024
PROJ_KC = 512
RWKV_CHUNK = 64
RWKV_INTERLEAVE = 4
ATT_BLOCK = 512
ATT_LOOKAHEAD = 1
OUT_TM = 256
VMEM_LIMIT = 48 * 1024 * 1024
MASK_VALUE = -1e30
ONES_ROWS = 16
Q_SCALE = HEAD ** -0.5 * math.log2(math.e)


def _dot(a, b):
    return jnp.dot(a, b, preferred_element_type=F32)


def _dot_nt(a, b):
    return lax.dot_general(a, b, (((1,), (1,)), ((), ())), preferred_element_type=F32)


def _dot_tn(a, b):
    return lax.dot_general(a, b, (((0,), (0,)), ((), ())), preferred_element_type=F32)


def _split2(x):
    hi = x.astype(BF16)
    lo = (x - hi.astype(F32)).astype(BF16)
    return hi, lo


def _sigmoid(x):
    return 1.0 / (1.0 + jnp.exp(-x))


def _silu(x):
    return x * _sigmoid(x)


def _inproj_kernel(x_ref, g_ref, w_ref, wl_ref, cos_ref, sin_ref, o_ref, ol_ref, hn_ref, rs_ref, *, tn):
    j = pl.program_id(1)
    q_tile = COL_Q // tn
    k_tile = COL_K // tn
    assert 0 < q_tile < k_tile
    is_rope = jnp.logical_or(j == q_tile, j == k_tile)
    col_blocks = [slice(c * LANES, (c + 1) * LANES) for c in range(tn // LANES)]

    @pl.when(j == 0)
    def _():
        ssq = None
        acc = None
        for k in range(D_MODEL // PROJ_KC):
            sl = slice(k * PROJ_KC, (k + 1) * PROJ_KC)
            xc = x_ref[:, sl]
            part = jnp.sum(xc * xc, axis=-1, keepdims=True)
            ssq = part if ssq is None else ssq + part
            hn = (xc * g_ref[:, sl]).astype(BF16)
            hn_ref[:, sl] = hn
            d = _dot(hn, w_ref[sl, :])
            acc = d if acc is None else acc + d
        rs = jnp.broadcast_to(lax.rsqrt(ssq * (1.0 / D_MODEL) + NORM_EPS), rs_ref.shape)
        rs_ref[...] = rs
        ol_ref[...] = _dot(hn_ref[...], wl_ref[...]) * rs
        for cb in col_blocks:
            o_ref[:, cb] = (acc[:, cb] * rs).astype(o_ref.dtype)

    @pl.when(is_rope)
    def _():
        acc = _dot(hn_ref[...], w_ref[...])
        cos = cos_ref[...]
        sin = sin_ref[...]
        lane = lax.broadcasted_iota(jnp.int32, cos.shape, 1)
        first_half = (lane % HEAD) < (HEAD // 2)
        scale = rs_ref[...] * jnp.where(j == q_tile, Q_SCALE, 1.0).astype(F32)
        for cb in col_blocks:
            blk = acc[:, cb]
            swapped = jnp.where(first_half, pltpu.roll(blk, LANES - HEAD // 2, 1),
                                pltpu.roll(blk, HEAD // 2, 1))
            o_ref[:, cb] = ((blk * cos + swapped * sin) * scale).astype(o_ref.dtype)

    @pl.when(jnp.logical_and(j > 0, jnp.logical_not(is_rope)))
    def _():
        acc = _dot(hn_ref[...], w_ref[...])
        rs = rs_ref[...]
        for cb in col_blocks:
            o_ref[:, cb] = (acc[:, cb] * rs).astype(o_ref.dtype)


def _inproj(x2, g, w_main, w_lora, cos, sin, *, tm):
    m = x2.shape[0]
    tn = PROJ_TN
    n_pos_tiles = cos.shape[0] // tm
    return pl.pallas_call(
        functools.partial(_inproj_kernel, tn=tn),
        out_shape=(jax.ShapeDtypeStruct((m, P_COLS), BF16), jax.ShapeDtypeStruct((m, LANES), F32)),
        grid=(m // tm, P_COLS // tn),
        in_specs=[
            pl.BlockSpec((tm, D_MODEL), lambda i, j: (i, 0)),
            pl.BlockSpec((1, D_MODEL), lambda i, j: (0, 0)),
            pl.BlockSpec((D_MODEL, tn), lambda i, j: (0, j)),
            pl.BlockSpec((D_MODEL, LANES), lambda i, j: (0, 0)),
            pl.BlockSpec((tm, LANES), lambda i, j: (i % n_pos_tiles, 0)),
            pl.BlockSpec((tm, LANES), lambda i, j: (i % n_pos_tiles, 0)),
        ],
        out_specs=(
            pl.BlockSpec((tm, tn), lambda i, j: (i, j)),
            pl.BlockSpec((tm, LANES), lambda i, j: (i, 0)),
        ),
        scratch_shapes=[pltpu.VMEM((tm, D_MODEL), BF16), pltpu.VMEM((tm, LANES), F32)],
        compiler_params=pltpu.CompilerParams(
            dimension_semantics=("arbitrary", "arbitrary"), vmem_limit_bytes=VMEM_LIMIT),
        name="inproj",
    )(x2, g, w_main, w_lora, cos, sin)


def _rwkv_kernel(rkv_ref, g_ref, lora_ref, pinit_rkv_ref, pinit_lora_ref, hinit_ref,
                 mu_rkv_ref, mu_lora_ref, w0_ref, a0_ref, wl_hi_ref, wl_lo_ref,
                 kk_ref, ka_ref, rk_ref, gnw_ref, gnb_ref,
                 y_ref, hout_ref,
                 h_scr, prev_rkv, prev_lora,
                 st_lhs, st_rhs, st_vs, st_v, st_kh, st_bh, st_gamma, st_bonus, *, n_chunks):
    C = RWKV_CHUNK
    S = 2 * C
    assert S == LANES
    b_id = pl.program_id(0)
    c = pl.program_id(1)
    pairs = range(N_PAIRS)

    @pl.when(c == 0)
    def _():
        prev_rkv[...] = pinit_rkv_ref[...]
        prev_lora[...] = pinit_lora_ref[...]

    @pl.when(c <= 1)
    def _():
        h_scr[...] = hinit_ref[0]

    @pl.when(jnp.logical_and(b_id == 0, c == 0))
    def _():
        for ref in (st_lhs, st_rhs, st_vs, st_v, st_kh, st_bh, st_gamma, st_bonus):
            ref[1] = jnp.zeros(ref.shape[1:], ref.dtype)

    def cols(p, base=0):
        return slice(base + p * LANES, base + (p + 1) * LANES)

    def make_stack():
        head0 = lax.broadcasted_iota(jnp.int32, (C, LANES), 1) < HEAD

        def stack(t):
            return jnp.concatenate([jnp.where(head0, t, 0.0), jnp.where(head0, 0.0, t)], axis=0).astype(BF16)
        return head0, stack

    def make_head_sum():
        li = lax.broadcasted_iota(jnp.int32, (2 * LANES, LANES), 0)
        lj = lax.broadcasted_iota(jnp.int32, (2 * LANES, LANES), 1)
        ones_bd2 = jnp.where(((li % LANES) // HEAD) == (lj // HEAD), 1.0, 0.0).astype(BF16)

        def head_sum(t):
            rows = jnp.concatenate([t[:, cols(p)] for p in pairs], axis=0)
            hi, lo = _split2(rows)
            s = _dot(jnp.concatenate([hi, lo], axis=1), ones_bd2)
            return jnp.concatenate([s[p * C:(p + 1) * C] for p in pairs], axis=1)
        return head_sum

    def prepare(slot):
        head0, stack = make_stack()
        head_sum = make_head_sum()
        row = lax.broadcasted_iota(jnp.int32, (C, 1), 0)

        def token_shift(x, prev8, mu):
            xp = jnp.where(row == 0, prev8[7:8, :], pltpu.roll(x, 1, 0))
            return x + (xp - x) * mu

        xl = lora_ref[...]
        ul = token_shift(xl, prev_lora[...], mu_lora_ref[...])
        prev_lora[...] = xl[C - 8:C]
        tl_hi, tl_lo = _split2(jnp.where(head0, jnp.tanh(ul), ul))
        wl_hi = wl_hi_ref[...]
        lo_out = _dot(tl_hi, wl_hi) + _dot(tl_lo, wl_hi) + _dot(tl_hi, wl_lo_ref[...])
        yield

        def shifted(base):
            sl = slice(base, base + RWKV_WIDTH)
            xs = rkv_ref[:, sl].astype(F32)
            out = token_shift(xs, prev_rkv[:, sl], mu_rkv_ref[:, sl])
            prev_rkv[:, sl] = xs[C - 8:C]
            return out

        r = shifted(0)
        yield
        k = shifted(RWKV_WIDTH)
        yield
        v = shifted(2 * RWKV_WIDTH)
        yield

        logdec = -math.exp(-0.5) * _sigmoid(w0_ref[...] + lo_out[:, 0:RWKV_WIDTH])
        yield
        a_lr = _sigmoid(a0_ref[...] + lo_out[:, RWKV_WIDTH:2 * RWKV_WIDTH])
        yield

        ti = lax.broadcasted_iota(jnp.int32, (C, C), 0)
        si = lax.broadcasted_iota(jnp.int32, (C, C), 1)
        tri = jnp.where(ti >= si, 1.0, 0.0).astype(BF16)
        ld_hi = logdec.astype(BF16)
        rem = logdec - ld_hi.astype(F32)
        ld_mid = rem.astype(BF16)
        ld_lo = (rem - ld_mid.astype(F32)).astype(BF16)
        cum = _dot(tri, ld_hi) + _dot(tri, ld_mid) + _dot(tri, ld_lo)
        cum_last = cum[C - 1:C, :]
        yield
        e_excl = jnp.exp(cum - logdec)
        e_incl = jnp.exp(cum)
        yield
        e_neg = jnp.exp(-cum)
        e_hat = jnp.exp(cum_last - cum)
        yield

        kk = k * kk_ref[...]
        kk = kk * lax.rsqrt(jnp.maximum(head_sum(kk * kk), 1e-24))
        yield
        k2 = k * (1.0 + (a_lr - 1.0) * ka_ref[...])
        b = kk * a_lr
        yield
        xa_all = -kk * e_excl
        xr_all = r * e_incl
        yield
        yb_all = b * e_neg
        yk_all = k2 * e_neg
        yield

        st_gamma[slot] = jnp.exp(cum_last)
        st_bonus[slot] = head_sum(r * k2 * rk_ref[...]) * v
        yield
        st_v[slot] = v.astype(BF16)
        st_kh[slot] = (k2 * e_hat).astype(BF16)
        st_bh[slot] = (b * e_hat).astype(BF16)
        yield
        for p in pairs:
            sl = cols(p)
            st_lhs[slot, p] = jnp.concatenate([xa_all[:, sl], xr_all[:, sl]], axis=0).astype(BF16)
            st_rhs[slot, p] = jnp.concatenate([stack(yb_all[:, sl]), stack(yk_all[:, sl])], axis=0)
            st_vs[slot, p] = stack(v[:, sl])
            yield

    def recur(slot):
        _, stack = make_stack()
        head_sum = make_head_sum()
        ri = lax.broadcasted_iota(jnp.int32, (S, 2 * S), 0)
        ci = lax.broadcasted_iota(jnp.int32, (S, 2 * S), 1)
        a_mask = (ri % C) >= (ci % C) + jnp.where(ri < C, 1, 0)
        ei = lax.broadcasted_iota(jnp.int32, (C, S), 0)
        ej = lax.broadcasted_iota(jnp.int32, (C, S), 1)
        eye = jnp.where(ei == ej % C, 1.0, 0.0).astype(F32)
        bi = lax.broadcasted_iota(jnp.int32, (LANES, LANES), 0)
        bj = lax.broadcasted_iota(jnp.int32, (LANES, LANES), 1)
        same_head = (bi // HEAD) == (bj // HEAD)

        def stage(fn):
            out = []
            for p in pairs:
                out.append(fn(p))
                yield
            return out

        lhs_ar = [st_lhs[slot, p] for p in pairs]
        vs = [st_vs[slot, p] for p in pairs]

        a_all = yield from stage(lambda p: jnp.where(a_mask, _dot_nt(lhs_ar[p], st_rhs[slot, p]), 0.0))
        a_ab = [a_all[p][0:C, 0:S] for p in pairs]
        a_x_v = yield from stage(lambda p: _dot(a_all[p][:, S:2 * S].astype(BF16), vs[p]))

        inv = [eye + a_ab[p] for p in pairs]
        apow = yield from stage(lambda p: _dot(a_ab[p].astype(BF16), stack(a_ab[p])))
        n = 2
        while 2 * n < C:
            res = yield from stage(
                lambda p: _dot(jnp.concatenate([inv[p], apow[p]], axis=0).astype(BF16), stack(apow[p])))
            inv = [inv[p] + res[p][0:C] for p in pairs]
            apow = [res[p][C:S] for p in pairs]
            n *= 2
        inv = yield from stage(lambda p: inv[p] + _dot(inv[p].astype(BF16), stack(apow[p])))

        s_old = [h_scr[p] for p in pairs]
        x_s = yield from stage(lambda p: _dot_nt(lhs_ar[p], s_old[p].astype(BF16)))
        us = yield from stage(lambda p: _dot(inv[p].astype(BF16), stack(x_s[p][0:C] + a_x_v[p][0:C])))
        ys = yield from stage(
            lambda p: x_s[p][C:S] + a_x_v[p][C:S] + _dot(a_all[p][C:S, 0:S].astype(BF16), stack(us[p])))

        def update(p):
            sl = cols(p)
            vu = jnp.concatenate([st_v[slot, :, sl], us[p].astype(BF16)], axis=0)
            khb = jnp.concatenate([st_kh[slot, :, sl], st_bh[slot, :, sl]], axis=0)
            h_scr[p] = s_old[p] * st_gamma[slot, :, sl] + jnp.where(same_head, _dot_tn(vu, khb), 0.0)

        yield from stage(update)

        y = jnp.concatenate(ys, axis=1)
        yc = y - head_sum(y) * (1.0 / HEAD)
        yield
        var = head_sum(yc * yc) * (1.0 / HEAD)
        y = yc * lax.rsqrt(var + GN_EPS) * gnw_ref[...] + gnb_ref[...] + st_bonus[slot]
        yield
        y_ref[...] = (y * _silu(g_ref[...].astype(F32))).astype(y_ref.dtype)

    def interleave(chain, filler, chain_steps_per_filler_step):
        active = [chain, filler]
        while active:
            for gen in list(active):
                for _ in range(chain_steps_per_filler_step if gen is chain else 1):
                    if next(gen, "done") == "done":
                        active.remove(gen)
                        break

    odd = jnp.bitwise_and(c, 1) == 1

    @pl.when(jnp.logical_not(odd))
    def _():
        interleave(recur(1), prepare(0), RWKV_INTERLEAVE)

    @pl.when(odd)
    def _():
        interleave(recur(0), prepare(1), RWKV_INTERLEAVE)

    @pl.when(c == n_chunks)
    def _():
        hout_ref[0] = h_scr[...]


def _rwkv(p_arr, lora_arr, pinit_rkv, pinit_lora, hinit, params, *, batch):
    C = RWKV_CHUNK
    rows = p_arr.shape[0]
    n_chunks = rows // (batch * C)
    const2 = lambda b, c: (0, 0)
    vec = lambda width: pl.BlockSpec((1, width), const2)
    prep_chunk = lambda b, c: b * n_chunks + jnp.minimum(c, n_chunks - 1)
    recur_chunk = lambda b, c: b * n_chunks + jnp.maximum(c - 1, 0)
    slots = lambda shape, dtype: pltpu.VMEM((2,) + shape, dtype)
    return pl.pallas_call(
        functools.partial(_rwkv_kernel, n_chunks=n_chunks),
        out_shape=(jax.ShapeDtypeStruct((rows, RWKV_WIDTH), BF16),
                   jax.ShapeDtypeStruct((batch, N_PAIRS, LANES, LANES), F32)),
        grid=(batch, n_chunks + 1),
        in_specs=[
            pl.BlockSpec((C, 3 * RWKV_WIDTH), lambda b, c: (prep_chunk(b, c), COL_RKV // (3 * RWKV_WIDTH))),
            pl.BlockSpec((C, RWKV_WIDTH), lambda b, c: (recur_chunk(b, c), COL_GR // RWKV_WIDTH)),
            pl.BlockSpec((C, LANES), lambda b, c: (prep_chunk(b, c), 0)),
            pl.BlockSpec((8, 3 * RWKV_WIDTH), const2),
            pl.BlockSpec((8, LANES), const2),
            pl.BlockSpec((1, N_PAIRS, LANES, LANES), lambda b, c: (0, 0, 0, 0)),
            vec(3 * RWKV_WIDTH), vec(LANES), vec(RWKV_WIDTH), vec(RWKV_WIDTH),
            pl.BlockSpec((LANES, 2 * RWKV_WIDTH), const2),
            pl.BlockSpec((LANES, 2 * RWKV_WIDTH), const2),
            vec(RWKV_WIDTH), vec(RWKV_WIDTH), vec(RWKV_WIDTH), vec(RWKV_WIDTH), vec(RWKV_WIDTH),
        ],
        out_specs=(
            pl.BlockSpec((C, RWKV_WIDTH), lambda b, c: (recur_chunk(b, c), 0)),
            pl.BlockSpec((1, N_PAIRS, LANES, LANES), lambda b, c: (b, 0, 0, 0)),
        ),
        scratch_shapes=[
            pltpu.VMEM((N_PAIRS, LANES, LANES), F32),
            pltpu.VMEM((8, 3 * RWKV_WIDTH), F32),
            pltpu.VMEM((8, LANES), F32),
            slots((N_PAIRS, 2 * C, LANES), BF16),
            slots((N_PAIRS, 4 * C, LANES), BF16),
            slots((N_PAIRS, 2 * C, LANES), BF16),
            slots((C, RWKV_WIDTH), BF16),
            slots((C, RWKV_WIDTH), BF16),
            slots((C, RWKV_WIDTH), BF16),
            slots((1, RWKV_WIDTH), F32),
            slots((C, RWKV_WIDTH), F32),
        ],
        compiler_params=pltpu.CompilerParams(
            dimension_semantics=("arbitrary", "arbitrary"), vmem_limit_bytes=VMEM_LIMIT),
        name="rwkv7_chunk",
    )(p_arr, p_arr, lora_arr, pinit_rkv, pinit_lora, hinit, *params)


def _attn_kernel(q_ref, k_ref, v_ref, g_ref, km_ref, vmt_ref, lq1_ref, lk1_ref, lq2_ref, lk2_ref, sw_ref,
                 o_ref, vt_scr, acc_scr, s_scr, *, t, n_blk):
    for c in range(n_blk):
        vt_scr[c, 0:LANES, :] = v_ref[c * t:(c + 1) * t, :].astype(F32).T.astype(BF16)
        vt_scr[c, LANES:LANES + ONES_ROWS, :] = jnp.ones((ONES_ROWS, t), BF16)

    comp0 = lax.broadcasted_iota(jnp.int32, (LANES, t), 0) < HEAD
    qst = []
    for i in range(n_blk):
        qt = q_ref[i * t:(i + 1) * t, :].astype(F32).T
        qst.append(jnp.concatenate([jnp.where(comp0, qt, 0.0), jnp.where(comp0, 0.0, qt)], axis=1).astype(BF16))

    m = []
    for i in range(n_blk):
        s = _dot(km_ref[...], qst[i])
        m.append(jnp.max(s, axis=0, keepdims=True))
        acc_scr[i] = _dot(vmt_ref[0], jnp.exp2(s - m[i]).astype(BF16))

    order = sorted(((i, j) for i in range(n_blk) for j in range(i + 1)), key=lambda ij: (ij[1], ij[0]))
    n_slots = s_scr.shape[0]

    def issue(n):
        i, j = order[n]
        s_scr[n % n_slots] = _dot(k_ref[j * t:(j + 1) * t, :], qst[i])

    causal = (lax.broadcasted_iota(jnp.int32, (t, 2 * t), 0)
              <= lax.broadcasted_iota(jnp.int32, (t, 2 * t), 1) % t)
    for n in range(min(ATT_LOOKAHEAD, len(order))):
        issue(n)
    for n, (i, j) in enumerate(order):
        if n + ATT_LOOKAHEAD < len(order):
            issue(n + ATT_LOOKAHEAD)
        s = s_scr[n % n_slots]
        if i == j:
            s = jnp.where(causal, s, MASK_VALUE)
        m_new = jnp.maximum(m[i], jnp.max(s, axis=0, keepdims=True))
        alpha = jnp.exp2(m[i] - m_new)
        m[i] = m_new
        acc_scr[i] = alpha * acc_scr[i] + _dot(vt_scr[j], jnp.exp2(s - m_new).astype(BF16))

    lam = (jnp.exp(jnp.sum(lq1_ref[...] * lk1_ref[...], axis=-1, keepdims=True))
           - jnp.exp(jnp.sum(lq2_ref[...] * lk2_ref[...], axis=-1, keepdims=True)) + LAMBDA_INIT)
    for i in range(n_blk):
        on = acc_scr[i, 0:LANES, :] / acc_scr[i, LANES:LANES + 1, :]
        ot = on[:, 0:t] - lam * on[:, t:2 * t]
        ot = ot * lax.rsqrt(jnp.mean(ot * ot, axis=0, keepdims=True) + SUBLN_EPS)
        g = g_ref[i * t:(i + 1) * t, :].astype(F32)
        o_ref[i * t:(i + 1) * t, :] = (ot.T * sw_ref[...] * (1.0 - LAMBDA_INIT) * _silu(g)).astype(o_ref.dtype)


def _attention(p_arr, p_meta, vm_t, lam_vecs, subln_w, *, batch, seq):
    t = ATT_BLOCK
    n_blk = seq // t
    lanes_blk = lambda col: col // LANES
    small = pl.BlockSpec((1, HEAD), lambda b, h: (0, 0))
    head_cols = lambda col: pl.BlockSpec((seq, LANES), lambda b, h: (b, lanes_blk(col) + h))
    return pl.pallas_call(
        functools.partial(_attn_kernel, t=t, n_blk=n_blk),
        out_shape=jax.ShapeDtypeStruct((batch * seq, DIFF_WIDTH), BF16),
        grid=(batch, DIFF_HEADS),
        in_specs=[
            head_cols(COL_Q), head_cols(COL_K), head_cols(COL_V), head_cols(COL_GD),
            pl.BlockSpec((N_META, LANES), lambda b, h: (0, lanes_blk(COL_K) + h)),
            pl.BlockSpec((1, LANES + ONES_ROWS, N_META), lambda b, h: (h, 0, 0)),
            small, small, small, small,
            pl.BlockSpec((1, LANES), lambda b, h: (0, 0)),
        ],
        out_specs=pl.BlockSpec((seq, LANES), lambda b, h: (b, h)),
        scratch_shapes=[
            pltpu.VMEM((n_blk, LANES + ONES_ROWS, t), BF16),
            pltpu.VMEM((n_blk, LANES + ONES_ROWS, 2 * t), F32),
            pltpu.VMEM((ATT_LOOKAHEAD + 1, t, 2 * t), F32),
        ],
        compiler_params=pltpu.CompilerParams(
            dimension_semantics=("arbitrary", "arbitrary"), vmem_limit_bytes=VMEM_LIMIT),
        name="diff_attn",
    )(p_arr, p_arr, p_arr, p_arr, p_meta, vm_t, *lam_vecs, subln_w)


def _outproj_kernel(yr_ref, yd_ref, w1_ref, w2_ref, x_ref, g_ref, o_ref):
    y = _dot(yr_ref[...], w1_ref[...]) + _dot(yd_ref[...], w2_ref[...])
    ms = jnp.mean(y * y, axis=-1, keepdims=True)
    o_ref[...] = x_ref[...] + y * lax.rsqrt(ms + NORM_EPS) * g_ref[...]


def _outproj(y_r, y_d, w1, w2, x2, g):
    m = x2.shape[0]
    tm = OUT_TM
    return pl.pallas_call(
        _outproj_kernel,
        out_shape=jax.ShapeDtypeStruct((m, D_MODEL), F32),
        grid=(m // tm,),
        in_specs=[
            pl.BlockSpec((tm, RWKV_WIDTH), lambda i: (i, 0)),
            pl.BlockSpec((tm, DIFF_WIDTH), lambda i: (i, 0)),
            pl.BlockSpec((RWKV_WIDTH, D_MODEL), lambda i: (0, 0)),
            pl.BlockSpec((DIFF_WIDTH, D_MODEL), lambda i: (0, 0)),
            pl.BlockSpec((tm, D_MODEL), lambda i: (i, 0)),
            pl.BlockSpec((1, D_MODEL), lambda i: (0, 0)),
        ],
        out_specs=pl.BlockSpec((tm, D_MODEL), lambda i: (i, 0)),
        compiler_params=pltpu.CompilerParams(
            dimension_semantics=("arbitrary",), vmem_limit_bytes=VMEM_LIMIT),
        name="outproj",
    )(y_r, y_d, w1, w2, x2, g)


def _rope_tables(first_pos, n_pos):
    pos = jnp.arange(first_pos, first_pos + n_pos, dtype=F32)
    inv_freq = ROPE_THETA ** (-jnp.arange(0, HEAD, 2, dtype=F32) / HEAD)
    ang = pos[:, None] * inv_freq[None, :]
    cos = jnp.cos(ang)
    sin = jnp.sin(ang)
    cos = jnp.concatenate([cos, cos, cos, cos], axis=-1)
    sin = jnp.concatenate([-sin, sin, -sin, sin], axis=-1)
    return cos, sin


def kernel(x, meta_tokens, pre_norm_w, w_in, rwkv_mu, rwkv_w0, rwkv_w_up, rwkv_a0, rwkv_a_up, rwkv_k_k, rwkv_k_a, rwkv_r_k, rwkv_gn_w, rwkv_gn_b, diff_lam_q1, diff_lam_k1, diff_lam_q2, diff_lam_k2, diff_subln_w, w_out, post_norm_w):
    batch, seq, d = x.shape
    assert d == D_MODEL and meta_tokens.shape == (N_META, D_MODEL)
    assert seq % RWKV_CHUNK == 0 and seq % ATT_BLOCK == 0
    assert (batch * seq) % PROJ_TM == 0 and seq % PROJ_TM == 0
    layer = 0
    x2 = x.reshape(batch * seq, D_MODEL)

    w = w_in[layer]
    rkv_end = 3 * RWKV_WIDTH
    lora_end = rkv_end + 2 * LORA
    w_main = jnp.concatenate([w[:, :rkv_end], w[:, lora_end:]], axis=1).astype(BF16)
    w_lora_in = w[:, rkv_end:lora_end].astype(BF16)
    g_pre = pre_norm_w[layer].reshape(1, D_MODEL)

    cos_m, sin_m = _rope_tables(0, N_META)
    cos_x, sin_x = _rope_tables(N_META, seq)
    p_meta, lora_meta = _inproj(meta_tokens.astype(x.dtype), g_pre, w_main, w_lora_in, cos_m, sin_m, tm=N_META)
    p_x, lora_x = _inproj(x2, g_pre, w_main, w_lora_in, cos_x, sin_x, tm=PROJ_TM)

    mu = rwkv_mu[layer]
    zeros = jnp.zeros((LORA, RWKV_WIDTH), F32)
    w_lora = jnp.concatenate([
        jnp.concatenate([rwkv_w_up[layer], zeros], axis=1),
        jnp.concatenate([zeros, rwkv_a_up[layer]], axis=1)], axis=0)
    w_lora_hi, w_lora_lo = _split2(w_lora)
    row = lambda t, n: t.reshape(1, n)
    rwkv_params = (
        row(mu[:rkv_end], rkv_end), row(mu[rkv_end:lora_end], LANES),
        row(rwkv_w0[layer], RWKV_WIDTH), row(rwkv_a0[layer], RWKV_WIDTH),
        w_lora_hi, w_lora_lo,
        row(rwkv_k_k[layer], RWKV_WIDTH), row(rwkv_k_a[layer], RWKV_WIDTH),
        row(rwkv_r_k[layer], RWKV_WIDTH), row(rwkv_gn_w[layer], RWKV_WIDTH), row(rwkv_gn_b[layer], RWKV_WIDTH),
    )
    pad = RWKV_CHUNK - N_META
    _, h_meta = _rwkv(jnp.pad(p_meta, ((pad, 0), (0, 0))), jnp.pad(lora_meta, ((pad, 0), (0, 0))),
                      jnp.zeros((8, rkv_end), F32), jnp.zeros((8, LANES), F32),
                      jnp.zeros((1, N_PAIRS, LANES, LANES), F32), rwkv_params, batch=1)
    y_r, _ = _rwkv(p_x, lora_x, p_meta[N_META - 8:, :rkv_end].astype(F32), lora_meta[N_META - 8:],
                   h_meta, rwkv_params, batch=batch)

    lam_vecs = tuple(t[layer].reshape(1, HEAD) for t in (diff_lam_q1, diff_lam_k1, diff_lam_q2, diff_lam_k2))
    vm_t = p_meta[:, COL_V:COL_V + DIFF_WIDTH].reshape(N_META, DIFF_HEADS, LANES).transpose(1, 2, 0)
    vm_t = jnp.concatenate([vm_t, jnp.ones((DIFF_HEADS, ONES_ROWS, N_META), BF16)], axis=1)
    y_d = _attention(p_x, p_meta, vm_t, lam_vecs, diff_subln_w[layer].reshape(1, LANES), batch=batch, seq=seq)

    wo = w_out[layer].astype(BF16)
    out = _outproj(y_r, y_d, wo[:RWKV_WIDTH], wo[RWKV_WIDTH:], x2, post_norm_w[layer].reshape(1, D_MODEL))
    return out.reshape(batch, seq, D_MODEL)
```

```python
import functools
import math

import jax
import jax.numpy as jnp
from jax import lax
from jax.experimental import pallas as pl
from jax.experimental.pallas import tpu as pltpu

F32 = jnp.float32
BF16 = jnp.bfloat16

D_MODEL = 2048
N_META = 16
HEAD = 64
LANES = 128
RWKV_WIDTH = 1024
N_PAIRS = RWKV_WIDTH // LANES
DIFF_WIDTH = 1024
DIFF_HEADS = DIFF_WIDTH // LANES
LORA = 64
ROPE_THETA = 10000.0
NORM_EPS = 1e-6
GN_EPS = 64e-5
SUBLN_EPS = 1e-5
LAMBDA_INIT = 0.8 - 0.6 * math.exp(-0.3 * 0)

COL_RKV = 0
COL_GR = 3 * RWKV_WIDTH
COL_Q = 4 * RWKV_WIDTH
COL_K = COL_Q + DIFF_WIDTH
COL_V = COL_K + DIFF_WIDTH
COL_GD = COL_V + DIFF_WIDTH
P_COLS = COL_GD + DIFF_WIDTH

PROJ_TM = 1024
PROJ_TN = 1024
PROJ_KC = 512
RWKV_CHUNK = 64
RWKV_INTERLEAVE = 4
ATT_BLOCK = 512
ATT_LOOKAHEAD = 1
OUT_TM = 512
OUT_SUB = 128
VMEM_LIMIT = 56 * 1024 * 1024
MASK_VALUE = -1e30
ONES_ROWS = 16
Q_SCALE = HEAD ** -0.5 * math.log2(math.e)


def _dot(a, b):
    return jnp.dot(a, b, preferred_element_type=F32)


def _dot_nt(a, b):
    return lax.dot_general(a, b, (((1,), (1,)), ((), ())), preferred_element_type=F32)


def _dot_tn(a, b):
    return lax.dot_general(a, b, (((0,), (0,)), ((), ())), preferred_element_type=F32)


def _split2(x):
    hi = x.astype(BF16)
    lo = (x - hi.astype(F32)).astype(BF16)
    return hi, lo


def _sigmoid(x):
    return 1.0 / (1.0 + jnp.exp(-x))


def _silu(x):
    return x * _sigmoid(x)


def _inproj_kernel(x_ref, g_ref, wa_ref, wb_ref, wl_ref, cos_ref, sin_ref, o_ref, ol_ref, hn_ref, rs_ref,
                   *, tn, n_a):
    j = pl.program_id(1)
    q_tile = COL_Q // tn
    k_tile = COL_K // tn
    assert 0 < n_a <= q_tile < k_tile
    is_rope = jnp.logical_or(j == q_tile, j == k_tile)
    col_blocks = [slice(c * LANES, (c + 1) * LANES) for c in range(tn // LANES)]

    def scaled_store(w_ref):
        acc = _dot(hn_ref[...], w_ref[...])
        rs = rs_ref[...]
        for cb in col_blocks:
            o_ref[:, cb] = (acc[:, cb] * rs).astype(o_ref.dtype)

    @pl.when(j == 0)
    def _():
        ssq = None
        acc = None
        for k in range(D_MODEL // PROJ_KC):
            sl = slice(k * PROJ_KC, (k + 1) * PROJ_KC)
            xc = x_ref[:, sl]
            part = jnp.sum(xc * xc, axis=-1, keepdims=True)
            ssq = part if ssq is None else ssq + part
            hn = (xc * g_ref[:, sl]).astype(BF16)
            hn_ref[:, sl] = hn
            d = _dot(hn, wa_ref[sl, :])
            acc = d if acc is None else acc + d
        rs = jnp.broadcast_to(lax.rsqrt(ssq * (1.0 / D_MODEL) + NORM_EPS), rs_ref.shape)
        rs_ref[...] = rs
        ol_ref[...] = _dot(hn_ref[...], wl_ref[...]) * rs
        for cb in col_blocks:
            o_ref[:, cb] = (acc[:, cb] * rs).astype(o_ref.dtype)

    @pl.when(is_rope)
    def _():
        acc = _dot(hn_ref[...], wb_ref[...])
        cos = cos_ref[...]
        sin = sin_ref[...]
        lane = lax.broadcasted_iota(jnp.int32, cos.shape, 1)
        first_half = (lane % HEAD) < (HEAD // 2)
        scale = rs_ref[...] * jnp.where(j == q_tile, Q_SCALE, 1.0).astype(F32)
        for cb in col_blocks:
            blk = acc[:, cb]
            swapped = jnp.where(first_half, pltpu.roll(blk, LANES - HEAD // 2, 1),
                                pltpu.roll(blk, HEAD // 2, 1))
            o_ref[:, cb] = ((blk * cos + swapped * sin) * scale).astype(o_ref.dtype)

    @pl.when(jnp.logical_and(j > 0, j < n_a))
    def _():
        scaled_store(wa_ref)

    @pl.when(jnp.logical_and(j >= n_a, jnp.logical_not(is_rope)))
    def _():
        scaled_store(wb_ref)


def _inproj(x2, g, w_a, w_b, w_lora, cos, sin, *, tm):
    m = x2.shape[0]
    tn = PROJ_TN
    n_a = w_a.shape[1] // tn
    assert w_a.shape[1] % tn == 0 and w_a.shape[1] + w_b.shape[1] == P_COLS
    n_pos_tiles = cos.shape[0] // tm
    return pl.pallas_call(
        functools.partial(_inproj_kernel, tn=tn, n_a=n_a),
        out_shape=(jax.ShapeDtypeStruct((m, P_COLS), BF16), jax.ShapeDtypeStruct((m, LANES), F32)),
        grid=(m // tm, P_COLS // tn),
        in_specs=[
            pl.BlockSpec((tm, D_MODEL), lambda i, j: (i, 0)),
            pl.BlockSpec((1, D_MODEL), lambda i, j: (0, 0)),
            pl.BlockSpec((D_MODEL, tn), lambda i, j: (0, jnp.minimum(j, n_a - 1))),
            pl.BlockSpec((D_MODEL, tn), lambda i, j: (0, jnp.maximum(j - n_a, 0))),
            pl.BlockSpec((D_MODEL, LANES), lambda i, j: (0, 0)),
            pl.BlockSpec((tm, LANES), lambda i, j: (i % n_pos_tiles, 0)),
            pl.BlockSpec((tm, LANES), lambda i, j: (i % n_pos_tiles, 0)),
        ],
        out_specs=(
            pl.BlockSpec((tm, tn), lambda i, j: (i, j)),
            pl.BlockSpec((tm, LANES), lambda i, j: (i, 0)),
        ),
        scratch_shapes=[pltpu.VMEM((tm, D_MODEL), BF16), pltpu.VMEM((tm, LANES), F32)],
        compiler_params=pltpu.CompilerParams(
            dimension_semantics=("arbitrary", "arbitrary"), vmem_limit_bytes=VMEM_LIMIT),
        name="inproj",
    )(x2, g, w_a, w_b, w_lora, cos, sin)


def _rwkv_kernel(rkv_ref, g_ref, lora_ref, pinit_rkv_ref, pinit_lora_ref, hinit_ref,
                 mu_rkv_ref, mu_lora_ref, w0_ref, a0_ref, wl_hi_ref, wl_lo_ref,
                 kk_ref, ka_ref, rk_ref, gnw_ref, gnb_ref,
                 y_ref, hout_ref,
                 h_scr, prev_rkv, prev_lora,
                 st_lhs, st_rhs, st_vs, st_v, st_kh, st_bh, st_gamma, st_bonus, *, n_chunks):
    C = RWKV_CHUNK
    S = 2 * C
    assert S == LANES
    b_id = pl.program_id(0)
    c = pl.program_id(1)
    pairs = range(N_PAIRS)

    @pl.when(c == 0)
    def _():
        prev_rkv[...] = pinit_rkv_ref[...]
        prev_lora[...] = pinit_lora_ref[...]

    @pl.when(c <= 1)
    def _():
        h_scr[...] = hinit_ref[0]

    @pl.when(jnp.logical_and(b_id == 0, c == 0))
    def _():
        for ref in (st_lhs, st_rhs, st_vs, st_v, st_kh, st_bh, st_gamma, st_bonus):
            ref[1] = jnp.zeros(ref.shape[1:], ref.dtype)

    def cols(p, base=0):
        return slice(base + p * LANES, base + (p + 1) * LANES)

    def make_stack():
        head0 = lax.broadcasted_iota(jnp.int32, (C, LANES), 1) < HEAD

        def stack(t):
            return jnp.concatenate([jnp.where(head0, t, 0.0), jnp.where(head0, 0.0, t)], axis=0).astype(BF16)
        return head0, stack

    def make_head_sum():
        li = lax.broadcasted_iota(jnp.int32, (2 * LANES, LANES), 0)
        lj = lax.broadcasted_iota(jnp.int32, (2 * LANES, LANES), 1)
        ones_bd2 = jnp.where(((li % LANES) // HEAD) == (lj // HEAD), 1.0, 0.0).astype(BF16)

        def head_sum(t):
            rows = jnp.concatenate([t[:, cols(p)] for p in pairs], axis=0)
            hi, lo = _split2(rows)
            s = _dot(jnp.concatenate([hi, lo], axis=1), ones_bd2)
            return jnp.concatenate([s[p * C:(p + 1) * C] for p in pairs], axis=1)
        return head_sum

    def prepare(slot):
        head0, stack = make_stack()
        head_sum = make_head_sum()
        row = lax.broadcasted_iota(jnp.int32, (C, 1), 0)

        def token_shift(x, prev8, mu):
            xp = jnp.where(row == 0, prev8[7:8, :], pltpu.roll(x, 1, 0))
            return x + (xp - x) * mu

        xl = lora_ref[...]
        ul = token_shift(xl, prev_lora[...], mu_lora_ref[...])
        prev_lora[...] = xl[C - 8:C]
        tl_hi, tl_lo = _split2(jnp.where(head0, jnp.tanh(ul), ul))
        wl_hi = wl_hi_ref[...]
        lo_out = _dot(tl_hi, wl_hi) + _dot(tl_lo, wl_hi) + _dot(tl_hi, wl_lo_ref[...])
        yield

        def shifted(base):
            sl = slice(base, base + RWKV_WIDTH)
            xs = rkv_ref[:, sl].astype(F32)
            out = token_shift(xs, prev_rkv[:, sl], mu_rkv_ref[:, sl])
            prev_rkv[:, sl] = xs[C - 8:C]
            return out

        r = shifted(0)
        yield
        k = shifted(RWKV_WIDTH)
        yield
        v = shifted(2 * RWKV_WIDTH)
        yield

        logdec = -math.exp(-0.5) * _sigmoid(w0_ref[...] + lo_out[:, 0:RWKV_WIDTH])
        yield
        a_lr = _sigmoid(a0_ref[...] + lo_out[:, RWKV_WIDTH:2 * RWKV_WIDTH])
        yield

        ti = lax.broadcasted_iota(jnp.int32, (C, C), 0)
        si = lax.broadcasted_iota(jnp.int32, (C, C), 1)
        tri = jnp.where(ti >= si, 1.0, 0.0).astype(BF16)
        ld_hi = logdec.astype(BF16)
        rem = logdec - ld_hi.astype(F32)
        ld_mid = rem.astype(BF16)
        ld_lo = (rem - ld_mid.astype(F32)).astype(BF16)
        cum = _dot(tri, ld_hi) + _dot(tri, ld_mid) + _dot(tri, ld_lo)
        cum_last = cum[C - 1:C, :]
        yield
        e_excl = jnp.exp(cum - logdec)
        e_incl = jnp.exp(cum)
        yield
        e_neg = jnp.exp(-cum)
        e_hat = jnp.exp(cum_last - cum)
        yield

        kk = k * kk_ref[...]
        kk = kk * lax.rsqrt(jnp.maximum(head_sum(kk * kk), 1e-24))
        yield
        k2 = k * (1.0 + (a_lr - 1.0) * ka_ref[...])
        b = kk * a_lr
        yield
        xa_all = -kk * e_excl
        xr_all = r * e_incl
        yield
        yb_all = b * e_neg
        yk_all = k2 * e_neg
        yield

        st_gamma[slot] = jnp.exp(cum_last)
        st_bonus[slot] = head_sum(r * k2 * rk_ref[...]) * v
        yield
        st_v[slot] = v.astype(BF16)
        st_kh[slot] = (k2 * e_hat).astype(BF16)
        st_bh[slot] = (b * e_hat).astype(BF16)
        yield
        for p in pairs:
            sl = cols(p)
            st_lhs[slot, p] = jnp.concatenate([xa_all[:, sl], xr_all[:, sl]], axis=0).astype(BF16)
            st_rhs[slot, p] = jnp.concatenate([stack(yb_all[:, sl]), stack(yk_all[:, sl])], axis=0)
            st_vs[slot, p] = stack(v[:, sl])
            yield

    def recur(slot):
        _, stack = make_stack()
        head_sum = make_head_sum()
        ri = lax.broadcasted_iota(jnp.int32, (S, 2 * S), 0)
        ci = lax.broadcasted_iota(jnp.int32, (S, 2 * S), 1)
        a_mask = (ri % C) >= (ci % C) + jnp.where(ri < C, 1, 0)
        ei = lax.broadcasted_iota(jnp.int32, (C, S), 0)
        ej = lax.broadcasted_iota(jnp.int32, (C, S), 1)
        eye = jnp.where(ei == ej % C, 1.0, 0.0).astype(F32)
        bi = lax.broadcasted_iota(jnp.int32, (LANES, LANES), 0)
        bj = lax.broadcasted_iota(jnp.int32, (LANES, LANES), 1)
        same_head = (bi // HEAD) == (bj // HEAD)

        def stage(fn):
            out = []
            for p in pairs:
                out.append(fn(p))
                yield
            return out

        lhs_ar = [st_lhs[slot, p] for p in pairs]
        vs = [st_vs[slot, p] for p in pairs]

        a_all = yield from stage(lambda p: jnp.where(a_mask, _dot_nt(lhs_ar[p], st_rhs[slot, p]), 0.0))
        a_ab = [a_all[p][0:C, 0:S] for p in pairs]
        a_x_v = yield from stage(lambda p: _dot(a_all[p][:, S:2 * S].astype(BF16), vs[p]))

        inv = [eye + a_ab[p] for p in pairs]
        apow = yield from stage(lambda p: _dot(a_ab[p].astype(BF16), stack(a_ab[p])))
        n = 2
        while 2 * n < C:
            res = yield from stage(
                lambda p: _dot(jnp.concatenate([inv[p], apow[p]], axis=0).astype(BF16), stack(apow[p])))
            inv = [inv[p] + res[p][0:C] for p in pairs]
            apow = [res[p][C:S] for p in pairs]
            n *= 2
        inv = yield from stage(lambda p: inv[p] + _dot(inv[p].astype(BF16), stack(apow[p])))

        s_old = [h_scr[p] for p in pairs]
        x_s = yield from stage(lambda p: _dot_nt(lhs_ar[p], s_old[p].astype(BF16)))
        us = yield from stage(lambda p: _dot(inv[p].astype(BF16), stack(x_s[p][0:C] + a_x_v[p][0:C])))
        ys = yield from stage(
            lambda p: x_s[p][C:S] + a_x_v[p][C:S] + _dot(a_all[p][C:S, 0:S].astype(BF16), stack(us[p])))

        def update(p):
            sl = cols(p)
            vu = jnp.concatenate([st_v[slot, :, sl], us[p].astype(BF16)], axis=0)
            khb = jnp.concatenate([st_kh[slot, :, sl], st_bh[slot, :, sl]], axis=0)
            h_scr[p] = s_old[p] * st_gamma[slot, :, sl] + jnp.where(same_head, _dot_tn(vu, khb), 0.0)

        yield from stage(update)

        y = jnp.concatenate(ys, axis=1)
        yc = y - head_sum(y) * (1.0 / HEAD)
        yield
        var = head_sum(yc * yc) * (1.0 / HEAD)
        y = yc * lax.rsqrt(var + GN_EPS) * gnw_ref[...] + gnb_ref[...] + st_bonus[slot]
        yield
        y_ref[...] = (y * _silu(g_ref[...].astype(F32))).astype(y_ref.dtype)

    def interleave(chain, filler, chain_steps_per_filler_step):
        active = [chain, filler]
        while active:
            for gen in list(active):
                for _ in range(chain_steps_per_filler_step if gen is chain else 1):
                    if next(gen, "done") == "done":
                        active.remove(gen)
                        break

    odd = jnp.bitwise_and(c, 1) == 1

    @pl.when(jnp.logical_not(odd))
    def _():
        interleave(recur(1), prepare(0), RWKV_INTERLEAVE)

    @pl.when(odd)
    def _():
        interleave(recur(0), prepare(1), RWKV_INTERLEAVE)

    @pl.when(c == n_chunks)
    def _():
        hout_ref[0] = h_scr[...]


def _rwkv(p_arr, lora_arr, pinit_rkv, pinit_lora, hinit, params, *, batch):
    C = RWKV_CHUNK
    rows = p_arr.shape[0]
    n_chunks = rows // (batch * C)
    const2 = lambda b, c: (0, 0)
    vec = lambda width: pl.BlockSpec((1, width), const2)
    prep_chunk = lambda b, c: b * n_chunks + jnp.minimum(c, n_chunks - 1)
    recur_chunk = lambda b, c: b * n_chunks + jnp.maximum(c - 1, 0)
    slots = lambda shape, dtype: pltpu.VMEM((2,) + shape, dtype)
    return pl.pallas_call(
        functools.partial(_rwkv_kernel, n_chunks=n_chunks),
        out_shape=(jax.ShapeDtypeStruct((rows, RWKV_WIDTH), BF16),
                   jax.ShapeDtypeStruct((batch, N_PAIRS, LANES, LANES), F32)),
        grid=(batch, n_chunks + 1),
        in_specs=[
            pl.BlockSpec((C, 3 * RWKV_WIDTH), lambda b, c: (prep_chunk(b, c), COL_RKV // (3 * RWKV_WIDTH))),
            pl.BlockSpec((C, RWKV_WIDTH), lambda b, c: (recur_chunk(b, c), COL_GR // RWKV_WIDTH)),
            pl.BlockSpec((C, LANES), lambda b, c: (prep_chunk(b, c), 0)),
            pl.BlockSpec((8, 3 * RWKV_WIDTH), const2),
            pl.BlockSpec((8, LANES), const2),
            pl.BlockSpec((1, N_PAIRS, LANES, LANES), lambda b, c: (0, 0, 0, 0)),
            vec(3 * RWKV_WIDTH), vec(LANES), vec(RWKV_WIDTH), vec(RWKV_WIDTH),
            pl.BlockSpec((LANES, 2 * RWKV_WIDTH), const2),
            pl.BlockSpec((LANES, 2 * RWKV_WIDTH), const2),
            vec(RWKV_WIDTH), vec(RWKV_WIDTH), vec(RWKV_WIDTH), vec(RWKV_WIDTH), vec(RWKV_WIDTH),
        ],
        out_specs=(
            pl.BlockSpec((C, RWKV_WIDTH), lambda b, c: (recur_chunk(b, c), 0)),
            pl.BlockSpec((1, N_PAIRS, LANES, LANES), lambda b, c: (b, 0, 0, 0)),
        ),
        scratch_shapes=[
            pltpu.VMEM((N_PAIRS, LANES, LANES), F32),
            pltpu.VMEM((8, 3 * RWKV_WIDTH), F32),
            pltpu.VMEM((8, LANES), F32),
            slots((N_PAIRS, 2 * C, LANES), BF16),
            slots((N_PAIRS, 4 * C, LANES), BF16),
            slots((N_PAIRS, 2 * C, LANES), BF16),
            slots((C, RWKV_WIDTH), BF16),
            slots((C, RWKV_WIDTH), BF16),
            slots((C, RWKV_WIDTH), BF16),
            slots((1, RWKV_WIDTH), F32),
            slots((C, RWKV_WIDTH), F32),
        ],
        compiler_params=pltpu.CompilerParams(
            dimension_semantics=("arbitrary", "arbitrary"), vmem_limit_bytes=VMEM_LIMIT),
        name="rwkv7_chunk",
    )(p_arr, p_arr, lora_arr, pinit_rkv, pinit_lora, hinit, *params)


def _attn_kernel(q_ref, k_ref, v_ref, g_ref, km_ref, vmt_ref, lq1_ref, lk1_ref, lq2_ref, lk2_ref, sw_ref,
                 o_ref, vt_scr, acc_scr, s_scr, *, t, n_blk):
    for c in range(n_blk):
        vt_scr[c, 0:LANES, :] = v_ref[c * t:(c + 1) * t, :].astype(F32).T.astype(BF16)
        vt_scr[c, LANES:LANES + ONES_ROWS, :] = jnp.ones((ONES_ROWS, t), BF16)

    comp0 = lax.broadcasted_iota(jnp.int32, (LANES, t), 0) < HEAD
    qst = []
    for i in range(n_blk):
        qt = q_ref[i * t:(i + 1) * t, :].astype(F32).T
        qst.append(jnp.concatenate([jnp.where(comp0, qt, 0.0), jnp.where(comp0, 0.0, qt)], axis=1).astype(BF16))

    m = []
    for i in range(n_blk):
        s = _dot(km_ref[...], qst[i])
        m.append(jnp.max(s, axis=0, keepdims=True))
        acc_scr[i] = _dot(vmt_ref[0], jnp.exp2(s - m[i]).astype(BF16))

    order = sorted(((i, j) for i in range(n_blk) for j in range(i + 1)), key=lambda ij: (ij[1], ij[0]))
    n_slots = s_scr.shape[0]

    def issue(n):
        i, j = order[n]
        s_scr[n % n_slots] = _dot(k_ref[j * t:(j + 1) * t, :], qst[i])

    causal = (lax.broadcasted_iota(jnp.int32, (t, 2 * t), 0)
              <= lax.broadcasted_iota(jnp.int32, (t, 2 * t), 1) % t)
    for n in range(min(ATT_LOOKAHEAD, len(order))):
        issue(n)
    for n, (i, j) in enumerate(order):
        if n + ATT_LOOKAHEAD < len(order):
            issue(n + ATT_LOOKAHEAD)
        s = s_scr[n % n_slots]
        if i == j:
            s = jnp.where(causal, s, MASK_VALUE)
        m_new = jnp.maximum(m[i], jnp.max(s, axis=0, keepdims=True))
        alpha = jnp.exp2(m[i] - m_new)
        m[i] = m_new
        acc_scr[i] = alpha * acc_scr[i] + _dot(vt_scr[j], jnp.exp2(s - m_new).astype(BF16))

    lam = (jnp.exp(jnp.sum(lq1_ref[...] * lk1_ref[...], axis=-1, keepdims=True))
           - jnp.exp(jnp.sum(lq2_ref[...] * lk2_ref[...], axis=-1, keepdims=True)) + LAMBDA_INIT)
    for i in range(n_blk):
        on = acc_scr[i, 0:LANES, :] / acc_scr[i, LANES:LANES + 1, :]
        ot = on[:, 0:t] - lam * on[:, t:2 * t]
        ot = ot * lax.rsqrt(jnp.mean(ot * ot, axis=0, keepdims=True) + SUBLN_EPS)
        g = g_ref[i * t:(i + 1) * t, :].astype(F32)
        o_ref[i * t:(i + 1) * t, :] = (ot.T * sw_ref[...] * (1.0 - LAMBDA_INIT) * _silu(g)).astype(o_ref.dtype)


def _attention(p_arr, p_meta, vm_t, lam_vecs, subln_w, *, batch, seq):
    t = ATT_BLOCK
    n_blk = seq // t
    lanes_blk = lambda col: col // LANES
    small = pl.BlockSpec((1, HEAD), lambda b, h: (0, 0))
    head_cols = lambda col: pl.BlockSpec((seq, LANES), lambda b, h: (b, lanes_blk(col) + h))
    return pl.pallas_call(
        functools.partial(_attn_kernel, t=t, n_blk=n_blk),
        out_shape=jax.ShapeDtypeStruct((batch * seq, DIFF_WIDTH), BF16),
        grid=(batch, DIFF_HEADS),
        in_specs=[
            head_cols(COL_Q), head_cols(COL_K), head_cols(COL_V), head_cols(COL_GD),
            pl.BlockSpec((N_META, LANES), lambda b, h: (0, lanes_blk(COL_K) + h)),
            pl.BlockSpec((1, LANES + ONES_ROWS, N_META), lambda b, h: (h, 0, 0)),
            small, small, small, small,
            pl.BlockSpec((1, LANES), lambda b, h: (0, 0)),
        ],
        out_specs=pl.BlockSpec((seq, LANES), lambda b, h: (b, h)),
        scratch_shapes=[
            pltpu.VMEM((n_blk, LANES + ONES_ROWS, t), BF16),
            pltpu.VMEM((n_blk, LANES + ONES_ROWS, 2 * t), F32),
            pltpu.VMEM((ATT_LOOKAHEAD + 1, t, 2 * t), F32),
        ],
        compiler_params=pltpu.CompilerParams(
            dimension_semantics=("arbitrary", "arbitrary"), vmem_limit_bytes=VMEM_LIMIT),
        name="diff_attn",
    )(p_arr, p_arr, p_arr, p_arr, p_meta, vm_t, *lam_vecs, subln_w)


def _outproj_kernel(yr_ref, yd_ref, w1_ref, w2_ref, x_ref, g_ref, o_ref):
    for r in range(yr_ref.shape[0] // OUT_SUB):
        rows = slice(r * OUT_SUB, (r + 1) * OUT_SUB)
        y = _dot(yr_ref[rows, :], w1_ref[...]) + _dot(yd_ref[rows, :], w2_ref[...])
        ms = jnp.mean(y * y, axis=-1, keepdims=True)
        o_ref[rows, :] = x_ref[rows, :] + y * lax.rsqrt(ms + NORM_EPS) * g_ref[...]


def _outproj(y_r, y_d, w1, w2, x2, g):
    m = x2.shape[0]
    tm = OUT_TM
    return pl.pallas_call(
        _outproj_kernel,
        out_shape=jax.ShapeDtypeStruct((m, D_MODEL), F32),
        grid=(m // tm,),
        in_specs=[
            pl.BlockSpec((tm, RWKV_WIDTH), lambda i: (i, 0)),
            pl.BlockSpec((tm, DIFF_WIDTH), lambda i: (i, 0)),
            pl.BlockSpec((RWKV_WIDTH, D_MODEL), lambda i: (0, 0)),
            pl.BlockSpec((DIFF_WIDTH, D_MODEL), lambda i: (0, 0)),
            pl.BlockSpec((tm, D_MODEL), lambda i: (i, 0)),
            pl.BlockSpec((1, D_MODEL), lambda i: (0, 0)),
        ],
        out_specs=pl.BlockSpec((tm, D_MODEL), lambda i: (i, 0)),
        compiler_params=pltpu.CompilerParams(
            dimension_semantics=("arbitrary",), vmem_limit_bytes=VMEM_LIMIT),
        name="outproj",
    )(y_r, y_d, w1, w2, x2, g)


def _rope_tables(first_pos, n_pos):
    pos = jnp.arange(first_pos, first_pos + n_pos, dtype=F32)
    inv_freq = ROPE_THETA ** (-jnp.arange(0, HEAD, 2, dtype=F32) / HEAD)
    ang = pos[:, None] * inv_freq[None, :]
    cos = jnp.cos(ang)
    sin = jnp.sin(ang)
    cos = jnp.concatenate([cos, cos, cos, cos], axis=-1)
    sin = jnp.concatenate([-sin, sin, -sin, sin], axis=-1)
    return cos, sin


def kernel(x, meta_tokens, pre_norm_w, w_in, rwkv_mu, rwkv_w0, rwkv_w_up, rwkv_a0, rwkv_a_up, rwkv_k_k, rwkv_k_a, rwkv_r_k, rwkv_gn_w, rwkv_gn_b, diff_lam_q1, diff_lam_k1, diff_lam_q2, diff_lam_k2, diff_subln_w, w_out, post_norm_w):
    batch, seq, d = x.shape
    assert d == D_MODEL and meta_tokens.shape == (N_META, D_MODEL)
    assert seq % RWKV_CHUNK == 0 and seq % ATT_BLOCK == 0
    assert (batch * seq) % PROJ_TM == 0 and seq % PROJ_TM == 0
    layer = 0
    x2 = x.reshape(batch * seq, D_MODEL)

    w = w_in[layer]
    rkv_end = 3 * RWKV_WIDTH
    lora_end = rkv_end + 2 * LORA
    w_a = w[:, :rkv_end].astype(BF16)
    w_b = w[:, lora_end:].astype(BF16)
    w_lora_in = w[:, rkv_end:lora_end].astype(BF16)
    g_pre = pre_norm_w[layer].reshape(1, D_MODEL)

    cos_m, sin_m = _rope_tables(0, N_META)
    cos_x, sin_x = _rope_tables(N_META, seq)
    p_meta, lora_meta = _inproj(meta_tokens.astype(x.dtype), g_pre, w_a, w_b, w_lora_in, cos_m, sin_m, tm=N_META)
    p_x, lora_x = _inproj(x2, g_pre, w_a, w_b, w_lora_in, cos_x, sin_x, tm=PROJ_TM)

    mu = rwkv_mu[layer]
    zeros = jnp.zeros((LORA, RWKV_WIDTH), F32)
    w_lora = jnp.concatenate([
        jnp.concatenate([rwkv_w_up[layer], zeros], axis=1),
        jnp.concatenate([zeros, rwkv_a_up[layer]], axis=1)], axis=0)
    w_lora_hi, w_lora_lo = _split2(w_lora)
    row = lambda t, n: t.reshape(1, n)
    rwkv_params = (
        row(mu[:rkv_end], rkv_end), row(mu[rkv_end:lora_end], LANES),
        row(rwkv_w0[layer], RWKV_WIDTH), row(rwkv_a0[layer], RWKV_WIDTH),
        w_lora_hi, w_lora_lo,
        row(rwkv_k_k[layer], RWKV_WIDTH), row(rwkv_k_a[layer], RWKV_WIDTH),
        row(rwkv_r_k[layer], RWKV_WIDTH), row(rwkv_gn_w[layer], RWKV_WIDTH), row(rwkv_gn_b[layer], RWKV_WIDTH),
    )
    pad = RWKV_CHUNK - N_META
    _, h_meta = _rwkv(jnp.pad(p_meta, ((pad, 0), (0, 0))), jnp.pad(lora_meta, ((pad, 0), (0, 0))),
                      jnp.zeros((8, rkv_end), F32), jnp.zeros((8, LANES), F32),
                      jnp.zeros((1, N_PAIRS, LANES, LANES), F32), rwkv_params, batch=1)
    y_r, _ = _rwkv(p_x, lora_x, p_meta[N_META - 8:, :rkv_end].astype(F32), lora_meta[N_META - 8:],
                   h_meta, rwkv_params, batch=batch)

    lam_vecs = tuple(t[layer].reshape(1, HEAD) for t in (diff_lam_q1, diff_lam_k1, diff_lam_q2, diff_lam_k2))
    vm_t = p_meta[:, COL_V:COL_V + DIFF_WIDTH].reshape(N_META, DIFF_HEADS, LANES).transpose(1, 2, 0)
    vm_t = jnp.concatenate([vm_t, jnp.ones((DIFF_HEADS, ONES_ROWS, N_META), BF16)], axis=1)
    y_d = _attention(p_x, p_meta, vm_t, lam_vecs, diff_subln_w[layer].reshape(1, LANES), batch=batch, seq=seq)

    wo = w_out[layer].astype(BF16)
    out = _outproj(y_r, y_d, wo[:RWKV_WIDTH], wo[RWKV_WIDTH:], x2, post_norm_w[layer].reshape(1, D_MODEL))
    return out.reshape(batch, seq, D_MODEL)
```

```python
import functools
import math

import jax
import jax.numpy as jnp
from jax import lax
from jax.experimental import pallas as pl
from jax.experimental.pallas import tpu as pltpu

F32 = jnp.float32
BF16 = jnp.bfloat16

D_MODEL = 2048
N_META = 16
HEAD = 64
LANES = 128
RWKV_WIDTH = 1024
N_PAIRS = RWKV_WIDTH // LANES
DIFF_WIDTH = 1024
DIFF_HEADS = DIFF_WIDTH // LANES
LORA = 64
ROPE_THETA = 10000.0
NORM_EPS = 1e-6
GN_EPS = 64e-5
SUBLN_EPS = 1e-5
LAMBDA_INIT = 0.8 - 0.6 * math.exp(-0.3 * 0)

COL_RKV = 0
COL_GR = 3 * RWKV_WIDTH
COL_Q = 4 * RWKV_WIDTH
COL_K = COL_Q + DIFF_WIDTH
COL_V = COL_K + DIFF_WIDTH
COL_GD = COL_V + DIFF_WIDTH
P_COLS = COL_GD + DIFF_WIDTH

PROJ_TM = 1024
PROJ_TN = 1024
PROJ_KC = 512
RWKV_CHUNK = 64
RWKV_INTERLEAVE = 4
ATT_BLOCK = 512
ATT_LOOKAHEAD = 1
OUT_TM = 256
VMEM_LIMIT = 48 * 1024 * 1024
MASK_VALUE = -1e30
ONES_ROWS = 16
Q_SCALE = HEAD ** -0.5 * math.log2(math.e)


def _dot(a, b):
    return jnp.dot(a, b, preferred_element_type=F32)


def _dot_nt(a, b):
    return lax.dot_general(a, b, (((1,), (1,)), ((), ())), preferred_element_type=F32)


def _dot_tn(a, b):
    return lax.dot_general(a, b, (((0,), (0,)), ((), ())), preferred_element_type=F32)


def _split2(x):
    hi = x.astype(BF16)
    lo = (x - hi.astype(F32)).astype(BF16)
    return hi, lo


def _sigmoid(x):
    return 1.0 / (1.0 + jnp.exp(-x))


def _silu(x):
    return x * _sigmoid(x)


def _interleave(chain, filler, chain_steps_per_filler_step):
    active = [chain, filler]
    while active:
        for gen in list(active):
            for _ in range(chain_steps_per_filler_step if gen is chain else 1):
                if next(gen, "done") == "done":
                    active.remove(gen)
                    break


def _inproj_kernel(x_ref, g_ref, w_ref, wl_ref, cos_ref, sin_ref, o_ref, ol_ref, hn_ref, rs_ref, *, tn):
    j = pl.program_id(1)
    q_tile = COL_Q // tn
    k_tile = COL_K // tn
    assert 0 < q_tile < k_tile
    is_rope = jnp.logical_or(j == q_tile, j == k_tile)
    col_blocks = [slice(c * LANES, (c + 1) * LANES) for c in range(tn // LANES)]

    @pl.when(j == 0)
    def _():
        ssq = None
        acc = None
        for k in range(D_MODEL // PROJ_KC):
            sl = slice(k * PROJ_KC, (k + 1) * PROJ_KC)
            xc = x_ref[:, sl]
            part = jnp.sum(xc * xc, axis=-1, keepdims=True)
            ssq = part if ssq is None else ssq + part
            hn = (xc * g_ref[:, sl]).astype(BF16)
            hn_ref[:, sl] = hn
            d = _dot(hn, w_ref[sl, :])
            acc = d if acc is None else acc + d
        rs = jnp.broadcast_to(lax.rsqrt(ssq * (1.0 / D_MODEL) + NORM_EPS), rs_ref.shape)
        rs_ref[...] = rs
        ol_ref[...] = _dot(hn_ref[...], wl_ref[...]) * rs
        for cb in col_blocks:
            o_ref[:, cb] = (acc[:, cb] * rs).astype(o_ref.dtype)

    @pl.when(is_rope)
    def _():
        acc = _dot(hn_ref[...], w_ref[...])
        cos = cos_ref[...]
        sin = sin_ref[...]
        lane = lax.broadcasted_iota(jnp.int32, cos.shape, 1)
        first_half = (lane % HEAD) < (HEAD // 2)
        scale = rs_ref[...] * jnp.where(j == q_tile, Q_SCALE, 1.0).astype(F32)
        for cb in col_blocks:
            blk = acc[:, cb]
            swapped = jnp.where(first_half, pltpu.roll(blk, LANES - HEAD // 2, 1),
                                pltpu.roll(blk, HEAD // 2, 1))
            o_ref[:, cb] = ((blk * cos + swapped * sin) * scale).astype(o_ref.dtype)

    @pl.when(jnp.logical_and(j > 0, jnp.logical_not(is_rope)))
    def _():
        acc = _dot(hn_ref[...], w_ref[...])
        rs = rs_ref[...]
        for cb in col_blocks:
            o_ref[:, cb] = (acc[:, cb] * rs).astype(o_ref.dtype)


def _inproj(x2, g, w_main, w_lora, cos, sin, *, tm):
    m = x2.shape[0]
    tn = PROJ_TN
    n_pos_tiles = cos.shape[0] // tm
    return pl.pallas_call(
        functools.partial(_inproj_kernel, tn=tn),
        out_shape=(jax.ShapeDtypeStruct((m, P_COLS), BF16), jax.ShapeDtypeStruct((m, LANES), F32)),
        grid=(m // tm, P_COLS // tn),
        in_specs=[
            pl.BlockSpec((tm, D_MODEL), lambda i, j: (i, 0)),
            pl.BlockSpec((1, D_MODEL), lambda i, j: (0, 0)),
            pl.BlockSpec((D_MODEL, tn), lambda i, j: (0, j)),
            pl.BlockSpec((D_MODEL, LANES), lambda i, j: (0, 0)),
            pl.BlockSpec((tm, LANES), lambda i, j: (i % n_pos_tiles, 0)),
            pl.BlockSpec((tm, LANES), lambda i, j: (i % n_pos_tiles, 0)),
        ],
        out_specs=(
            pl.BlockSpec((tm, tn), lambda i, j: (i, j)),
            pl.BlockSpec((tm, LANES), lambda i, j: (i, 0)),
        ),
        scratch_shapes=[pltpu.VMEM((tm, D_MODEL), BF16), pltpu.VMEM((tm, LANES), F32)],
        compiler_params=pltpu.CompilerParams(
            dimension_semantics=("arbitrary", "arbitrary"), vmem_limit_bytes=VMEM_LIMIT),
        name="inproj",
    )(x2, g, w_main, w_lora, cos, sin)


def _rwkv_kernel(rkv_ref, g_ref, lora_ref, pinit_rkv_ref, pinit_lora_ref, hinit_ref,
                 mu_rkv_ref, mu_lora_ref, w0_ref, a0_ref, wl_hi_ref, wl_lo_ref,
                 kk_ref, ka_ref, rk_ref, gnw_ref, gnb_ref,
                 y_ref, hout_ref,
                 h_scr, prev_rkv, prev_lora,
                 st_lhs, st_rhs, st_vs, st_v, st_kh, st_bh, st_gamma, st_bonus, *, n_chunks):
    C = RWKV_CHUNK
    S = 2 * C
    assert S == LANES
    b_id = pl.program_id(0)
    c = pl.program_id(1)
    pairs = range(N_PAIRS)

    @pl.when(c == 0)
    def _():
        prev_rkv[...] = pinit_rkv_ref[...]
        prev_lora[...] = pinit_lora_ref[...]

    @pl.when(c <= 1)
    def _():
        h_scr[...] = hinit_ref[0]

    @pl.when(jnp.logical_and(b_id == 0, c == 0))
    def _():
        for ref in (st_lhs, st_rhs, st_vs, st_v, st_kh, st_bh, st_gamma, st_bonus):
            ref[1] = jnp.zeros(ref.shape[1:], ref.dtype)

    def cols(p, base=0):
        return slice(base + p * LANES, base + (p + 1) * LANES)

    def make_stack():
        head0 = lax.broadcasted_iota(jnp.int32, (C, LANES), 1) < HEAD

        def stack(t):
            return jnp.concatenate([jnp.where(head0, t, 0.0), jnp.where(head0, 0.0, t)], axis=0).astype(BF16)
        return head0, stack

    def make_head_sum():
        li = lax.broadcasted_iota(jnp.int32, (2 * LANES, LANES), 0)
        lj = lax.broadcasted_iota(jnp.int32, (2 * LANES, LANES), 1)
        ones_bd2 = jnp.where(((li % LANES) // HEAD) == (lj // HEAD), 1.0, 0.0).astype(BF16)

        def head_sum(t):
            rows = jnp.concatenate([t[:, cols(p)] for p in pairs], axis=0)
            hi, lo = _split2(rows)
            s = _dot(jnp.concatenate([hi, lo], axis=1), ones_bd2)
            return jnp.concatenate([s[p * C:(p + 1) * C] for p in pairs], axis=1)
        return head_sum

    def prepare(slot):
        head0, stack = make_stack()
        head_sum = make_head_sum()
        row = lax.broadcasted_iota(jnp.int32, (C, 1), 0)

        def token_shift(x, prev8, mu):
            xp = jnp.where(row == 0, prev8[7:8, :], pltpu.roll(x, 1, 0))
            return x + (xp - x) * mu

        xl = lora_ref[...]
        ul = token_shift(xl, prev_lora[...], mu_lora_ref[...])
        prev_lora[...] = xl[C - 8:C]
        tl_hi, tl_lo = _split2(jnp.where(head0, jnp.tanh(ul), ul))
        wl_hi = wl_hi_ref[...]
        lo_out = _dot(tl_hi, wl_hi) + _dot(tl_lo, wl_hi) + _dot(tl_hi, wl_lo_ref[...])
        yield

        def shifted(base):
            sl = slice(base, base + RWKV_WIDTH)
            xs = rkv_ref[:, sl].astype(F32)
            out = token_shift(xs, prev_rkv[:, sl], mu_rkv_ref[:, sl])
            prev_rkv[:, sl] = xs[C - 8:C]
            return out

        r = shifted(0)
        yield
        k = shifted(RWKV_WIDTH)
        yield
        v = shifted(2 * RWKV_WIDTH)
        yield

        logdec = -math.exp(-0.5) * _sigmoid(w0_ref[...] + lo_out[:, 0:RWKV_WIDTH])
        yield
        a_lr = _sigmoid(a0_ref[...] + lo_out[:, RWKV_WIDTH:2 * RWKV_WIDTH])
        yield

        ti = lax.broadcasted_iota(jnp.int32, (C, C), 0)
        si = lax.broadcasted_iota(jnp.int32, (C, C), 1)
        tri = jnp.where(ti >= si, 1.0, 0.0).astype(BF16)
        ld_hi = logdec.astype(BF16)
        rem = logdec - ld_hi.astype(F32)
        ld_mid = rem.astype(BF16)
        ld_lo = (rem - ld_mid.astype(F32)).astype(BF16)
        cum = _dot(tri, ld_hi) + _dot(tri, ld_mid) + _dot(tri, ld_lo)
        cum_last = cum[C - 1:C, :]
        yield
        e_excl = jnp.exp(cum - logdec)
        e_incl = jnp.exp(cum)
        yield
        e_neg = jnp.exp(-cum)
        e_hat = jnp.exp(cum_last - cum)
        yield

        kk = k * kk_ref[...]
        kk = kk * lax.rsqrt(jnp.maximum(head_sum(kk * kk), 1e-24))
        yield
        k2 = k * (1.0 + (a_lr - 1.0) * ka_ref[...])
        b = kk * a_lr
        yield
        xa_all = -kk * e_excl
        xr_all = r * e_incl
        yield
        yb_all = b * e_neg
        yk_all = k2 * e_neg
        yield

        st_gamma[slot] = jnp.exp(cum_last)
        st_bonus[slot] = head_sum(r * k2 * rk_ref[...]) * v
        yield
        st_v[slot] = v.astype(BF16)
        st_kh[slot] = (k2 * e_hat).astype(BF16)
        st_bh[slot] = (b * e_hat).astype(BF16)
        yield
        for p in pairs:
            sl = cols(p)
            st_lhs[slot, p] = jnp.concatenate([xa_all[:, sl], xr_all[:, sl]], axis=0).astype(BF16)
            st_rhs[slot, p] = jnp.concatenate([stack(yb_all[:, sl]), stack(yk_all[:, sl])], axis=0)
            st_vs[slot, p] = stack(v[:, sl])
            yield

    def recur(slot):
        _, stack = make_stack()
        head_sum = make_head_sum()
        ri = lax.broadcasted_iota(jnp.int32, (S, 2 * S), 0)
        ci = lax.broadcasted_iota(jnp.int32, (S, 2 * S), 1)
        a_mask = (ri % C) >= (ci % C) + jnp.where(ri < C, 1, 0)
        ei = lax.broadcasted_iota(jnp.int32, (C, S), 0)
        ej = lax.broadcasted_iota(jnp.int32, (C, S), 1)
        eye = jnp.where(ei == ej % C, 1.0, 0.0).astype(F32)
        bi = lax.broadcasted_iota(jnp.int32, (LANES, LANES), 0)
        bj = lax.broadcasted_iota(jnp.int32, (LANES, LANES), 1)
        same_head = (bi // HEAD) == (bj // HEAD)

        def stage(fn):
            out = []
            for p in pairs:
                out.append(fn(p))
                yield
            return out

        lhs_ar = [st_lhs[slot, p] for p in pairs]
        vs = [st_vs[slot, p] for p in pairs]

        a_all = yield from stage(lambda p: jnp.where(a_mask, _dot_nt(lhs_ar[p], st_rhs[slot, p]), 0.0))
        a_ab = [a_all[p][0:C, 0:S] for p in pairs]
        a_x_v = yield from stage(lambda p: _dot(a_all[p][:, S:2 * S].astype(BF16), vs[p]))

        inv = [eye + a_ab[p] for p in pairs]
        apow = yield from stage(lambda p: _dot(a_ab[p].astype(BF16), stack(a_ab[p])))
        n = 2
        while 2 * n < C:
            res = yield from stage(
                lambda p: _dot(jnp.concatenate([inv[p], apow[p]], axis=0).astype(BF16), stack(apow[p])))
            inv = [inv[p] + res[p][0:C] for p in pairs]
            apow = [res[p][C:S] for p in pairs]
            n *= 2
        inv = yield from stage(lambda p: inv[p] + _dot(inv[p].astype(BF16), stack(apow[p])))

        s_old = [h_scr[p] for p in pairs]
        x_s = yield from stage(lambda p: _dot_nt(lhs_ar[p], s_old[p].astype(BF16)))
        us = yield from stage(lambda p: _dot(inv[p].astype(BF16), stack(x_s[p][0:C] + a_x_v[p][0:C])))
        ys = yield from stage(
            lambda p: x_s[p][C:S] + a_x_v[p][C:S] + _dot(a_all[p][C:S, 0:S].astype(BF16), stack(us[p])))

        def update(p):
            sl = cols(p)
            vu = jnp.concatenate([st_v[slot, :, sl], us[p].astype(BF16)], axis=0)
            khb = jnp.concatenate([st_kh[slot, :, sl], st_bh[slot, :, sl]], axis=0)
            h_scr[p] = s_old[p] * st_gamma[slot, :, sl] + jnp.where(same_head, _dot_tn(vu, khb), 0.0)

        yield from stage(update)

        y = jnp.concatenate(ys, axis=1)
        yc = y - head_sum(y) * (1.0 / HEAD)
        yield
        var = head_sum(yc * yc) * (1.0 / HEAD)
        y = yc * lax.rsqrt(var + GN_EPS) * gnw_ref[...] + gnb_ref[...] + st_bonus[slot]
        yield
        y_ref[...] = (y * _silu(g_ref[...].astype(F32))).astype(y_ref.dtype)

    odd = jnp.bitwise_and(c, 1) == 1

    @pl.when(jnp.logical_not(odd))
    def _():
        _interleave(recur(1), prepare(0), RWKV_INTERLEAVE)

    @pl.when(odd)
    def _():
        _interleave(recur(0), prepare(1), RWKV_INTERLEAVE)

    @pl.when(c == n_chunks)
    def _():
        hout_ref[0] = h_scr[...]


def _rwkv(p_arr, lora_arr, pinit_rkv, pinit_lora, hinit, params, *, batch):
    C = RWKV_CHUNK
    rows = p_arr.shape[0]
    n_chunks = rows // (batch * C)
    const2 = lambda b, c: (0, 0)
    vec = lambda width: pl.BlockSpec((1, width), const2)
    prep_chunk = lambda b, c: b * n_chunks + jnp.minimum(c, n_chunks - 1)
    recur_chunk = lambda b, c: b * n_chunks + jnp.maximum(c - 1, 0)
    slots = lambda shape, dtype: pltpu.VMEM((2,) + shape, dtype)
    return pl.pallas_call(
        functools.partial(_rwkv_kernel, n_chunks=n_chunks),
        out_shape=(jax.ShapeDtypeStruct((rows, RWKV_WIDTH), BF16),
                   jax.ShapeDtypeStruct((batch, N_PAIRS, LANES, LANES), F32)),
        grid=(batch, n_chunks + 1),
        in_specs=[
            pl.BlockSpec((C, 3 * RWKV_WIDTH), lambda b, c: (prep_chunk(b, c), COL_RKV // (3 * RWKV_WIDTH))),
            pl.BlockSpec((C, RWKV_WIDTH), lambda b, c: (recur_chunk(b, c), COL_GR // RWKV_WIDTH)),
            pl.BlockSpec((C, LANES), lambda b, c: (prep_chunk(b, c), 0)),
            pl.BlockSpec((8, 3 * RWKV_WIDTH), const2),
            pl.BlockSpec((8, LANES), const2),
            pl.BlockSpec((1, N_PAIRS, LANES, LANES), lambda b, c: (0, 0, 0, 0)),
            vec(3 * RWKV_WIDTH), vec(LANES), vec(RWKV_WIDTH), vec(RWKV_WIDTH),
            pl.BlockSpec((LANES, 2 * RWKV_WIDTH), const2),
            pl.BlockSpec((LANES, 2 * RWKV_WIDTH), const2),
            vec(RWKV_WIDTH), vec(RWKV_WIDTH), vec(RWKV_WIDTH), vec(RWKV_WIDTH), vec(RWKV_WIDTH),
        ],
        out_specs=(
            pl.BlockSpec((C, RWKV_WIDTH), lambda b, c: (recur_chunk(b, c), 0)),
            pl.BlockSpec((1, N_PAIRS, LANES, LANES), lambda b, c: (b, 0, 0, 0)),
        ),
        scratch_shapes=[
            pltpu.VMEM((N_PAIRS, LANES, LANES), F32),
            pltpu.VMEM((8, 3 * RWKV_WIDTH), F32),
            pltpu.VMEM((8, LANES), F32),
            slots((N_PAIRS, 2 * C, LANES), BF16),
            slots((N_PAIRS, 4 * C, LANES), BF16),
            slots((N_PAIRS, 2 * C, LANES), BF16),
            slots((C, RWKV_WIDTH), BF16),
            slots((C, RWKV_WIDTH), BF16),
            slots((C, RWKV_WIDTH), BF16),
            slots((1, RWKV_WIDTH), F32),
            slots((C, RWKV_WIDTH), F32),
        ],
        compiler_params=pltpu.CompilerParams(
            dimension_semantics=("arbitrary", "arbitrary"), vmem_limit_bytes=VMEM_LIMIT),
        name="rwkv7_chunk",
    )(p_arr, p_arr, lora_arr, pinit_rkv, pinit_lora, hinit, *params)


def _attn_kernel(q_ref, k_ref, v_ref, g_ref, km_ref, vmt_ref, lq1_ref, lk1_ref, lq2_ref, lk2_ref, sw_ref,
                 o_ref, vt_scr, qst_scr, m_scr, acc_scr, s_scr, *, t, n_blk):
    g_id = pl.program_id(0)

    @pl.when(g_id == 0)
    def _():
        for ref in (vt_scr, qst_scr, m_scr, acc_scr):
            ref[1] = jnp.ones(ref.shape[1:], ref.dtype)

    def setup(slot):
        comp0 = lax.broadcasted_iota(jnp.int32, (LANES, t), 0) < HEAD
        for c in range(n_blk):
            vt_scr[slot, c, 0:LANES, :] = v_ref[c * t:(c + 1) * t, :].astype(F32).T.astype(BF16)
            vt_scr[slot, c, LANES:LANES + ONES_ROWS, :] = jnp.ones((ONES_ROWS, t), BF16)
            yield
        for i in range(n_blk):
            qt = q_ref[i * t:(i + 1) * t, :].astype(F32).T
            qst = jnp.concatenate([jnp.where(comp0, qt, 0.0), jnp.where(comp0, 0.0, qt)], axis=1).astype(BF16)
            qst_scr[slot, i] = qst
            yield
            s = _dot(km_ref[...], qst)
            m = jnp.max(s, axis=0, keepdims=True)
            m_scr[slot, i] = m
            acc_scr[slot, i] = _dot(vmt_ref[0], jnp.exp2(s - m).astype(BF16))
            yield

    def blocks(slot):
        order = sorted(((i, j) for i in range(n_blk) for j in range(i + 1)), key=lambda ij: (ij[1], ij[0]))
        n_slots = s_scr.shape[0]
        m = [m_scr[slot, i] for i in range(n_blk)]

        def issue(n):
            i, j = order[n]
            s_scr[n % n_slots] = _dot(k_ref[j * t:(j + 1) * t, :], qst_scr[slot, i])

        causal = (lax.broadcasted_iota(jnp.int32, (t, 2 * t), 0)
                  <= lax.broadcasted_iota(jnp.int32, (t, 2 * t), 1) % t)
        for n in range(min(ATT_LOOKAHEAD, len(order))):
            issue(n)
        for n, (i, j) in enumerate(order):
            if n + ATT_LOOKAHEAD < len(order):
                issue(n + ATT_LOOKAHEAD)
            s = s_scr[n % n_slots]
            if i == j:
                s = jnp.where(causal, s, MASK_VALUE)
            m_new = jnp.maximum(m[i], jnp.max(s, axis=0, keepdims=True))
            alpha = jnp.exp2(m[i] - m_new)
            m[i] = m_new
            acc_scr[slot, i] = alpha * acc_scr[slot, i] + _dot(vt_scr[slot, j], jnp.exp2(s - m_new).astype(BF16))
            yield

        lam = (jnp.exp(jnp.sum(lq1_ref[...] * lk1_ref[...], axis=-1, keepdims=True))
               - jnp.exp(jnp.sum(lq2_ref[...] * lk2_ref[...], axis=-1, keepdims=True)) + LAMBDA_INIT)
        for i in range(n_blk):
            on = acc_scr[slot, i, 0:LANES, :] / acc_scr[slot, i, LANES:LANES + 1, :]
            ot = on[:, 0:t] - lam * on[:, t:2 * t]
            ot = ot * lax.rsqrt(jnp.mean(ot * ot, axis=0, keepdims=True) + SUBLN_EPS)
            g = g_ref[i * t:(i + 1) * t, :].astype(F32)
            o_ref[i * t:(i + 1) * t, :] = (ot.T * sw_ref[...] * (1.0 - LAMBDA_INIT) * _silu(g)).astype(o_ref.dtype)
            yield

    odd = jnp.bitwise_and(g_id, 1) == 1

    @pl.when(jnp.logical_not(odd))
    def _():
        _interleave(blocks(1), setup(0), 1)

    @pl.when(odd)
    def _():
        _interleave(blocks(0), setup(1), 1)


def _attention(p_arr, p_meta, vm_t, lam_vecs, subln_w, *, batch, seq):
    t = ATT_BLOCK
    n_blk = seq // t
    n_heads_total = batch * DIFF_HEADS
    lanes_blk = lambda col: col // LANES
    small = pl.BlockSpec((1, HEAD), lambda g: (0, 0))
    setup_head = lambda g: jnp.minimum(g, n_heads_total - 1)
    block_head = lambda g: jnp.maximum(g - 1, 0)

    def head_cols(col, head_of):
        def index(g):
            h = head_of(g)
            return (h // DIFF_HEADS, lanes_blk(col) + h % DIFF_HEADS)
        return pl.BlockSpec((seq, LANES), index)

    slots = lambda shape, dtype: pltpu.VMEM((2,) + shape, dtype)
    return pl.pallas_call(
        functools.partial(_attn_kernel, t=t, n_blk=n_blk),
        out_shape=jax.ShapeDtypeStruct((batch * seq, DIFF_WIDTH), BF16),
        grid=(n_heads_total + 1,),
        in_specs=[
            head_cols(COL_Q, setup_head), head_cols(COL_K, block_head), head_cols(COL_V, setup_head),
            head_cols(COL_GD, block_head),
            pl.BlockSpec((N_META, LANES), lambda g: (0, lanes_blk(COL_K) + setup_head(g) % DIFF_HEADS)),
            pl.BlockSpec((1, LANES + ONES_ROWS, N_META), lambda g: (setup_head(g) % DIFF_HEADS, 0, 0)),
            small, small, small, small,
            pl.BlockSpec((1, LANES), lambda g: (0, 0)),
        ],
        out_specs=pl.BlockSpec((seq, LANES), lambda g: (block_head(g) // DIFF_HEADS, block_head(g) % DIFF_HEADS)),
        scratch_shapes=[
            slots((n_blk, LANES + ONES_ROWS, t), BF16),
            slots((n_blk, LANES, 2 * t), BF16),
            slots((n_blk, 1, 2 * t), F32),
            slots((n_blk, LANES + ONES_ROWS, 2 * t), F32),
            pltpu.VMEM((ATT_LOOKAHEAD + 1, t, 2 * t), F32),
        ],
        compiler_params=pltpu.CompilerParams(
            dimension_semantics=("arbitrary",), vmem_limit_bytes=VMEM_LIMIT),
        name="diff_attn",
    )(p_arr, p_arr, p_arr, p_arr, p_meta, vm_t, *lam_vecs, subln_w)


def _outproj_kernel(yr_ref, yd_ref, w1_ref, w2_ref, x_ref, g_ref, o_ref):
    y = _dot(yr_ref[...], w1_ref[...]) + _dot(yd_ref[...], w2_ref[...])
    ms = jnp.mean(y * y, axis=-1, keepdims=True)
    o_ref[...] = x_ref[...] + y * lax.rsqrt(ms + NORM_EPS) * g_ref[...]


def _outproj(y_r, y_d, w1, w2, x2, g):
    m = x2.shape[0]
    tm = OUT_TM
    return pl.pallas_call(
        _outproj_kernel,
        out_shape=jax.ShapeDtypeStruct((m, D_MODEL), F32),
        grid=(m // tm,),
        in_specs=[
            pl.BlockSpec((tm, RWKV_WIDTH), lambda i: (i, 0)),
            pl.BlockSpec((tm, DIFF_WIDTH), lambda i: (i, 0)),
            pl.BlockSpec((RWKV_WIDTH, D_MODEL), lambda i: (0, 0)),
            pl.BlockSpec((DIFF_WIDTH, D_MODEL), lambda i: (0, 0)),
            pl.BlockSpec((tm, D_MODEL), lambda i: (i, 0)),
            pl.BlockSpec((1, D_MODEL), lambda i: (0, 0)),
        ],
        out_specs=pl.BlockSpec((tm, D_MODEL), lambda i: (i, 0)),
        compiler_params=pltpu.CompilerParams(
            dimension_semantics=("arbitrary",), vmem_limit_bytes=VMEM_LIMIT),
        name="outproj",
    )(y_r, y_d, w1, w2, x2, g)


def _rope_tables(first_pos, n_pos):
    pos = jnp.arange(first_pos, first_pos + n_pos, dtype=F32)
    inv_freq = ROPE_THETA ** (-jnp.arange(0, HEAD, 2, dtype=F32) / HEAD)
    ang = pos[:, None] * inv_freq[None, :]
    cos = jnp.cos(ang)
    sin = jnp.sin(ang)
    cos = jnp.concatenate([cos, cos, cos, cos], axis=-1)
    sin = jnp.concatenate([-sin, sin, -sin, sin], axis=-1)
    return cos, sin


def kernel(x, meta_tokens, pre_norm_w, w_in, rwkv_mu, rwkv_w0, rwkv_w_up, rwkv_a0, rwkv_a_up, rwkv_k_k, rwkv_k_a, rwkv_r_k, rwkv_gn_w, rwkv_gn_b, diff_lam_q1, diff_lam_k1, diff_lam_q2, diff_lam_k2, diff_subln_w, w_out, post_norm_w):
    batch, seq, d = x.shape
    assert d == D_MODEL and meta_tokens.shape == (N_META, D_MODEL)
    assert seq % RWKV_CHUNK == 0 and seq % ATT_BLOCK == 0
    assert (batch * seq) % PROJ_TM == 0 and seq % PROJ_TM == 0
    layer = 0
    x2 = x.reshape(batch * seq, D_MODEL)

    w = w_in[layer]
    rkv_end = 3 * RWKV_WIDTH
    lora_end = rkv_end + 2 * LORA
    w_main = jnp.concatenate([w[:, :rkv_end], w[:, lora_end:]], axis=1).astype(BF16)
    w_lora_in = w[:, rkv_end:lora_end].astype(BF16)
    g_pre = pre_norm_w[layer].reshape(1, D_MODEL)

    cos_m, sin_m = _rope_tables(0, N_META)
    cos_x, sin_x = _rope_tables(N_META, seq)
    p_meta, lora_meta = _inproj(meta_tokens.astype(x.dtype), g_pre, w_main, w_lora_in, cos_m, sin_m, tm=N_META)
    p_x, lora_x = _inproj(x2, g_pre, w_main, w_lora_in, cos_x, sin_x, tm=PROJ_TM)

    mu = rwkv_mu[layer]
    zeros = jnp.zeros((LORA, RWKV_WIDTH), F32)
    w_lora = jnp.concatenate([
        jnp.concatenate([rwkv_w_up[layer], zeros], axis=1),
        jnp.concatenate([zeros, rwkv_a_up[layer]], axis=1)], axis=0)
    w_lora_hi, w_lora_lo = _split2(w_lora)
    row = lambda t, n: t.reshape(1, n)
    rwkv_params = (
        row(mu[:rkv_end], rkv_end), row(mu[rkv_end:lora_end], LANES),
        row(rwkv_w0[layer], RWKV_WIDTH), row(rwkv_a0[layer], RWKV_WIDTH),
        w_lora_hi, w_lora_lo,
        row(rwkv_k_k[layer], RWKV_WIDTH), row(rwkv_k_a[layer], RWKV_WIDTH),
        row(rwkv_r_k[layer], RWKV_WIDTH), row(rwkv_gn_w[layer], RWKV_WIDTH), row(rwkv_gn_b[layer], RWKV_WIDTH),
    )
    pad = RWKV_CHUNK - N_META
    _, h_meta = _rwkv(jnp.pad(p_meta, ((pad, 0), (0, 0))), jnp.pad(lora_meta, ((pad, 0), (0, 0))),
                      jnp.zeros((8, rkv_end), F32), jnp.zeros((8, LANES), F32),
                      jnp.zeros((1, N_PAIRS, LANES, LANES), F32), rwkv_params, batch=1)
    y_r, _ = _rwkv(p_x, lora_x, p_meta[N_META - 8:, :rkv_end].astype(F32), lora_meta[N_META - 8:],
                   h_meta, rwkv_params, batch=batch)

    lam_vecs = tuple(t[layer].reshape(1, HEAD) for t in (diff_lam_q1, diff_lam_k1, diff_lam_q2, diff_lam_k2))
    vm_t = p_meta[:, COL_V:COL_V + DIFF_WIDTH].reshape(N_META, DIFF_HEADS, LANES).transpose(1, 2, 0)
    vm_t = jnp.concatenate([vm_t, jnp.ones((DIFF_HEADS, ONES_ROWS, N_META), BF16)], axis=1)
    y_d = _attention(p_x, p_meta, vm_t, lam_vecs, diff_subln_w[layer].reshape(1, LANES), batch=batch, seq=seq)

    wo = w_out[layer].astype(BF16)
    out = _outproj(y_r, y_d, wo[:RWKV_WIDTH], wo[RWKV_WIDTH:], x2, post_norm_w[layer].reshape(1, D_MODEL))
    return out.reshape(batch, seq, D_MODEL)
```

```python
import functools
import math

import jax
import jax.numpy as jnp
from jax import lax
from jax.experimental import pallas as pl
from jax.experimental.pallas import tpu as pltpu

F32 = jnp.float32
BF16 = jnp.bfloat16

D_MODEL = 2048
N_META = 16
HEAD = 64
LANES = 128
RWKV_WIDTH = 1024
N_PAIRS = RWKV_WIDTH // LANES
DIFF_WIDTH = 1024
DIFF_HEADS = DIFF_WIDTH // LANES
LORA = 64
ROPE_THETA = 10000.0
NORM_EPS = 1e-6
GN_EPS = 64e-5
SUBLN_EPS = 1e-5
LAMBDA_INIT = 0.8 - 0.6 * math.exp(-0.3 * 0)

COL_RKV = 0
COL_GR = 3 * RWKV_WIDTH
COL_Q = 4 * RWKV_WIDTH
COL_K = COL_Q + DIFF_WIDTH
COL_V = COL_K + DIFF_WIDTH
COL_GD = COL_V + DIFF_WIDTH
P_COLS = COL_GD + DIFF_WIDTH

WPREP_ROWS = 256
PROJ_TM = 1024
PROJ_TN = 1024
PROJ_KC = 512
RWKV_CHUNK = 64
RWKV_INTERLEAVE = 4
ATT_BLOCK = 512
ATT_LOOKAHEAD = 1
OUT_TM = 256
VMEM_LIMIT = 48 * 1024 * 1024
MASK_VALUE = -1e30
ONES_ROWS = 16
Q_SCALE = HEAD ** -0.5 * math.log2(math.e)


def _dot(a, b):
    return jnp.dot(a, b, preferred_element_type=F32)


def _dot_nt(a, b):
    return lax.dot_general(a, b, (((1,), (1,)), ((), ())), preferred_element_type=F32)


def _dot_tn(a, b):
    return lax.dot_general(a, b, (((0,), (0,)), ((), ())), preferred_element_type=F32)


def _split2(x):
    hi = x.astype(BF16)
    lo = (x - hi.astype(F32)).astype(BF16)
    return hi, lo


def _sigmoid(x):
    return 1.0 / (1.0 + jnp.exp(-x))


def _silu(x):
    return x * _sigmoid(x)


def _interleave(chain, filler, chain_steps_per_filler_step):
    active = [chain, filler]
    while active:
        for gen in list(active):
            for _ in range(chain_steps_per_filler_step if gen is chain else 1):
                if next(gen, "done") == "done":
                    active.remove(gen)
                    break


def _wprep_kernel(w_ref, main_ref, lora_ref):
    rkv_end = 3 * RWKV_WIDTH
    lora_end = rkv_end + 2 * LORA
    main_ref[:, 0:rkv_end] = w_ref[:, 0:rkv_end].astype(BF16)
    main_ref[:, rkv_end:P_COLS] = w_ref[:, lora_end:lora_end + P_COLS - rkv_end].astype(BF16)
    lora_ref[...] = w_ref[:, rkv_end:lora_end].astype(BF16)


def _wprep(w):
    rows, cols = w.shape
    assert cols == P_COLS + 2 * LORA and rows % WPREP_ROWS == 0
    return pl.pallas_call(
        _wprep_kernel,
        out_shape=(jax.ShapeDtypeStruct((rows, P_COLS), BF16), jax.ShapeDtypeStruct((rows, 2 * LORA), BF16)),
        grid=(rows // WPREP_ROWS,),
        in_specs=[pl.BlockSpec((WPREP_ROWS, cols), lambda i: (i, 0))],
        out_specs=(pl.BlockSpec((WPREP_ROWS, P_COLS), lambda i: (i, 0)),
                   pl.BlockSpec((WPREP_ROWS, 2 * LORA), lambda i: (i, 0))),
        compiler_params=pltpu.CompilerParams(dimension_semantics=("arbitrary",), vmem_limit_bytes=VMEM_LIMIT),
        name="wprep",
    )(w)


def _inproj_kernel(x_ref, g_ref, w_ref, wl_ref, cos_ref, sin_ref, o_ref, ol_ref, hn_ref, rs_ref, *, tn):
    j = pl.program_id(1)
    q_tile = COL_Q // tn
    k_tile = COL_K // tn
    assert 0 < q_tile < k_tile
    is_rope = jnp.logical_or(j == q_tile, j == k_tile)
    col_blocks = [slice(c * LANES, (c + 1) * LANES) for c in range(tn // LANES)]

    @pl.when(j == 0)
    def _():
        ssq = None
        acc = None
        for k in range(D_MODEL // PROJ_KC):
            sl = slice(k * PROJ_KC, (k + 1) * PROJ_KC)
            xc = x_ref[:, sl]
            part = jnp.sum(xc * xc, axis=-1, keepdims=True)
            ssq = part if ssq is None else ssq + part
            hn = (xc * g_ref[:, sl]).astype(BF16)
            hn_ref[:, sl] = hn
            d = _dot(hn, w_ref[sl, :])
            acc = d if acc is None else acc + d
        rs = jnp.broadcast_to(lax.rsqrt(ssq * (1.0 / D_MODEL) + NORM_EPS), rs_ref.shape)
        rs_ref[...] = rs
        ol_ref[...] = _dot(hn_ref[...], wl_ref[...]) * rs
        for cb in col_blocks:
            o_ref[:, cb] = (acc[:, cb] * rs).astype(o_ref.dtype)

    @pl.when(is_rope)
    def _():
        acc = _dot(hn_ref[...], w_ref[...])
        cos = cos_ref[...]
        sin = sin_ref[...]
        lane = lax.broadcasted_iota(jnp.int32, cos.shape, 1)
        first_half = (lane % HEAD) < (HEAD // 2)
        scale = rs_ref[...] * jnp.where(j == q_tile, Q_SCALE, 1.0).astype(F32)
        for cb in col_blocks:
            blk = acc[:, cb]
            swapped = jnp.where(first_half, pltpu.roll(blk, LANES - HEAD // 2, 1),
                                pltpu.roll(blk, HEAD // 2, 1))
            o_ref[:, cb] = ((blk * cos + swapped * sin) * scale).astype(o_ref.dtype)

    @pl.when(jnp.logical_and(j > 0, jnp.logical_not(is_rope)))
    def _():
        acc = _dot(hn_ref[...], w_ref[...])
        rs = rs_ref[...]
        for cb in col_blocks:
            o_ref[:, cb] = (acc[:, cb] * rs).astype(o_ref.dtype)


def _inproj(x2, g, w_main, w_lora, cos, sin, *, tm):
    m = x2.shape[0]
    tn = PROJ_TN
    n_pos_tiles = cos.shape[0] // tm
    return pl.pallas_call(
        functools.partial(_inproj_kernel, tn=tn),
        out_shape=(jax.ShapeDtypeStruct((m, P_COLS), BF16), jax.ShapeDtypeStruct((m, LANES), F32)),
        grid=(m // tm, P_COLS // tn),
        in_specs=[
            pl.BlockSpec((tm, D_MODEL), lambda i, j: (i, 0)),
            pl.BlockSpec((1, D_MODEL), lambda i, j: (0, 0)),
            pl.BlockSpec((D_MODEL, tn), lambda i, j: (0, j)),
            pl.BlockSpec((D_MODEL, LANES), lambda i, j: (0, 0)),
            pl.BlockSpec((tm, LANES), lambda i, j: (i % n_pos_tiles, 0)),
            pl.BlockSpec((tm, LANES), lambda i, j: (i % n_pos_tiles, 0)),
        ],
        out_specs=(
            pl.BlockSpec((tm, tn), lambda i, j: (i, j)),
            pl.BlockSpec((tm, LANES), lambda i, j: (i, 0)),
        ),
        scratch_shapes=[pltpu.VMEM((tm, D_MODEL), BF16), pltpu.VMEM((tm, LANES), F32)],
        compiler_params=pltpu.CompilerParams(
            dimension_semantics=("arbitrary", "arbitrary"), vmem_limit_bytes=VMEM_LIMIT),
        name="inproj",
    )(x2, g, w_main, w_lora, cos, sin)


def _rwkv_kernel(rkv_ref, g_ref, lora_ref, pinit_rkv_ref, pinit_lora_ref, hinit_ref,
                 mu_rkv_ref, mu_lora_ref, w0_ref, a0_ref, wl_hi_ref, wl_lo_ref,
                 kk_ref, ka_ref, rk_ref, gnw_ref, gnb_ref,
                 y_ref, hout_ref,
                 h_scr, prev_rkv, prev_lora,
                 st_lhs, st_rhs, st_vs, st_v, st_kh, st_bh, st_gamma, st_bonus, *, n_chunks):
    C = RWKV_CHUNK
    S = 2 * C
    assert S == LANES
    b_id = pl.program_id(0)
    c = pl.program_id(1)
    pairs = range(N_PAIRS)

    @pl.when(c == 0)
    def _():
        prev_rkv[...] = pinit_rkv_ref[...]
        prev_lora[...] = pinit_lora_ref[...]

    @pl.when(c <= 1)
    def _():
        h_scr[...] = hinit_ref[0]

    @pl.when(jnp.logical_and(b_id == 0, c == 0))
    def _():
        for ref in (st_lhs, st_rhs, st_vs, st_v, st_kh, st_bh, st_gamma, st_bonus):
            ref[1] = jnp.zeros(ref.shape[1:], ref.dtype)

    def cols(p, base=0):
        return slice(base + p * LANES, base + (p + 1) * LANES)

    def make_stack():
        head0 = lax.broadcasted_iota(jnp.int32, (C, LANES), 1) < HEAD

        def stack(t):
            return jnp.concatenate([jnp.where(head0, t, 0.0), jnp.where(head0, 0.0, t)], axis=0).astype(BF16)
        return head0, stack

    def make_head_sum():
        li = lax.broadcasted_iota(jnp.int32, (2 * LANES, LANES), 0)
        lj = lax.broadcasted_iota(jnp.int32, (2 * LANES, LANES), 1)
        ones_bd2 = jnp.where(((li % LANES) // HEAD) == (lj // HEAD), 1.0, 0.0).astype(BF16)

        def head_sum(t):
            rows = jnp.concatenate([t[:, cols(p)] for p in pairs], axis=0)
            hi, lo = _split2(rows)
            s = _dot(jnp.concatenate([hi, lo], axis=1), ones_bd2)
            return jnp.concatenate([s[p * C:(p + 1) * C] for p in pairs], axis=1)
        return head_sum

    def prepare(slot):
        head0, stack = make_stack()
        head_sum = make_head_sum()
        row = lax.broadcasted_iota(jnp.int32, (C, 1), 0)

        def token_shift(x, prev8, mu):
            xp = jnp.where(row == 0, prev8[7:8, :], pltpu.roll(x, 1, 0))
            return x + (xp - x) * mu

        xl = lora_ref[...]
        ul = token_shift(xl, prev_lora[...], mu_lora_ref[...])
        prev_lora[...] = xl[C - 8:C]
        tl_hi, tl_lo = _split2(jnp.where(head0, jnp.tanh(ul), ul))
        wl_hi = wl_hi_ref[...]
        lo_out = _dot(tl_hi, wl_hi) + _dot(tl_lo, wl_hi) + _dot(tl_hi, wl_lo_ref[...])
        yield

        def shifted(base):
            sl = slice(base, base + RWKV_WIDTH)
            xs = rkv_ref[:, sl].astype(F32)
            out = token_shift(xs, prev_rkv[:, sl], mu_rkv_ref[:, sl])
            prev_rkv[:, sl] = xs[C - 8:C]
            return out

        r = shifted(0)
        yield
        k = shifted(RWKV_WIDTH)
        yield
        v = shifted(2 * RWKV_WIDTH)
        yield

        logdec = -math.exp(-0.5) * _sigmoid(w0_ref[...] + lo_out[:, 0:RWKV_WIDTH])
        yield
        a_lr = _sigmoid(a0_ref[...] + lo_out[:, RWKV_WIDTH:2 * RWKV_WIDTH])
        yield

        ti = lax.broadcasted_iota(jnp.int32, (C, C), 0)
        si = lax.broadcasted_iota(jnp.int32, (C, C), 1)
        tri = jnp.where(ti >= si, 1.0, 0.0).astype(BF16)
        ld_hi = logdec.astype(BF16)
        rem = logdec - ld_hi.astype(F32)
        ld_mid = rem.astype(BF16)
        ld_lo = (rem - ld_mid.astype(F32)).astype(BF16)
        cum = _dot(tri, ld_hi) + _dot(tri, ld_mid) + _dot(tri, ld_lo)
        cum_last = cum[C - 1:C, :]
        yield
        e_excl = jnp.exp(cum - logdec)
        e_incl = jnp.exp(cum)
        yield
        e_neg = jnp.exp(-cum)
        e_hat = jnp.exp(cum_last - cum)
        yield

        kk = k * kk_ref[...]
        kk = kk * lax.rsqrt(jnp.maximum(head_sum(kk * kk), 1e-24))
        yield
        k2 = k * (1.0 + (a_lr - 1.0) * ka_ref[...])
        b = kk * a_lr
        yield
        xa_all = -kk * e_excl
        xr_all = r * e_incl
        yield
        yb_all = b * e_neg
        yk_all = k2 * e_neg
        yield

        st_gamma[slot] = jnp.exp(cum_last)
        st_bonus[slot] = head_sum(r * k2 * rk_ref[...]) * v
        yield
        st_v[slot] = v.astype(BF16)
        st_kh[slot] = (k2 * e_hat).astype(BF16)
        st_bh[slot] = (b * e_hat).astype(BF16)
        yield
        for p in pairs:
            sl = cols(p)
            st_lhs[slot, p] = jnp.concatenate([xa_all[:, sl], xr_all[:, sl]], axis=0).astype(BF16)
            st_rhs[slot, p] = jnp.concatenate([stack(yb_all[:, sl]), stack(yk_all[:, sl])], axis=0)
            st_vs[slot, p] = stack(v[:, sl])
            yield

    def recur(slot):
        _, stack = make_stack()
        head_sum = make_head_sum()
        ri = lax.broadcasted_iota(jnp.int32, (S, 2 * S), 0)
        ci = lax.broadcasted_iota(jnp.int32, (S, 2 * S), 1)
        a_mask = (ri % C) >= (ci % C) + jnp.where(ri < C, 1, 0)
        ei = lax.broadcasted_iota(jnp.int32, (C, S), 0)
        ej = lax.broadcasted_iota(jnp.int32, (C, S), 1)
        eye = jnp.where(ei == ej % C, 1.0, 0.0).astype(F32)
        bi = lax.broadcasted_iota(jnp.int32, (LANES, LANES), 0)
        bj = lax.broadcasted_iota(jnp.int32, (LANES, LANES), 1)
        same_head = (bi // HEAD) == (bj // HEAD)

        def stage(fn):
            out = []
            for p in pairs:
                out.append(fn(p))
                yield
            return out

        lhs_ar = [st_lhs[slot, p] for p in pairs]
        vs = [st_vs[slot, p] for p in pairs]

        a_all = yield from stage(lambda p: jnp.where(a_mask, _dot_nt(lhs_ar[p], st_rhs[slot, p]), 0.0))
        a_ab = [a_all[p][0:C, 0:S] for p in pairs]
        a_x_v = yield from stage(lambda p: _dot(a_all[p][:, S:2 * S].astype(BF16), vs[p]))

        inv = [eye + a_ab[p] for p in pairs]
        apow = yield from stage(lambda p: _dot(a_ab[p].astype(BF16), stack(a_ab[p])))
        n = 2
        while 2 * n < C:
            res = yield from stage(
                lambda p: _dot(jnp.concatenate([inv[p], apow[p]], axis=0).astype(BF16), stack(apow[p])))
            inv = [inv[p] + res[p][0:C] for p in pairs]
            apow = [res[p][C:S] for p in pairs]
            n *= 2
        inv = yield from stage(lambda p: inv[p] + _dot(inv[p].astype(BF16), stack(apow[p])))

        s_old = [h_scr[p] for p in pairs]
        x_s = yield from stage(lambda p: _dot_nt(lhs_ar[p], s_old[p].astype(BF16)))
        us = yield from stage(lambda p: _dot(inv[p].astype(BF16), stack(x_s[p][0:C] + a_x_v[p][0:C])))
        ys = yield from stage(
            lambda p: x_s[p][C:S] + a_x_v[p][C:S] + _dot(a_all[p][C:S, 0:S].astype(BF16), stack(us[p])))

        def update(p):
            sl = cols(p)
            vu = jnp.concatenate([st_v[slot, :, sl], us[p].astype(BF16)], axis=0)
            khb = jnp.concatenate([st_kh[slot, :, sl], st_bh[slot, :, sl]], axis=0)
            h_scr[p] = s_old[p] * st_gamma[slot, :, sl] + jnp.where(same_head, _dot_tn(vu, khb), 0.0)

        yield from stage(update)

        y = jnp.concatenate(ys, axis=1)
        yc = y - head_sum(y) * (1.0 / HEAD)
        yield
        var = head_sum(yc * yc) * (1.0 / HEAD)
        y = yc * lax.rsqrt(var + GN_EPS) * gnw_ref[...] + gnb_ref[...] + st_bonus[slot]
        yield
        y_ref[...] = (y * _silu(g_ref[...].astype(F32))).astype(y_ref.dtype)

    odd = jnp.bitwise_and(c, 1) == 1

    @pl.when(jnp.logical_not(odd))
    def _():
        _interleave(recur(1), prepare(0), RWKV_INTERLEAVE)

    @pl.when(odd)
    def _():
        _interleave(recur(0), prepare(1), RWKV_INTERLEAVE)

    @pl.when(c == n_chunks)
    def _():
        hout_ref[0] = h_scr[...]


def _rwkv(p_arr, lora_arr, pinit_rkv, pinit_lora, hinit, params, *, batch):
    C = RWKV_CHUNK
    rows = p_arr.shape[0]
    n_chunks = rows // (batch * C)
    const2 = lambda b, c: (0, 0)
    vec = lambda width: pl.BlockSpec((1, width), const2)
    prep_chunk = lambda b, c: b * n_chunks + jnp.minimum(c, n_chunks - 1)
    recur_chunk = lambda b, c: b * n_chunks + jnp.maximum(c - 1, 0)
    slots = lambda shape, dtype: pltpu.VMEM((2,) + shape, dtype)
    return pl.pallas_call(
        functools.partial(_rwkv_kernel, n_chunks=n_chunks),
        out_shape=(jax.ShapeDtypeStruct((rows, RWKV_WIDTH), BF16),
                   jax.ShapeDtypeStruct((batch, N_PAIRS, LANES, LANES), F32)),
        grid=(batch, n_chunks + 1),
        in_specs=[
            pl.BlockSpec((C, 3 * RWKV_WIDTH), lambda b, c: (prep_chunk(b, c), COL_RKV // (3 * RWKV_WIDTH))),
            pl.BlockSpec((C, RWKV_WIDTH), lambda b, c: (recur_chunk(b, c), COL_GR // RWKV_WIDTH)),
            pl.BlockSpec((C, LANES), lambda b, c: (prep_chunk(b, c), 0)),
            pl.BlockSpec((8, 3 * RWKV_WIDTH), const2),
            pl.BlockSpec((8, LANES), const2),
            pl.BlockSpec((1, N_PAIRS, LANES, LANES), lambda b, c: (0, 0, 0, 0)),
            vec(3 * RWKV_WIDTH), vec(LANES), vec(RWKV_WIDTH), vec(RWKV_WIDTH),
            pl.BlockSpec((LANES, 2 * RWKV_WIDTH), const2),
            pl.BlockSpec((LANES, 2 * RWKV_WIDTH), const2),
            vec(RWKV_WIDTH), vec(RWKV_WIDTH), vec(RWKV_WIDTH), vec(RWKV_WIDTH), vec(RWKV_WIDTH),
        ],
        out_specs=(
            pl.BlockSpec((C, RWKV_WIDTH), lambda b, c: (recur_chunk(b, c), 0)),
            pl.BlockSpec((1, N_PAIRS, LANES, LANES), lambda b, c: (b, 0, 0, 0)),
        ),
        scratch_shapes=[
            pltpu.VMEM((N_PAIRS, LANES, LANES), F32),
            pltpu.VMEM((8, 3 * RWKV_WIDTH), F32),
            pltpu.VMEM((8, LANES), F32),
            slots((N_PAIRS, 2 * C, LANES), BF16),
            slots((N_PAIRS, 4 * C, LANES), BF16),
            slots((N_PAIRS, 2 * C, LANES), BF16),
            slots((C, RWKV_WIDTH), BF16),
            slots((C, RWKV_WIDTH), BF16),
            slots((C, RWKV_WIDTH), BF16),
            slots((1, RWKV_WIDTH), F32),
            slots((C, RWKV_WIDTH), F32),
        ],
        compiler_params=pltpu.CompilerParams(
            dimension_semantics=("arbitrary", "arbitrary"), vmem_limit_bytes=VMEM_LIMIT),
        name="rwkv7_chunk",
    )(p_arr, p_arr, lora_arr, pinit_rkv, pinit_lora, hinit, *params)


def _attn_kernel(q_ref, k_ref, v_ref, g_ref, km_ref, vmt_ref, lq1_ref, lk1_ref, lq2_ref, lk2_ref, sw_ref,
                 o_ref, vt_scr, qst_scr, m_scr, acc_scr, s_scr, *, t, n_blk):
    g_id = pl.program_id(0)

    @pl.when(g_id == 0)
    def _():
        for ref in (vt_scr, qst_scr, m_scr, acc_scr):
            ref[1] = jnp.ones(ref.shape[1:], ref.dtype)

    def setup(slot):
        comp0 = lax.broadcasted_iota(jnp.int32, (LANES, t), 0) < HEAD
        for c in range(n_blk):
            vt_scr[slot, c, 0:LANES, :] = v_ref[c * t:(c + 1) * t, :].astype(F32).T.astype(BF16)
            vt_scr[slot, c, LANES:LANES + ONES_ROWS, :] = jnp.ones((ONES_ROWS, t), BF16)
            yield
        for i in range(n_blk):
            qt = q_ref[i * t:(i + 1) * t, :].astype(F32).T
            qst = jnp.concatenate([jnp.where(comp0, qt, 0.0), jnp.where(comp0, 0.0, qt)], axis=1).astype(BF16)
            qst_scr[slot, i] = qst
            yield
            s = _dot(km_ref[...], qst)
            m = jnp.max(s, axis=0, keepdims=True)
            m_scr[slot, i] = m
            acc_scr[slot, i] = _dot(vmt_ref[0], jnp.exp2(s - m).astype(BF16))
            yield

    def blocks(slot):
        order = sorted(((i, j) for i in range(n_blk) for j in range(i + 1)), key=lambda ij: (ij[1], ij[0]))
        n_slots = s_scr.shape[0]
        m = [m_scr[slot, i] for i in range(n_blk)]

        def issue(n):
            i, j = order[n]
            s_scr[n % n_slots] = _dot(k_ref[j * t:(j + 1) * t, :], qst_scr[slot, i])

        causal = (lax.broadcasted_iota(jnp.int32, (t, 2 * t), 0)
                  <= lax.broadcasted_iota(jnp.int32, (t, 2 * t), 1) % t)
        for n in range(min(ATT_LOOKAHEAD, len(order))):
            issue(n)
        for n, (i, j) in enumerate(order):
            if n + ATT_LOOKAHEAD < len(order):
                issue(n + ATT_LOOKAHEAD)
            s = s_scr[n % n_slots]
            if i == j:
                s = jnp.where(causal, s, MASK_VALUE)
            m_new = jnp.maximum(m[i], jnp.max(s, axis=0, keepdims=True))
            alpha = jnp.exp2(m[i] - m_new)
            m[i] = m_new
            acc_scr[slot, i] = alpha * acc_scr[slot, i] + _dot(vt_scr[slot, j], jnp.exp2(s - m_new).astype(BF16))
            yield

        lam = (jnp.exp(jnp.sum(lq1_ref[...] * lk1_ref[...], axis=-1, keepdims=True))
               - jnp.exp(jnp.sum(lq2_ref[...] * lk2_ref[...], axis=-1, keepdims=True)) + LAMBDA_INIT)
        for i in range(n_blk):
            on = acc_scr[slot, i, 0:LANES, :] / acc_scr[slot, i, LANES:LANES + 1, :]
            ot = on[:, 0:t] - lam * on[:, t:2 * t]
            ot = ot * lax.rsqrt(jnp.mean(ot * ot, axis=0, keepdims=True) + SUBLN_EPS)
            g = g_ref[i * t:(i + 1) * t, :].astype(F32)
            o_ref[i * t:(i + 1) * t, :] = (ot.T * sw_ref[...] * (1.0 - LAMBDA_INIT) * _silu(g)).astype(o_ref.dtype)
            yield

    odd = jnp.bitwise_and(g_id, 1) == 1

    @pl.when(jnp.logical_not(odd))
    def _():
        _interleave(blocks(1), setup(0), 1)

    @pl.when(odd)
    def _():
        _interleave(blocks(0), setup(1), 1)


def _attention(p_arr, p_meta, vm_t, lam_vecs, subln_w, *, batch, seq):
    t = ATT_BLOCK
    n_blk = seq // t
    n_heads_total = batch * DIFF_HEADS
    lanes_blk = lambda col: col // LANES
    small = pl.BlockSpec((1, HEAD), lambda g: (0, 0))
    setup_head = lambda g: jnp.minimum(g, n_heads_total - 1)
    block_head = lambda g: jnp.maximum(g - 1, 0)

    def head_cols(col, head_of):
        def index(g):
            h = head_of(g)
            return (h // DIFF_HEADS, lanes_blk(col) + h % DIFF_HEADS)
        return pl.BlockSpec((seq, LANES), index)

    slots = lambda shape, dtype: pltpu.VMEM((2,) + shape, dtype)
    return pl.pallas_call(
        functools.partial(_attn_kernel, t=t, n_blk=n_blk),
        out_shape=jax.ShapeDtypeStruct((batch * seq, DIFF_WIDTH), BF16),
        grid=(n_heads_total + 1,),
        in_specs=[
            head_cols(COL_Q, setup_head), head_cols(COL_K, block_head), head_cols(COL_V, setup_head),
            head_cols(COL_GD, block_head),
            pl.BlockSpec((N_META, LANES), lambda g: (0, lanes_blk(COL_K) + setup_head(g) % DIFF_HEADS)),
            pl.BlockSpec((1, LANES + ONES_ROWS, N_META), lambda g: (setup_head(g) % DIFF_HEADS, 0, 0)),
            small, small, small, small,
            pl.BlockSpec((1, LANES), lambda g: (0, 0)),
        ],
        out_specs=pl.BlockSpec((seq, LANES), lambda g: (block_head(g) // DIFF_HEADS, block_head(g) % DIFF_HEADS)),
        scratch_shapes=[
            slots((n_blk, LANES + ONES_ROWS, t), BF16),
            slots((n_blk, LANES, 2 * t), BF16),
            slots((n_blk, 1, 2 * t), F32),
            slots((n_blk, LANES + ONES_ROWS, 2 * t), F32),
            pltpu.VMEM((ATT_LOOKAHEAD + 1, t, 2 * t), F32),
        ],
        compiler_params=pltpu.CompilerParams(
            dimension_semantics=("arbitrary",), vmem_limit_bytes=VMEM_LIMIT),
        name="diff_attn",
    )(p_arr, p_arr, p_arr, p_arr, p_meta, vm_t, *lam_vecs, subln_w)


def _outproj_kernel(yr_ref, yd_ref, w1_ref, w2_ref, x_ref, g_ref, o_ref):
    y = _dot(yr_ref[...], w1_ref[...]) + _dot(yd_ref[...], w2_ref[...])
    ms = jnp.mean(y * y, axis=-1, keepdims=True)
    o_ref[...] = x_ref[...] + y * lax.rsqrt(ms + NORM_EPS) * g_ref[...]


def _outproj(y_r, y_d, w1, w2, x2, g):
    m = x2.shape[0]
    tm = OUT_TM
    return pl.pallas_call(
        _outproj_kernel,
        out_shape=jax.ShapeDtypeStruct((m, D_MODEL), F32),
        grid=(m // tm,),
        in_specs=[
            pl.BlockSpec((tm, RWKV_WIDTH), lambda i: (i, 0)),
            pl.BlockSpec((tm, DIFF_WIDTH), lambda i: (i, 0)),
            pl.BlockSpec((RWKV_WIDTH, D_MODEL), lambda i: (0, 0)),
            pl.BlockSpec((DIFF_WIDTH, D_MODEL), lambda i: (0, 0)),
            pl.BlockSpec((tm, D_MODEL), lambda i: (i, 0)),
            pl.BlockSpec((1, D_MODEL), lambda i: (0, 0)),
        ],
        out_specs=pl.BlockSpec((tm, D_MODEL), lambda i: (i, 0)),
        compiler_params=pltpu.CompilerParams(
            dimension_semantics=("arbitrary",), vmem_limit_bytes=VMEM_LIMIT),
        name="outproj",
    )(y_r, y_d, w1, w2, x2, g)


def _rope_tables(first_pos, n_pos):
    pos = jnp.arange(first_pos, first_pos + n_pos, dtype=F32)
    inv_freq = ROPE_THETA ** (-jnp.arange(0, HEAD, 2, dtype=F32) / HEAD)
    ang = pos[:, None] * inv_freq[None, :]
    cos = jnp.cos(ang)
    sin = jnp.sin(ang)
    cos = jnp.concatenate([cos, cos, cos, cos], axis=-1)
    sin = jnp.concatenate([-sin, sin, -sin, sin], axis=-1)
    return cos, sin


def kernel(x, meta_tokens, pre_norm_w, w_in, rwkv_mu, rwkv_w0, rwkv_w_up, rwkv_a0, rwkv_a_up, rwkv_k_k, rwkv_k_a, rwkv_r_k, rwkv_gn_w, rwkv_gn_b, diff_lam_q1, diff_lam_k1, diff_lam_q2, diff_lam_k2, diff_subln_w, w_out, post_norm_w):
    batch, seq, d = x.shape
    assert d == D_MODEL and meta_tokens.shape == (N_META, D_MODEL)
    assert seq % RWKV_CHUNK == 0 and seq % ATT_BLOCK == 0
    assert (batch * seq) % PROJ_TM == 0 and seq % PROJ_TM == 0
    layer = 0
    x2 = x.reshape(batch * seq, D_MODEL)

    w = w_in[layer]
    rkv_end = 3 * RWKV_WIDTH
    lora_end = rkv_end + 2 * LORA
    w_main, w_lora_in = _wprep(w)
    g_pre = pre_norm_w[layer].reshape(1, D_MODEL)

    cos_m, sin_m = _rope_tables(0, N_META)
    cos_x, sin_x = _rope_tables(N_META, seq)
    p_meta, lora_meta = _inproj(meta_tokens.astype(x.dtype), g_pre, w_main, w_lora_in, cos_m, sin_m, tm=N_META)
    p_x, lora_x = _inproj(x2, g_pre, w_main, w_lora_in, cos_x, sin_x, tm=PROJ_TM)

    mu = rwkv_mu[layer]
    zeros = jnp.zeros((LORA, RWKV_WIDTH), F32)
    w_lora = jnp.concatenate([
        jnp.concatenate([rwkv_w_up[layer], zeros], axis=1),
        jnp.concatenate([zeros, rwkv_a_up[layer]], axis=1)], axis=0)
    w_lora_hi, w_lora_lo = _split2(w_lora)
    row = lambda t, n: t.reshape(1, n)
    rwkv_params = (
        row(mu[:rkv_end], rkv_end), row(mu[rkv_end:lora_end], LANES),
        row(rwkv_w0[layer], RWKV_WIDTH), row(rwkv_a0[layer], RWKV_WIDTH),
        w_lora_hi, w_lora_lo,
        row(rwkv_k_k[layer], RWKV_WIDTH), row(rwkv_k_a[layer], RWKV_WIDTH),
        row(rwkv_r_k[layer], RWKV_WIDTH), row(rwkv_gn_w[layer], RWKV_WIDTH), row(rwkv_gn_b[layer], RWKV_WIDTH),
    )
    pad = RWKV_CHUNK - N_META
    _, h_meta = _rwkv(jnp.pad(p_meta, ((pad, 0), (0, 0))), jnp.pad(lora_meta, ((pad, 0), (0, 0))),
                      jnp.zeros((8, rkv_end), F32), jnp.zeros((8, LANES), F32),
                      jnp.zeros((1, N_PAIRS, LANES, LANES), F32), rwkv_params, batch=1)
    y_r, _ = _rwkv(p_x, lora_x, p_meta[N_META - 8:, :rkv_end].astype(F32), lora_meta[N_META - 8:],
                   h_meta, rwkv_params, batch=batch)

    lam_vecs = tuple(t[layer].reshape(1, HEAD) for t in (diff_lam_q1, diff_lam_k1, diff_lam_q2, diff_lam_k2))
    vm_t = p_meta[:, COL_V:COL_V + DIFF_WIDTH].reshape(N_META, DIFF_HEADS, LANES).transpose(1, 2, 0)
    vm_t = jnp.concatenate([vm_t, jnp.ones((DIFF_HEADS, ONES_ROWS, N_META), BF16)], axis=1)
    y_d = _attention(p_x, p_meta, vm_t, lam_vecs, diff_subln_w[layer].reshape(1, LANES), batch=batch, seq=seq)

    wo = w_out[layer].astype(BF16)
    out = _outproj(y_r, y_d, wo[:RWKV_WIDTH], wo[RWKV_WIDTH:], x2, post_norm_w[layer].reshape(1, D_MODEL))
    return out.reshape(batch, seq, D_MODEL)
```

```python
import functools
import math

import jax
import jax.numpy as jnp
from jax import lax
from jax.experimental import pallas as pl
from jax.experimental.pallas import tpu as pltpu

F32 = jnp.float32
BF16 = jnp.bfloat16

D_MODEL = 2048
N_META = 16
HEAD = 64
LANES = 128
RWKV_WIDTH = 1024
N_PAIRS = RWKV_WIDTH // LANES
DIFF_WIDTH = 1024
DIFF_HEADS = DIFF_WIDTH // LANES
LORA = 64
ROPE_THETA = 10000.0
NORM_EPS = 1e-6
GN_EPS = 64e-5
SUBLN_EPS = 1e-5
LAMBDA_INIT = 0.8 - 0.6 * math.exp(-0.3 * 0)

COL_RKV = 0
COL_GR = 3 * RWKV_WIDTH
COL_Q = 4 * RWKV_WIDTH
COL_K = COL_Q + DIFF_WIDTH
COL_V = COL_K + DIFF_WIDTH
COL_GD = COL_V + DIFF_WIDTH
P_COLS = COL_GD + DIFF_WIDTH

WPREP_ROWS = 256
PROJ_TM = 1024
PROJ_TN = 1024
PROJ_KC = 512
RWKV_CHUNK = 64
RWKV_GROUP = 8
RWKV_INTERLEAVE = 4
ATT_BLOCK = 512
ATT_LOOKAHEAD = 1
OUT_TM = 256
VMEM_LIMIT = 48 * 1024 * 1024
MASK_VALUE = -1e30
ONES_ROWS = 16
Q_SCALE = HEAD ** -0.5 * math.log2(math.e)


def _dot(a, b):
    return jnp.dot(a, b, preferred_element_type=F32)


def _dot_nt(a, b):
    return lax.dot_general(a, b, (((1,), (1,)), ((), ())), preferred_element_type=F32)


def _dot_tn(a, b):
    return lax.dot_general(a, b, (((0,), (0,)), ((), ())), preferred_element_type=F32)


def _split2(x):
    hi = x.astype(BF16)
    lo = (x - hi.astype(F32)).astype(BF16)
    return hi, lo


def _sigmoid(x):
    return 1.0 / (1.0 + jnp.exp(-x))


def _silu(x):
    return x * _sigmoid(x)


def _interleave(chain, filler, chain_steps_per_filler_step):
    active = [chain, filler]
    while active:
        for gen in list(active):
            for _ in range(chain_steps_per_filler_step if gen is chain else 1):
                if next(gen, "done") == "done":
                    active.remove(gen)
                    break


def _wprep_kernel(w_ref, main_ref, lora_ref):
    rkv_end = 3 * RWKV_WIDTH
    lora_end = rkv_end + 2 * LORA
    main_ref[:, 0:rkv_end] = w_ref[:, 0:rkv_end].astype(BF16)
    main_ref[:, rkv_end:P_COLS] = w_ref[:, lora_end:lora_end + P_COLS - rkv_end].astype(BF16)
    lora_ref[...] = w_ref[:, rkv_end:lora_end].astype(BF16)


def _wprep(w):
    rows, cols = w.shape
    assert cols == P_COLS + 2 * LORA and rows % WPREP_ROWS == 0
    return pl.pallas_call(
        _wprep_kernel,
        out_shape=(jax.ShapeDtypeStruct((rows, P_COLS), BF16), jax.ShapeDtypeStruct((rows, 2 * LORA), BF16)),
        grid=(rows // WPREP_ROWS,),
        in_specs=[pl.BlockSpec((WPREP_ROWS, cols), lambda i: (i, 0))],
        out_specs=(pl.BlockSpec((WPREP_ROWS, P_COLS), lambda i: (i, 0)),
                   pl.BlockSpec((WPREP_ROWS, 2 * LORA), lambda i: (i, 0))),
        compiler_params=pltpu.CompilerParams(dimension_semantics=("arbitrary",), vmem_limit_bytes=VMEM_LIMIT),
        name="wprep",
    )(w)


def _inproj_kernel(*refs, tn):
    n_k = D_MODEL // PROJ_KC
    x_refs = refs[:n_k]
    g_ref, w_ref, wl_ref, cos_ref, sin_ref, o_ref, ol_ref, hn_ref, rs_ref = refs[n_k:]
    j = pl.program_id(1)
    q_tile = COL_Q // tn
    k_tile = COL_K // tn
    assert 0 < q_tile < k_tile
    is_rope = jnp.logical_or(j == q_tile, j == k_tile)
    col_blocks = [slice(c * LANES, (c + 1) * LANES) for c in range(tn // LANES)]

    @pl.when(j == 0)
    def _():
        ssq = None
        acc = None
        for k in range(n_k):
            sl = slice(k * PROJ_KC, (k + 1) * PROJ_KC)
            xc = x_refs[k][...]
            part = jnp.sum(xc * xc, axis=-1, keepdims=True)
            ssq = part if ssq is None else ssq + part
            hn = (xc * g_ref[:, sl]).astype(BF16)
            hn_ref[:, sl] = hn
            d = _dot(hn, w_ref[sl, :])
            acc = d if acc is None else acc + d
        rs = jnp.broadcast_to(lax.rsqrt(ssq * (1.0 / D_MODEL) + NORM_EPS), rs_ref.shape)
        rs_ref[...] = rs
        ol_ref[...] = _dot(hn_ref[...], wl_ref[...]) * rs
        for cb in col_blocks:
            o_ref[:, cb] = (acc[:, cb] * rs).astype(o_ref.dtype)

    @pl.when(is_rope)
    def _():
        acc = _dot(hn_ref[...], w_ref[...])
        cos = cos_ref[...]
        sin = sin_ref[...]
        lane = lax.broadcasted_iota(jnp.int32, cos.shape, 1)
        first_half = (lane % HEAD) < (HEAD // 2)
        scale = rs_ref[...] * jnp.where(j == q_tile, Q_SCALE, 1.0).astype(F32)
        for cb in col_blocks:
            blk = acc[:, cb]
            swapped = jnp.where(first_half, pltpu.roll(blk, LANES - HEAD // 2, 1),
                                pltpu.roll(blk, HEAD // 2, 1))
            o_ref[:, cb] = ((blk * cos + swapped * sin) * scale).astype(o_ref.dtype)

    @pl.when(jnp.logical_and(j > 0, jnp.logical_not(is_rope)))
    def _():
        acc = _dot(hn_ref[...], w_ref[...])
        rs = rs_ref[...]
        for cb in col_blocks:
            o_ref[:, cb] = (acc[:, cb] * rs).astype(o_ref.dtype)


def _inproj(x2, g, w_main, w_lora, cos, sin, *, tm):
    m = x2.shape[0]
    tn = PROJ_TN
    n_pos_tiles = cos.shape[0] // tm
    n_row_tiles = m // tm
    n_col_tiles = P_COLS // tn
    n_k = D_MODEL // PROJ_KC
    assert n_k <= n_col_tiles

    def x_chunk(k):
        first_step = n_col_tiles - n_k + k
        return pl.BlockSpec(
            (tm, PROJ_KC), lambda i, j: (jnp.minimum(i + (j >= first_step).astype(jnp.int32), n_row_tiles - 1), k))

    return pl.pallas_call(
        functools.partial(_inproj_kernel, tn=tn),
        out_shape=(jax.ShapeDtypeStruct((m, P_COLS), BF16), jax.ShapeDtypeStruct((m, LANES), F32)),
        grid=(n_row_tiles, n_col_tiles),
        in_specs=[x_chunk(k) for k in range(n_k)] + [
            pl.BlockSpec((1, D_MODEL), lambda i, j: (0, 0)),
            pl.BlockSpec((D_MODEL, tn), lambda i, j: (0, j)),
            pl.BlockSpec((D_MODEL, LANES), lambda i, j: (0, 0)),
            pl.BlockSpec((tm, LANES), lambda i, j: (i % n_pos_tiles, 0)),
            pl.BlockSpec((tm, LANES), lambda i, j: (i % n_pos_tiles, 0)),
        ],
        out_specs=(
            pl.BlockSpec((tm, tn), lambda i, j: (i, j)),
            pl.BlockSpec((tm, LANES), lambda i, j: (i, 0)),
        ),
        scratch_shapes=[pltpu.VMEM((tm, D_MODEL), BF16), pltpu.VMEM((tm, LANES), F32)],
        compiler_params=pltpu.CompilerParams(
            dimension_semantics=("arbitrary", "arbitrary"), vmem_limit_bytes=VMEM_LIMIT),
        name="inproj",
    )(*([x2] * n_k), g, w_main, w_lora, cos, sin)


def _rwkv_kernel(rkv_ref, g_ref, lora_ref, pinit_rkv_ref, pinit_lora_ref, hinit_ref,
                 mu_rkv_ref, mu_lora_ref, w0_ref, a0_ref, wl_hi_ref, wl_lo_ref,
                 kk_ref, ka_ref, rk_ref, gnw_ref, gnb_ref,
                 y_ref, hout_ref,
                 h_scr, prev_rkv, prev_lora,
                 st_lhs, st_rhs, st_vs, st_v, st_kh, st_bh, st_gamma, st_bonus, *, n_chunks):
    C = RWKV_CHUNK
    S = 2 * C
    assert S == LANES
    b_id = pl.program_id(0)
    c = pl.program_id(1)
    pairs = range(N_PAIRS)

    @pl.when(c == 0)
    def _():
        prev_rkv[...] = pinit_rkv_ref[...]
        prev_lora[...] = pinit_lora_ref[...]

    @pl.when(c <= 1)
    def _():
        h_scr[...] = hinit_ref[0]

    @pl.when(jnp.logical_and(b_id == 0, c == 0))
    def _():
        for ref in (st_lhs, st_rhs, st_vs, st_v, st_kh, st_bh, st_gamma, st_bonus):
            ref[1] = jnp.zeros(ref.shape[1:], ref.dtype)

    def cols(p, base=0):
        return slice(base + p * LANES, base + (p + 1) * LANES)

    def make_stack():
        head0 = lax.broadcasted_iota(jnp.int32, (C, LANES), 1) < HEAD

        def stack(t):
            return jnp.concatenate([jnp.where(head0, t, 0.0), jnp.where(head0, 0.0, t)], axis=0).astype(BF16)
        return head0, stack

    def make_head_sum():
        li = lax.broadcasted_iota(jnp.int32, (2 * LANES, LANES), 0)
        lj = lax.broadcasted_iota(jnp.int32, (2 * LANES, LANES), 1)
        ones_bd2 = jnp.where(((li % LANES) // HEAD) == (lj // HEAD), 1.0, 0.0).astype(BF16)

        def head_sum(t):
            rows = jnp.concatenate([t[:, cols(p)] for p in pairs], axis=0)
            hi, lo = _split2(rows)
            s = _dot(jnp.concatenate([hi, lo], axis=1), ones_bd2)
            return jnp.concatenate([s[p * C:(p + 1) * C] for p in pairs], axis=1)
        return head_sum

    def prepare(slot):
        head0, stack = make_stack()
        head_sum = make_head_sum()
        row = lax.broadcasted_iota(jnp.int32, (C, 1), 0)

        def token_shift(x, prev8, mu):
            xp = jnp.where(row == 0, prev8[7:8, :], pltpu.roll(x, 1, 0))
            return x + (xp - x) * mu

        xl = lora_ref[...]
        ul = token_shift(xl, prev_lora[...], mu_lora_ref[...])
        prev_lora[...] = xl[C - 8:C]
        tl_hi, tl_lo = _split2(jnp.where(head0, jnp.tanh(ul), ul))
        wl_hi = wl_hi_ref[...]
        lo_out = _dot(tl_hi, wl_hi) + _dot(tl_lo, wl_hi) + _dot(tl_hi, wl_lo_ref[...])
        yield

        def shifted(base):
            sl = slice(base, base + RWKV_WIDTH)
            xs = rkv_ref[:, sl].astype(F32)
            out = token_shift(xs, prev_rkv[:, sl], mu_rkv_ref[:, sl])
            prev_rkv[:, sl] = xs[C - 8:C]
            return out

        r = shifted(0)
        yield
        k = shifted(RWKV_WIDTH)
        yield
        v = shifted(2 * RWKV_WIDTH)
        yield

        logdec = -math.exp(-0.5) * _sigmoid(w0_ref[...] + lo_out[:, 0:RWKV_WIDTH])
        yield
        a_lr = _sigmoid(a0_ref[...] + lo_out[:, RWKV_WIDTH:2 * RWKV_WIDTH])
        yield

        ti = lax.broadcasted_iota(jnp.int32, (C, C), 0)
        si = lax.broadcasted_iota(jnp.int32, (C, C), 1)
        tri = jnp.where(ti >= si, 1.0, 0.0).astype(BF16)
        ld_hi = logdec.astype(BF16)
        rem = logdec - ld_hi.astype(F32)
        ld_mid = rem.astype(BF16)
        ld_lo = (rem - ld_mid.astype(F32)).astype(BF16)
        cum = _dot(tri, ld_hi) + _dot(tri, ld_mid) + _dot(tri, ld_lo)
        cum_last = cum[C - 1:C, :]
        yield
        e_excl = jnp.exp(cum - logdec)
        e_incl = jnp.exp(cum)
        yield
        e_neg = jnp.exp(-cum)
        e_hat = jnp.exp(cum_last - cum)
        yield

        kk = k * kk_ref[...]
        kk = kk * lax.rsqrt(jnp.maximum(head_sum(kk * kk), 1e-24))
        yield
        k2 = k * (1.0 + (a_lr - 1.0) * ka_ref[...])
        b = kk * a_lr
        yield
        xa_all = -kk * e_excl
        xr_all = r * e_incl
        yield
        yb_all = b * e_neg
        yk_all = k2 * e_neg
        yield

        st_gamma[slot] = jnp.exp(cum_last)
        st_bonus[slot] = head_sum(r * k2 * rk_ref[...]) * v
        yield
        st_v[slot] = v.astype(BF16)
        st_kh[slot] = (k2 * e_hat).astype(BF16)
        st_bh[slot] = (b * e_hat).astype(BF16)
        yield
        for p in pairs:
            sl = cols(p)
            st_lhs[slot, p] = jnp.concatenate([xa_all[:, sl], xr_all[:, sl]], axis=0).astype(BF16)
            st_rhs[slot, p] = jnp.concatenate([stack(yb_all[:, sl]), stack(yk_all[:, sl])], axis=0)
            st_vs[slot, p] = stack(v[:, sl])
            yield

    def recur(slot):
        _, stack = make_stack()
        head_sum = make_head_sum()
        ri = lax.broadcasted_iota(jnp.int32, (S, 2 * S), 0)
        ci = lax.broadcasted_iota(jnp.int32, (S, 2 * S), 1)
        a_mask = (ri % C) >= (ci % C) + jnp.where(ri < C, 1, 0)
        ei = lax.broadcasted_iota(jnp.int32, (C, S), 0)
        ej = lax.broadcasted_iota(jnp.int32, (C, S), 1)
        eye = jnp.where(ei == ej % C, 1.0, 0.0).astype(F32)
        bi = lax.broadcasted_iota(jnp.int32, (LANES, LANES), 0)
        bj = lax.broadcasted_iota(jnp.int32, (LANES, LANES), 1)
        same_head = (bi // HEAD) == (bj // HEAD)

        ys_all = {}

        def chain(group):
            def stage(fn):
                out = {}
                for p in group:
                    out[p] = fn(p)
                    yield
                return out

            lhs_ar = {p: st_lhs[slot, p] for p in group}
            vs = {p: st_vs[slot, p] for p in group}

            a_all = yield from stage(lambda p: jnp.where(a_mask, _dot_nt(lhs_ar[p], st_rhs[slot, p]), 0.0))
            a_ab = {p: a_all[p][0:C, 0:S] for p in group}
            a_x_v = yield from stage(lambda p: _dot(a_all[p][:, S:2 * S].astype(BF16), vs[p]))

            inv = {p: eye + a_ab[p] for p in group}
            apow = yield from stage(lambda p: _dot(a_ab[p].astype(BF16), stack(a_ab[p])))
            n = 2
            while 2 * n < C:
                res = yield from stage(
                    lambda p: _dot(jnp.concatenate([inv[p], apow[p]], axis=0).astype(BF16), stack(apow[p])))
                inv = {p: inv[p] + res[p][0:C] for p in group}
                apow = {p: res[p][C:S] for p in group}
                n *= 2
            inv = yield from stage(lambda p: inv[p] + _dot(inv[p].astype(BF16), stack(apow[p])))

            s_old = {p: h_scr[p] for p in group}
            x_s = yield from stage(lambda p: _dot_nt(lhs_ar[p], s_old[p].astype(BF16)))
            us = yield from stage(lambda p: _dot(inv[p].astype(BF16), stack(x_s[p][0:C] + a_x_v[p][0:C])))
            ys = yield from stage(
                lambda p: x_s[p][C:S] + a_x_v[p][C:S] + _dot(a_all[p][C:S, 0:S].astype(BF16), stack(us[p])))
            ys_all.update(ys)

            def update(p):
                sl = cols(p)
                vu = jnp.concatenate([st_v[slot, :, sl], us[p].astype(BF16)], axis=0)
                khb = jnp.concatenate([st_kh[slot, :, sl], st_bh[slot, :, sl]], axis=0)
                h_scr[p] = s_old[p] * st_gamma[slot, :, sl] + jnp.where(same_head, _dot_tn(vu, khb), 0.0)

            yield from stage(update)

        for first in range(0, N_PAIRS, RWKV_GROUP):
            yield from chain(range(first, first + RWKV_GROUP))

        y = jnp.concatenate([ys_all[p] for p in pairs], axis=1)
        yc = y - head_sum(y) * (1.0 / HEAD)
        yield
        var = head_sum(yc * yc) * (1.0 / HEAD)
        y = yc * lax.rsqrt(var + GN_EPS) * gnw_ref[...] + gnb_ref[...] + st_bonus[slot]
        yield
        y_ref[...] = (y * _silu(g_ref[...].astype(F32))).astype(y_ref.dtype)

    odd = jnp.bitwise_and(c, 1) == 1

    @pl.when(jnp.logical_not(odd))
    def _():
        _interleave(recur(1), prepare(0), RWKV_INTERLEAVE)

    @pl.when(odd)
    def _():
        _interleave(recur(0), prepare(1), RWKV_INTERLEAVE)

    @pl.when(c == n_chunks)
    def _():
        hout_ref[0] = h_scr[...]


def _rwkv(p_arr, lora_arr, pinit_rkv, pinit_lora, hinit, params, *, batch):
    C = RWKV_CHUNK
    rows = p_arr.shape[0]
    n_chunks = rows // (batch * C)
    const2 = lambda b, c: (0, 0)
    vec = lambda width: pl.BlockSpec((1, width), const2)
    prep_chunk = lambda b, c: b * n_chunks + jnp.minimum(c, n_chunks - 1)
    recur_chunk = lambda b, c: b * n_chunks + jnp.maximum(c - 1, 0)
    slots = lambda shape, dtype: pltpu.VMEM((2,) + shape, dtype)
    return pl.pallas_call(
        functools.partial(_rwkv_kernel, n_chunks=n_chunks),
        out_shape=(jax.ShapeDtypeStruct((rows, RWKV_WIDTH), BF16),
                   jax.ShapeDtypeStruct((batch, N_PAIRS, LANES, LANES), F32)),
        grid=(batch, n_chunks + 1),
        in_specs=[
            pl.BlockSpec((C, 3 * RWKV_WIDTH), lambda b, c: (prep_chunk(b, c), COL_RKV // (3 * RWKV_WIDTH))),
            pl.BlockSpec((C, RWKV_WIDTH), lambda b, c: (recur_chunk(b, c), COL_GR // RWKV_WIDTH)),
            pl.BlockSpec((C, LANES), lambda b, c: (prep_chunk(b, c), 0)),
            pl.BlockSpec((8, 3 * RWKV_WIDTH), const2),
            pl.BlockSpec((8, LANES), const2),
            pl.BlockSpec((1, N_PAIRS, LANES, LANES), lambda b, c: (0, 0, 0, 0)),
            vec(3 * RWKV_WIDTH), vec(LANES), vec(RWKV_WIDTH), vec(RWKV_WIDTH),
            pl.BlockSpec((LANES, 2 * RWKV_WIDTH), const2),
            pl.BlockSpec((LANES, 2 * RWKV_WIDTH), const2),
            vec(RWKV_WIDTH), vec(RWKV_WIDTH), vec(RWKV_WIDTH), vec(RWKV_WIDTH), vec(RWKV_WIDTH),
        ],
        out_specs=(
            pl.BlockSpec((C, RWKV_WIDTH), lambda b, c: (recur_chunk(b, c), 0)),
            pl.BlockSpec((1, N_PAIRS, LANES, LANES), lambda b, c: (b, 0, 0, 0)),
        ),
        scratch_shapes=[
            pltpu.VMEM((N_PAIRS, LANES, LANES), F32),
            pltpu.VMEM((8, 3 * RWKV_WIDTH), F32),
            pltpu.VMEM((8, LANES), F32),
            slots((N_PAIRS, 2 * C, LANES), BF16),
            slots((N_PAIRS, 4 * C, LANES), BF16),
            slots((N_PAIRS, 2 * C, LANES), BF16),
            slots((C, RWKV_WIDTH), BF16),
            slots((C, RWKV_WIDTH), BF16),
            slots((C, RWKV_WIDTH), BF16),
            slots((1, RWKV_WIDTH), F32),
            slots((C, RWKV_WIDTH), F32),
        ],
        compiler_params=pltpu.CompilerParams(
            dimension_semantics=("arbitrary", "arbitrary"), vmem_limit_bytes=VMEM_LIMIT),
        name="rwkv7_chunk",
    )(p_arr, p_arr, lora_arr, pinit_rkv, pinit_lora, hinit, *params)


def _attn_kernel(q_ref, k_ref, v_ref, g_ref, km_ref, vmt_ref, lq1_ref, lk1_ref, lq2_ref, lk2_ref, sw_ref,
                 o_ref, vt_scr, qst_scr, m_scr, acc_scr, s_scr, *, t, n_blk):
    g_id = pl.program_id(0)

    @pl.when(g_id == 0)
    def _():
        for ref in (vt_scr, qst_scr, m_scr, acc_scr):
            ref[1] = jnp.ones(ref.shape[1:], ref.dtype)

    def setup(slot):
        comp0 = lax.broadcasted_iota(jnp.int32, (LANES, t), 0) < HEAD
        for c in range(n_blk):
            vt_scr[slot, c, 0:LANES, :] = v_ref[c * t:(c + 1) * t, :].astype(F32).T.astype(BF16)
            vt_scr[slot, c, LANES:LANES + ONES_ROWS, :] = jnp.ones((ONES_ROWS, t), BF16)
            yield
        for i in range(n_blk):
            qt = q_ref[i * t:(i + 1) * t, :].astype(F32).T
            qst = jnp.concatenate([jnp.where(comp0, qt, 0.0), jnp.where(comp0, 0.0, qt)], axis=1).astype(BF16)
            qst_scr[slot, i] = qst
            yield
            s = _dot(km_ref[...], qst)
            m = jnp.max(s, axis=0, keepdims=True)
            m_scr[slot, i] = m
            acc_scr[slot, i] = _dot(vmt_ref[0], jnp.exp2(s - m).astype(BF16))
            yield

    def blocks(slot):
        order = sorted(((i, j) for i in range(n_blk) for j in range(i + 1)), key=lambda ij: (ij[1], ij[0]))
        n_slots = s_scr.shape[0]
        m = [m_scr[slot, i] for i in range(n_blk)]

        def issue(n):
            i, j = order[n]
            s_scr[n % n_slots] = _dot(k_ref[j * t:(j + 1) * t, :], qst_scr[slot, i])

        causal = (lax.broadcasted_iota(jnp.int32, (t, 2 * t), 0)
                  <= lax.broadcasted_iota(jnp.int32, (t, 2 * t), 1) % t)
        for n in range(min(ATT_LOOKAHEAD, len(order))):
            issue(n)
        for n, (i, j) in enumerate(order):
            if n + ATT_LOOKAHEAD < len(order):
                issue(n + ATT_LOOKAHEAD)
            s = s_scr[n % n_slots]
            if i == j:
                s = jnp.where(causal, s, MASK_VALUE)
            m_new = jnp.maximum(m[i], jnp.max(s, axis=0, keepdims=True))
            alpha = jnp.exp2(m[i] - m_new)
            m[i] = m_new
            acc_scr[slot, i] = alpha * acc_scr[slot, i] + _dot(vt_scr[slot, j], jnp.exp2(s - m_new).astype(BF16))
            yield

        lam = (jnp.exp(jnp.sum(lq1_ref[...] * lk1_ref[...], axis=-1, keepdims=True))
               - jnp.exp(jnp.sum(lq2_ref[...] * lk2_ref[...], axis=-1, keepdims=True)) + LAMBDA_INIT)
        for i in range(n_blk):
            on = acc_scr[slot, i, 0:LANES, :] / acc_scr[slot, i, LANES:LANES + 1, :]
            ot = on[:, 0:t] - lam * on[:, t:2 * t]
            ot = ot * lax.rsqrt(jnp.mean(ot * ot, axis=0, keepdims=True) + SUBLN_EPS)
            g = g_ref[i * t:(i + 1) * t, :].astype(F32)
            o_ref[i * t:(i + 1) * t, :] = (ot.T * sw_ref[...] * (1.0 - LAMBDA_INIT) * _silu(g)).astype(o_ref.dtype)
            yield

    odd = jnp.bitwise_and(g_id, 1) == 1

    @pl.when(jnp.logical_not(odd))
    def _():
        _interleave(blocks(1), setup(0), 1)

    @pl.when(odd)
    def _():
        _interleave(blocks(0), setup(1), 1)


def _attention(p_arr, p_meta, vm_t, lam_vecs, subln_w, *, batch, seq):
    t = ATT_BLOCK
    n_blk = seq // t
    n_heads_total = batch * DIFF_HEADS
    lanes_blk = lambda col: col // LANES
    small = pl.BlockSpec((1, HEAD), lambda g: (0, 0))
    setup_head = lambda g: jnp.minimum(g, n_heads_total - 1)
    block_head = lambda g: jnp.maximum(g - 1, 0)

    def head_cols(col, head_of):
        def index(g):
            h = head_of(g)
            return (h // DIFF_HEADS, lanes_blk(col) + h % DIFF_HEADS)
        return pl.BlockSpec((seq, LANES), index)

    slots = lambda shape, dtype: pltpu.VMEM((2,) + shape, dtype)
    return pl.pallas_call(
        functools.partial(_attn_kernel, t=t, n_blk=n_blk),
        out_shape=jax.ShapeDtypeStruct((batch * seq, DIFF_WIDTH), BF16),
        grid=(n_heads_total + 1,),
        in_specs=[
            head_cols(COL_Q, setup_head), head_cols(COL_K, block_head), head_cols(COL_V, setup_head),
            head_cols(COL_GD, block_head),
            pl.BlockSpec((N_META, LANES), lambda g: (0, lanes_blk(COL_K) + setup_head(g) % DIFF_HEADS)),
            pl.BlockSpec((1, LANES + ONES_ROWS, N_META), lambda g: (setup_head(g) % DIFF_HEADS, 0, 0)),
            small, small, small, small,
            pl.BlockSpec((1, LANES), lambda g: (0, 0)),
        ],
        out_specs=pl.BlockSpec((seq, LANES), lambda g: (block_head(g) // DIFF_HEADS, block_head(g) % DIFF_HEADS)),
        scratch_shapes=[
            slots((n_blk, LANES + ONES_ROWS, t), BF16),
            slots((n_blk, LANES, 2 * t), BF16),
            slots((n_blk, 1, 2 * t), F32),
            slots((n_blk, LANES + ONES_ROWS, 2 * t), F32),
            pltpu.VMEM((ATT_LOOKAHEAD + 1, t, 2 * t), F32),
        ],
        compiler_params=pltpu.CompilerParams(
            dimension_semantics=("arbitrary",), vmem_limit_bytes=VMEM_LIMIT),
        name="diff_attn",
    )(p_arr, p_arr, p_arr, p_arr, p_meta, vm_t, *lam_vecs, subln_w)


def _outproj_kernel(yr_ref, yd_ref, w1_ref, w2_ref, x_ref, g_ref, o_ref):
    y = _dot(yr_ref[...], w1_ref[...]) + _dot(yd_ref[...], w2_ref[...])
    ms = jnp.mean(y * y, axis=-1, keepdims=True)
    o_ref[...] = x_ref[...] + y * lax.rsqrt(ms + NORM_EPS) * g_ref[...]


def _outproj(y_r, y_d, w1, w2, x2, g):
    m = x2.shape[0]
    tm = OUT_TM
    return pl.pallas_call(
        _outproj_kernel,
        out_shape=jax.ShapeDtypeStruct((m, D_MODEL), F32),
        grid=(m // tm,),
        in_specs=[
            pl.BlockSpec((tm, RWKV_WIDTH), lambda i: (i, 0)),
            pl.BlockSpec((tm, DIFF_WIDTH), lambda i: (i, 0)),
            pl.BlockSpec((RWKV_WIDTH, D_MODEL), lambda i: (0, 0)),
            pl.BlockSpec((DIFF_WIDTH, D_MODEL), lambda i: (0, 0)),
            pl.BlockSpec((tm, D_MODEL), lambda i: (i, 0)),
            pl.BlockSpec((1, D_MODEL), lambda i: (0, 0)),
        ],
        out_specs=pl.BlockSpec((tm, D_MODEL), lambda i: (i, 0)),
        compiler_params=pltpu.CompilerParams(
            dimension_semantics=("arbitrary",), vmem_limit_bytes=VMEM_LIMIT),
        name="outproj",
    )(y_r, y_d, w1, w2, x2, g)


def _rope_tables(first_pos, n_pos):
    pos = jnp.arange(first_pos, first_pos + n_pos, dtype=F32)
    inv_freq = ROPE_THETA ** (-jnp.arange(0, HEAD, 2, dtype=F32) / HEAD)
    ang = pos[:, None] * inv_freq[None, :]
    cos = jnp.cos(ang)
    sin = jnp.sin(ang)
    cos = jnp.concatenate([cos, cos, cos, cos], axis=-1)
    sin = jnp.concatenate([-sin, sin, -sin, sin], axis=-1)
    return cos, sin


def kernel(x, meta_tokens, pre_norm_w, w_in, rwkv_mu, rwkv_w0, rwkv_w_up, rwkv_a0, rwkv_a_up, rwkv_k_k, rwkv_k_a, rwkv_r_k, rwkv_gn_w, rwkv_gn_b, diff_lam_q1, diff_lam_k1, diff_lam_q2, diff_lam_k2, diff_subln_w, w_out, post_norm_w):
    batch, seq, d = x.shape
    assert d == D_MODEL and meta_tokens.shape == (N_META, D_MODEL)
    assert seq % RWKV_CHUNK == 0 and seq % ATT_BLOCK == 0
    assert (batch * seq) % PROJ_TM == 0 and seq % PROJ_TM == 0
    layer = 0
    x2 = x.reshape(batch * seq, D_MODEL)

    w = w_in[layer]
    rkv_end = 3 * RWKV_WIDTH
    lora_end = rkv_end + 2 * LORA
    w_main, w_lora_in = _wprep(w)
    g_pre = pre_norm_w[layer].reshape(1, D_MODEL)

    cos_m, sin_m = _rope_tables(0, N_META)
    cos_x, sin_x = _rope_tables(N_META, seq)
    p_meta, lora_meta = _inproj(meta_tokens.astype(x.dtype), g_pre, w_main, w_lora_in, cos_m, sin_m, tm=N_META)
    p_x, lora_x = _inproj(x2, g_pre, w_main, w_lora_in, cos_x, sin_x, tm=PROJ_TM)

    mu = rwkv_mu[layer]
    zeros = jnp.zeros((LORA, RWKV_WIDTH), F32)
    w_lora = jnp.concatenate([
        jnp.concatenate([rwkv_w_up[layer], zeros], axis=1),
        jnp.concatenate([zeros, rwkv_a_up[layer]], axis=1)], axis=0)
    w_lora_hi, w_lora_lo = _split2(w_lora)
    row = lambda t, n: t.reshape(1, n)
    rwkv_params = (
        row(mu[:rkv_end], rkv_end), row(mu[rkv_end:lora_end], LANES),
        row(rwkv_w0[layer], RWKV_WIDTH), row(rwkv_a0[layer], RWKV_WIDTH),
        w_lora_hi, w_lora_lo,
        row(rwkv_k_k[layer], RWKV_WIDTH), row(rwkv_k_a[layer], RWKV_WIDTH),
        row(rwkv_r_k[layer], RWKV_WIDTH), row(rwkv_gn_w[layer], RWKV_WIDTH), row(rwkv_gn_b[layer], RWKV_WIDTH),
    )
    pad = RWKV_CHUNK - N_META
    _, h_meta = _rwkv(jnp.pad(p_meta, ((pad, 0), (0, 0))), jnp.pad(lora_meta, ((pad, 0), (0, 0))),
                      jnp.zeros((8, rkv_end), F32), jnp.zeros((8, LANES), F32),
                      jnp.zeros((1, N_PAIRS, LANES, LANES), F32), rwkv_params, batch=1)
    y_r, _ = _rwkv(p_x, lora_x, p_meta[N_META - 8:, :rkv_end].astype(F32), lora_meta[N_META - 8:],
                   h_meta, rwkv_params, batch=batch)

    lam_vecs = tuple(t[layer].reshape(1, HEAD) for t in (diff_lam_q1, diff_lam_k1, diff_lam_q2, diff_lam_k2))
    vm_t = p_meta[:, COL_V:COL_V + DIFF_WIDTH].reshape(N_META, DIFF_HEADS, LANES).transpose(1, 2, 0)
    vm_t = jnp.concatenate([vm_t, jnp.ones((DIFF_HEADS, ONES_ROWS, N_META), BF16)], axis=1)
    y_d = _attention(p_x, p_meta, vm_t, lam_vecs, diff_subln_w[layer].reshape(1, LANES), batch=batch, seq=seq)

    wo = w_out[layer].astype(BF16)
    out = _outproj(y_r, y_d, wo[:RWKV_WIDTH], wo[RWKV_WIDTH:], x2, post_norm_w[layer].reshape(1, D_MODEL))
    return out.reshape(batch, seq, D_MODEL)
```

```python
import functools
import math

import jax
import jax.numpy as jnp
from jax import lax
from jax.experimental import pallas as pl
from jax.experimental.pallas import tpu as pltpu

F32 = jnp.float32
BF16 = jnp.bfloat16

D_MODEL = 2048
N_META = 16
HEAD = 64
LANES = 128
RWKV_WIDTH = 1024
N_PAIRS = RWKV_WIDTH // LANES
DIFF_WIDTH = 1024
DIFF_HEADS = DIFF_WIDTH // LANES
LORA = 64
ROPE_THETA = 10000.0
NORM_EPS = 1e-6
GN_EPS = 64e-5
SUBLN_EPS = 1e-5
LAMBDA_INIT = 0.8 - 0.6 * math.exp(-0.3 * 0)

COL_RKV = 0
COL_GR = 3 * RWKV_WIDTH
COL_Q = 4 * RWKV_WIDTH
COL_K = COL_Q + DIFF_WIDTH
COL_V = COL_K + DIFF_WIDTH
COL_GD = COL_V + DIFF_WIDTH
P_COLS = COL_GD + DIFF_WIDTH

WPREP_ROWS = 256
PROJ_TM = 1024
PROJ_TN = 1024
PROJ_KC = 512
RWKV_CHUNK = 64
RWKV_SLOTS = 3
RWKV_WEIGHTS = (3, 2, 1)
ATT_BLOCK = 512
ATT_LOOKAHEAD = 1
OUT_TM = 256
VMEM_LIMIT = 48 * 1024 * 1024
MASK_VALUE = -1e30
ONES_ROWS = 16
Q_SCALE = HEAD ** -0.5 * math.log2(math.e)


def _dot(a, b):
    return jnp.dot(a, b, preferred_element_type=F32)


def _dot_nt(a, b):
    return lax.dot_general(a, b, (((1,), (1,)), ((), ())), preferred_element_type=F32)


def _dot_tn(a, b):
    return lax.dot_general(a, b, (((0,), (0,)), ((), ())), preferred_element_type=F32)


def _split2(x):
    hi = x.astype(BF16)
    lo = (x - hi.astype(F32)).astype(BF16)
    return hi, lo


def _sigmoid(x):
    return 1.0 / (1.0 + jnp.exp(-x))


def _silu(x):
    return x * _sigmoid(x)


def _interleave(*weighted_generators):
    active = [list(gw) for gw in weighted_generators]
    while active:
        for item in list(active):
            gen, weight = item
            for _ in range(weight):
                if next(gen, "done") == "done":
                    active.remove(item)
                    break


def _wprep_kernel(w_ref, main_ref, lora_ref):
    rkv_end = 3 * RWKV_WIDTH
    lora_end = rkv_end + 2 * LORA
    main_ref[:, 0:rkv_end] = w_ref[:, 0:rkv_end].astype(BF16)
    main_ref[:, rkv_end:P_COLS] = w_ref[:, lora_end:lora_end + P_COLS - rkv_end].astype(BF16)
    lora_ref[...] = w_ref[:, rkv_end:lora_end].astype(BF16)


def _wprep(w):
    rows, cols = w.shape
    assert cols == P_COLS + 2 * LORA and rows % WPREP_ROWS == 0
    return pl.pallas_call(
        _wprep_kernel,
        out_shape=(jax.ShapeDtypeStruct((rows, P_COLS), BF16), jax.ShapeDtypeStruct((rows, 2 * LORA), BF16)),
        grid=(rows // WPREP_ROWS,),
        in_specs=[pl.BlockSpec((WPREP_ROWS, cols), lambda i: (i, 0))],
        out_specs=(pl.BlockSpec((WPREP_ROWS, P_COLS), lambda i: (i, 0)),
                   pl.BlockSpec((WPREP_ROWS, 2 * LORA), lambda i: (i, 0))),
        compiler_params=pltpu.CompilerParams(dimension_semantics=("arbitrary",), vmem_limit_bytes=VMEM_LIMIT),
        name="wprep",
    )(w)


def _inproj_kernel(*refs, tn):
    n_k = D_MODEL // PROJ_KC
    x_refs = refs[:n_k]
    g_ref, w_ref, wl_ref, cos_ref, sin_ref, o_ref, ol_ref, hn_ref, rs_ref = refs[n_k:]
    j = pl.program_id(1)
    q_tile = COL_Q // tn
    k_tile = COL_K // tn
    assert 0 < q_tile < k_tile
    is_rope = jnp.logical_or(j == q_tile, j == k_tile)
    col_blocks = [slice(c * LANES, (c + 1) * LANES) for c in range(tn // LANES)]

    @pl.when(j == 0)
    def _():
        ssq = None
        acc = None
        for k in range(n_k):
            sl = slice(k * PROJ_KC, (k + 1) * PROJ_KC)
            xc = x_refs[k][...]
            part = jnp.sum(xc * xc, axis=-1, keepdims=True)
            ssq = part if ssq is None else ssq + part
            hn = (xc * g_ref[:, sl]).astype(BF16)
            hn_ref[:, sl] = hn
            d = _dot(hn, w_ref[sl, :])
            acc = d if acc is None else acc + d
        rs = jnp.broadcast_to(lax.rsqrt(ssq * (1.0 / D_MODEL) + NORM_EPS), rs_ref.shape)
        rs_ref[...] = rs
        ol_ref[...] = _dot(hn_ref[...], wl_ref[...]) * rs
        for cb in col_blocks:
            o_ref[:, cb] = (acc[:, cb] * rs).astype(o_ref.dtype)

    @pl.when(is_rope)
    def _():
        acc = _dot(hn_ref[...], w_ref[...])
        cos = cos_ref[...]
        sin = sin_ref[...]
        lane = lax.broadcasted_iota(jnp.int32, cos.shape, 1)
        first_half = (lane % HEAD) < (HEAD // 2)
        scale = rs_ref[...] * jnp.where(j == q_tile, Q_SCALE, 1.0).astype(F32)
        for cb in col_blocks:
            blk = acc[:, cb]
            swapped = jnp.where(first_half, pltpu.roll(blk, LANES - HEAD // 2, 1),
                                pltpu.roll(blk, HEAD // 2, 1))
            o_ref[:, cb] = ((blk * cos + swapped * sin) * scale).astype(o_ref.dtype)

    @pl.when(jnp.logical_and(j > 0, jnp.logical_not(is_rope)))
    def _():
        acc = _dot(hn_ref[...], w_ref[...])
        rs = rs_ref[...]
        for cb in col_blocks:
            o_ref[:, cb] = (acc[:, cb] * rs).astype(o_ref.dtype)


def _inproj(x2, g, w_main, w_lora, cos, sin, *, tm):
    m = x2.shape[0]
    tn = PROJ_TN
    n_pos_tiles = cos.shape[0] // tm
    n_row_tiles = m // tm
    n_col_tiles = P_COLS // tn
    n_k = D_MODEL // PROJ_KC
    assert n_k <= n_col_tiles

    def x_chunk(k):
        first_step = n_col_tiles - n_k + k
        return pl.BlockSpec(
            (tm, PROJ_KC), lambda i, j: (jnp.minimum(i + (j >= first_step).astype(jnp.int32), n_row_tiles - 1), k))

    return pl.pallas_call(
        functools.partial(_inproj_kernel, tn=tn),
        out_shape=(jax.ShapeDtypeStruct((m, P_COLS), BF16), jax.ShapeDtypeStruct((m, LANES), F32)),
        grid=(n_row_tiles, n_col_tiles),
        in_specs=[x_chunk(k) for k in range(n_k)] + [
            pl.BlockSpec((1, D_MODEL), lambda i, j: (0, 0)),
            pl.BlockSpec((D_MODEL, tn), lambda i, j: (0, j)),
            pl.BlockSpec((D_MODEL, LANES), lambda i, j: (0, 0)),
            pl.BlockSpec((tm, LANES), lambda i, j: (i % n_pos_tiles, 0)),
            pl.BlockSpec((tm, LANES), lambda i, j: (i % n_pos_tiles, 0)),
        ],
        out_specs=(
            pl.BlockSpec((tm, tn), lambda i, j: (i, j)),
            pl.BlockSpec((tm, LANES), lambda i, j: (i, 0)),
        ),
        scratch_shapes=[pltpu.VMEM((tm, D_MODEL), BF16), pltpu.VMEM((tm, LANES), F32)],
        compiler_params=pltpu.CompilerParams(
            dimension_semantics=("arbitrary", "arbitrary"), vmem_limit_bytes=VMEM_LIMIT),
        name="inproj",
    )(*([x2] * n_k), g, w_main, w_lora, cos, sin)


def _rwkv_kernel(rkv_ref, g_ref, lora_ref, pinit_rkv_ref, pinit_lora_ref, hinit_ref,
                 mu_rkv_ref, mu_lora_ref, w0_ref, a0_ref, wl_hi_ref, wl_lo_ref,
                 kk_ref, ka_ref, rk_ref, gnw_ref, gnb_ref,
                 y_ref, hout_ref,
                 h_scr, prev_rkv, prev_lora,
                 st_lhs, st_rhs, st_vs, st_v, st_kh, st_bh, st_gamma, st_bonus,
                 st_inv, st_arb, st_axv, *, n_chunks):
    C = RWKV_CHUNK
    S = 2 * C
    assert S == LANES
    b_id = pl.program_id(0)
    c = pl.program_id(1)
    pairs = range(N_PAIRS)
    staging = (st_lhs, st_rhs, st_vs, st_v, st_kh, st_bh, st_gamma, st_bonus, st_inv, st_arb, st_axv)

    @pl.when(c == 0)
    def _():
        prev_rkv[...] = pinit_rkv_ref[...]
        prev_lora[...] = pinit_lora_ref[...]

    @pl.when(c <= 2)
    def _():
        h_scr[...] = hinit_ref[0]

    @pl.when(jnp.logical_and(b_id == 0, c == 0))
    def _():
        for ref in staging:
            for slot in range(1, RWKV_SLOTS):
                ref[slot] = jnp.zeros(ref.shape[1:], ref.dtype)

    def cols(p, base=0):
        return slice(base + p * LANES, base + (p + 1) * LANES)

    def make_stack():
        head0 = lax.broadcasted_iota(jnp.int32, (C, LANES), 1) < HEAD

        def stack(t):
            return jnp.concatenate([jnp.where(head0, t, 0.0), jnp.where(head0, 0.0, t)], axis=0).astype(BF16)
        return head0, stack

    def make_head_sum():
        li = lax.broadcasted_iota(jnp.int32, (2 * LANES, LANES), 0)
        lj = lax.broadcasted_iota(jnp.int32, (2 * LANES, LANES), 1)
        ones_bd2 = jnp.where(((li % LANES) // HEAD) == (lj // HEAD), 1.0, 0.0).astype(BF16)

        def head_sum(t):
            rows = jnp.concatenate([t[:, cols(p)] for p in pairs], axis=0)
            hi, lo = _split2(rows)
            s = _dot(jnp.concatenate([hi, lo], axis=1), ones_bd2)
            return jnp.concatenate([s[p * C:(p + 1) * C] for p in pairs], axis=1)
        return head_sum

    def prepare(slot):
        head0, stack = make_stack()
        head_sum = make_head_sum()
        row = lax.broadcasted_iota(jnp.int32, (C, 1), 0)

        def token_shift(x, prev8, mu):
            xp = jnp.where(row == 0, prev8[7:8, :], pltpu.roll(x, 1, 0))
            return x + (xp - x) * mu

        xl = lora_ref[...]
        ul = token_shift(xl, prev_lora[...], mu_lora_ref[...])
        prev_lora[...] = xl[C - 8:C]
        tl_hi, tl_lo = _split2(jnp.where(head0, jnp.tanh(ul), ul))
        wl_hi = wl_hi_ref[...]
        lo_out = _dot(tl_hi, wl_hi) + _dot(tl_lo, wl_hi) + _dot(tl_hi, wl_lo_ref[...])
        yield

        def shifted(base):
            sl = slice(base, base + RWKV_WIDTH)
            xs = rkv_ref[:, sl].astype(F32)
            out = token_shift(xs, prev_rkv[:, sl], mu_rkv_ref[:, sl])
            prev_rkv[:, sl] = xs[C - 8:C]
            return out

        r = shifted(0)
        yield
        k = shifted(RWKV_WIDTH)
        yield
        v = shifted(2 * RWKV_WIDTH)
        yield

        logdec = -math.exp(-0.5) * _sigmoid(w0_ref[...] + lo_out[:, 0:RWKV_WIDTH])
        yield
        a_lr = _sigmoid(a0_ref[...] + lo_out[:, RWKV_WIDTH:2 * RWKV_WIDTH])
        yield

        ti = lax.broadcasted_iota(jnp.int32, (C, C), 0)
        si = lax.broadcasted_iota(jnp.int32, (C, C), 1)
        tri = jnp.where(ti >= si, 1.0, 0.0).astype(BF16)
        ld_hi = logdec.astype(BF16)
        rem = logdec - ld_hi.astype(F32)
        ld_mid = rem.astype(BF16)
        ld_lo = (rem - ld_mid.astype(F32)).astype(BF16)
        cum = _dot(tri, ld_hi) + _dot(tri, ld_mid) + _dot(tri, ld_lo)
        cum_last = cum[C - 1:C, :]
        yield
        e_excl = jnp.exp(cum - logdec)
        e_incl = jnp.exp(cum)
        yield
        e_neg = jnp.exp(-cum)
        e_hat = jnp.exp(cum_last - cum)
        yield

        kk = k * kk_ref[...]
        kk = kk * lax.rsqrt(jnp.maximum(head_sum(kk * kk), 1e-24))
        yield
        k2 = k * (1.0 + (a_lr - 1.0) * ka_ref[...])
        b = kk * a_lr
        yield
        xa_all = -kk * e_excl
        xr_all = r * e_incl
        yield
        yb_all = b * e_neg
        yk_all = k2 * e_neg
        yield

        st_gamma[slot] = jnp.exp(cum_last)
        st_bonus[slot] = head_sum(r * k2 * rk_ref[...]) * v
        yield
        st_v[slot] = v.astype(BF16)
        st_kh[slot] = (k2 * e_hat).astype(BF16)
        st_bh[slot] = (b * e_hat).astype(BF16)
        yield
        for p in pairs:
            sl = cols(p)
            st_lhs[slot, p] = jnp.concatenate([xa_all[:, sl], xr_all[:, sl]], axis=0).astype(BF16)
            st_rhs[slot, p] = jnp.concatenate([stack(yb_all[:, sl]), stack(yk_all[:, sl])], axis=0)
            st_vs[slot, p] = stack(v[:, sl])
            yield

    def stage(fn):
        out = []
        for p in pairs:
            out.append(fn(p))
            yield
        return out

    def invert(slot):
        _, stack = make_stack()
        ri = lax.broadcasted_iota(jnp.int32, (S, 2 * S), 0)
        ci = lax.broadcasted_iota(jnp.int32, (S, 2 * S), 1)
        a_mask = (ri % C) >= (ci % C) + jnp.where(ri < C, 1, 0)
        ei = lax.broadcasted_iota(jnp.int32, (C, S), 0)
        ej = lax.broadcasted_iota(jnp.int32, (C, S), 1)
        eye = jnp.where(ei == ej % C, 1.0, 0.0).astype(F32)

        a_all = yield from stage(lambda p: jnp.where(a_mask, _dot_nt(st_lhs[slot, p], st_rhs[slot, p]), 0.0))
        a_ab = [a_all[p][0:C, 0:S] for p in pairs]

        def values(p):
            st_arb[slot, p] = a_all[p][C:S, 0:S].astype(BF16)
            st_axv[slot, p] = _dot(a_all[p][:, S:2 * S].astype(BF16), st_vs[slot, p])

        yield from stage(values)

        inv = [eye + a_ab[p] for p in pairs]
        apow = yield from stage(lambda p: _dot(a_ab[p].astype(BF16), stack(a_ab[p])))
        n = 2
        while 2 * n < C:
            res = yield from stage(
                lambda p: _dot(jnp.concatenate([inv[p], apow[p]], axis=0).astype(BF16), stack(apow[p])))
            inv = [inv[p] + res[p][0:C] for p in pairs]
            apow = [res[p][C:S] for p in pairs]
            n *= 2

        def last(p):
            st_inv[slot, p] = (inv[p] + _dot(inv[p].astype(BF16), stack(apow[p]))).astype(BF16)

        yield from stage(last)

    def state(slot):
        _, stack = make_stack()
        head_sum = make_head_sum()
        bi = lax.broadcasted_iota(jnp.int32, (LANES, LANES), 0)
        bj = lax.broadcasted_iota(jnp.int32, (LANES, LANES), 1)
        same_head = (bi // HEAD) == (bj // HEAD)

        s_old = [h_scr[p] for p in pairs]
        x_s = yield from stage(lambda p: _dot_nt(st_lhs[slot, p], s_old[p].astype(BF16)))
        us = yield from stage(lambda p: _dot(st_inv[slot, p], stack(x_s[p][0:C] + st_axv[slot, p, 0:C])))
        ys = yield from stage(
            lambda p: x_s[p][C:S] + st_axv[slot, p, C:S] + _dot(st_arb[slot, p], stack(us[p])))

        def update(p):
            sl = cols(p)
            vu = jnp.concatenate([st_v[slot, :, sl], us[p].astype(BF16)], axis=0)
            khb = jnp.concatenate([st_kh[slot, :, sl], st_bh[slot, :, sl]], axis=0)
            h_scr[p] = s_old[p] * st_gamma[slot, :, sl] + jnp.where(same_head, _dot_tn(vu, khb), 0.0)

        yield from stage(update)

        y = jnp.concatenate(ys, axis=1)
        yc = y - head_sum(y) * (1.0 / HEAD)
        yield
        var = head_sum(yc * yc) * (1.0 / HEAD)
        y = yc * lax.rsqrt(var + GN_EPS) * gnw_ref[...] + gnb_ref[...] + st_bonus[slot]
        yield
        y_ref[...] = (y * _silu(g_ref[...].astype(F32))).astype(y_ref.dtype)

    phase = lax.rem(c, RWKV_SLOTS)
    for r in range(RWKV_SLOTS):
        @pl.when(phase == r)
        def _():
            _interleave((invert((r - 1) % RWKV_SLOTS), RWKV_WEIGHTS[0]),
                        (state((r - 2) % RWKV_SLOTS), RWKV_WEIGHTS[1]),
                        (prepare(r), RWKV_WEIGHTS[2]))

    @pl.when(c == n_chunks + 1)
    def _():
        hout_ref[0] = h_scr[...]


def _rwkv(p_arr, lora_arr, pinit_rkv, pinit_lora, hinit, params, *, batch):
    C = RWKV_CHUNK
    rows = p_arr.shape[0]
    n_chunks = rows // (batch * C)
    const2 = lambda b, c: (0, 0)
    vec = lambda width: pl.BlockSpec((1, width), const2)
    prep_chunk = lambda b, c: b * n_chunks + jnp.minimum(c, n_chunks - 1)
    recur_chunk = lambda b, c: b * n_chunks + jnp.maximum(c - 2, 0)
    slots = lambda shape, dtype: pltpu.VMEM((RWKV_SLOTS,) + shape, dtype)
    return pl.pallas_call(
        functools.partial(_rwkv_kernel, n_chunks=n_chunks),
        out_shape=(jax.ShapeDtypeStruct((rows, RWKV_WIDTH), BF16),
                   jax.ShapeDtypeStruct((batch, N_PAIRS, LANES, LANES), F32)),
        grid=(batch, n_chunks + 2),
        in_specs=[
            pl.BlockSpec((C, 3 * RWKV_WIDTH), lambda b, c: (prep_chunk(b, c), COL_RKV // (3 * RWKV_WIDTH))),
            pl.BlockSpec((C, RWKV_WIDTH), lambda b, c: (recur_chunk(b, c), COL_GR // RWKV_WIDTH)),
            pl.BlockSpec((C, LANES), lambda b, c: (prep_chunk(b, c), 0)),
            pl.BlockSpec((8, 3 * RWKV_WIDTH), const2),
            pl.BlockSpec((8, LANES), const2),
            pl.BlockSpec((1, N_PAIRS, LANES, LANES), lambda b, c: (0, 0, 0, 0)),
            vec(3 * RWKV_WIDTH), vec(LANES), vec(RWKV_WIDTH), vec(RWKV_WIDTH),
            pl.BlockSpec((LANES, 2 * RWKV_WIDTH), const2),
            pl.BlockSpec((LANES, 2 * RWKV_WIDTH), const2),
            vec(RWKV_WIDTH), vec(RWKV_WIDTH), vec(RWKV_WIDTH), vec(RWKV_WIDTH), vec(RWKV_WIDTH),
        ],
        out_specs=(
            pl.BlockSpec((C, RWKV_WIDTH), lambda b, c: (recur_chunk(b, c), 0)),
            pl.BlockSpec((1, N_PAIRS, LANES, LANES), lambda b, c: (b, 0, 0, 0)),
        ),
        scratch_shapes=[
            pltpu.VMEM((N_PAIRS, LANES, LANES), F32),
            pltpu.VMEM((8, 3 * RWKV_WIDTH), F32),
            pltpu.VMEM((8, LANES), F32),
            slots((N_PAIRS, 2 * C, LANES), BF16),
            slots((N_PAIRS, 4 * C, LANES), BF16),
            slots((N_PAIRS, 2 * C, LANES), BF16),
            slots((C, RWKV_WIDTH), BF16),
            slots((C, RWKV_WIDTH), BF16),
            slots((C, RWKV_WIDTH), BF16),
            slots((1, RWKV_WIDTH), F32),
            slots((C, RWKV_WIDTH), F32),
            slots((N_PAIRS, C, 2 * C), BF16),
            slots((N_PAIRS, C, 2 * C), BF16),
            slots((N_PAIRS, 2 * C, LANES), F32),
        ],
        compiler_params=pltpu.CompilerParams(
            dimension_semantics=("arbitrary", "arbitrary"), vmem_limit_bytes=VMEM_LIMIT),
        name="rwkv7_chunk",
    )(p_arr, p_arr, lora_arr, pinit_rkv, pinit_lora, hinit, *params)


def _attn_kernel(q_ref, k_ref, v_ref, g_ref, km_ref, vmt_ref, lq1_ref, lk1_ref, lq2_ref, lk2_ref, sw_ref,
                 o_ref, vt_scr, qst_scr, m_scr, acc_scr, s_scr, *, t, n_blk):
    g_id = pl.program_id(0)

    @pl.when(g_id == 0)
    def _():
        for ref in (vt_scr, qst_scr, m_scr, acc_scr):
            ref[1] = jnp.ones(ref.shape[1:], ref.dtype)

    def setup(slot):
        comp0 = lax.broadcasted_iota(jnp.int32, (LANES, t), 0) < HEAD
        for c in range(n_blk):
            vt_scr[slot, c, 0:LANES, :] = v_ref[c * t:(c + 1) * t, :].astype(F32).T.astype(BF16)
            vt_scr[slot, c, LANES:LANES + ONES_ROWS, :] = jnp.ones((ONES_ROWS, t), BF16)
            yield
        for i in range(n_blk):
            qt = q_ref[i * t:(i + 1) * t, :].astype(F32).T
            qst = jnp.concatenate([jnp.where(comp0, qt, 0.0), jnp.where(comp0, 0.0, qt)], axis=1).astype(BF16)
            qst_scr[slot, i] = qst
            yield
            s = _dot(km_ref[...], qst)
            m = jnp.max(s, axis=0, keepdims=True)
            m_scr[slot, i] = m
            acc_scr[slot, i] = _dot(vmt_ref[0], jnp.exp2(s - m).astype(BF16))
            yield

    def blocks(slot):
        order = sorted(((i, j) for i in range(n_blk) for j in range(i + 1)), key=lambda ij: (ij[1], ij[0]))
        n_slots = s_scr.shape[0]
        m = [m_scr[slot, i] for i in range(n_blk)]

        def issue(n):
            i, j = order[n]
            s_scr[n % n_slots] = _dot(k_ref[j * t:(j + 1) * t, :], qst_scr[slot, i])

        causal = (lax.broadcasted_iota(jnp.int32, (t, 2 * t), 0)
                  <= lax.broadcasted_iota(jnp.int32, (t, 2 * t), 1) % t)
        for n in range(min(ATT_LOOKAHEAD, len(order))):
            issue(n)
        for n, (i, j) in enumerate(order):
            if n + ATT_LOOKAHEAD < len(order):
                issue(n + ATT_LOOKAHEAD)
            s = s_scr[n % n_slots]
            if i == j:
                s = jnp.where(causal, s, MASK_VALUE)
            m_new = jnp.maximum(m[i], jnp.max(s, axis=0, keepdims=True))
            alpha = jnp.exp2(m[i] - m_new)
            m[i] = m_new
            acc_scr[slot, i] = alpha * acc_scr[slot, i] + _dot(vt_scr[slot, j], jnp.exp2(s - m_new).astype(BF16))
            yield

        lam = (jnp.exp(jnp.sum(lq1_ref[...] * lk1_ref[...], axis=-1, keepdims=True))
               - jnp.exp(jnp.sum(lq2_ref[...] * lk2_ref[...], axis=-1, keepdims=True)) + LAMBDA_INIT)
        for i in range(n_blk):
            on = acc_scr[slot, i, 0:LANES, :] / acc_scr[slot, i, LANES:LANES + 1, :]
            ot = on[:, 0:t] - lam * on[:, t:2 * t]
            ot = ot * lax.rsqrt(jnp.mean(ot * ot, axis=0, keepdims=True) + SUBLN_EPS)
            g = g_ref[i * t:(i + 1) * t, :].astype(F32)
            o_ref[i * t:(i + 1) * t, :] = (ot.T * sw_ref[...] * (1.0 - LAMBDA_INIT) * _silu(g)).astype(o_ref.dtype)
            yield

    odd = jnp.bitwise_and(g_id, 1) == 1

    @pl.when(jnp.logical_not(odd))
    def _():
        _interleave((blocks(1), 1), (setup(0), 1))

    @pl.when(odd)
    def _():
        _interleave((blocks(0), 1), (setup(1), 1))


def _attention(p_arr, p_meta, vm_t, lam_vecs, subln_w, *, batch, seq):
    t = ATT_BLOCK
    n_blk = seq // t
    n_heads_total = batch * DIFF_HEADS
    lanes_blk = lambda col: col // LANES
    small = pl.BlockSpec((1, HEAD), lambda g: (0, 0))
    setup_head = lambda g: jnp.minimum(g, n_heads_total - 1)
    block_head = lambda g: jnp.maximum(g - 1, 0)

    def head_cols(col, head_of):
        def index(g):
            h = head_of(g)
            return (h // DIFF_HEADS, lanes_blk(col) + h % DIFF_HEADS)
        return pl.BlockSpec((seq, LANES), index)

    slots = lambda shape, dtype: pltpu.VMEM((2,) + shape, dtype)
    return pl.pallas_call(
        functools.partial(_attn_kernel, t=t, n_blk=n_blk),
        out_shape=jax.ShapeDtypeStruct((batch * seq, DIFF_WIDTH), BF16),
        grid=(n_heads_total + 1,),
        in_specs=[
            head_cols(COL_Q, setup_head), head_cols(COL_K, block_head), head_cols(COL_V, setup_head),
            head_cols(COL_GD, block_head),
            pl.BlockSpec((N_META, LANES), lambda g: (0, lanes_blk(COL_K) + setup_head(g) % DIFF_HEADS)),
            pl.BlockSpec((1, LANES + ONES_ROWS, N_META), lambda g: (setup_head(g) % DIFF_HEADS, 0, 0)),
            small, small, small, small,
            pl.BlockSpec((1, LANES), lambda g: (0, 0)),
        ],
        out_specs=pl.BlockSpec((seq, LANES), lambda g: (block_head(g) // DIFF_HEADS, block_head(g) % DIFF_HEADS)),
        scratch_shapes=[
            slots((n_blk, LANES + ONES_ROWS, t), BF16),
            slots((n_blk, LANES, 2 * t), BF16),
            slots((n_blk, 1, 2 * t), F32),
            slots((n_blk, LANES + ONES_ROWS, 2 * t), F32),
            pltpu.VMEM((ATT_LOOKAHEAD + 1, t, 2 * t), F32),
        ],
        compiler_params=pltpu.CompilerParams(
            dimension_semantics=("arbitrary",), vmem_limit_bytes=VMEM_LIMIT),
        name="diff_attn",
    )(p_arr, p_arr, p_arr, p_arr, p_meta, vm_t, *lam_vecs, subln_w)


def _outproj_kernel(yr_ref, yd_ref, w1_ref, w2_ref, x_ref, g_ref, o_ref):
    y = _dot(yr_ref[...], w1_ref[...]) + _dot(yd_ref[...], w2_ref[...])
    ms = jnp.mean(y * y, axis=-1, keepdims=True)
    o_ref[...] = x_ref[...] + y * lax.rsqrt(ms + NORM_EPS) * g_ref[...]


def _outproj(y_r, y_d, w1, w2, x2, g):
    m = x2.shape[0]
    tm = OUT_TM
    return pl.pallas_call(
        _outproj_kernel,
        out_shape=jax.ShapeDtypeStruct((m, D_MODEL), F32),
        grid=(m // tm,),
        in_specs=[
            pl.BlockSpec((tm, RWKV_WIDTH), lambda i: (i, 0)),
            pl.BlockSpec((tm, DIFF_WIDTH), lambda i: (i, 0)),
            pl.BlockSpec((RWKV_WIDTH, D_MODEL), lambda i: (0, 0)),
            pl.BlockSpec((DIFF_WIDTH, D_MODEL), lambda i: (0, 0)),
            pl.BlockSpec((tm, D_MODEL), lambda i: (i, 0)),
            pl.BlockSpec((1, D_MODEL), lambda i: (0, 0)),
        ],
        out_specs=pl.BlockSpec((tm, D_MODEL), lambda i: (i, 0)),
        compiler_params=pltpu.CompilerParams(
            dimension_semantics=("arbitrary",), vmem_limit_bytes=VMEM_LIMIT),
        name="outproj",
    )(y_r, y_d, w1, w2, x2, g)


def _rope_tables(first_pos, n_pos):
    pos = jnp.arange(first_pos, first_pos + n_pos, dtype=F32)
    inv_freq = ROPE_THETA ** (-jnp.arange(0, HEAD, 2, dtype=F32) / HEAD)
    ang = pos[:, None] * inv_freq[None, :]
    cos = jnp.cos(ang)
    sin = jnp.sin(ang)
    cos = jnp.concatenate([cos, cos, cos, cos], axis=-1)
    sin = jnp.concatenate([-sin, sin, -sin, sin], axis=-1)
    return cos, sin


def kernel(x, meta_tokens, pre_norm_w, w_in, rwkv_mu, rwkv_w0, rwkv_w_up, rwkv_a0, rwkv_a_up, rwkv_k_k, rwkv_k_a, rwkv_r_k, rwkv_gn_w, rwkv_gn_b, diff_lam_q1, diff_lam_k1, diff_lam_q2, diff_lam_k2, diff_subln_w, w_out, post_norm_w):
    batch, seq, d = x.shape
    assert d == D_MODEL and meta_tokens.shape == (N_META, D_MODEL)
    assert seq % RWKV_CHUNK == 0 and seq % ATT_BLOCK == 0
    assert (batch * seq) % PROJ_TM == 0 and seq % PROJ_TM == 0
    layer = 0
    x2 = x.reshape(batch * seq, D_MODEL)

    w = w_in[layer]
    rkv_end = 3 * RWKV_WIDTH
    lora_end = rkv_end + 2 * LORA
    w_main, w_lora_in = _wprep(w)
    g_pre = pre_norm_w[layer].reshape(1, D_MODEL)

    cos_m, sin_m = _rope_tables(0, N_META)
    cos_x, sin_x = _rope_tables(N_META, seq)
    p_meta, lora_meta = _inproj(meta_tokens.astype(x.dtype), g_pre, w_main, w_lora_in, cos_m, sin_m, tm=N_META)
    p_x, lora_x = _inproj(x2, g_pre, w_main, w_lora_in, cos_x, sin_x, tm=PROJ_TM)

    mu = rwkv_mu[layer]
    zeros = jnp.zeros((LORA, RWKV_WIDTH), F32)
    w_lora = jnp.concatenate([
        jnp.concatenate([rwkv_w_up[layer], zeros], axis=1),
        jnp.concatenate([zeros, rwkv_a_up[layer]], axis=1)], axis=0)
    w_lora_hi, w_lora_lo = _split2(w_lora)
    row = lambda t, n: t.reshape(1, n)
    rwkv_params = (
        row(mu[:rkv_end], rkv_end), row(mu[rkv_end:lora_end], LANES),
        row(rwkv_w0[layer], RWKV_WIDTH), row(rwkv_a0[layer], RWKV_WIDTH),
        w_lora_hi, w_lora_lo,
        row(rwkv_k_k[layer], RWKV_WIDTH), row(rwkv_k_a[layer], RWKV_WIDTH),
        row(rwkv_r_k[layer], RWKV_WIDTH), row(rwkv_gn_w[layer], RWKV_WIDTH), row(rwkv_gn_b[layer], RWKV_WIDTH),
    )
    pad = RWKV_CHUNK - N_META
    _, h_meta = _rwkv(jnp.pad(p_meta, ((pad, 0), (0, 0))), jnp.pad(lora_meta, ((pad, 0), (0, 0))),
                      jnp.zeros((8, rkv_end), F32), jnp.zeros((8, LANES), F32),
                      jnp.zeros((1, N_PAIRS, LANES, LANES), F32), rwkv_params, batch=1)
    y_r, _ = _rwkv(p_x, lora_x, p_meta[N_META - 8:, :rkv_end].astype(F32), lora_meta[N_META - 8:],
                   h_meta, rwkv_params, batch=batch)

    lam_vecs = tuple(t[layer].reshape(1, HEAD) for t in (diff_lam_q1, diff_lam_k1, diff_lam_q2, diff_lam_k2))
    vm_t = p_meta[:, COL_V:COL_V + DIFF_WIDTH].reshape(N_META, DIFF_HEADS, LANES).transpose(1, 2, 0)
    vm_t = jnp.concatenate([vm_t, jnp.ones((DIFF_HEADS, ONES_ROWS, N_META), BF16)], axis=1)
    y_d = _attention(p_x, p_meta, vm_t, lam_vecs, diff_subln_w[layer].reshape(1, LANES), batch=batch, seq=seq)

    wo = w_out[layer].astype(BF16)
    out = _outproj(y_r, y_d, wo[:RWKV_WIDTH], wo[RWKV_WIDTH:], x2, post_norm_w[layer].reshape(1, D_MODEL))
    return out.reshape(batch, seq, D_MODEL)
```

```python
import functools
import math

import jax
import jax.numpy as jnp
from jax import lax
from jax.experimental import pallas as pl
from jax.experimental.pallas import tpu as pltpu

F32 = jnp.float32
BF16 = jnp.bfloat16

D_MODEL = 2048
N_META = 16
HEAD = 64
LANES = 128
RWKV_WIDTH = 1024
N_PAIRS = RWKV_WIDTH // LANES
DIFF_WIDTH = 1024
DIFF_HEADS = DIFF_WIDTH // LANES
LORA = 64
ROPE_THETA = 10000.0
NORM_EPS = 1e-6
GN_EPS = 64e-5
SUBLN_EPS = 1e-5
LAMBDA_INIT = 0.8 - 0.6 * math.exp(-0.3 * 0)

COL_RKV = 0
COL_GR = 3 * RWKV_WIDTH
COL_Q = 4 * RWKV_WIDTH
COL_K = COL_Q + DIFF_WIDTH
COL_V = COL_K + DIFF_WIDTH
COL_GD = COL_V + DIFF_WIDTH
P_COLS = COL_GD + DIFF_WIDTH

WPREP_ROWS = 256
PROJ_TM = 1024
PROJ_TN = 1024
PROJ_KC = 512
RWKV_CHUNK = 64
RWKV_SLOTS = 3
RWKV_WEIGHTS = (3, 2, 1)
ATT_BLOCK = 512
ATT_LOOKAHEAD = 1
ATT_WEIGHTS = (3, 2)
OUT_TM = 256
VMEM_LIMIT = 48 * 1024 * 1024
MASK_VALUE = -1e30
ONES_ROWS = 16
Q_SCALE = HEAD ** -0.5 * math.log2(math.e)


def _dot(a, b):
    return jnp.dot(a, b, preferred_element_type=F32)


def _dot_nt(a, b):
    return lax.dot_general(a, b, (((1,), (1,)), ((), ())), preferred_element_type=F32)


def _dot_tn(a, b):
    return lax.dot_general(a, b, (((0,), (0,)), ((), ())), preferred_element_type=F32)


def _split2(x):
    hi = x.astype(BF16)
    lo = (x - hi.astype(F32)).astype(BF16)
    return hi, lo


def _sigmoid(x):
    return 1.0 / (1.0 + jnp.exp(-x))


def _silu(x):
    return x * _sigmoid(x)


def _interleave(*weighted_generators):
    active = [list(gw) for gw in weighted_generators]
    while active:
        for item in list(active):
            gen, weight = item
            for _ in range(weight):
                if next(gen, "done") == "done":
                    active.remove(item)
                    break


def _wprep_kernel(w_ref, main_ref, lora_ref):
    rkv_end = 3 * RWKV_WIDTH
    lora_end = rkv_end + 2 * LORA
    main_ref[:, 0:rkv_end] = w_ref[:, 0:rkv_end].astype(BF16)
    main_ref[:, rkv_end:P_COLS] = w_ref[:, lora_end:lora_end + P_COLS - rkv_end].astype(BF16)
    lora_ref[...] = w_ref[:, rkv_end:lora_end].astype(BF16)


def _wprep(w):
    rows, cols = w.shape
    assert cols == P_COLS + 2 * LORA and rows % WPREP_ROWS == 0
    return pl.pallas_call(
        _wprep_kernel,
        out_shape=(jax.ShapeDtypeStruct((rows, P_COLS), BF16), jax.ShapeDtypeStruct((rows, 2 * LORA), BF16)),
        grid=(rows // WPREP_ROWS,),
        in_specs=[pl.BlockSpec((WPREP_ROWS, cols), lambda i: (i, 0))],
        out_specs=(pl.BlockSpec((WPREP_ROWS, P_COLS), lambda i: (i, 0)),
                   pl.BlockSpec((WPREP_ROWS, 2 * LORA), lambda i: (i, 0))),
        compiler_params=pltpu.CompilerParams(dimension_semantics=("arbitrary",), vmem_limit_bytes=VMEM_LIMIT),
        name="wprep",
    )(w)


def _inproj_kernel(*refs, tn):
    n_k = D_MODEL // PROJ_KC
    x_refs = refs[:n_k]
    g_ref, w_ref, wl_ref, cos_ref, sin_ref, o_ref, ol_ref, hn_ref, rs_ref = refs[n_k:]
    j = pl.program_id(1)
    q_tile = COL_Q // tn
    k_tile = COL_K // tn
    assert 0 < q_tile < k_tile
    is_rope = jnp.logical_or(j == q_tile, j == k_tile)
    col_blocks = [slice(c * LANES, (c + 1) * LANES) for c in range(tn // LANES)]

    @pl.when(j == 0)
    def _():
        ssq = None
        acc = None
        for k in range(n_k):
            sl = slice(k * PROJ_KC, (k + 1) * PROJ_KC)
            xc = x_refs[k][...]
            part = jnp.sum(xc * xc, axis=-1, keepdims=True)
            ssq = part if ssq is None else ssq + part
            hn = (xc * g_ref[:, sl]).astype(BF16)
            hn_ref[:, sl] = hn
            d = _dot(hn, w_ref[sl, :])
            acc = d if acc is None else acc + d
        rs = jnp.broadcast_to(lax.rsqrt(ssq * (1.0 / D_MODEL) + NORM_EPS), rs_ref.shape)
        rs_ref[...] = rs
        ol_ref[...] = _dot(hn_ref[...], wl_ref[...]) * rs
        for cb in col_blocks:
            o_ref[:, cb] = (acc[:, cb] * rs).astype(o_ref.dtype)

    @pl.when(is_rope)
    def _():
        acc = _dot(hn_ref[...], w_ref[...])
        cos = cos_ref[...]
        sin = sin_ref[...]
        lane = lax.broadcasted_iota(jnp.int32, cos.shape, 1)
        first_half = (lane % HEAD) < (HEAD // 2)
        scale = rs_ref[...] * jnp.where(j == q_tile, Q_SCALE, 1.0).astype(F32)
        for cb in col_blocks:
            blk = acc[:, cb]
            swapped = jnp.where(first_half, pltpu.roll(blk, LANES - HEAD // 2, 1),
                                pltpu.roll(blk, HEAD // 2, 1))
            o_ref[:, cb] = ((blk * cos + swapped * sin) * scale).astype(o_ref.dtype)

    @pl.when(jnp.logical_and(j > 0, jnp.logical_not(is_rope)))
    def _():
        acc = _dot(hn_ref[...], w_ref[...])
        rs = rs_ref[...]
        for cb in col_blocks:
            o_ref[:, cb] = (acc[:, cb] * rs).astype(o_ref.dtype)


def _inproj(x2, g, w_main, w_lora, cos, sin, *, tm):
    m = x2.shape[0]
    tn = PROJ_TN
    n_pos_tiles = cos.shape[0] // tm
    n_row_tiles = m // tm
    n_col_tiles = P_COLS // tn
    n_k = D_MODEL // PROJ_KC
    assert n_k <= n_col_tiles

    def x_chunk(k):
        first_step = n_col_tiles - n_k + k
        return pl.BlockSpec(
            (tm, PROJ_KC), lambda i, j: (jnp.minimum(i + (j >= first_step).astype(jnp.int32), n_row_tiles - 1), k))

    return pl.pallas_call(
        functools.partial(_inproj_kernel, tn=tn),
        out_shape=(jax.ShapeDtypeStruct((m, P_COLS), BF16), jax.ShapeDtypeStruct((m, LANES), F32)),
        grid=(n_row_tiles, n_col_tiles),
        in_specs=[x_chunk(k) for k in range(n_k)] + [
            pl.BlockSpec((1, D_MODEL), lambda i, j: (0, 0)),
            pl.BlockSpec((D_MODEL, tn), lambda i, j: (0, j)),
            pl.BlockSpec((D_MODEL, LANES), lambda i, j: (0, 0)),
            pl.BlockSpec((tm, LANES), lambda i, j: (i % n_pos_tiles, 0)),
            pl.BlockSpec((tm, LANES), lambda i, j: (i % n_pos_tiles, 0)),
        ],
        out_specs=(
            pl.BlockSpec((tm, tn), lambda i, j: (i, j)),
            pl.BlockSpec((tm, LANES), lambda i, j: (i, 0)),
        ),
        scratch_shapes=[pltpu.VMEM((tm, D_MODEL), BF16), pltpu.VMEM((tm, LANES), F32)],
        compiler_params=pltpu.CompilerParams(
            dimension_semantics=("arbitrary", "arbitrary"), vmem_limit_bytes=VMEM_LIMIT),
        name="inproj",
    )(*([x2] * n_k), g, w_main, w_lora, cos, sin)


def _rwkv_kernel(rkv_ref, g_ref, lora_ref, pinit_rkv_ref, pinit_lora_ref, hinit_ref,
                 mu_rkv_ref, mu_lora_ref, w0_ref, a0_ref, wl_hi_ref, wl_lo_ref,
                 kk_ref, ka_ref, rk_ref, gnw_ref, gnb_ref,
                 y_ref, hout_ref,
                 h_scr, prev_rkv, prev_lora,
                 st_lhs, st_rhs, st_vs, st_v, st_kh, st_bh, st_gamma, st_bonus,
                 st_inv, st_arb, st_axv, *, n_chunks):
    C = RWKV_CHUNK
    S = 2 * C
    assert S == LANES
    b_id = pl.program_id(0)
    c = pl.program_id(1)
    pairs = range(N_PAIRS)
    staging = (st_lhs, st_rhs, st_vs, st_v, st_kh, st_bh, st_gamma, st_bonus, st_inv, st_arb, st_axv)

    @pl.when(c == 0)
    def _():
        prev_rkv[...] = pinit_rkv_ref[...]
        prev_lora[...] = pinit_lora_ref[...]

    @pl.when(c <= RWKV_SLOTS - 1)
    def _():
        h_scr[...] = hinit_ref[0]

    @pl.when(jnp.logical_and(b_id == 0, c == 0))
    def _():
        for ref in staging:
            for slot in range(1, RWKV_SLOTS):
                ref[slot] = jnp.zeros(ref.shape[1:], ref.dtype)

    def cols(p, base=0):
        return slice(base + p * LANES, base + (p + 1) * LANES)

    def make_stack():
        head0 = lax.broadcasted_iota(jnp.int32, (C, LANES), 1) < HEAD

        def stack(t):
            return jnp.concatenate([jnp.where(head0, t, 0.0), jnp.where(head0, 0.0, t)], axis=0).astype(BF16)
        return head0, stack

    def make_head_sum():
        li = lax.broadcasted_iota(jnp.int32, (2 * LANES, LANES), 0)
        lj = lax.broadcasted_iota(jnp.int32, (2 * LANES, LANES), 1)
        ones_bd2 = jnp.where(((li % LANES) // HEAD) == (lj // HEAD), 1.0, 0.0).astype(BF16)

        def head_sum(t):
            rows = jnp.concatenate([t[:, cols(p)] for p in pairs], axis=0)
            hi, lo = _split2(rows)
            s = _dot(jnp.concatenate([hi, lo], axis=1), ones_bd2)
            return jnp.concatenate([s[p * C:(p + 1) * C] for p in pairs], axis=1)
        return head_sum

    def prepare(slot):
        head0, stack = make_stack()
        head_sum = make_head_sum()
        row = lax.broadcasted_iota(jnp.int32, (C, 1), 0)

        def token_shift(x, prev8, mu):
            xp = jnp.where(row == 0, prev8[7:8, :], pltpu.roll(x, 1, 0))
            return x + (xp - x) * mu

        xl = lora_ref[...]
        ul = token_shift(xl, prev_lora[...], mu_lora_ref[...])
        prev_lora[...] = xl[C - 8:C]
        tl_hi, tl_lo = _split2(jnp.where(head0, jnp.tanh(ul), ul))
        wl_hi = wl_hi_ref[...]
        lo_out = _dot(tl_hi, wl_hi) + _dot(tl_lo, wl_hi) + _dot(tl_hi, wl_lo_ref[...])
        yield

        def shifted(base):
            sl = slice(base, base + RWKV_WIDTH)
            xs = rkv_ref[:, sl].astype(F32)
            out = token_shift(xs, prev_rkv[:, sl], mu_rkv_ref[:, sl])
            prev_rkv[:, sl] = xs[C - 8:C]
            return out

        r = shifted(0)
        yield
        k = shifted(RWKV_WIDTH)
        yield
        v = shifted(2 * RWKV_WIDTH)
        yield

        logdec = -math.exp(-0.5) * _sigmoid(w0_ref[...] + lo_out[:, 0:RWKV_WIDTH])
        yield
        a_lr = _sigmoid(a0_ref[...] + lo_out[:, RWKV_WIDTH:2 * RWKV_WIDTH])
        yield

        ti = lax.broadcasted_iota(jnp.int32, (C, C), 0)
        si = lax.broadcasted_iota(jnp.int32, (C, C), 1)
        tri = jnp.where(ti >= si, 1.0, 0.0).astype(BF16)
        ld_hi = logdec.astype(BF16)
        rem = logdec - ld_hi.astype(F32)
        ld_mid = rem.astype(BF16)
        ld_lo = (rem - ld_mid.astype(F32)).astype(BF16)
        cum = _dot(tri, ld_hi) + _dot(tri, ld_mid) + _dot(tri, ld_lo)
        cum_last = cum[C - 1:C, :]
        yield
        e_excl = jnp.exp(cum - logdec)
        e_incl = jnp.exp(cum)
        yield
        e_neg = jnp.exp(-cum)
        e_hat = jnp.exp(cum_last - cum)
        yield

        kk = k * kk_ref[...]
        kk = kk * lax.rsqrt(jnp.maximum(head_sum(kk * kk), 1e-24))
        yield
        k2 = k * (1.0 + (a_lr - 1.0) * ka_ref[...])
        b = kk * a_lr
        yield
        xa_all = -kk * e_excl
        xr_all = r * e_incl
        yield
        yb_all = b * e_neg
        yk_all = k2 * e_neg
        yield

        st_gamma[slot] = jnp.exp(cum_last)
        st_bonus[slot] = head_sum(r * k2 * rk_ref[...]) * v
        yield
        st_v[slot] = v.astype(BF16)
        st_kh[slot] = (k2 * e_hat).astype(BF16)
        st_bh[slot] = (b * e_hat).astype(BF16)
        yield
        for p in pairs:
            sl = cols(p)
            st_lhs[slot, p] = jnp.concatenate([xa_all[:, sl], xr_all[:, sl]], axis=0).astype(BF16)
            st_rhs[slot, p] = jnp.concatenate([stack(yb_all[:, sl]), stack(yk_all[:, sl])], axis=0)
            st_vs[slot, p] = stack(v[:, sl])
            yield

    def stage(fn):
        out = []
        for p in pairs:
            out.append(fn(p))
            yield
        return out

    def invert(slot):
        _, stack = make_stack()
        ri = lax.broadcasted_iota(jnp.int32, (S, 2 * S), 0)
        ci = lax.broadcasted_iota(jnp.int32, (S, 2 * S), 1)
        a_mask = (ri % C) >= (ci % C) + jnp.where(ri < C, 1, 0)
        ei = lax.broadcasted_iota(jnp.int32, (C, S), 0)
        ej = lax.broadcasted_iota(jnp.int32, (C, S), 1)
        eye = jnp.where(ei == ej % C, 1.0, 0.0).astype(F32)

        a_all = yield from stage(lambda p: jnp.where(a_mask, _dot_nt(st_lhs[slot, p], st_rhs[slot, p]), 0.0))
        a_ab = [a_all[p][0:C, 0:S] for p in pairs]

        def values(p):
            st_arb[slot, p] = a_all[p][C:S, 0:S].astype(BF16)
            st_axv[slot, p] = _dot(a_all[p][:, S:2 * S].astype(BF16), st_vs[slot, p])

        yield from stage(values)

        inv = [eye + a_ab[p] for p in pairs]
        apow = yield from stage(lambda p: _dot(a_ab[p].astype(BF16), stack(a_ab[p])))
        n = 2
        while 2 * n < C:
            res = yield from stage(
                lambda p: _dot(jnp.concatenate([inv[p], apow[p]], axis=0).astype(BF16), stack(apow[p])))
            inv = [inv[p] + res[p][0:C] for p in pairs]
            apow = [res[p][C:S] for p in pairs]
            n *= 2

        def last(p):
            st_inv[slot, p] = (inv[p] + _dot(inv[p].astype(BF16), stack(apow[p]))).astype(BF16)

        yield from stage(last)

    def state(slot):
        _, stack = make_stack()
        head_sum = make_head_sum()
        bi = lax.broadcasted_iota(jnp.int32, (LANES, LANES), 0)
        bj = lax.broadcasted_iota(jnp.int32, (LANES, LANES), 1)
        same_head = (bi // HEAD) == (bj // HEAD)

        s_old = [h_scr[p] for p in pairs]
        x_s = yield from stage(lambda p: _dot_nt(st_lhs[slot, p], s_old[p].astype(BF16)))
        us = yield from stage(lambda p: _dot(st_inv[slot, p], stack(x_s[p][0:C] + st_axv[slot, p, 0:C])))
        ys = yield from stage(
            lambda p: x_s[p][C:S] + st_axv[slot, p, C:S] + _dot(st_arb[slot, p], stack(us[p])))

        def update(p):
            sl = cols(p)
            vu = jnp.concatenate([st_v[slot, :, sl], us[p].astype(BF16)], axis=0)
            khb = jnp.concatenate([st_kh[slot, :, sl], st_bh[slot, :, sl]], axis=0)
            h_scr[p] = s_old[p] * st_gamma[slot, :, sl] + jnp.where(same_head, _dot_tn(vu, khb), 0.0)

        yield from stage(update)

        y = jnp.concatenate(ys, axis=1)
        yc = y - head_sum(y) * (1.0 / HEAD)
        yield
        var = head_sum(yc * yc) * (1.0 / HEAD)
        y = yc * lax.rsqrt(var + GN_EPS) * gnw_ref[...] + gnb_ref[...] + st_bonus[slot]
        yield
        y_ref[...] = (y * _silu(g_ref[...].astype(F32))).astype(y_ref.dtype)

    phase = lax.rem(c, RWKV_SLOTS)
    for r in range(RWKV_SLOTS):
        @pl.when(phase == r)
        def _():
            _interleave((invert((r - 1) % RWKV_SLOTS), RWKV_WEIGHTS[0]),
                        (state((r - 2) % RWKV_SLOTS), RWKV_WEIGHTS[1]),
                        (prepare(r), RWKV_WEIGHTS[2]))

    @pl.when(c == n_chunks + RWKV_SLOTS - 2)
    def _():
        hout_ref[0] = h_scr[...]


def _rwkv(p_arr, lora_arr, pinit_rkv, pinit_lora, hinit, params, *, batch):
    C = RWKV_CHUNK
    rows = p_arr.shape[0]
    n_chunks = rows // (batch * C)
    const2 = lambda b, c: (0, 0)
    vec = lambda width: pl.BlockSpec((1, width), const2)
    prep_chunk = lambda b, c: b * n_chunks + jnp.minimum(c, n_chunks - 1)
    recur_chunk = lambda b, c: b * n_chunks + jnp.maximum(c - (RWKV_SLOTS - 1), 0)
    slots = lambda shape, dtype: pltpu.VMEM((RWKV_SLOTS,) + shape, dtype)
    return pl.pallas_call(
        functools.partial(_rwkv_kernel, n_chunks=n_chunks),
        out_shape=(jax.ShapeDtypeStruct((rows, RWKV_WIDTH), BF16),
                   jax.ShapeDtypeStruct((batch, N_PAIRS, LANES, LANES), F32)),
        grid=(batch, n_chunks + RWKV_SLOTS - 1),
        in_specs=[
            pl.BlockSpec((C, 3 * RWKV_WIDTH), lambda b, c: (prep_chunk(b, c), COL_RKV // (3 * RWKV_WIDTH))),
            pl.BlockSpec((C, RWKV_WIDTH), lambda b, c: (recur_chunk(b, c), COL_GR // RWKV_WIDTH)),
            pl.BlockSpec((C, LANES), lambda b, c: (prep_chunk(b, c), 0)),
            pl.BlockSpec((8, 3 * RWKV_WIDTH), const2),
            pl.BlockSpec((8, LANES), const2),
            pl.BlockSpec((1, N_PAIRS, LANES, LANES), lambda b, c: (0, 0, 0, 0)),
            vec(3 * RWKV_WIDTH), vec(LANES), vec(RWKV_WIDTH), vec(RWKV_WIDTH),
            pl.BlockSpec((LANES, 2 * RWKV_WIDTH), const2),
            pl.BlockSpec((LANES, 2 * RWKV_WIDTH), const2),
            vec(RWKV_WIDTH), vec(RWKV_WIDTH), vec(RWKV_WIDTH), vec(RWKV_WIDTH), vec(RWKV_WIDTH),
        ],
        out_specs=(
            pl.BlockSpec((C, RWKV_WIDTH), lambda b, c: (recur_chunk(b, c), 0)),
            pl.BlockSpec((1, N_PAIRS, LANES, LANES), lambda b, c: (b, 0, 0, 0)),
        ),
        scratch_shapes=[
            pltpu.VMEM((N_PAIRS, LANES, LANES), F32),
            pltpu.VMEM((8, 3 * RWKV_WIDTH), F32),
            pltpu.VMEM((8, LANES), F32),
            slots((N_PAIRS, 2 * C, LANES), BF16),
            slots((N_PAIRS, 4 * C, LANES), BF16),
            slots((N_PAIRS, 2 * C, LANES), BF16),
            slots((C, RWKV_WIDTH), BF16),
            slots((C, RWKV_WIDTH), BF16),
            slots((C, RWKV_WIDTH), BF16),
            slots((1, RWKV_WIDTH), F32),
            slots((C, RWKV_WIDTH), F32),
            slots((N_PAIRS, C, 2 * C), BF16),
            slots((N_PAIRS, C, 2 * C), BF16),
            slots((N_PAIRS, 2 * C, LANES), F32),
        ],
        compiler_params=pltpu.CompilerParams(
            dimension_semantics=("arbitrary", "arbitrary"), vmem_limit_bytes=VMEM_LIMIT),
        name="rwkv7_chunk",
    )(p_arr, p_arr, lora_arr, pinit_rkv, pinit_lora, hinit, *params)


def _attn_kernel(q_ref, k_ref, v_ref, g_ref, km_ref, vmt_ref, lq1_ref, lk1_ref, lq2_ref, lk2_ref, sw_ref,
                 o_ref, vt_scr, qst_scr, m_scr, acc_scr, s_scr, *, t, n_blk):
    g_id = pl.program_id(0)

    @pl.when(g_id == 0)
    def _():
        for ref in (vt_scr, qst_scr, m_scr, acc_scr):
            ref[1] = jnp.ones(ref.shape[1:], ref.dtype)

    def setup(slot):
        comp0 = lax.broadcasted_iota(jnp.int32, (LANES, t // 2), 0) < HEAD
        for c in range(n_blk):
            vt_scr[slot, c, 0:LANES, :] = v_ref[c * t:(c + 1) * t, :].astype(F32).T.astype(BF16)
            vt_scr[slot, c, LANES:LANES + ONES_ROWS, :] = jnp.ones((ONES_ROWS, t), BF16)
            yield
        for i in range(n_blk):
            qt = q_ref[i * t:(i + 1) * t, :].astype(F32).T
            parts = []
            for half in (qt[:, 0:t // 2], qt[:, t // 2:t]):
                parts += [jnp.where(comp0, half, 0.0), jnp.where(comp0, 0.0, half)]
            qst = jnp.concatenate(parts, axis=1).astype(BF16)
            qst_scr[slot, i] = qst
            yield
            s = _dot(km_ref[...], qst)
            m = jnp.max(s, axis=0, keepdims=True)
            m_scr[slot, i] = m
            acc_scr[slot, i] = _dot(vmt_ref[0], jnp.exp2(s - m).astype(BF16))
            yield

    def blocks(slot):
        h = t // 2
        full_rows, lo_rows, hi_rows = slice(0, t), slice(0, h), slice(h, t)
        all_cols, hi_cols = slice(0, 2 * t), slice(t, 2 * t)
        units = []
        for i, j in sorted(((i, j) for i in range(n_blk) for j in range(i + 1)), key=lambda ij: (ij[1], ij[0])):
            if i == j:
                units += [(i, j, lo_rows, all_cols, True), (i, j, hi_rows, hi_cols, True)]
            else:
                units.append((i, j, full_rows, all_cols, False))
        n_slots = s_scr.shape[0]
        m = [m_scr[slot, i] for i in range(n_blk)]

        def extent(sl):
            return sl.stop - sl.start

        def issue(n):
            i, j, rows, cols, _ = units[n]
            keys = k_ref[j * t + rows.start:j * t + rows.stop, :]
            s_scr[n % n_slots, 0:extent(rows), 0:extent(cols)] = _dot(keys, qst_scr[slot, i, :, cols])

        def visible(n_cols):
            kpos = lax.broadcasted_iota(jnp.int32, (h, n_cols), 0)
            col = lax.broadcasted_iota(jnp.int32, (h, n_cols), 1)
            return kpos <= col % h + jnp.where(col >= t, h, 0)

        visible = {n_cols: visible(n_cols) for n_cols in (t, 2 * t)}
        for n in range(min(ATT_LOOKAHEAD, len(units))):
            issue(n)
        for n, (i, j, rows, cols, masked) in enumerate(units):
            if n + ATT_LOOKAHEAD < len(units):
                issue(n + ATT_LOOKAHEAD)
            s = s_scr[n % n_slots, 0:extent(rows), 0:extent(cols)]
            if masked:
                s = jnp.where(visible[extent(cols)], s, MASK_VALUE)
            m_old = m[i][:, cols]
            m_new = jnp.maximum(m_old, jnp.max(s, axis=0, keepdims=True))
            alpha = jnp.exp2(m_old - m_new)
            m[i] = m_new if extent(cols) == 2 * t else jnp.concatenate([m[i][:, 0:cols.start], m_new], axis=1)
            acc_scr[slot, i, :, cols] = (alpha * acc_scr[slot, i, :, cols]
                                         + _dot(vt_scr[slot, j, :, rows], jnp.exp2(s - m_new).astype(BF16)))
            yield

        lam = (jnp.exp(jnp.sum(lq1_ref[...] * lk1_ref[...], axis=-1, keepdims=True))
               - jnp.exp(jnp.sum(lq2_ref[...] * lk2_ref[...], axis=-1, keepdims=True)) + LAMBDA_INIT)
        for i in range(n_blk):
            on = acc_scr[slot, i, 0:LANES, :] / acc_scr[slot, i, LANES:LANES + 1, :]
            comp = lambda c: jnp.concatenate([on[:, c * h:(c + 1) * h], on[:, t + c * h:t + (c + 1) * h]], axis=1)
            ot = comp(0) - lam * comp(1)
            ot = ot * lax.rsqrt(jnp.mean(ot * ot, axis=0, keepdims=True) + SUBLN_EPS)
            g = g_ref[i * t:(i + 1) * t, :].astype(F32)
            o_ref[i * t:(i + 1) * t, :] = (ot.T * sw_ref[...] * (1.0 - LAMBDA_INIT) * _silu(g)).astype(o_ref.dtype)
            yield

    odd = jnp.bitwise_and(g_id, 1) == 1

    @pl.when(jnp.logical_not(odd))
    def _():
        _interleave((blocks(1), ATT_WEIGHTS[0]), (setup(0), ATT_WEIGHTS[1]))

    @pl.when(odd)
    def _():
        _interleave((blocks(0), ATT_WEIGHTS[0]), (setup(1), ATT_WEIGHTS[1]))


def _attention(p_arr, p_meta, vm_t, lam_vecs, subln_w, *, batch, seq):
    t = ATT_BLOCK
    n_blk = seq // t
    n_heads_total = batch * DIFF_HEADS
    lanes_blk = lambda col: col // LANES
    small = pl.BlockSpec((1, HEAD), lambda g: (0, 0))
    setup_head = lambda g: jnp.minimum(g, n_heads_total - 1)
    block_head = lambda g: jnp.maximum(g - 1, 0)

    def head_cols(col, head_of):
        def index(g):
            h = head_of(g)
            return (h // DIFF_HEADS, lanes_blk(col) + h % DIFF_HEADS)
        return pl.BlockSpec((seq, LANES), index)

    slots = lambda shape, dtype: pltpu.VMEM((2,) + shape, dtype)
    return pl.pallas_call(
        functools.partial(_attn_kernel, t=t, n_blk=n_blk),
        out_shape=jax.ShapeDtypeStruct((batch * seq, DIFF_WIDTH), BF16),
        grid=(n_heads_total + 1,),
        in_specs=[
            head_cols(COL_Q, setup_head), head_cols(COL_K, block_head), head_cols(COL_V, setup_head),
            head_cols(COL_GD, block_head),
            pl.BlockSpec((N_META, LANES), lambda g: (0, lanes_blk(COL_K) + setup_head(g) % DIFF_HEADS)),
            pl.BlockSpec((1, LANES + ONES_ROWS, N_META), lambda g: (setup_head(g) % DIFF_HEADS, 0, 0)),
            small, small, small, small,
            pl.BlockSpec((1, LANES), lambda g: (0, 0)),
        ],
        out_specs=pl.BlockSpec((seq, LANES), lambda g: (block_head(g) // DIFF_HEADS, block_head(g) % DIFF_HEADS)),
        scratch_shapes=[
            slots((n_blk, LANES + ONES_ROWS, t), BF16),
            slots((n_blk, LANES, 2 * t), BF16),
            slots((n_blk, 1, 2 * t), F32),
            slots((n_blk, LANES + ONES_ROWS, 2 * t), F32),
            pltpu.VMEM((ATT_LOOKAHEAD + 1, t, 2 * t), F32),
        ],
        compiler_params=pltpu.CompilerParams(
            dimension_semantics=("arbitrary",), vmem_limit_bytes=VMEM_LIMIT),
        name="diff_attn",
    )(p_arr, p_arr, p_arr, p_arr, p_meta, vm_t, *lam_vecs, subln_w)


def _outproj_kernel(yr_ref, yd_ref, w1_ref, w2_ref, x_ref, g_ref, o_ref):
    y = _dot(yr_ref[...], w1_ref[...]) + _dot(yd_ref[...], w2_ref[...])
    ms = jnp.mean(y * y, axis=-1, keepdims=True)
    o_ref[...] = x_ref[...] + y * lax.rsqrt(ms + NORM_EPS) * g_ref[...]


def _outproj(y_r, y_d, w1, w2, x2, g):
    m = x2.shape[0]
    tm = OUT_TM
    return pl.pallas_call(
        _outproj_kernel,
        out_shape=jax.ShapeDtypeStruct((m, D_MODEL), F32),
        grid=(m // tm,),
        in_specs=[
            pl.BlockSpec((tm, RWKV_WIDTH), lambda i: (i, 0)),
            pl.BlockSpec((tm, DIFF_WIDTH), lambda i: (i, 0)),
            pl.BlockSpec((RWKV_WIDTH, D_MODEL), lambda i: (0, 0)),
            pl.BlockSpec((DIFF_WIDTH, D_MODEL), lambda i: (0, 0)),
            pl.BlockSpec((tm, D_MODEL), lambda i: (i, 0)),
            pl.BlockSpec((1, D_MODEL), lambda i: (0, 0)),
        ],
        out_specs=pl.BlockSpec((tm, D_MODEL), lambda i: (i, 0)),
        compiler_params=pltpu.CompilerParams(
            dimension_semantics=("arbitrary",), vmem_limit_bytes=VMEM_LIMIT),
        name="outproj",
    )(y_r, y_d, w1, w2, x2, g)


def _rope_tables(first_pos, n_pos):
    pos = jnp.arange(first_pos, first_pos + n_pos, dtype=F32)
    inv_freq = ROPE_THETA ** (-jnp.arange(0, HEAD, 2, dtype=F32) / HEAD)
    ang = pos[:, None] * inv_freq[None, :]
    cos = jnp.cos(ang)
    sin = jnp.sin(ang)
    cos = jnp.concatenate([cos, cos, cos, cos], axis=-1)
    sin = jnp.concatenate([-sin, sin, -sin, sin], axis=-1)
    return cos, sin


def kernel(x, meta_tokens, pre_norm_w, w_in, rwkv_mu, rwkv_w0, rwkv_w_up, rwkv_a0, rwkv_a_up, rwkv_k_k, rwkv_k_a, rwkv_r_k, rwkv_gn_w, rwkv_gn_b, diff_lam_q1, diff_lam_k1, diff_lam_q2, diff_lam_k2, diff_subln_w, w_out, post_norm_w):
    batch, seq, d = x.shape
    assert d == D_MODEL and meta_tokens.shape == (N_META, D_MODEL)
    assert seq % RWKV_CHUNK == 0 and seq % ATT_BLOCK == 0
    assert (batch * seq) % PROJ_TM == 0 and seq % PROJ_TM == 0
    layer = 0
    x2 = x.reshape(batch * seq, D_MODEL)

    w = w_in[layer]
    rkv_end = 3 * RWKV_WIDTH
    lora_end = rkv_end + 2 * LORA
    w_main, w_lora_in = _wprep(w)
    g_pre = pre_norm_w[layer].reshape(1, D_MODEL)

    cos_m, sin_m = _rope_tables(0, N_META)
    cos_x, sin_x = _rope_tables(N_META, seq)
    p_meta, lora_meta = _inproj(meta_tokens.astype(x.dtype), g_pre, w_main, w_lora_in, cos_m, sin_m, tm=N_META)
    p_x, lora_x = _inproj(x2, g_pre, w_main, w_lora_in, cos_x, sin_x, tm=PROJ_TM)

    mu = rwkv_mu[layer]
    zeros = jnp.zeros((LORA, RWKV_WIDTH), F32)
    w_lora = jnp.concatenate([
        jnp.concatenate([rwkv_w_up[layer], zeros], axis=1),
        jnp.concatenate([zeros, rwkv_a_up[layer]], axis=1)], axis=0)
    w_lora_hi, w_lora_lo = _split2(w_lora)
    row = lambda t, n: t.reshape(1, n)
    rwkv_params = (
        row(mu[:rkv_end], rkv_end), row(mu[rkv_end:lora_end], LANES),
        row(rwkv_w0[layer], RWKV_WIDTH), row(rwkv_a0[layer], RWKV_WIDTH),
        w_lora_hi, w_lora_lo,
        row(rwkv_k_k[layer], RWKV_WIDTH), row(rwkv_k_a[layer], RWKV_WIDTH),
        row(rwkv_r_k[layer], RWKV_WIDTH), row(rwkv_gn_w[layer], RWKV_WIDTH), row(rwkv_gn_b[layer], RWKV_WIDTH),
    )
    pad = RWKV_CHUNK - N_META
    _, h_meta = _rwkv(jnp.pad(p_meta, ((pad, 0), (0, 0))), jnp.pad(lora_meta, ((pad, 0), (0, 0))),
                      jnp.zeros((8, rkv_end), F32), jnp.zeros((8, LANES), F32),
                      jnp.zeros((1, N_PAIRS, LANES, LANES), F32), rwkv_params, batch=1)
    y_r, _ = _rwkv(p_x, lora_x, p_meta[N_META - 8:, :rkv_end].astype(F32), lora_meta[N_META - 8:],
                   h_meta, rwkv_params, batch=batch)

    lam_vecs = tuple(t[layer].reshape(1, HEAD) for t in (diff_lam_q1, diff_lam_k1, diff_lam_q2, diff_lam_k2))
    vm_t = p_meta[:, COL_V:COL_V + DIFF_WIDTH].reshape(N_META, DIFF_HEADS, LANES).transpose(1, 2, 0)
    vm_t = jnp.concatenate([vm_t, jnp.ones((DIFF_HEADS, ONES_ROWS, N_META), BF16)], axis=1)
    y_d = _attention(p_x, p_meta, vm_t, lam_vecs, diff_subln_w[layer].reshape(1, LANES), batch=batch, seq=seq)

    wo = w_out[layer].astype(BF16)
    out = _outproj(y_r, y_d, wo[:RWKV_WIDTH], wo[RWKV_WIDTH:], x2, post_norm_w[layer].reshape(1, D_MODEL))
    return out.reshape(batch, seq, D_MODEL)
```

```python
import functools
import math

import jax
import jax.numpy as jnp
from jax import lax
from jax.experimental import pallas as pl
from jax.experimental.pallas import tpu as pltpu

F32 = jnp.float32
BF16 = jnp.bfloat16

D_MODEL = 2048
N_META = 16
HEAD = 64
LANES = 128
RWKV_WIDTH = 1024
N_PAIRS = RWKV_WIDTH // LANES
DIFF_WIDTH = 1024
DIFF_HEADS = DIFF_WIDTH // LANES
LORA = 64
ROPE_THETA = 10000.0
NORM_EPS = 1e-6
GN_EPS = 64e-5
SUBLN_EPS = 1e-5
LAMBDA_INIT = 0.8 - 0.6 * math.exp(-0.3 * 0)

COL_RKV = 0
COL_GR = 3 * RWKV_WIDTH
COL_Q = 4 * RWKV_WIDTH
COL_K = COL_Q + DIFF_WIDTH
COL_V = COL_K + DIFF_WIDTH
COL_GD = COL_V + DIFF_WIDTH
P_COLS = COL_GD + DIFF_WIDTH

WPREP_ROWS = 256
PROJ_TM = 1024
PROJ_TN = 1024
PROJ_KC = 512
RWKV_CHUNK = 64
RWKV_SLOTS = 3
RWKV_WEIGHTS = (3, 2, 1)
ATT_BLOCK = 512
ATT_LOOKAHEAD = 1
ATT_WEIGHTS = (3, 2)
OUT_TM = 256
VMEM_LIMIT = 48 * 1024 * 1024
MASK_VALUE = -1e30
ONES_ROWS = 16
Q_SCALE = HEAD ** -0.5 * math.log2(math.e)


def _dot(a, b):
    return jnp.dot(a, b, preferred_element_type=F32)


def _dot_nt(a, b):
    return lax.dot_general(a, b, (((1,), (1,)), ((), ())), preferred_element_type=F32)


def _dot_tn(a, b):
    return lax.dot_general(a, b, (((0,), (0,)), ((), ())), preferred_element_type=F32)


def _split2(x):
    hi = x.astype(BF16)
    lo = (x - hi.astype(F32)).astype(BF16)
    return hi, lo


def _sigmoid(x):
    return 1.0 / (1.0 + jnp.exp(-x))


def _silu(x):
    return x * _sigmoid(x)


def _interleave(*weighted_generators):
    active = [list(gw) for gw in weighted_generators]
    while active:
        for item in list(active):
            gen, weight = item
            for _ in range(weight):
                if next(gen, "done") == "done":
                    active.remove(item)
                    break


def _rotate_half(blk, cos, sin):
    lane = lax.broadcasted_iota(jnp.int32, blk.shape, 1)
    first_half = (lane % HEAD) < (HEAD // 2)
    swapped = jnp.where(first_half, pltpu.roll(blk, LANES - HEAD // 2, 1), pltpu.roll(blk, HEAD // 2, 1))
    return blk * cos + swapped * sin


def _wprep_kernel(w_ref, meta_ref, g_ref, cos_ref, sin_ref, main_ref, lora_ref, pm_ref, lm_ref,
                  acc_ref, lacc_ref, ssq_ref):
    i = pl.program_id(0)
    rkv_end = 3 * RWKV_WIDTH
    lora_end = rkv_end + 2 * LORA
    w_rkv = w_ref[:, 0:rkv_end].astype(BF16)
    w_lora = w_ref[:, rkv_end:lora_end].astype(BF16)
    w_rest = w_ref[:, lora_end:lora_end + P_COLS - rkv_end].astype(BF16)
    main_ref[:, 0:rkv_end] = w_rkv
    main_ref[:, rkv_end:P_COLS] = w_rest
    lora_ref[...] = w_lora

    @pl.when(i == 0)
    def _():
        acc_ref[...] = jnp.zeros_like(acc_ref)
        lacc_ref[...] = jnp.zeros_like(lacc_ref)
        ssq_ref[...] = jnp.zeros_like(ssq_ref)

    xm = meta_ref[0]
    hn = (xm * g_ref[0]).astype(BF16)
    ssq_ref[...] += jnp.broadcast_to(jnp.sum(xm * xm, axis=-1, keepdims=True), ssq_ref.shape)
    acc_ref[:, 0:rkv_end] += _dot(hn, w_rkv)
    acc_ref[:, rkv_end:P_COLS] += _dot(hn, w_rest)
    lacc_ref[...] += _dot(hn, w_lora)

    @pl.when(i == pl.num_programs(0) - 1)
    def _():
        rs = lax.rsqrt(ssq_ref[...] * (1.0 / D_MODEL) + NORM_EPS)
        lm_ref[...] = lacc_ref[...] * rs
        for c in range(P_COLS // LANES):
            cb = slice(c * LANES, (c + 1) * LANES)
            blk = acc_ref[:, cb]
            if COL_Q <= c * LANES < COL_K:
                blk = _rotate_half(blk, cos_ref[...], sin_ref[...]) * Q_SCALE
            elif COL_K <= c * LANES < COL_V:
                blk = _rotate_half(blk, cos_ref[...], sin_ref[...])
            pm_ref[:, cb] = (blk * rs).astype(pm_ref.dtype)


def _wprep(w, meta_tokens, g, cos, sin):
    rows, cols = w.shape
    assert cols == P_COLS + 2 * LORA and rows % WPREP_ROWS == 0
    n_steps = rows // WPREP_ROWS
    meta_k = meta_tokens.reshape(N_META, n_steps, WPREP_ROWS).transpose(1, 0, 2)
    g_k = g.reshape(n_steps, 1, WPREP_ROWS)
    const = lambda i: (0, 0)
    return pl.pallas_call(
        _wprep_kernel,
        out_shape=(jax.ShapeDtypeStruct((rows, P_COLS), BF16), jax.ShapeDtypeStruct((rows, 2 * LORA), BF16),
                   jax.ShapeDtypeStruct((N_META, P_COLS), BF16), jax.ShapeDtypeStruct((N_META, LANES), F32)),
        grid=(n_steps,),
        in_specs=[
            pl.BlockSpec((WPREP_ROWS, cols), lambda i: (i, 0)),
            pl.BlockSpec((1, N_META, WPREP_ROWS), lambda i: (i, 0, 0)),
            pl.BlockSpec((1, 1, WPREP_ROWS), lambda i: (i, 0, 0)),
            pl.BlockSpec((N_META, LANES), const),
            pl.BlockSpec((N_META, LANES), const),
        ],
        out_specs=(pl.BlockSpec((WPREP_ROWS, P_COLS), lambda i: (i, 0)),
                   pl.BlockSpec((WPREP_ROWS, 2 * LORA), lambda i: (i, 0)),
                   pl.BlockSpec((N_META, P_COLS), const),
                   pl.BlockSpec((N_META, LANES), const)),
        scratch_shapes=[pltpu.VMEM((N_META, P_COLS), F32), pltpu.VMEM((N_META, LANES), F32),
                        pltpu.VMEM((N_META, LANES), F32)],
        compiler_params=pltpu.CompilerParams(dimension_semantics=("arbitrary",), vmem_limit_bytes=VMEM_LIMIT),
        name="wprep",
    )(w, meta_k, g_k, cos, sin)


def _inproj_kernel(*refs, tn):
    n_k = D_MODEL // PROJ_KC
    x_refs = refs[:n_k]
    g_ref, w_ref, wl_ref, cos_ref, sin_ref, o_ref, ol_ref, hn_ref, rs_ref = refs[n_k:]
    j = pl.program_id(1)
    q_tile = COL_Q // tn
    k_tile = COL_K // tn
    assert 0 < q_tile < k_tile
    is_rope = jnp.logical_or(j == q_tile, j == k_tile)
    col_blocks = [slice(c * LANES, (c + 1) * LANES) for c in range(tn // LANES)]

    @pl.when(j == 0)
    def _():
        ssq = None
        acc = None
        for k in range(n_k):
            sl = slice(k * PROJ_KC, (k + 1) * PROJ_KC)
            xc = x_refs[k][...]
            part = jnp.sum(xc * xc, axis=-1, keepdims=True)
            ssq = part if ssq is None else ssq + part
            hn = (xc * g_ref[:, sl]).astype(BF16)
            hn_ref[:, sl] = hn
            d = _dot(hn, w_ref[sl, :])
            acc = d if acc is None else acc + d
        rs = jnp.broadcast_to(lax.rsqrt(ssq * (1.0 / D_MODEL) + NORM_EPS), rs_ref.shape)
        rs_ref[...] = rs
        ol_ref[...] = _dot(hn_ref[...], wl_ref[...]) * rs
        for cb in col_blocks:
            o_ref[:, cb] = (acc[:, cb] * rs).astype(o_ref.dtype)

    @pl.when(is_rope)
    def _():
        acc = _dot(hn_ref[...], w_ref[...])
        cos = cos_ref[...]
        sin = sin_ref[...]
        scale = rs_ref[...] * jnp.where(j == q_tile, Q_SCALE, 1.0).astype(F32)
        for cb in col_blocks:
            o_ref[:, cb] = (_rotate_half(acc[:, cb], cos, sin) * scale).astype(o_ref.dtype)

    @pl.when(jnp.logical_and(j > 0, jnp.logical_not(is_rope)))
    def _():
        acc = _dot(hn_ref[...], w_ref[...])
        rs = rs_ref[...]
        for cb in col_blocks:
            o_ref[:, cb] = (acc[:, cb] * rs).astype(o_ref.dtype)


def _inproj(x2, g, w_main, w_lora, cos, sin):
    m = x2.shape[0]
    tm = PROJ_TM
    tn = PROJ_TN
    n_pos_tiles = cos.shape[0] // tm
    n_row_tiles = m // tm
    n_col_tiles = P_COLS // tn
    n_k = D_MODEL // PROJ_KC
    assert n_k <= n_col_tiles

    def x_chunk(k):
        first_step = n_col_tiles - n_k + k
        return pl.BlockSpec(
            (tm, PROJ_KC), lambda i, j: (jnp.minimum(i + (j >= first_step).astype(jnp.int32), n_row_tiles - 1), k))

    return pl.pallas_call(
        functools.partial(_inproj_kernel, tn=tn),
        out_shape=(jax.ShapeDtypeStruct((m, P_COLS), BF16), jax.ShapeDtypeStruct((m, LANES), F32)),
        grid=(n_row_tiles, n_col_tiles),
        in_specs=[x_chunk(k) for k in range(n_k)] + [
            pl.BlockSpec((1, D_MODEL), lambda i, j: (0, 0)),
            pl.BlockSpec((D_MODEL, tn), lambda i, j: (0, j)),
            pl.BlockSpec((D_MODEL, LANES), lambda i, j: (0, 0)),
            pl.BlockSpec((tm, LANES), lambda i, j: (i % n_pos_tiles, 0)),
            pl.BlockSpec((tm, LANES), lambda i, j: (i % n_pos_tiles, 0)),
        ],
        out_specs=(
            pl.BlockSpec((tm, tn), lambda i, j: (i, j)),
            pl.BlockSpec((tm, LANES), lambda i, j: (i, 0)),
        ),
        scratch_shapes=[pltpu.VMEM((tm, D_MODEL), BF16), pltpu.VMEM((tm, LANES), F32)],
        compiler_params=pltpu.CompilerParams(
            dimension_semantics=("arbitrary", "arbitrary"), vmem_limit_bytes=VMEM_LIMIT),
        name="inproj",
    )(*([x2] * n_k), g, w_main, w_lora, cos, sin)


def _rwkv_kernel(rkv_ref, g_ref, lora_ref, pinit_rkv_ref, pinit_lora_ref, hinit_ref,
                 mu_rkv_ref, mu_lora_ref, w0_ref, a0_ref, wl_hi_ref, wl_lo_ref,
                 kk_ref, ka_ref, rk_ref, gnw_ref, gnb_ref,
                 y_ref, hout_ref,
                 h_scr, prev_rkv, prev_lora,
                 st_lhs, st_rhs, st_vs, st_v, st_kh, st_bh, st_gamma, st_bonus,
                 st_inv, st_arb, st_axv, *, n_chunks):
    C = RWKV_CHUNK
    S = 2 * C
    assert S == LANES
    b_id = pl.program_id(0)
    c = pl.program_id(1)
    pairs = range(N_PAIRS)
    staging = (st_lhs, st_rhs, st_vs, st_v, st_kh, st_bh, st_gamma, st_bonus, st_inv, st_arb, st_axv)

    @pl.when(c == 0)
    def _():
        prev_rkv[...] = pinit_rkv_ref[...]
        prev_lora[...] = pinit_lora_ref[...]

    @pl.when(c <= RWKV_SLOTS - 1)
    def _():
        h_scr[...] = hinit_ref[0]

    @pl.when(jnp.logical_and(b_id == 0, c == 0))
    def _():
        for ref in staging:
            for slot in range(1, RWKV_SLOTS):
                ref[slot] = jnp.zeros(ref.shape[1:], ref.dtype)

    def cols(p, base=0):
        return slice(base + p * LANES, base + (p + 1) * LANES)

    def make_stack():
        head0 = lax.broadcasted_iota(jnp.int32, (C, LANES), 1) < HEAD

        def stack(t):
            return jnp.concatenate([jnp.where(head0, t, 0.0), jnp.where(head0, 0.0, t)], axis=0).astype(BF16)
        return head0, stack

    def make_head_sum():
        li = lax.broadcasted_iota(jnp.int32, (2 * LANES, LANES), 0)
        lj = lax.broadcasted_iota(jnp.int32, (2 * LANES, LANES), 1)
        ones_bd2 = jnp.where(((li % LANES) // HEAD) == (lj // HEAD), 1.0, 0.0).astype(BF16)

        def head_sum(t):
            rows = jnp.concatenate([t[:, cols(p)] for p in pairs], axis=0)
            hi, lo = _split2(rows)
            s = _dot(jnp.concatenate([hi, lo], axis=1), ones_bd2)
            return jnp.concatenate([s[p * C:(p + 1) * C] for p in pairs], axis=1)
        return head_sum

    def prepare(slot):
        head0, stack = make_stack()
        head_sum = make_head_sum()
        row = lax.broadcasted_iota(jnp.int32, (C, 1), 0)

        def token_shift(x, prev8, mu):
            xp = jnp.where(row == 0, prev8[7:8, :], pltpu.roll(x, 1, 0))
            return x + (xp - x) * mu

        xl = lora_ref[...]
        ul = token_shift(xl, prev_lora[...], mu_lora_ref[...])
        prev_lora[...] = xl[C - 8:C]
        tl_hi, tl_lo = _split2(jnp.where(head0, jnp.tanh(ul), ul))
        wl_hi = wl_hi_ref[...]
        lo_out = _dot(tl_hi, wl_hi) + _dot(tl_lo, wl_hi) + _dot(tl_hi, wl_lo_ref[...])
        yield

        def shifted(base):
            sl = slice(base, base + RWKV_WIDTH)
            xs = rkv_ref[:, sl].astype(F32)
            out = token_shift(xs, prev_rkv[:, sl], mu_rkv_ref[:, sl])
            prev_rkv[:, sl] = xs[C - 8:C]
            return out

        r = shifted(0)
        yield
        k = shifted(RWKV_WIDTH)
        yield
        v = shifted(2 * RWKV_WIDTH)
        yield

        logdec = -math.exp(-0.5) * _sigmoid(w0_ref[...] + lo_out[:, 0:RWKV_WIDTH])
        yield
        a_lr = _sigmoid(a0_ref[...] + lo_out[:, RWKV_WIDTH:2 * RWKV_WIDTH])
        yield

        ti = lax.broadcasted_iota(jnp.int32, (C, C), 0)
        si = lax.broadcasted_iota(jnp.int32, (C, C), 1)
        tri = jnp.where(ti >= si, 1.0, 0.0).astype(BF16)
        ld_hi = logdec.astype(BF16)
        rem = logdec - ld_hi.astype(F32)
        ld_mid = rem.astype(BF16)
        ld_lo = (rem - ld_mid.astype(F32)).astype(BF16)
        cum = _dot(tri, ld_hi) + _dot(tri, ld_mid) + _dot(tri, ld_lo)
        cum_last = cum[C - 1:C, :]
        yield
        e_excl = jnp.exp(cum - logdec)
        e_incl = jnp.exp(cum)
        yield
        e_neg = jnp.exp(-cum)
        e_hat = jnp.exp(cum_last - cum)
        yield

        kk = k * kk_ref[...]
        kk = kk * lax.rsqrt(jnp.maximum(head_sum(kk * kk), 1e-24))
        yield
        k2 = k * (1.0 + (a_lr - 1.0) * ka_ref[...])
        b = kk * a_lr
        yield
        xa_all = -kk * e_excl
        xr_all = r * e_incl
        yield
        yb_all = b * e_neg
        yk_all = k2 * e_neg
        yield

        st_gamma[slot] = jnp.exp(cum_last)
        st_bonus[slot] = head_sum(r * k2 * rk_ref[...]) * v
        yield
        st_v[slot] = v.astype(BF16)
        st_kh[slot] = (k2 * e_hat).astype(BF16)
        st_bh[slot] = (b * e_hat).astype(BF16)
        yield
        for p in pairs:
            sl = cols(p)
            st_lhs[slot, p] = jnp.concatenate([xa_all[:, sl], xr_all[:, sl]], axis=0).astype(BF16)
            st_rhs[slot, p] = jnp.concatenate([stack(yb_all[:, sl]), stack(yk_all[:, sl])], axis=0)
            st_vs[slot, p] = stack(v[:, sl])
            yield

    def stage(fn):
        out = []
        for p in pairs:
            out.append(fn(p))
            yield
        return out

    def invert(slot):
        _, stack = make_stack()
        ri = lax.broadcasted_iota(jnp.int32, (S, 2 * S), 0)
        ci = lax.broadcasted_iota(jnp.int32, (S, 2 * S), 1)
        a_mask = (ri % C) >= (ci % C) + jnp.where(ri < C, 1, 0)
        ei = lax.broadcasted_iota(jnp.int32, (C, S), 0)
        ej = lax.broadcasted_iota(jnp.int32, (C, S), 1)
        eye = jnp.where(ei == ej % C, 1.0, 0.0).astype(F32)

        a_all = yield from stage(lambda p: jnp.where(a_mask, _dot_nt(st_lhs[slot, p], st_rhs[slot, p]), 0.0))
        a_ab = [a_all[p][0:C, 0:S] for p in pairs]

        def values(p):
            st_arb[slot, p] = a_all[p][C:S, 0:S].astype(BF16)
            st_axv[slot, p] = _dot(a_all[p][:, S:2 * S].astype(BF16), st_vs[slot, p])

        yield from stage(values)

        inv = [eye + a_ab[p] for p in pairs]
        apow = yield from stage(lambda p: _dot(a_ab[p].astype(BF16), stack(a_ab[p])))
        n = 2
        while 2 * n < C:
            res = yield from stage(
                lambda p: _dot(jnp.concatenate([inv[p], apow[p]], axis=0).astype(BF16), stack(apow[p])))
            inv = [inv[p] + res[p][0:C] for p in pairs]
            apow = [res[p][C:S] for p in pairs]
            n *= 2

        def last(p):
            st_inv[slot, p] = (inv[p] + _dot(inv[p].astype(BF16), stack(apow[p]))).astype(BF16)

        yield from stage(last)

    def state(slot):
        _, stack = make_stack()
        head_sum = make_head_sum()
        bi = lax.broadcasted_iota(jnp.int32, (LANES, LANES), 0)
        bj = lax.broadcasted_iota(jnp.int32, (LANES, LANES), 1)
        same_head = (bi // HEAD) == (bj // HEAD)

        s_old = [h_scr[p] for p in pairs]
        x_s = yield from stage(lambda p: _dot_nt(st_lhs[slot, p], s_old[p].astype(BF16)))
        us = yield from stage(lambda p: _dot(st_inv[slot, p], stack(x_s[p][0:C] + st_axv[slot, p, 0:C])))
        ys = yield from stage(
            lambda p: x_s[p][C:S] + st_axv[slot, p, C:S] + _dot(st_arb[slot, p], stack(us[p])))

        def update(p):
            sl = cols(p)
            vu = jnp.concatenate([st_v[slot, :, sl], us[p].astype(BF16)], axis=0)
            khb = jnp.concatenate([st_kh[slot, :, sl], st_bh[slot, :, sl]], axis=0)
            h_scr[p] = s_old[p] * st_gamma[slot, :, sl] + jnp.where(same_head, _dot_tn(vu, khb), 0.0)

        yield from stage(update)

        y = jnp.concatenate(ys, axis=1)
        yc = y - head_sum(y) * (1.0 / HEAD)
        yield
        var = head_sum(yc * yc) * (1.0 / HEAD)
        y = yc * lax.rsqrt(var + GN_EPS) * gnw_ref[...] + gnb_ref[...] + st_bonus[slot]
        yield
        y_ref[...] = (y * _silu(g_ref[...].astype(F32))).astype(y_ref.dtype)

    phase = lax.rem(c, RWKV_SLOTS)
    for r in range(RWKV_SLOTS):
        @pl.when(phase == r)
        def _():
            _interleave((invert((r - 1) % RWKV_SLOTS), RWKV_WEIGHTS[0]),
                        (state((r - 2) % RWKV_SLOTS), RWKV_WEIGHTS[1]),
                        (prepare(r), RWKV_WEIGHTS[2]))

    @pl.when(c == n_chunks + RWKV_SLOTS - 2)
    def _():
        hout_ref[0] = h_scr[...]


def _rwkv(p_arr, lora_arr, pinit_rkv, pinit_lora, hinit, params, *, batch):
    C = RWKV_CHUNK
    rows = p_arr.shape[0]
    n_chunks = rows // (batch * C)
    const2 = lambda b, c: (0, 0)
    vec = lambda width: pl.BlockSpec((1, width), const2)
    prep_chunk = lambda b, c: b * n_chunks + jnp.minimum(c, n_chunks - 1)
    recur_chunk = lambda b, c: b * n_chunks + jnp.maximum(c - (RWKV_SLOTS - 1), 0)
    slots = lambda shape, dtype: pltpu.VMEM((RWKV_SLOTS,) + shape, dtype)
    return pl.pallas_call(
        functools.partial(_rwkv_kernel, n_chunks=n_chunks),
        out_shape=(jax.ShapeDtypeStruct((rows, RWKV_WIDTH), BF16),
                   jax.ShapeDtypeStruct((batch, N_PAIRS, LANES, LANES), F32)),
        grid=(batch, n_chunks + RWKV_SLOTS - 1),
        in_specs=[
            pl.BlockSpec((C, 3 * RWKV_WIDTH), lambda b, c: (prep_chunk(b, c), COL_RKV // (3 * RWKV_WIDTH))),
            pl.BlockSpec((C, RWKV_WIDTH), lambda b, c: (recur_chunk(b, c), COL_GR // RWKV_WIDTH)),
            pl.BlockSpec((C, LANES), lambda b, c: (prep_chunk(b, c), 0)),
            pl.BlockSpec((8, 3 * RWKV_WIDTH), const2),
            pl.BlockSpec((8, LANES), const2),
            pl.BlockSpec((1, N_PAIRS, LANES, LANES), lambda b, c: (0, 0, 0, 0)),
            vec(3 * RWKV_WIDTH), vec(LANES), vec(RWKV_WIDTH), vec(RWKV_WIDTH),
            pl.BlockSpec((LANES, 2 * RWKV_WIDTH), const2),
            pl.BlockSpec((LANES, 2 * RWKV_WIDTH), const2),
            vec(RWKV_WIDTH), vec(RWKV_WIDTH), vec(RWKV_WIDTH), vec(RWKV_WIDTH), vec(RWKV_WIDTH),
        ],
        out_specs=(
            pl.BlockSpec((C, RWKV_WIDTH), lambda b, c: (recur_chunk(b, c), 0)),
            pl.BlockSpec((1, N_PAIRS, LANES, LANES), lambda b, c: (b, 0, 0, 0)),
        ),
        scratch_shapes=[
            pltpu.VMEM((N_PAIRS, LANES, LANES), F32),
            pltpu.VMEM((8, 3 * RWKV_WIDTH), F32),
            pltpu.VMEM((8, LANES), F32),
            slots((N_PAIRS, 2 * C, LANES), BF16),
            slots((N_PAIRS, 4 * C, LANES), BF16),
            slots((N_PAIRS, 2 * C, LANES), BF16),
            slots((C, RWKV_WIDTH), BF16),
            slots((C, RWKV_WIDTH), BF16),
            slots((C, RWKV_WIDTH), BF16),
            slots((1, RWKV_WIDTH), F32),
            slots((C, RWKV_WIDTH), F32),
            slots((N_PAIRS, C, 2 * C), BF16),
            slots((N_PAIRS, C, 2 * C), BF16),
            slots((N_PAIRS, 2 * C, LANES), F32),
        ],
        compiler_params=pltpu.CompilerParams(
            dimension_semantics=("arbitrary", "arbitrary"), vmem_limit_bytes=VMEM_LIMIT),
        name="rwkv7_chunk",
    )(p_arr, p_arr, lora_arr, pinit_rkv, pinit_lora, hinit, *params)


def _attn_kernel(q_ref, k_ref, v_ref, g_ref, km_ref, vmt_ref, lq1_ref, lk1_ref, lq2_ref, lk2_ref, sw_ref,
                 o_ref, vt_scr, qst_scr, m_scr, acc_scr, s_scr, *, t, n_blk):
    g_id = pl.program_id(0)

    @pl.when(g_id == 0)
    def _():
        for ref in (vt_scr, qst_scr, m_scr, acc_scr):
            ref[1] = jnp.ones(ref.shape[1:], ref.dtype)

    def setup(slot):
        comp0 = lax.broadcasted_iota(jnp.int32, (LANES, t // 2), 0) < HEAD
        for c in range(n_blk):
            vt_scr[slot, c, 0:LANES, :] = v_ref[c * t:(c + 1) * t, :].astype(F32).T.astype(BF16)
            vt_scr[slot, c, LANES:LANES + ONES_ROWS, :] = jnp.ones((ONES_ROWS, t), BF16)
            yield
        for i in range(n_blk):
            qt = q_ref[i * t:(i + 1) * t, :].astype(F32).T
            parts = []
            for half in (qt[:, 0:t // 2], qt[:, t // 2:t]):
                parts += [jnp.where(comp0, half, 0.0), jnp.where(comp0, 0.0, half)]
            qst = jnp.concatenate(parts, axis=1).astype(BF16)
            qst_scr[slot, i] = qst
            yield
            s = _dot(km_ref[...], qst)
            m = jnp.max(s, axis=0, keepdims=True)
            m_scr[slot, i] = m
            acc_scr[slot, i] = _dot(vmt_ref[0], jnp.exp2(s - m).astype(BF16))
            yield

    def blocks(slot):
        h = t // 2
        full_rows, lo_rows, hi_rows = slice(0, t), slice(0, h), slice(h, t)
        all_cols, hi_cols = slice(0, 2 * t), slice(t, 2 * t)
        units = []
        for i, j in sorted(((i, j) for i in range(n_blk) for j in range(i + 1)), key=lambda ij: (ij[1], ij[0])):
            if i == j:
                units += [(i, j, lo_rows, all_cols, True), (i, j, hi_rows, hi_cols, True)]
            else:
                units.append((i, j, full_rows, all_cols, False))
        n_slots = s_scr.shape[0]
        m = [m_scr[slot, i] for i in range(n_blk)]

        def extent(sl):
            return sl.stop - sl.start

        def issue(n):
            i, j, rows, cols, _ = units[n]
            keys = k_ref[j * t + rows.start:j * t + rows.stop, :]
            s_scr[n % n_slots, 0:extent(rows), 0:extent(cols)] = _dot(keys, qst_scr[slot, i, :, cols])

        def visible(n_cols):
            kpos = lax.broadcasted_iota(jnp.int32, (h, n_cols), 0)
            col = lax.broadcasted_iota(jnp.int32, (h, n_cols), 1)
            return kpos <= col % h + jnp.where(col >= t, h, 0)

        visible = {n_cols: visible(n_cols) for n_cols in (t, 2 * t)}
        for n in range(min(ATT_LOOKAHEAD, len(units))):
            issue(n)
        for n, (i, j, rows, cols, masked) in enumerate(units):
            if n + ATT_LOOKAHEAD < len(units):
                issue(n + ATT_LOOKAHEAD)
            s = s_scr[n % n_slots, 0:extent(rows), 0:extent(cols)]
            if masked:
                s = jnp.where(visible[extent(cols)], s, MASK_VALUE)
            m_old = m[i][:, cols]
            m_new = jnp.maximum(m_old, jnp.max(s, axis=0, keepdims=True))
            alpha = jnp.exp2(m_old - m_new)
            m[i] = m_new if extent(cols) == 2 * t else jnp.concatenate([m[i][:, 0:cols.start], m_new], axis=1)
            acc_scr[slot, i, :, cols] = (alpha * acc_scr[slot, i, :, cols]
                                         + _dot(vt_scr[slot, j, :, rows], jnp.exp2(s - m_new).astype(BF16)))
            yield

        lam = (jnp.exp(jnp.sum(lq1_ref[...] * lk1_ref[...], axis=-1, keepdims=True))
               - jnp.exp(jnp.sum(lq2_ref[...] * lk2_ref[...], axis=-1, keepdims=True)) + LAMBDA_INIT)
        for i in range(n_blk):
            on = acc_scr[slot, i, 0:LANES, :] / acc_scr[slot, i, LANES:LANES + 1, :]
            comp = lambda c: jnp.concatenate([on[:, c * h:(c + 1) * h], on[:, t + c * h:t + (c + 1) * h]], axis=1)
            ot = comp(0) - lam * comp(1)
            ot = ot * lax.rsqrt(jnp.mean(ot * ot, axis=0, keepdims=True) + SUBLN_EPS)
            g = g_ref[i * t:(i + 1) * t, :].astype(F32)
            o_ref[i * t:(i + 1) * t, :] = (ot.T * sw_ref[...] * (1.0 - LAMBDA_INIT) * _silu(g)).astype(o_ref.dtype)
            yield

    odd = jnp.bitwise_and(g_id, 1) == 1

    @pl.when(jnp.logical_not(odd))
    def _():
        _interleave((blocks(1), ATT_WEIGHTS[0]), (setup(0), ATT_WEIGHTS[1]))

    @pl.when(odd)
    def _():
        _interleave((blocks(0), ATT_WEIGHTS[0]), (setup(1), ATT_WEIGHTS[1]))


def _attention(p_arr, p_meta, vm_t, lam_vecs, subln_w, *, batch, seq):
    t = ATT_BLOCK
    n_blk = seq // t
    n_heads_total = batch * DIFF_HEADS
    lanes_blk = lambda col: col // LANES
    small = pl.BlockSpec((1, HEAD), lambda g: (0, 0))
    setup_head = lambda g: jnp.minimum(g, n_heads_total - 1)
    block_head = lambda g: jnp.maximum(g - 1, 0)

    def head_cols(col, head_of):
        def index(g):
            h = head_of(g)
            return (h // DIFF_HEADS, lanes_blk(col) + h % DIFF_HEADS)
        return pl.BlockSpec((seq, LANES), index)

    slots = lambda shape, dtype: pltpu.VMEM((2,) + shape, dtype)
    return pl.pallas_call(
        functools.partial(_attn_kernel, t=t, n_blk=n_blk),
        out_shape=jax.ShapeDtypeStruct((batch * seq, DIFF_WIDTH), BF16),
        grid=(n_heads_total + 1,),
        in_specs=[
            head_cols(COL_Q, setup_head), head_cols(COL_K, block_head), head_cols(COL_V, setup_head),
            head_cols(COL_GD, block_head),
            pl.BlockSpec((N_META, LANES), lambda g: (0, lanes_blk(COL_K) + setup_head(g) % DIFF_HEADS)),
            pl.BlockSpec((1, LANES + ONES_ROWS, N_META), lambda g: (setup_head(g) % DIFF_HEADS, 0, 0)),
            small, small, small, small,
            pl.BlockSpec((1, LANES), lambda g: (0, 0)),
        ],
        out_specs=pl.BlockSpec((seq, LANES), lambda g: (block_head(g) // DIFF_HEADS, block_head(g) % DIFF_HEADS)),
        scratch_shapes=[
            slots((n_blk, LANES + ONES_ROWS, t), BF16),
            slots((n_blk, LANES, 2 * t), BF16),
            slots((n_blk, 1, 2 * t), F32),
            slots((n_blk, LANES + ONES_ROWS, 2 * t), F32),
            pltpu.VMEM((ATT_LOOKAHEAD + 1, t, 2 * t), F32),
        ],
        compiler_params=pltpu.CompilerParams(
            dimension_semantics=("arbitrary",), vmem_limit_bytes=VMEM_LIMIT),
        name="diff_attn",
    )(p_arr, p_arr, p_arr, p_arr, p_meta, vm_t, *lam_vecs, subln_w)


def _outproj_kernel(yr_ref, yd_ref, w1_ref, w2_ref, x_ref, g_ref, o_ref):
    y = _dot(yr_ref[...], w1_ref[...]) + _dot(yd_ref[...], w2_ref[...])
    ms = jnp.mean(y * y, axis=-1, keepdims=True)
    o_ref[...] = x_ref[...] + y * lax.rsqrt(ms + NORM_EPS) * g_ref[...]


def _outproj(y_r, y_d, w1, w2, x2, g):
    m = x2.shape[0]
    tm = OUT_TM
    return pl.pallas_call(
        _outproj_kernel,
        out_shape=jax.ShapeDtypeStruct((m, D_MODEL), F32),
        grid=(m // tm,),
        in_specs=[
            pl.BlockSpec((tm, RWKV_WIDTH), lambda i: (i, 0)),
            pl.BlockSpec((tm, DIFF_WIDTH), lambda i: (i, 0)),
            pl.BlockSpec((RWKV_WIDTH, D_MODEL), lambda i: (0, 0)),
            pl.BlockSpec((DIFF_WIDTH, D_MODEL), lambda i: (0, 0)),
            pl.BlockSpec((tm, D_MODEL), lambda i: (i, 0)),
            pl.BlockSpec((1, D_MODEL), lambda i: (0, 0)),
        ],
        out_specs=pl.BlockSpec((tm, D_MODEL), lambda i: (i, 0)),
        compiler_params=pltpu.CompilerParams(
            dimension_semantics=("arbitrary",), vmem_limit_bytes=VMEM_LIMIT),
        name="outproj",
    )(y_r, y_d, w1, w2, x2, g)


def _rope_tables(first_pos, n_pos):
    pos = jnp.arange(first_pos, first_pos + n_pos, dtype=F32)
    inv_freq = ROPE_THETA ** (-jnp.arange(0, HEAD, 2, dtype=F32) / HEAD)
    ang = pos[:, None] * inv_freq[None, :]
    cos = jnp.cos(ang)
    sin = jnp.sin(ang)
    cos = jnp.concatenate([cos, cos, cos, cos], axis=-1)
    sin = jnp.concatenate([-sin, sin, -sin, sin], axis=-1)
    return cos, sin


def kernel(x, meta_tokens, pre_norm_w, w_in, rwkv_mu, rwkv_w0, rwkv_w_up, rwkv_a0, rwkv_a_up, rwkv_k_k, rwkv_k_a, rwkv_r_k, rwkv_gn_w, rwkv_gn_b, diff_lam_q1, diff_lam_k1, diff_lam_q2, diff_lam_k2, diff_subln_w, w_out, post_norm_w):
    batch, seq, d = x.shape
    assert d == D_MODEL and meta_tokens.shape == (N_META, D_MODEL)
    assert seq % RWKV_CHUNK == 0 and seq % ATT_BLOCK == 0
    assert (batch * seq) % PROJ_TM == 0 and seq % PROJ_TM == 0
    layer = 0
    x2 = x.reshape(batch * seq, D_MODEL)

    w = w_in[layer]
    rkv_end = 3 * RWKV_WIDTH
    lora_end = rkv_end + 2 * LORA
    g_pre = pre_norm_w[layer].reshape(1, D_MODEL)
    cos_m, sin_m = _rope_tables(0, N_META)
    cos_x, sin_x = _rope_tables(N_META, seq)
    w_main, w_lora_in, p_meta, lora_meta = _wprep(w, meta_tokens.astype(x.dtype), g_pre, cos_m, sin_m)
    p_x, lora_x = _inproj(x2, g_pre, w_main, w_lora_in, cos_x, sin_x)

    mu = rwkv_mu[layer]
    zeros = jnp.zeros((LORA, RWKV_WIDTH), F32)
    w_lora = jnp.concatenate([
        jnp.concatenate([rwkv_w_up[layer], zeros], axis=1),
        jnp.concatenate([zeros, rwkv_a_up[layer]], axis=1)], axis=0)
    w_lora_hi, w_lora_lo = _split2(w_lora)
    row = lambda t, n: t.reshape(1, n)
    rwkv_params = (
        row(mu[:rkv_end], rkv_end), row(mu[rkv_end:lora_end], LANES),
        row(rwkv_w0[layer], RWKV_WIDTH), row(rwkv_a0[layer], RWKV_WIDTH),
        w_lora_hi, w_lora_lo,
        row(rwkv_k_k[layer], RWKV_WIDTH), row(rwkv_k_a[layer], RWKV_WIDTH),
        row(rwkv_r_k[layer], RWKV_WIDTH), row(rwkv_gn_w[layer], RWKV_WIDTH), row(rwkv_gn_b[layer], RWKV_WIDTH),
    )
    pad = RWKV_CHUNK - N_META
    _, h_meta = _rwkv(jnp.pad(p_meta, ((pad, 0), (0, 0))), jnp.pad(lora_meta, ((pad, 0), (0, 0))),
                      jnp.zeros((8, rkv_end), F32), jnp.zeros((8, LANES), F32),
                      jnp.zeros((1, N_PAIRS, LANES, LANES), F32), rwkv_params, batch=1)
    y_r, _ = _rwkv(p_x, lora_x, p_meta[N_META - 8:, :rkv_end].astype(F32), lora_meta[N_META - 8:],
                   h_meta, rwkv_params, batch=batch)

    lam_vecs = tuple(t[layer].reshape(1, HEAD) for t in (diff_lam_q1, diff_lam_k1, diff_lam_q2, diff_lam_k2))
    vm_t = p_meta[:, COL_V:COL_V + DIFF_WIDTH].reshape(N_META, DIFF_HEADS, LANES).transpose(1, 2, 0)
    vm_t = jnp.concatenate([vm_t, jnp.ones((DIFF_HEADS, ONES_ROWS, N_META), BF16)], axis=1)
    y_d = _attention(p_x, p_meta, vm_t, lam_vecs, diff_subln_w[layer].reshape(1, LANES), batch=batch, seq=seq)

    wo = w_out[layer].astype(BF16)
    out = _outproj(y_r, y_d, wo[:RWKV_WIDTH], wo[RWKV_WIDTH:], x2, post_norm_w[layer].reshape(1, D_MODEL))
    return out.reshape(batch, seq, D_MODEL)
```

```python
import functools
import math

import jax
import jax.numpy as jnp
from jax import lax
from jax.experimental import pallas as pl
from jax.experimental.pallas import tpu as pltpu

F32 = jnp.float32
BF16 = jnp.bfloat16

D_MODEL = 2048
N_META = 16
HEAD = 64
LANES = 128
SUBLANES = 8
V7X_VMEM_BYTES = 64 * 1024 * 1024
RWKV_WIDTH = 1024
N_PAIRS = RWKV_WIDTH // LANES
DIFF_WIDTH = 1024
DIFF_HEADS = DIFF_WIDTH // LANES
LORA = 64
ROPE_THETA = 10000.0
NORM_EPS = 1e-6
GN_EPS = 64e-5
SUBLN_EPS = 1e-5
LAMBDA_INIT = 0.8 - 0.6 * math.exp(-0.3 * 0)

COL_RKV = 0
COL_GR = 3 * RWKV_WIDTH
COL_Q = 4 * RWKV_WIDTH
COL_K = COL_Q + DIFF_WIDTH
COL_V = COL_K + DIFF_WIDTH
COL_GD = COL_V + DIFF_WIDTH
P_COLS = COL_GD + DIFF_WIDTH

WPREP_ROWS = 256
PROJ_TM = 1024
PROJ_TN = 1024
PROJ_KC = 512
RWKV_CHUNK = 64
RWKV_SLOTS = 3
RWKV_WEIGHTS = (3, 2, 1)
ATT_BLOCK = 512
ATT_LOOKAHEAD = 1
ATT_WEIGHTS = (3, 2)
OUT_TM = 256
VMEM_LIMIT = V7X_VMEM_BYTES * 3 // 4
MASK_VALUE = -1e30
ONES_ROWS = 16
Q_SCALE = HEAD ** -0.5 * math.log2(math.e)


def _dot(a, b):
    return jnp.dot(a, b, preferred_element_type=F32)


def _dot_nt(a, b):
    return lax.dot_general(a, b, (((1,), (1,)), ((), ())), preferred_element_type=F32)


def _dot_tn(a, b):
    return lax.dot_general(a, b, (((0,), (0,)), ((), ())), preferred_element_type=F32)


def _split2(x):
    hi = x.astype(BF16)
    lo = (x - hi.astype(F32)).astype(BF16)
    return hi, lo


def _sigmoid(x):
    return 1.0 / (1.0 + jnp.exp(-x))


def _silu(x):
    return x * _sigmoid(x)


def _interleave(*weighted_generators):
    active = [list(gw) for gw in weighted_generators]
    while active:
        for item in list(active):
            gen, weight = item
            for _ in range(weight):
                if next(gen, "done") == "done":
                    active.remove(item)
                    break


def _rotate_half(blk, cos, sin):
    lane = lax.broadcasted_iota(jnp.int32, blk.shape, 1)
    first_half = (lane % HEAD) < (HEAD // 2)
    swapped = jnp.where(first_half, pltpu.roll(blk, LANES - HEAD // 2, 1), pltpu.roll(blk, HEAD // 2, 1))
    return blk * cos + swapped * sin


def _wprep_kernel(w_ref, meta_ref, g_ref, cos_ref, sin_ref, main_ref, lora_ref, pm_ref, lm_ref,
                  acc_ref, lacc_ref, ssq_ref):
    i = pl.program_id(0)
    rkv_end = 3 * RWKV_WIDTH
    lora_end = rkv_end + 2 * LORA
    w_rkv = w_ref[:, 0:rkv_end].astype(BF16)
    w_lora = w_ref[:, rkv_end:lora_end].astype(BF16)
    w_rest = w_ref[:, lora_end:lora_end + P_COLS - rkv_end].astype(BF16)
    main_ref[:, 0:rkv_end] = w_rkv
    main_ref[:, rkv_end:P_COLS] = w_rest
    lora_ref[...] = w_lora

    @pl.when(i == 0)
    def _():
        acc_ref[...] = jnp.zeros_like(acc_ref)
        lacc_ref[...] = jnp.zeros_like(lacc_ref)
        ssq_ref[...] = jnp.zeros_like(ssq_ref)

    xm = meta_ref[0]
    hn = (xm * g_ref[0]).astype(BF16)
    ssq_ref[...] += jnp.broadcast_to(jnp.sum(xm * xm, axis=-1, keepdims=True), ssq_ref.shape)
    acc_ref[:, 0:rkv_end] += _dot(hn, w_rkv)
    acc_ref[:, rkv_end:P_COLS] += _dot(hn, w_rest)
    lacc_ref[...] += _dot(hn, w_lora)

    @pl.when(i == pl.num_programs(0) - 1)
    def _():
        rs = lax.rsqrt(ssq_ref[...] * (1.0 / D_MODEL) + NORM_EPS)
        lm_ref[...] = lacc_ref[...] * rs
        for c in range(P_COLS // LANES):
            cb = slice(c * LANES, (c + 1) * LANES)
            blk = acc_ref[:, cb]
            if COL_Q <= c * LANES < COL_K:
                blk = _rotate_half(blk, cos_ref[...], sin_ref[...]) * Q_SCALE
            elif COL_K <= c * LANES < COL_V:
                blk = _rotate_half(blk, cos_ref[...], sin_ref[...])
            pm_ref[:, cb] = (blk * rs).astype(pm_ref.dtype)


def _wprep(w, meta_tokens, g, cos, sin):
    rows, cols = w.shape
    assert cols == P_COLS + 2 * LORA and rows % WPREP_ROWS == 0
    n_steps = rows // WPREP_ROWS
    meta_k = meta_tokens.reshape(N_META, n_steps, WPREP_ROWS).transpose(1, 0, 2)
    g_k = g.reshape(n_steps, 1, WPREP_ROWS)
    const = lambda i: (0, 0)
    return pl.pallas_call(
        _wprep_kernel,
        out_shape=(jax.ShapeDtypeStruct((rows, P_COLS), BF16), jax.ShapeDtypeStruct((rows, 2 * LORA), BF16),
                   jax.ShapeDtypeStruct((N_META, P_COLS), BF16), jax.ShapeDtypeStruct((N_META, LANES), F32)),
        grid=(n_steps,),
        in_specs=[
            pl.BlockSpec((WPREP_ROWS, cols), lambda i: (i, 0)),
            pl.BlockSpec((1, N_META, WPREP_ROWS), lambda i: (i, 0, 0)),
            pl.BlockSpec((1, 1, WPREP_ROWS), lambda i: (i, 0, 0)),
            pl.BlockSpec((N_META, LANES), const),
            pl.BlockSpec((N_META, LANES), const),
        ],
        out_specs=(pl.BlockSpec((WPREP_ROWS, P_COLS), lambda i: (i, 0)),
                   pl.BlockSpec((WPREP_ROWS, 2 * LORA), lambda i: (i, 0)),
                   pl.BlockSpec((N_META, P_COLS), const),
                   pl.BlockSpec((N_META, LANES), const)),
        scratch_shapes=[pltpu.VMEM((N_META, P_COLS), F32), pltpu.VMEM((N_META, LANES), F32),
                        pltpu.VMEM((N_META, LANES), F32)],
        compiler_params=pltpu.CompilerParams(dimension_semantics=("arbitrary",), vmem_limit_bytes=VMEM_LIMIT),
        name="wprep",
    )(w, meta_k, g_k, cos, sin)


def _inproj_kernel(*refs, tn):
    n_k = D_MODEL // PROJ_KC
    x_refs = refs[:n_k]
    g_ref, w_ref, wl_ref, cos_ref, sin_ref, o_ref, ol_ref, hn_ref, rs_ref = refs[n_k:]
    j = pl.program_id(1)
    q_tile = COL_Q // tn
    k_tile = COL_K // tn
    assert 0 < q_tile < k_tile
    is_rope = jnp.logical_or(j == q_tile, j == k_tile)
    col_blocks = [slice(c * LANES, (c + 1) * LANES) for c in range(tn // LANES)]

    @pl.when(j == 0)
    def _():
        ssq = None
        acc = None
        for k in range(n_k):
            sl = slice(k * PROJ_KC, (k + 1) * PROJ_KC)
            xc = x_refs[k][...]
            part = jnp.sum(xc * xc, axis=-1, keepdims=True)
            ssq = part if ssq is None else ssq + part
            hn = (xc * g_ref[:, sl]).astype(BF16)
            hn_ref[:, sl] = hn
            d = _dot(hn, w_ref[sl, :])
            acc = d if acc is None else acc + d
        rs = jnp.broadcast_to(lax.rsqrt(ssq * (1.0 / D_MODEL) + NORM_EPS), rs_ref.shape)
        rs_ref[...] = rs
        ol_ref[...] = _dot(hn_ref[...], wl_ref[...]) * rs
        for cb in col_blocks:
            o_ref[:, cb] = (acc[:, cb] * rs).astype(o_ref.dtype)

    @pl.when(is_rope)
    def _():
        acc = _dot(hn_ref[...], w_ref[...])
        cos = cos_ref[...]
        sin = sin_ref[...]
        scale = rs_ref[...] * jnp.where(j == q_tile, Q_SCALE, 1.0).astype(F32)
        for cb in col_blocks:
            o_ref[:, cb] = (_rotate_half(acc[:, cb], cos, sin) * scale).astype(o_ref.dtype)

    @pl.when(jnp.logical_and(j > 0, jnp.logical_not(is_rope)))
    def _():
        acc = _dot(hn_ref[...], w_ref[...])
        rs = rs_ref[...]
        for cb in col_blocks:
            o_ref[:, cb] = (acc[:, cb] * rs).astype(o_ref.dtype)


def _inproj(x2, g, w_main, w_lora, cos, sin):
    m = x2.shape[0]
    tm = PROJ_TM
    tn = PROJ_TN
    n_pos_tiles = cos.shape[0] // tm
    n_row_tiles = m // tm
    n_col_tiles = P_COLS // tn
    n_k = D_MODEL // PROJ_KC
    assert n_k <= n_col_tiles

    def x_chunk(k):
        first_step = n_col_tiles - n_k + k
        return pl.BlockSpec(
            (tm, PROJ_KC), lambda i, j: (jnp.minimum(i + (j >= first_step).astype(jnp.int32), n_row_tiles - 1), k))

    return pl.pallas_call(
        functools.partial(_inproj_kernel, tn=tn),
        out_shape=(jax.ShapeDtypeStruct((m, P_COLS), BF16), jax.ShapeDtypeStruct((m, LANES), F32)),
        grid=(n_row_tiles, n_col_tiles),
        in_specs=[x_chunk(k) for k in range(n_k)] + [
            pl.BlockSpec((1, D_MODEL), lambda i, j: (0, 0)),
            pl.BlockSpec((D_MODEL, tn), lambda i, j: (0, j)),
            pl.BlockSpec((D_MODEL, LANES), lambda i, j: (0, 0)),
            pl.BlockSpec((tm, LANES), lambda i, j: (i % n_pos_tiles, 0)),
            pl.BlockSpec((tm, LANES), lambda i, j: (i % n_pos_tiles, 0)),
        ],
        out_specs=(
            pl.BlockSpec((tm, tn), lambda i, j: (i, j)),
            pl.BlockSpec((tm, LANES), lambda i, j: (i, 0)),
        ),
        scratch_shapes=[pltpu.VMEM((tm, D_MODEL), BF16), pltpu.VMEM((tm, LANES), F32)],
        compiler_params=pltpu.CompilerParams(
            dimension_semantics=("arbitrary", "arbitrary"), vmem_limit_bytes=VMEM_LIMIT),
        name="inproj",
    )(*([x2] * n_k), g, w_main, w_lora, cos, sin)


def _rwkv_kernel(rkv_ref, g_ref, lora_ref, pinit_rkv_ref, pinit_lora_ref, hinit_ref,
                 mu_rkv_ref, mu_lora_ref, w0_ref, a0_ref, wl_hi_ref, wl_lo_ref,
                 kk_ref, ka_ref, rk_ref, gnw_ref, gnb_ref,
                 y_ref, hout_ref,
                 h_scr, prev_rkv, prev_lora,
                 st_lhs, st_rhs, st_vs, st_v, st_kh, st_bh, st_gamma, st_bonus,
                 st_inv, st_arb, st_axv, *, n_chunks):
    C = RWKV_CHUNK
    S = 2 * C
    assert S == LANES
    b_id = pl.program_id(0)
    c = pl.program_id(1)
    pairs = range(N_PAIRS)
    staging = (st_lhs, st_rhs, st_vs, st_v, st_kh, st_bh, st_gamma, st_bonus, st_inv, st_arb, st_axv)

    @pl.when(c == 0)
    def _():
        prev_rkv[...] = pinit_rkv_ref[...]
        prev_lora[...] = pinit_lora_ref[...]

    @pl.when(c <= RWKV_SLOTS - 1)
    def _():
        h_scr[...] = hinit_ref[0]

    @pl.when(jnp.logical_and(b_id == 0, c == 0))
    def _():
        for ref in staging:
            for slot in range(1, RWKV_SLOTS):
                ref[slot] = jnp.zeros(ref.shape[1:], ref.dtype)

    def cols(p, base=0):
        return slice(base + p * LANES, base + (p + 1) * LANES)

    def make_stack():
        head0 = lax.broadcasted_iota(jnp.int32, (C, LANES), 1) < HEAD

        def stack(t):
            return jnp.concatenate([jnp.where(head0, t, 0.0), jnp.where(head0, 0.0, t)], axis=0).astype(BF16)
        return head0, stack

    def make_head_sum():
        li = lax.broadcasted_iota(jnp.int32, (2 * LANES, LANES), 0)
        lj = lax.broadcasted_iota(jnp.int32, (2 * LANES, LANES), 1)
        ones_bd2 = jnp.where(((li % LANES) // HEAD) == (lj // HEAD), 1.0, 0.0).astype(BF16)

        def head_sum(t):
            rows = jnp.concatenate([t[:, cols(p)] for p in pairs], axis=0)
            hi, lo = _split2(rows)
            s = _dot(jnp.concatenate([hi, lo], axis=1), ones_bd2)
            return jnp.concatenate([s[p * C:(p + 1) * C] for p in pairs], axis=1)
        return head_sum

    def prepare(slot):
        head0, stack = make_stack()
        head_sum = make_head_sum()
        row = lax.broadcasted_iota(jnp.int32, (C, 1), 0)

        def token_shift(x, prev_tile, mu):
            xp = jnp.where(row == 0, prev_tile[SUBLANES - 1:SUBLANES, :], pltpu.roll(x, 1, 0))
            return x + (xp - x) * mu

        xl = lora_ref[...]
        ul = token_shift(xl, prev_lora[...], mu_lora_ref[...])
        prev_lora[...] = xl[C - SUBLANES:C]
        tl_hi, tl_lo = _split2(jnp.where(head0, jnp.tanh(ul), ul))
        wl_hi = wl_hi_ref[...]
        lo_out = _dot(tl_hi, wl_hi) + _dot(tl_lo, wl_hi) + _dot(tl_hi, wl_lo_ref[...])
        yield

        def shifted(base):
            sl = slice(base, base + RWKV_WIDTH)
            xs = rkv_ref[:, sl].astype(F32)
            out = token_shift(xs, prev_rkv[:, sl], mu_rkv_ref[:, sl])
            prev_rkv[:, sl] = xs[C - SUBLANES:C]
            return out

        r = shifted(0)
        yield
        k = shifted(RWKV_WIDTH)
        yield
        v = shifted(2 * RWKV_WIDTH)
        yield

        logdec = -math.exp(-0.5) * _sigmoid(w0_ref[...] + lo_out[:, 0:RWKV_WIDTH])
        yield
        a_lr = _sigmoid(a0_ref[...] + lo_out[:, RWKV_WIDTH:2 * RWKV_WIDTH])
        yield

        ti = lax.broadcasted_iota(jnp.int32, (C, C), 0)
        si = lax.broadcasted_iota(jnp.int32, (C, C), 1)
        tri = jnp.where(ti >= si, 1.0, 0.0).astype(BF16)
        ld_hi = logdec.astype(BF16)
        rem = logdec - ld_hi.astype(F32)
        ld_mid = rem.astype(BF16)
        ld_lo = (rem - ld_mid.astype(F32)).astype(BF16)
        cum = _dot(tri, ld_hi) + _dot(tri, ld_mid) + _dot(tri, ld_lo)
        cum_last = cum[C - 1:C, :]
        yield
        e_excl = jnp.exp(cum - logdec)
        e_incl = jnp.exp(cum)
        yield
        e_neg = jnp.exp(-cum)
        e_hat = jnp.exp(cum_last - cum)
        yield

        kk = k * kk_ref[...]
        kk = kk * lax.rsqrt(jnp.maximum(head_sum(kk * kk), 1e-24))
        yield
        k2 = k * (1.0 + (a_lr - 1.0) * ka_ref[...])
        b = kk * a_lr
        yield
        xa_all = -kk * e_excl
        xr_all = r * e_incl
        yield
        yb_all = b * e_neg
        yk_all = k2 * e_neg
        yield

        st_gamma[slot] = jnp.exp(cum_last)
        st_bonus[slot] = head_sum(r * k2 * rk_ref[...]) * v
        yield
        st_v[slot] = v.astype(BF16)
        st_kh[slot] = (k2 * e_hat).astype(BF16)
        st_bh[slot] = (b * e_hat).astype(BF16)
        yield
        for p in pairs:
            sl = cols(p)
            st_lhs[slot, p] = jnp.concatenate([xa_all[:, sl], xr_all[:, sl]], axis=0).astype(BF16)
            st_rhs[slot, p] = jnp.concatenate([stack(yb_all[:, sl]), stack(yk_all[:, sl])], axis=0)
            st_vs[slot, p] = stack(v[:, sl])
            yield

    def stage(fn):
        out = []
        for p in pairs:
            out.append(fn(p))
            yield
        return out

    def invert(slot):
        _, stack = make_stack()
        ri = lax.broadcasted_iota(jnp.int32, (S, 2 * S), 0)
        ci = lax.broadcasted_iota(jnp.int32, (S, 2 * S), 1)
        a_mask = (ri % C) >= (ci % C) + jnp.where(ri < C, 1, 0)
        ei = lax.broadcasted_iota(jnp.int32, (C, S), 0)
        ej = lax.broadcasted_iota(jnp.int32, (C, S), 1)
        eye = jnp.where(ei == ej % C, 1.0, 0.0).astype(F32)

        a_all = yield from stage(lambda p: jnp.where(a_mask, _dot_nt(st_lhs[slot, p], st_rhs[slot, p]), 0.0))
        a_ab = [a_all[p][0:C, 0:S] for p in pairs]

        def values(p):
            st_arb[slot, p] = a_all[p][C:S, 0:S].astype(BF16)
            st_axv[slot, p] = _dot(a_all[p][:, S:2 * S].astype(BF16), st_vs[slot, p])

        yield from stage(values)

        inv = [eye + a_ab[p] for p in pairs]
        apow = yield from stage(lambda p: _dot(a_ab[p].astype(BF16), stack(a_ab[p])))
        n = 2
        while 2 * n < C:
            res = yield from stage(
                lambda p: _dot(jnp.concatenate([inv[p], apow[p]], axis=0).astype(BF16), stack(apow[p])))
            inv = [inv[p] + res[p][0:C] for p in pairs]
            apow = [res[p][C:S] for p in pairs]
            n *= 2

        def last(p):
            st_inv[slot, p] = (inv[p] + _dot(inv[p].astype(BF16), stack(apow[p]))).astype(BF16)

        yield from stage(last)

    def state(slot):
        _, stack = make_stack()
        head_sum = make_head_sum()
        bi = lax.broadcasted_iota(jnp.int32, (LANES, LANES), 0)
        bj = lax.broadcasted_iota(jnp.int32, (LANES, LANES), 1)
        same_head = (bi // HEAD) == (bj // HEAD)

        s_old = [h_scr[p] for p in pairs]
        x_s = yield from stage(lambda p: _dot_nt(st_lhs[slot, p], s_old[p].astype(BF16)))
        us = yield from stage(lambda p: _dot(st_inv[slot, p], stack(x_s[p][0:C] + st_axv[slot, p, 0:C])))
        ys = yield from stage(
            lambda p: x_s[p][C:S] + st_axv[slot, p, C:S] + _dot(st_arb[slot, p], stack(us[p])))

        def update(p):
            sl = cols(p)
            vu = jnp.concatenate([st_v[slot, :, sl], us[p].astype(BF16)], axis=0)
            khb = jnp.concatenate([st_kh[slot, :, sl], st_bh[slot, :, sl]], axis=0)
            h_scr[p] = s_old[p] * st_gamma[slot, :, sl] + jnp.where(same_head, _dot_tn(vu, khb), 0.0)

        yield from stage(update)

        y = jnp.concatenate(ys, axis=1)
        yc = y - head_sum(y) * (1.0 / HEAD)
        yield
        var = head_sum(yc * yc) * (1.0 / HEAD)
        y = yc * lax.rsqrt(var + GN_EPS) * gnw_ref[...] + gnb_ref[...] + st_bonus[slot]
        yield
        y_ref[...] = (y * _silu(g_ref[...].astype(F32))).astype(y_ref.dtype)

    phase = lax.rem(c, RWKV_SLOTS)
    for r in range(RWKV_SLOTS):
        @pl.when(phase == r)
        def _():
            _interleave((invert((r - 1) % RWKV_SLOTS), RWKV_WEIGHTS[0]),
                        (state((r - 2) % RWKV_SLOTS), RWKV_WEIGHTS[1]),
                        (prepare(r), RWKV_WEIGHTS[2]))

    @pl.when(c == n_chunks + RWKV_SLOTS - 2)
    def _():
        hout_ref[0] = h_scr[...]


def _rwkv(p_arr, lora_arr, pinit_rkv, pinit_lora, hinit, params, *, batch):
    C = RWKV_CHUNK
    rows = p_arr.shape[0]
    n_chunks = rows // (batch * C)
    const2 = lambda b, c: (0, 0)
    vec = lambda width: pl.BlockSpec((1, width), const2)
    prep_chunk = lambda b, c: b * n_chunks + jnp.minimum(c, n_chunks - 1)
    recur_chunk = lambda b, c: b * n_chunks + jnp.maximum(c - (RWKV_SLOTS - 1), 0)
    slots = lambda shape, dtype: pltpu.VMEM((RWKV_SLOTS,) + shape, dtype)
    return pl.pallas_call(
        functools.partial(_rwkv_kernel, n_chunks=n_chunks),
        out_shape=(jax.ShapeDtypeStruct((rows, RWKV_WIDTH), BF16),
                   jax.ShapeDtypeStruct((batch, N_PAIRS, LANES, LANES), F32)),
        grid=(batch, n_chunks + RWKV_SLOTS - 1),
        in_specs=[
            pl.BlockSpec((C, 3 * RWKV_WIDTH), lambda b, c: (prep_chunk(b, c), COL_RKV // (3 * RWKV_WIDTH))),
            pl.BlockSpec((C, RWKV_WIDTH), lambda b, c: (recur_chunk(b, c), COL_GR // RWKV_WIDTH)),
            pl.BlockSpec((C, LANES), lambda b, c: (prep_chunk(b, c), 0)),
            pl.BlockSpec((SUBLANES, 3 * RWKV_WIDTH), const2),
            pl.BlockSpec((SUBLANES, LANES), const2),
            pl.BlockSpec((1, N_PAIRS, LANES, LANES), lambda b, c: (0, 0, 0, 0)),
            vec(3 * RWKV_WIDTH), vec(LANES), vec(RWKV_WIDTH), vec(RWKV_WIDTH),
            pl.BlockSpec((LANES, 2 * RWKV_WIDTH), const2),
            pl.BlockSpec((LANES, 2 * RWKV_WIDTH), const2),
            vec(RWKV_WIDTH), vec(RWKV_WIDTH), vec(RWKV_WIDTH), vec(RWKV_WIDTH), vec(RWKV_WIDTH),
        ],
        out_specs=(
            pl.BlockSpec((C, RWKV_WIDTH), lambda b, c: (recur_chunk(b, c), 0)),
            pl.BlockSpec((1, N_PAIRS, LANES, LANES), lambda b, c: (b, 0, 0, 0)),
        ),
        scratch_shapes=[
            pltpu.VMEM((N_PAIRS, LANES, LANES), F32),
            pltpu.VMEM((SUBLANES, 3 * RWKV_WIDTH), F32),
            pltpu.VMEM((SUBLANES, LANES), F32),
            slots((N_PAIRS, 2 * C, LANES), BF16),
            slots((N_PAIRS, 4 * C, LANES), BF16),
            slots((N_PAIRS, 2 * C, LANES), BF16),
            slots((C, RWKV_WIDTH), BF16),
            slots((C, RWKV_WIDTH), BF16),
            slots((C, RWKV_WIDTH), BF16),
            slots((1, RWKV_WIDTH), F32),
            slots((C, RWKV_WIDTH), F32),
            slots((N_PAIRS, C, 2 * C), BF16),
            slots((N_PAIRS, C, 2 * C), BF16),
            slots((N_PAIRS, 2 * C, LANES), F32),
        ],
        compiler_params=pltpu.CompilerParams(
            dimension_semantics=("arbitrary", "arbitrary"), vmem_limit_bytes=VMEM_LIMIT),
        name="rwkv7_chunk",
    )(p_arr, p_arr, lora_arr, pinit_rkv, pinit_lora, hinit, *params)


def _attn_kernel(q_ref, k_ref, v_ref, g_ref, km_ref, vmt_ref, lq1_ref, lk1_ref, lq2_ref, lk2_ref, sw_ref,
                 o_ref, vt_scr, qst_scr, m_scr, acc_scr, s_scr, *, t, n_blk):
    g_id = pl.program_id(0)

    @pl.when(g_id == 0)
    def _():
        for ref in (vt_scr, qst_scr, m_scr, acc_scr):
            ref[1] = jnp.ones(ref.shape[1:], ref.dtype)

    def setup(slot):
        comp0 = lax.broadcasted_iota(jnp.int32, (LANES, t // 2), 0) < HEAD
        for c in range(n_blk):
            vt_scr[slot, c, 0:LANES, :] = v_ref[c * t:(c + 1) * t, :].astype(F32).T.astype(BF16)
            vt_scr[slot, c, LANES:LANES + ONES_ROWS, :] = jnp.ones((ONES_ROWS, t), BF16)
            yield
        for i in range(n_blk):
            qt = q_ref[i * t:(i + 1) * t, :].astype(F32).T
            parts = []
            for half in (qt[:, 0:t // 2], qt[:, t // 2:t]):
                parts += [jnp.where(comp0, half, 0.0), jnp.where(comp0, 0.0, half)]
            qst = jnp.concatenate(parts, axis=1).astype(BF16)
            qst_scr[slot, i] = qst
            yield
            s = _dot(km_ref[...], qst)
            m = jnp.max(s, axis=0, keepdims=True)
            m_scr[slot, i] = m
            acc_scr[slot, i] = _dot(vmt_ref[0], jnp.exp2(s - m).astype(BF16))
            yield

    def blocks(slot):
        h = t // 2
        full_rows, lo_rows, hi_rows = slice(0, t), slice(0, h), slice(h, t)
        all_cols, hi_cols = slice(0, 2 * t), slice(t, 2 * t)
        units = []
        for i, j in sorted(((i, j) for i in range(n_blk) for j in range(i + 1)), key=lambda ij: (ij[1], ij[0])):
            if i == j:
                units += [(i, j, lo_rows, all_cols, True), (i, j, hi_rows, hi_cols, True)]
            else:
                units.append((i, j, full_rows, all_cols, False))
        n_slots = s_scr.shape[0]
        m = [m_scr[slot, i] for i in range(n_blk)]

        def extent(sl):
            return sl.stop - sl.start

        def issue(n):
            i, j, rows, cols, _ = units[n]
            keys = k_ref[j * t + rows.start:j * t + rows.stop, :]
            s_scr[n % n_slots, 0:extent(rows), 0:extent(cols)] = _dot(keys, qst_scr[slot, i, :, cols])

        def visible(n_cols):
            kpos = lax.broadcasted_iota(jnp.int32, (h, n_cols), 0)
            col = lax.broadcasted_iota(jnp.int32, (h, n_cols), 1)
            return kpos <= col % h + jnp.where(col >= t, h, 0)

        visible = {n_cols: visible(n_cols) for n_cols in (t, 2 * t)}
        for n in range(min(ATT_LOOKAHEAD, len(units))):
            issue(n)
        for n, (i, j, rows, cols, masked) in enumerate(units):
            if n + ATT_LOOKAHEAD < len(units):
                issue(n + ATT_LOOKAHEAD)
            s = s_scr[n % n_slots, 0:extent(rows), 0:extent(cols)]
            if masked:
                s = jnp.where(visible[extent(cols)], s, MASK_VALUE)
            m_old = m[i][:, cols]
            m_new = jnp.maximum(m_old, jnp.max(s, axis=0, keepdims=True))
            alpha = jnp.exp2(m_old - m_new)
            m[i] = m_new if extent(cols) == 2 * t else jnp.concatenate([m[i][:, 0:cols.start], m_new], axis=1)
            acc_scr[slot, i, :, cols] = (alpha * acc_scr[slot, i, :, cols]
                                         + _dot(vt_scr[slot, j, :, rows], jnp.exp2(s - m_new).astype(BF16)))
            yield

        lam = (jnp.exp(jnp.sum(lq1_ref[...] * lk1_ref[...], axis=-1, keepdims=True))
               - jnp.exp(jnp.sum(lq2_ref[...] * lk2_ref[...], axis=-1, keepdims=True)) + LAMBDA_INIT)
        for i in range(n_blk):
            on = acc_scr[slot, i, 0:LANES, :] / acc_scr[slot, i, LANES:LANES + 1, :]
            comp = lambda c: jnp.concatenate([on[:, c * h:(c + 1) * h], on[:, t + c * h:t + (c + 1) * h]], axis=1)
            ot = comp(0) - lam * comp(1)
            ot = ot * lax.rsqrt(jnp.mean(ot * ot, axis=0, keepdims=True) + SUBLN_EPS)
            g = g_ref[i * t:(i + 1) * t, :].astype(F32)
            o_ref[i * t:(i + 1) * t, :] = (ot.T * sw_ref[...] * (1.0 - LAMBDA_INIT) * _silu(g)).astype(o_ref.dtype)
            yield

    odd = jnp.bitwise_and(g_id, 1) == 1

    @pl.when(jnp.logical_not(odd))
    def _():
        _interleave((blocks(1), ATT_WEIGHTS[0]), (setup(0), ATT_WEIGHTS[1]))

    @pl.when(odd)
    def _():
        _interleave((blocks(0), ATT_WEIGHTS[0]), (setup(1), ATT_WEIGHTS[1]))


def _attention(p_arr, p_meta, vm_t, lam_vecs, subln_w, *, batch, seq):
    t = ATT_BLOCK
    n_blk = seq // t
    n_heads_total = batch * DIFF_HEADS
    lanes_blk = lambda col: col // LANES
    small = pl.BlockSpec((1, HEAD), lambda g: (0, 0))
    setup_head = lambda g: jnp.minimum(g, n_heads_total - 1)
    block_head = lambda g: jnp.maximum(g - 1, 0)

    def head_cols(col, head_of):
        def index(g):
            h = head_of(g)
            return (h // DIFF_HEADS, lanes_blk(col) + h % DIFF_HEADS)
        return pl.BlockSpec((seq, LANES), index)

    slots = lambda shape, dtype: pltpu.VMEM((2,) + shape, dtype)
    return pl.pallas_call(
        functools.partial(_attn_kernel, t=t, n_blk=n_blk),
        out_shape=jax.ShapeDtypeStruct((batch * seq, DIFF_WIDTH), BF16),
        grid=(n_heads_total + 1,),
        in_specs=[
            head_cols(COL_Q, setup_head), head_cols(COL_K, block_head), head_cols(COL_V, setup_head),
            head_cols(COL_GD, block_head),
            pl.BlockSpec((N_META, LANES), lambda g: (0, lanes_blk(COL_K) + setup_head(g) % DIFF_HEADS)),
            pl.BlockSpec((1, LANES + ONES_ROWS, N_META), lambda g: (setup_head(g) % DIFF_HEADS, 0, 0)),
            small, small, small, small,
            pl.BlockSpec((1, LANES), lambda g: (0, 0)),
        ],
        out_specs=pl.BlockSpec((seq, LANES), lambda g: (block_head(g) // DIFF_HEADS, block_head(g) % DIFF_HEADS)),
        scratch_shapes=[
            slots((n_blk, LANES + ONES_ROWS, t), BF16),
            slots((n_blk, LANES, 2 * t), BF16),
            slots((n_blk, 1, 2 * t), F32),
            slots((n_blk, LANES + ONES_ROWS, 2 * t), F32),
            pltpu.VMEM((ATT_LOOKAHEAD + 1, t, 2 * t), F32),
        ],
        compiler_params=pltpu.CompilerParams(
            dimension_semantics=("arbitrary",), vmem_limit_bytes=VMEM_LIMIT),
        name="diff_attn",
    )(p_arr, p_arr, p_arr, p_arr, p_meta, vm_t, *lam_vecs, subln_w)


def _outproj_kernel(yr_ref, yd_ref, w1_ref, w2_ref, x_ref, g_ref, o_ref):
    y = _dot(yr_ref[...], w1_ref[...]) + _dot(yd_ref[...], w2_ref[...])
    ms = jnp.mean(y * y, axis=-1, keepdims=True)
    o_ref[...] = x_ref[...] + y * lax.rsqrt(ms + NORM_EPS) * g_ref[...]


def _outproj(y_r, y_d, w, x2, g):
    m = x2.shape[0]
    tm = OUT_TM
    assert RWKV_WIDTH == DIFF_WIDTH and w.shape == (RWKV_WIDTH + DIFF_WIDTH, D_MODEL)
    return pl.pallas_call(
        _outproj_kernel,
        out_shape=jax.ShapeDtypeStruct((m, D_MODEL), F32),
        grid=(m // tm,),
        in_specs=[
            pl.BlockSpec((tm, RWKV_WIDTH), lambda i: (i, 0)),
            pl.BlockSpec((tm, DIFF_WIDTH), lambda i: (i, 0)),
            pl.BlockSpec((RWKV_WIDTH, D_MODEL), lambda i: (0, 0)),
            pl.BlockSpec((DIFF_WIDTH, D_MODEL), lambda i: (1, 0)),
            pl.BlockSpec((tm, D_MODEL), lambda i: (i, 0)),
            pl.BlockSpec((1, D_MODEL), lambda i: (0, 0)),
        ],
        out_specs=pl.BlockSpec((tm, D_MODEL), lambda i: (i, 0)),
        compiler_params=pltpu.CompilerParams(
            dimension_semantics=("arbitrary",), vmem_limit_bytes=VMEM_LIMIT),
        name="outproj",
    )(y_r, y_d, w, w, x2, g)


def _rope_tables(first_pos, n_pos):
    pos = jnp.arange(first_pos, first_pos + n_pos, dtype=F32)
    inv_freq = ROPE_THETA ** (-jnp.arange(0, HEAD, 2, dtype=F32) / HEAD)
    ang = pos[:, None] * inv_freq[None, :]
    cos = jnp.cos(ang)
    sin = jnp.sin(ang)
    cos = jnp.concatenate([cos, cos, cos, cos], axis=-1)
    sin = jnp.concatenate([-sin, sin, -sin, sin], axis=-1)
    return cos, sin


def kernel(x, meta_tokens, pre_norm_w, w_in, rwkv_mu, rwkv_w0, rwkv_w_up, rwkv_a0, rwkv_a_up, rwkv_k_k, rwkv_k_a, rwkv_r_k, rwkv_gn_w, rwkv_gn_b, diff_lam_q1, diff_lam_k1, diff_lam_q2, diff_lam_k2, diff_subln_w, w_out, post_norm_w):
    batch, seq, d = x.shape
    assert d == D_MODEL and meta_tokens.shape == (N_META, D_MODEL)
    assert seq % RWKV_CHUNK == 0 and seq % ATT_BLOCK == 0
    assert (batch * seq) % PROJ_TM == 0 and seq % PROJ_TM == 0
    layer = 0
    x2 = x.reshape(batch * seq, D_MODEL)

    w = w_in[layer]
    rkv_end = 3 * RWKV_WIDTH
    lora_end = rkv_end + 2 * LORA
    g_pre = pre_norm_w[layer].reshape(1, D_MODEL)
    cos_m, sin_m = _rope_tables(0, N_META)
    cos_x, sin_x = _rope_tables(N_META, seq)
    w_main, w_lora_in, p_meta, lora_meta = _wprep(w, meta_tokens.astype(x.dtype), g_pre, cos_m, sin_m)
    p_x, lora_x = _inproj(x2, g_pre, w_main, w_lora_in, cos_x, sin_x)

    mu = rwkv_mu[layer]
    zeros = jnp.zeros((LORA, RWKV_WIDTH), F32)
    w_lora = jnp.concatenate([
        jnp.concatenate([rwkv_w_up[layer], zeros], axis=1),
        jnp.concatenate([zeros, rwkv_a_up[layer]], axis=1)], axis=0)
    w_lora_hi, w_lora_lo = _split2(w_lora)
    row = lambda t, n: t.reshape(1, n)
    rwkv_params = (
        row(mu[:rkv_end], rkv_end), row(mu[rkv_end:lora_end], LANES),
        row(rwkv_w0[layer], RWKV_WIDTH), row(rwkv_a0[layer], RWKV_WIDTH),
        w_lora_hi, w_lora_lo,
        row(rwkv_k_k[layer], RWKV_WIDTH), row(rwkv_k_a[layer], RWKV_WIDTH),
        row(rwkv_r_k[layer], RWKV_WIDTH), row(rwkv_gn_w[layer], RWKV_WIDTH), row(rwkv_gn_b[layer], RWKV_WIDTH),
    )
    pad = RWKV_CHUNK - N_META
    _, h_meta = _rwkv(jnp.pad(p_meta, ((pad, 0), (0, 0))), jnp.pad(lora_meta, ((pad, 0), (0, 0))),
                      jnp.zeros((SUBLANES, rkv_end), F32), jnp.zeros((SUBLANES, LANES), F32),
                      jnp.zeros((1, N_PAIRS, LANES, LANES), F32), rwkv_params, batch=1)
    y_r, _ = _rwkv(p_x, lora_x, p_meta[N_META - SUBLANES:, :rkv_end].astype(F32), lora_meta[N_META - SUBLANES:],
                   h_meta, rwkv_params, batch=batch)

    lam_vecs = tuple(t[layer].reshape(1, HEAD) for t in (diff_lam_q1, diff_lam_k1, diff_lam_q2, diff_lam_k2))
    vm_t = p_meta[:, COL_V:COL_V + DIFF_WIDTH].reshape(N_META, DIFF_HEADS, LANES).transpose(1, 2, 0)
    vm_t = jnp.concatenate([vm_t, jnp.ones((DIFF_HEADS, ONES_ROWS, N_META), BF16)], axis=1)
    y_d = _attention(p_x, p_meta, vm_t, lam_vecs, diff_subln_w[layer].reshape(1, LANES), batch=batch, seq=seq)

    out = _outproj(y_r, y_d, w_out[layer].astype(BF16), x2, post_norm_w[layer].reshape(1, D_MODEL))
    return out.reshape(batch, seq, D_MODEL)
```

```python
import functools
import math

import jax
import jax.numpy as jnp
from jax import lax
from jax.experimental import pallas as pl
from jax.experimental.pallas import tpu as pltpu

F32 = jnp.float32
BF16 = jnp.bfloat16

D_MODEL = 2048
N_META = 16
HEAD = 64
LANES = 128
SUBLANES = 8
V7X_VMEM_BYTES = 64 * 1024 * 1024
RWKV_WIDTH = 1024
N_PAIRS = RWKV_WIDTH // LANES
DIFF_WIDTH = 1024
DIFF_HEADS = DIFF_WIDTH // LANES
LORA = 64
ROPE_THETA = 10000.0
NORM_EPS = 1e-6
GN_EPS = 64e-5
SUBLN_EPS = 1e-5
LAMBDA_INIT = 0.8 - 0.6 * math.exp(-0.3 * 0)

COL_RKV = 0
COL_GR = 3 * RWKV_WIDTH
COL_Q = 4 * RWKV_WIDTH
COL_K = COL_Q + DIFF_WIDTH
COL_V = COL_K + DIFF_WIDTH
COL_GD = COL_V + DIFF_WIDTH
P_COLS = COL_GD + DIFF_WIDTH

WPREP_ROWS = 256
PROJ_TM = 1024
PROJ_TN = 1024
PROJ_KC = 512
RWKV_CHUNK = 64
RWKV_SLOTS = 3
RWKV_WEIGHTS = (3, 2, 1)
ATT_BLOCK = 512
ATT_LOOKAHEAD = 1
ATT_WEIGHTS = (3, 2)
OUT_TM = 256
VMEM_LIMIT = V7X_VMEM_BYTES * 3 // 4
MASK_VALUE = -1e30
ONES_ROWS = 16
Q_SCALE = HEAD ** -0.5 * math.log2(math.e)


def _dot(a, b):
    return jnp.dot(a, b, preferred_element_type=F32)


def _dot_nt(a, b):
    return lax.dot_general(a, b, (((1,), (1,)), ((), ())), preferred_element_type=F32)


def _dot_tn(a, b):
    return lax.dot_general(a, b, (((0,), (0,)), ((), ())), preferred_element_type=F32)


def _split2(x):
    hi = x.astype(BF16)
    lo = (x - hi.astype(F32)).astype(BF16)
    return hi, lo


def _sigmoid(x):
    return 1.0 / (1.0 + jnp.exp(-x))


def _silu(x):
    return x * _sigmoid(x)


def _interleave(*weighted_generators):
    active = [list(gw) for gw in weighted_generators]
    while active:
        for item in list(active):
            gen, weight = item
            for _ in range(weight):
                if next(gen, "done") == "done":
                    active.remove(item)
                    break


def _rotate_half(blk, cos, sin):
    lane = lax.broadcasted_iota(jnp.int32, blk.shape, 1)
    first_half = (lane % HEAD) < (HEAD // 2)
    swapped = jnp.where(first_half, pltpu.roll(blk, LANES - HEAD // 2, 1), pltpu.roll(blk, HEAD // 2, 1))
    return blk * cos + swapped * sin


def _wprep_kernel(w_ref, meta_ref, g_ref, cos_ref, sin_ref, main_ref, lora_ref, pm_ref, lm_ref,
                  acc_ref, lacc_ref, ssq_ref):
    i = pl.program_id(0)
    rkv_end = 3 * RWKV_WIDTH
    lora_end = rkv_end + 2 * LORA
    w_rkv = w_ref[:, 0:rkv_end].astype(BF16)
    w_lora = w_ref[:, rkv_end:lora_end].astype(BF16)
    w_rest = w_ref[:, lora_end:lora_end + P_COLS - rkv_end].astype(BF16)
    main_ref[:, 0:rkv_end] = w_rkv
    main_ref[:, rkv_end:P_COLS] = w_rest
    lora_ref[...] = w_lora

    @pl.when(i == 0)
    def _():
        acc_ref[...] = jnp.zeros_like(acc_ref)
        lacc_ref[...] = jnp.zeros_like(lacc_ref)
        ssq_ref[...] = jnp.zeros_like(ssq_ref)

    xm = meta_ref[0]
    hn = (xm * g_ref[0]).astype(BF16)
    ssq_ref[...] += jnp.broadcast_to(jnp.sum(xm * xm, axis=-1, keepdims=True), ssq_ref.shape)
    acc_ref[:, 0:rkv_end] += _dot(hn, w_rkv)
    acc_ref[:, rkv_end:P_COLS] += _dot(hn, w_rest)
    lacc_ref[...] += _dot(hn, w_lora)

    @pl.when(i == pl.num_programs(0) - 1)
    def _():
        rs = lax.rsqrt(ssq_ref[...] * (1.0 / D_MODEL) + NORM_EPS)
        lm_ref[...] = lacc_ref[...] * rs
        for c in range(P_COLS // LANES):
            cb = slice(c * LANES, (c + 1) * LANES)
            blk = acc_ref[:, cb]
            if COL_Q <= c * LANES < COL_K:
                blk = _rotate_half(blk, cos_ref[...], sin_ref[...]) * Q_SCALE
            elif COL_K <= c * LANES < COL_V:
                blk = _rotate_half(blk, cos_ref[...], sin_ref[...])
            pm_ref[:, cb] = (blk * rs).astype(pm_ref.dtype)


def _wprep(w, meta_tokens, g, cos, sin):
    rows, cols = w.shape
    assert cols == P_COLS + 2 * LORA and rows % WPREP_ROWS == 0
    n_steps = rows // WPREP_ROWS
    meta_k = meta_tokens.reshape(N_META, n_steps, WPREP_ROWS).transpose(1, 0, 2)
    g_k = g.reshape(n_steps, 1, WPREP_ROWS)
    const = lambda i: (0, 0)
    return pl.pallas_call(
        _wprep_kernel,
        out_shape=(jax.ShapeDtypeStruct((rows, P_COLS), BF16), jax.ShapeDtypeStruct((rows, 2 * LORA), BF16),
                   jax.ShapeDtypeStruct((N_META, P_COLS), BF16), jax.ShapeDtypeStruct((N_META, LANES), F32)),
        grid=(n_steps,),
        in_specs=[
            pl.BlockSpec((WPREP_ROWS, cols), lambda i: (i, 0)),
            pl.BlockSpec((1, N_META, WPREP_ROWS), lambda i: (i, 0, 0)),
            pl.BlockSpec((1, 1, WPREP_ROWS), lambda i: (i, 0, 0)),
            pl.BlockSpec((N_META, LANES), const),
            pl.BlockSpec((N_META, LANES), const),
        ],
        out_specs=(pl.BlockSpec((WPREP_ROWS, P_COLS), lambda i: (i, 0)),
                   pl.BlockSpec((WPREP_ROWS, 2 * LORA), lambda i: (i, 0)),
                   pl.BlockSpec((N_META, P_COLS), const),
                   pl.BlockSpec((N_META, LANES), const)),
        scratch_shapes=[pltpu.VMEM((N_META, P_COLS), F32), pltpu.VMEM((N_META, LANES), F32),
                        pltpu.VMEM((N_META, LANES), F32)],
        compiler_params=pltpu.CompilerParams(dimension_semantics=("arbitrary",), vmem_limit_bytes=VMEM_LIMIT),
        name="wprep",
    )(w, meta_k, g_k, cos, sin)


def _inproj_kernel(*refs, tn):
    n_k = D_MODEL // PROJ_KC
    x_refs = refs[:n_k]
    g_ref, w_ref, wl_ref, cos_ref, sin_ref, o_ref, ol_ref, hn_ref, rs_ref = refs[n_k:]
    j = pl.program_id(1)
    q_tile = COL_Q // tn
    k_tile = COL_K // tn
    assert 0 < q_tile < k_tile
    is_rope = jnp.logical_or(j == q_tile, j == k_tile)
    col_blocks = [slice(c * LANES, (c + 1) * LANES) for c in range(tn // LANES)]

    @pl.when(j == 0)
    def _():
        ssq = None
        acc = None
        for k in range(n_k):
            sl = slice(k * PROJ_KC, (k + 1) * PROJ_KC)
            xc = x_refs[k][...]
            part = jnp.sum(xc * xc, axis=-1, keepdims=True)
            ssq = part if ssq is None else ssq + part
            hn = (xc * g_ref[:, sl]).astype(BF16)
            hn_ref[:, sl] = hn
            d = _dot(hn, w_ref[sl, :])
            acc = d if acc is None else acc + d
        rs = jnp.broadcast_to(lax.rsqrt(ssq * (1.0 / D_MODEL) + NORM_EPS), rs_ref.shape)
        rs_ref[...] = rs
        ol_ref[...] = _dot(hn_ref[...], wl_ref[...]) * rs
        for cb in col_blocks:
            o_ref[:, cb] = (acc[:, cb] * rs).astype(o_ref.dtype)

    @pl.when(is_rope)
    def _():
        acc = _dot(hn_ref[...], w_ref[...])
        cos = cos_ref[...]
        sin = sin_ref[...]
        scale = rs_ref[...] * jnp.where(j == q_tile, Q_SCALE, 1.0).astype(F32)
        for cb in col_blocks:
            o_ref[:, cb] = (_rotate_half(acc[:, cb], cos, sin) * scale).astype(o_ref.dtype)

    @pl.when(jnp.logical_and(j > 0, jnp.logical_not(is_rope)))
    def _():
        acc = _dot(hn_ref[...], w_ref[...])
        rs = rs_ref[...]
        for cb in col_blocks:
            o_ref[:, cb] = (acc[:, cb] * rs).astype(o_ref.dtype)


def _inproj(x2, g, w_main, w_lora, cos, sin):
    m = x2.shape[0]
    tm = PROJ_TM
    tn = PROJ_TN
    n_pos_tiles = cos.shape[0] // tm
    n_row_tiles = m // tm
    n_col_tiles = P_COLS // tn
    n_k = D_MODEL // PROJ_KC
    assert n_k <= n_col_tiles

    def x_chunk(k):
        first_step = n_col_tiles - n_k + k
        return pl.BlockSpec(
            (tm, PROJ_KC), lambda i, j: (jnp.minimum(i + (j >= first_step).astype(jnp.int32), n_row_tiles - 1), k))

    return pl.pallas_call(
        functools.partial(_inproj_kernel, tn=tn),
        out_shape=(jax.ShapeDtypeStruct((m, P_COLS), BF16), jax.ShapeDtypeStruct((m, LANES), F32)),
        grid=(n_row_tiles, n_col_tiles),
        in_specs=[x_chunk(k) for k in range(n_k)] + [
            pl.BlockSpec((1, D_MODEL), lambda i, j: (0, 0)),
            pl.BlockSpec((D_MODEL, tn), lambda i, j: (0, j)),
            pl.BlockSpec((D_MODEL, LANES), lambda i, j: (0, 0)),
            pl.BlockSpec((tm, LANES), lambda i, j: (i % n_pos_tiles, 0)),
            pl.BlockSpec((tm, LANES), lambda i, j: (i % n_pos_tiles, 0)),
        ],
        out_specs=(
            pl.BlockSpec((tm, tn), lambda i, j: (i, j)),
            pl.BlockSpec((tm, LANES), lambda i, j: (i, 0)),
        ),
        scratch_shapes=[pltpu.VMEM((tm, D_MODEL), BF16), pltpu.VMEM((tm, LANES), F32)],
        compiler_params=pltpu.CompilerParams(
            dimension_semantics=("arbitrary", "arbitrary"), vmem_limit_bytes=VMEM_LIMIT),
        name="inproj",
    )(*([x2] * n_k), g, w_main, w_lora, cos, sin)


def _rwkv_kernel(rkv_ref, g_ref, lora_ref, pinit_rkv_ref, pinit_lora_ref, hinit_ref,
                 mu_rkv_ref, mu_lora_ref, w0_ref, a0_ref, wl_ref,
                 kk_ref, ka_ref, rk_ref, gnw_ref, gnb_ref,
                 y_ref, hout_ref,
                 h_scr, prev_rkv, prev_lora,
                 st_lhs, st_rhs, st_vs, st_v, st_kh, st_bh, st_gamma, st_bonus,
                 st_inv, st_arb, st_axv, *, n_chunks):
    C = RWKV_CHUNK
    S = 2 * C
    assert S == LANES
    b_id = pl.program_id(0)
    c = pl.program_id(1)
    pairs = range(N_PAIRS)
    staging = (st_lhs, st_rhs, st_vs, st_v, st_kh, st_bh, st_gamma, st_bonus, st_inv, st_arb, st_axv)

    @pl.when(c == 0)
    def _():
        prev_rkv[...] = pinit_rkv_ref[...]
        prev_lora[...] = pinit_lora_ref[...]

    @pl.when(c <= RWKV_SLOTS - 1)
    def _():
        h_scr[...] = hinit_ref[0]

    @pl.when(jnp.logical_and(b_id == 0, c == 0))
    def _():
        for ref in staging:
            for slot in range(1, RWKV_SLOTS):
                ref[slot] = jnp.zeros(ref.shape[1:], ref.dtype)

    def cols(p, base=0):
        return slice(base + p * LANES, base + (p + 1) * LANES)

    def make_stack():
        head0 = lax.broadcasted_iota(jnp.int32, (C, LANES), 1) < HEAD

        def stack(t):
            return jnp.concatenate([jnp.where(head0, t, 0.0), jnp.where(head0, 0.0, t)], axis=0).astype(BF16)
        return head0, stack

    def make_head_sum():
        li = lax.broadcasted_iota(jnp.int32, (2 * LANES, LANES), 0)
        lj = lax.broadcasted_iota(jnp.int32, (2 * LANES, LANES), 1)
        ones_bd2 = jnp.where(((li % LANES) // HEAD) == (lj // HEAD), 1.0, 0.0).astype(BF16)

        def head_sum(t):
            rows = jnp.concatenate([t[:, cols(p)] for p in pairs], axis=0)
            hi, lo = _split2(rows)
            s = _dot(jnp.concatenate([hi, lo], axis=1), ones_bd2)
            return jnp.concatenate([s[p * C:(p + 1) * C] for p in pairs], axis=1)
        return head_sum

    def prepare(slot):
        head0, stack = make_stack()
        head_sum = make_head_sum()
        row = lax.broadcasted_iota(jnp.int32, (C, 1), 0)

        def token_shift(x, prev_tile, mu):
            xp = jnp.where(row == 0, prev_tile[SUBLANES - 1:SUBLANES, :], pltpu.roll(x, 1, 0))
            return x + (xp - x) * mu

        xl = lora_ref[...]
        ul = token_shift(xl, prev_lora[...], mu_lora_ref[...])
        prev_lora[...] = xl[C - SUBLANES:C]
        lo_out = _dot(jnp.where(head0, jnp.tanh(ul), ul).astype(BF16), wl_ref[...])
        yield

        def shifted(base):
            sl = slice(base, base + RWKV_WIDTH)
            xs = rkv_ref[:, sl].astype(F32)
            out = token_shift(xs, prev_rkv[:, sl], mu_rkv_ref[:, sl])
            prev_rkv[:, sl] = xs[C - SUBLANES:C]
            return out

        r = shifted(0)
        yield
        k = shifted(RWKV_WIDTH)
        yield
        v = shifted(2 * RWKV_WIDTH)
        yield

        logdec = -math.exp(-0.5) * _sigmoid(w0_ref[...] + lo_out[:, 0:RWKV_WIDTH])
        yield
        a_lr = _sigmoid(a0_ref[...] + lo_out[:, RWKV_WIDTH:2 * RWKV_WIDTH])
        yield

        ti = lax.broadcasted_iota(jnp.int32, (C, C), 0)
        si = lax.broadcasted_iota(jnp.int32, (C, C), 1)
        tri = jnp.where(ti >= si, 1.0, 0.0).astype(BF16)
        ld_hi, ld_lo = _split2(logdec)
        cum = _dot(tri, ld_hi) + _dot(tri, ld_lo)
        cum_last = cum[C - 1:C, :]
        yield
        e_excl = jnp.exp(cum - logdec)
        e_incl = jnp.exp(cum)
        yield
        e_neg = jnp.exp(-cum)
        e_hat = jnp.exp(cum_last - cum)
        yield

        kk = k * kk_ref[...]
        kk = kk * lax.rsqrt(jnp.maximum(head_sum(kk * kk), 1e-24))
        yield
        k2 = k * (1.0 + (a_lr - 1.0) * ka_ref[...])
        b = kk * a_lr
        yield
        xa_all = -kk * e_excl
        xr_all = r * e_incl
        yield
        yb_all = b * e_neg
        yk_all = k2 * e_neg
        yield

        st_gamma[slot] = jnp.exp(cum_last)
        st_bonus[slot] = head_sum(r * k2 * rk_ref[...]) * v
        yield
        st_v[slot] = v.astype(BF16)
        st_kh[slot] = (k2 * e_hat).astype(BF16)
        st_bh[slot] = (b * e_hat).astype(BF16)
        yield
        for p in pairs:
            sl = cols(p)
            st_lhs[slot, p] = jnp.concatenate([xa_all[:, sl], xr_all[:, sl]], axis=0).astype(BF16)
            st_rhs[slot, p] = jnp.concatenate([stack(yb_all[:, sl]), stack(yk_all[:, sl])], axis=0)
            st_vs[slot, p] = stack(v[:, sl])
            yield

    def stage(fn):
        out = []
        for p in pairs:
            out.append(fn(p))
            yield
        return out

    def invert(slot):
        _, stack = make_stack()
        ri = lax.broadcasted_iota(jnp.int32, (S, 2 * S), 0)
        ci = lax.broadcasted_iota(jnp.int32, (S, 2 * S), 1)
        a_mask = (ri % C) >= (ci % C) + jnp.where(ri < C, 1, 0)
        ei = lax.broadcasted_iota(jnp.int32, (C, S), 0)
        ej = lax.broadcasted_iota(jnp.int32, (C, S), 1)
        eye = jnp.where(ei == ej % C, 1.0, 0.0).astype(F32)

        a_all = yield from stage(lambda p: jnp.where(a_mask, _dot_nt(st_lhs[slot, p], st_rhs[slot, p]), 0.0))
        a_ab = [a_all[p][0:C, 0:S] for p in pairs]

        def values(p):
            st_arb[slot, p] = a_all[p][C:S, 0:S].astype(BF16)
            st_axv[slot, p] = _dot(a_all[p][:, S:2 * S].astype(BF16), st_vs[slot, p])

        yield from stage(values)

        inv = [eye + a_ab[p] for p in pairs]
        apow = yield from stage(lambda p: _dot(a_ab[p].astype(BF16), stack(a_ab[p])))
        n = 2
        while 2 * n < C:
            res = yield from stage(
                lambda p: _dot(jnp.concatenate([inv[p], apow[p]], axis=0).astype(BF16), stack(apow[p])))
            inv = [inv[p] + res[p][0:C] for p in pairs]
            apow = [res[p][C:S] for p in pairs]
            n *= 2

        def last(p):
            st_inv[slot, p] = (inv[p] + _dot(inv[p].astype(BF16), stack(apow[p]))).astype(BF16)

        yield from stage(last)

    def state(slot):
        _, stack = make_stack()
        head_sum = make_head_sum()
        bi = lax.broadcasted_iota(jnp.int32, (LANES, LANES), 0)
        bj = lax.broadcasted_iota(jnp.int32, (LANES, LANES), 1)
        same_head = (bi // HEAD) == (bj // HEAD)

        s_old = [h_scr[p] for p in pairs]
        x_s = yield from stage(lambda p: _dot_nt(st_lhs[slot, p], s_old[p].astype(BF16)))
        us = yield from stage(lambda p: _dot(st_inv[slot, p], stack(x_s[p][0:C] + st_axv[slot, p, 0:C])))
        ys = yield from stage(
            lambda p: x_s[p][C:S] + st_axv[slot, p, C:S] + _dot(st_arb[slot, p], stack(us[p])))

        def update(p):
            sl = cols(p)
            vu = jnp.concatenate([st_v[slot, :, sl], us[p].astype(BF16)], axis=0)
            khb = jnp.concatenate([st_kh[slot, :, sl], st_bh[slot, :, sl]], axis=0)
            h_scr[p] = s_old[p] * st_gamma[slot, :, sl] + jnp.where(same_head, _dot_tn(vu, khb), 0.0)

        yield from stage(update)

        y = jnp.concatenate(ys, axis=1)
        yc = y - head_sum(y) * (1.0 / HEAD)
        yield
        var = head_sum(yc * yc) * (1.0 / HEAD)
        y = yc * lax.rsqrt(var + GN_EPS) * gnw_ref[...] + gnb_ref[...] + st_bonus[slot]
        yield
        y_ref[...] = (y * _silu(g_ref[...].astype(F32))).astype(y_ref.dtype)

    phase = lax.rem(c, RWKV_SLOTS)
    for r in range(RWKV_SLOTS):
        @pl.when(phase == r)
        def _():
            _interleave((invert((r - 1) % RWKV_SLOTS), RWKV_WEIGHTS[0]),
                        (state((r - 2) % RWKV_SLOTS), RWKV_WEIGHTS[1]),
                        (prepare(r), RWKV_WEIGHTS[2]))

    @pl.when(c == n_chunks + RWKV_SLOTS - 2)
    def _():
        hout_ref[0] = h_scr[...]


def _rwkv(p_arr, lora_arr, pinit_rkv, pinit_lora, hinit, params, *, batch):
    C = RWKV_CHUNK
    rows = p_arr.shape[0]
    n_chunks = rows // (batch * C)
    const2 = lambda b, c: (0, 0)
    vec = lambda width: pl.BlockSpec((1, width), const2)
    prep_chunk = lambda b, c: b * n_chunks + jnp.minimum(c, n_chunks - 1)
    recur_chunk = lambda b, c: b * n_chunks + jnp.maximum(c - (RWKV_SLOTS - 1), 0)
    slots = lambda shape, dtype: pltpu.VMEM((RWKV_SLOTS,) + shape, dtype)
    return pl.pallas_call(
        functools.partial(_rwkv_kernel, n_chunks=n_chunks),
        out_shape=(jax.ShapeDtypeStruct((rows, RWKV_WIDTH), BF16),
                   jax.ShapeDtypeStruct((batch, N_PAIRS, LANES, LANES), F32)),
        grid=(batch, n_chunks + RWKV_SLOTS - 1),
        in_specs=[
            pl.BlockSpec((C, 3 * RWKV_WIDTH), lambda b, c: (prep_chunk(b, c), COL_RKV // (3 * RWKV_WIDTH))),
            pl.BlockSpec((C, RWKV_WIDTH), lambda b, c: (recur_chunk(b, c), COL_GR // RWKV_WIDTH)),
            pl.BlockSpec((C, LANES), lambda b, c: (prep_chunk(b, c), 0)),
            pl.BlockSpec((SUBLANES, 3 * RWKV_WIDTH), const2),
            pl.BlockSpec((SUBLANES, LANES), const2),
            pl.BlockSpec((1, N_PAIRS, LANES, LANES), lambda b, c: (0, 0, 0, 0)),
            vec(3 * RWKV_WIDTH), vec(LANES), vec(RWKV_WIDTH), vec(RWKV_WIDTH),
            pl.BlockSpec((LANES, 2 * RWKV_WIDTH), const2),
            vec(RWKV_WIDTH), vec(RWKV_WIDTH), vec(RWKV_WIDTH), vec(RWKV_WIDTH), vec(RWKV_WIDTH),
        ],
        out_specs=(
            pl.BlockSpec((C, RWKV_WIDTH), lambda b, c: (recur_chunk(b, c), 0)),
            pl.BlockSpec((1, N_PAIRS, LANES, LANES), lambda b, c: (b, 0, 0, 0)),
        ),
        scratch_shapes=[
            pltpu.VMEM((N_PAIRS, LANES, LANES), F32),
            pltpu.VMEM((SUBLANES, 3 * RWKV_WIDTH), F32),
            pltpu.VMEM((SUBLANES, LANES), F32),
            slots((N_PAIRS, 2 * C, LANES), BF16),
            slots((N_PAIRS, 4 * C, LANES), BF16),
            slots((N_PAIRS, 2 * C, LANES), BF16),
            slots((C, RWKV_WIDTH), BF16),
            slots((C, RWKV_WIDTH), BF16),
            slots((C, RWKV_WIDTH), BF16),
            slots((1, RWKV_WIDTH), F32),
            slots((C, RWKV_WIDTH), F32),
            slots((N_PAIRS, C, 2 * C), BF16),
            slots((N_PAIRS, C, 2 * C), BF16),
            slots((N_PAIRS, 2 * C, LANES), F32),
        ],
        compiler_params=pltpu.CompilerParams(
            dimension_semantics=("arbitrary", "arbitrary"), vmem_limit_bytes=VMEM_LIMIT),
        name="rwkv7_chunk",
    )(p_arr, p_arr, lora_arr, pinit_rkv, pinit_lora, hinit, *params)


def _attn_kernel(q_ref, k_ref, v_ref, g_ref, km_ref, vmt_ref, lq1_ref, lk1_ref, lq2_ref, lk2_ref, sw_ref,
                 o_ref, vt_scr, qst_scr, m_scr, acc_scr, s_scr, *, t, n_blk):
    g_id = pl.program_id(0)

    @pl.when(g_id == 0)
    def _():
        for ref in (vt_scr, qst_scr, m_scr, acc_scr):
            ref[1] = jnp.ones(ref.shape[1:], ref.dtype)

    def setup(slot):
        comp0 = lax.broadcasted_iota(jnp.int32, (LANES, t // 2), 0) < HEAD
        for c in range(n_blk):
            vt_scr[slot, c, 0:LANES, :] = v_ref[c * t:(c + 1) * t, :].astype(F32).T.astype(BF16)
            vt_scr[slot, c, LANES:LANES + ONES_ROWS, :] = jnp.ones((ONES_ROWS, t), BF16)
            yield
        for i in range(n_blk):
            qt = q_ref[i * t:(i + 1) * t, :].astype(F32).T
            parts = []
            for half in (qt[:, 0:t // 2], qt[:, t // 2:t]):
                parts += [jnp.where(comp0, half, 0.0), jnp.where(comp0, 0.0, half)]
            qst = jnp.concatenate(parts, axis=1).astype(BF16)
            qst_scr[slot, i] = qst
            yield
            s = _dot(km_ref[...], qst)
            m = jnp.max(s, axis=0, keepdims=True)
            m_scr[slot, i] = m
            acc_scr[slot, i] = _dot(vmt_ref[0], jnp.exp2(s - m).astype(BF16))
            yield

    def blocks(slot):
        h = t // 2
        full_rows, lo_rows, hi_rows = slice(0, t), slice(0, h), slice(h, t)
        all_cols, hi_cols = slice(0, 2 * t), slice(t, 2 * t)
        units = []
        for i, j in sorted(((i, j) for i in range(n_blk) for j in range(i + 1)), key=lambda ij: (ij[1], ij[0])):
            if i == j:
                units += [(i, j, lo_rows, all_cols, True), (i, j, hi_rows, hi_cols, True)]
            else:
                units.append((i, j, full_rows, all_cols, False))
        n_slots = s_scr.shape[0]
        m = [m_scr[slot, i] for i in range(n_blk)]

        def extent(sl):
            return sl.stop - sl.start

        def issue(n):
            i, j, rows, cols, _ = units[n]
            keys = k_ref[j * t + rows.start:j * t + rows.stop, :]
            s_scr[n % n_slots, 0:extent(rows), 0:extent(cols)] = _dot(keys, qst_scr[slot, i, :, cols])

        def visible(n_cols):
            kpos = lax.broadcasted_iota(jnp.int32, (h, n_cols), 0)
            col = lax.broadcasted_iota(jnp.int32, (h, n_cols), 1)
            return kpos <= col % h + jnp.where(col >= t, h, 0)

        visible = {n_cols: visible(n_cols) for n_cols in (t, 2 * t)}
        for n in range(min(ATT_LOOKAHEAD, len(units))):
            issue(n)
        for n, (i, j, rows, cols, masked) in enumerate(units):
            if n + ATT_LOOKAHEAD < len(units):
                issue(n + ATT_LOOKAHEAD)
            s = s_scr[n % n_slots, 0:extent(rows), 0:extent(cols)]
            if masked:
                s = jnp.where(visible[extent(cols)], s, MASK_VALUE)
            m_old = m[i][:, cols]
            m_new = jnp.maximum(m_old, jnp.max(s, axis=0, keepdims=True))
            alpha = jnp.exp2(m_old - m_new)
            m[i] = m_new if extent(cols) == 2 * t else jnp.concatenate([m[i][:, 0:cols.start], m_new], axis=1)
            acc_scr[slot, i, :, cols] = (alpha * acc_scr[slot, i, :, cols]
                                         + _dot(vt_scr[slot, j, :, rows], jnp.exp2(s - m_new).astype(BF16)))
            yield

        lam = (jnp.exp(jnp.sum(lq1_ref[...] * lk1_ref[...], axis=-1, keepdims=True))
               - jnp.exp(jnp.sum(lq2_ref[...] * lk2_ref[...], axis=-1, keepdims=True)) + LAMBDA_INIT)
        for i in range(n_blk):
            on = acc_scr[slot, i, 0:LANES, :] / acc_scr[slot, i, LANES:LANES + 1, :]
            comp = lambda c: jnp.concatenate([on[:, c * h:(c + 1) * h], on[:, t + c * h:t + (c + 1) * h]], axis=1)
            ot = comp(0) - lam * comp(1)
            ot = ot * lax.rsqrt(jnp.mean(ot * ot, axis=0, keepdims=True) + SUBLN_EPS)
            g = g_ref[i * t:(i + 1) * t, :].astype(F32)
            o_ref[i * t:(i + 1) * t, :] = (ot.T * sw_ref[...] * (1.0 - LAMBDA_INIT) * _silu(g)).astype(o_ref.dtype)
            yield

    odd = jnp.bitwise_and(g_id, 1) == 1

    @pl.when(jnp.logical_not(odd))
    def _():
        _interleave((blocks(1), ATT_WEIGHTS[0]), (setup(0), ATT_WEIGHTS[1]))

    @pl.when(odd)
    def _():
        _interleave((blocks(0), ATT_WEIGHTS[0]), (setup(1), ATT_WEIGHTS[1]))


def _attention(p_arr, p_meta, vm_t, lam_vecs, subln_w, *, batch, seq):
    t = ATT_BLOCK
    n_blk = seq // t
    n_heads_total = batch * DIFF_HEADS
    lanes_blk = lambda col: col // LANES
    small = pl.BlockSpec((1, HEAD), lambda g: (0, 0))
    setup_head = lambda g: jnp.minimum(g, n_heads_total - 1)
    block_head = lambda g: jnp.maximum(g - 1, 0)

    def head_cols(col, head_of):
        def index(g):
            h = head_of(g)
            return (h // DIFF_HEADS, lanes_blk(col) + h % DIFF_HEADS)
        return pl.BlockSpec((seq, LANES), index)

    slots = lambda shape, dtype: pltpu.VMEM((2,) + shape, dtype)
    return pl.pallas_call(
        functools.partial(_attn_kernel, t=t, n_blk=n_blk),
        out_shape=jax.ShapeDtypeStruct((batch * seq, DIFF_WIDTH), BF16),
        grid=(n_heads_total + 1,),
        in_specs=[
            head_cols(COL_Q, setup_head), head_cols(COL_K, block_head), head_cols(COL_V, setup_head),
            head_cols(COL_GD, block_head),
            pl.BlockSpec((N_META, LANES), lambda g: (0, lanes_blk(COL_K) + setup_head(g) % DIFF_HEADS)),
            pl.BlockSpec((1, LANES + ONES_ROWS, N_META), lambda g: (setup_head(g) % DIFF_HEADS, 0, 0)),
            small, small, small, small,
            pl.BlockSpec((1, LANES), lambda g: (0, 0)),
        ],
        out_specs=pl.BlockSpec((seq, LANES), lambda g: (block_head(g) // DIFF_HEADS, block_head(g) % DIFF_HEADS)),
        scratch_shapes=[
            slots((n_blk, LANES + ONES_ROWS, t), BF16),
            slots((n_blk, LANES, 2 * t), BF16),
            slots((n_blk, 1, 2 * t), F32),
            slots((n_blk, LANES + ONES_ROWS, 2 * t), F32),
            pltpu.VMEM((ATT_LOOKAHEAD + 1, t, 2 * t), F32),
        ],
        compiler_params=pltpu.CompilerParams(
            dimension_semantics=("arbitrary",), vmem_limit_bytes=VMEM_LIMIT),
        name="diff_attn",
    )(p_arr, p_arr, p_arr, p_arr, p_meta, vm_t, *lam_vecs, subln_w)


def _outproj_kernel(yr_ref, yd_ref, w1_ref, w2_ref, x_ref, g_ref, o_ref):
    y = _dot(yr_ref[...], w1_ref[...]) + _dot(yd_ref[...], w2_ref[...])
    ms = jnp.mean(y * y, axis=-1, keepdims=True)
    o_ref[...] = x_ref[...] + y * lax.rsqrt(ms + NORM_EPS) * g_ref[...]


def _outproj(y_r, y_d, w, x2, g):
    m = x2.shape[0]
    tm = OUT_TM
    assert RWKV_WIDTH == DIFF_WIDTH and w.shape == (RWKV_WIDTH + DIFF_WIDTH, D_MODEL)
    return pl.pallas_call(
        _outproj_kernel,
        out_shape=jax.ShapeDtypeStruct((m, D_MODEL), F32),
        grid=(m // tm,),
        in_specs=[
            pl.BlockSpec((tm, RWKV_WIDTH), lambda i: (i, 0)),
            pl.BlockSpec((tm, DIFF_WIDTH), lambda i: (i, 0)),
            pl.BlockSpec((RWKV_WIDTH, D_MODEL), lambda i: (0, 0)),
            pl.BlockSpec((DIFF_WIDTH, D_MODEL), lambda i: (1, 0)),
            pl.BlockSpec((tm, D_MODEL), lambda i: (i, 0)),
            pl.BlockSpec((1, D_MODEL), lambda i: (0, 0)),
        ],
        out_specs=pl.BlockSpec((tm, D_MODEL), lambda i: (i, 0)),
        compiler_params=pltpu.CompilerParams(
            dimension_semantics=("arbitrary",), vmem_limit_bytes=VMEM_LIMIT),
        name="outproj",
    )(y_r, y_d, w, w, x2, g)


def _rope_tables(first_pos, n_pos):
    pos = jnp.arange(first_pos, first_pos + n_pos, dtype=F32)
    inv_freq = ROPE_THETA ** (-jnp.arange(0, HEAD, 2, dtype=F32) / HEAD)
    ang = pos[:, None] * inv_freq[None, :]
    cos = jnp.cos(ang)
    sin = jnp.sin(ang)
    cos = jnp.concatenate([cos, cos, cos, cos], axis=-1)
    sin = jnp.concatenate([-sin, sin, -sin, sin], axis=-1)
    return cos, sin


def kernel(x, meta_tokens, pre_norm_w, w_in, rwkv_mu, rwkv_w0, rwkv_w_up, rwkv_a0, rwkv_a_up, rwkv_k_k, rwkv_k_a, rwkv_r_k, rwkv_gn_w, rwkv_gn_b, diff_lam_q1, diff_lam_k1, diff_lam_q2, diff_lam_k2, diff_subln_w, w_out, post_norm_w):
    batch, seq, d = x.shape
    assert d == D_MODEL and meta_tokens.shape == (N_META, D_MODEL)
    assert seq % RWKV_CHUNK == 0 and seq % ATT_BLOCK == 0
    assert (batch * seq) % PROJ_TM == 0 and seq % PROJ_TM == 0
    layer = 0
    x2 = x.reshape(batch * seq, D_MODEL)

    w = w_in[layer]
    rkv_end = 3 * RWKV_WIDTH
    lora_end = rkv_end + 2 * LORA
    g_pre = pre_norm_w[layer].reshape(1, D_MODEL)
    cos_m, sin_m = _rope_tables(0, N_META)
    cos_x, sin_x = _rope_tables(N_META, seq)
    w_main, w_lora_in, p_meta, lora_meta = _wprep(w, meta_tokens.astype(x.dtype), g_pre, cos_m, sin_m)
    p_x, lora_x = _inproj(x2, g_pre, w_main, w_lora_in, cos_x, sin_x)

    mu = rwkv_mu[layer]
    zeros = jnp.zeros((LORA, RWKV_WIDTH), F32)
    w_lora = jnp.concatenate([
        jnp.concatenate([rwkv_w_up[layer], zeros], axis=1),
        jnp.concatenate([zeros, rwkv_a_up[layer]], axis=1)], axis=0)
    row = lambda t, n: t.reshape(1, n)
    rwkv_params = (
        row(mu[:rkv_end], rkv_end), row(mu[rkv_end:lora_end], LANES),
        row(rwkv_w0[layer], RWKV_WIDTH), row(rwkv_a0[layer], RWKV_WIDTH),
        w_lora.astype(BF16),
        row(rwkv_k_k[layer], RWKV_WIDTH), row(rwkv_k_a[layer], RWKV_WIDTH),
        row(rwkv_r_k[layer], RWKV_WIDTH), row(rwkv_gn_w[layer], RWKV_WIDTH), row(rwkv_gn_b[layer], RWKV_WIDTH),
    )
    pad = RWKV_CHUNK - N_META
    _, h_meta = _rwkv(jnp.pad(p_meta, ((pad, 0), (0, 0))), jnp.pad(lora_meta, ((pad, 0), (0, 0))),
                      jnp.zeros((SUBLANES, rkv_end), F32), jnp.zeros((SUBLANES, LANES), F32),
                      jnp.zeros((1, N_PAIRS, LANES, LANES), F32), rwkv_params, batch=1)
    y_r, _ = _rwkv(p_x, lora_x, p_meta[N_META - SUBLANES:, :rkv_end].astype(F32), lora_meta[N_META - SUBLANES:],
                   h_meta, rwkv_params, batch=batch)

    lam_vecs = tuple(t[layer].reshape(1, HEAD) for t in (diff_lam_q1, diff_lam_k1, diff_lam_q2, diff_lam_k2))
    vm_t = p_meta[:, COL_V:COL_V + DIFF_WIDTH].reshape(N_META, DIFF_HEADS, LANES).transpose(1, 2, 0)
    vm_t = jnp.concatenate([vm_t, jnp.ones((DIFF_HEADS, ONES_ROWS, N_META), BF16)], axis=1)
    y_d = _attention(p_x, p_meta, vm_t, lam_vecs, diff_subln_w[layer].reshape(1, LANES), batch=batch, seq=seq)

    out = _outproj(y_r, y_d, w_out[layer].astype(BF16), x2, post_norm_w[layer].reshape(1, D_MODEL))
    return out.reshape(batch, seq, D_MODEL)
```

```python
import functools
import math

import jax
import jax.numpy as jnp
from jax import lax
from jax.experimental import pallas as pl
from jax.experimental.pallas import tpu as pltpu

F32 = jnp.float32
BF16 = jnp.bfloat16

D_MODEL = 2048
N_META = 16
HEAD = 64
LANES = 128
SUBLANES = 8
V7X_VMEM_BYTES = 64 * 1024 * 1024
RWKV_WIDTH = 1024
N_PAIRS = RWKV_WIDTH // LANES
DIFF_WIDTH = 1024
DIFF_HEADS = DIFF_WIDTH // LANES
LORA = 64
ROPE_THETA = 10000.0
NORM_EPS = 1e-6
GN_EPS = 64e-5
SUBLN_EPS = 1e-5
LAMBDA_INIT = 0.8 - 0.6 * math.exp(-0.3 * 0)

COL_RKV = 0
COL_GR = 3 * RWKV_WIDTH
COL_Q = 4 * RWKV_WIDTH
COL_K = COL_Q + DIFF_WIDTH
COL_V = COL_K + DIFF_WIDTH
COL_GD = COL_V + DIFF_WIDTH
P_COLS = COL_GD + DIFF_WIDTH

WPREP_ROWS = 256
PROJ_TM = 1024
PROJ_TN = 1024
PROJ_KC = 512
RWKV_CHUNK = 64
HEADSUM_LANES = 256
RWKV_SLOTS = 3
RWKV_WEIGHTS = (3, 2, 1)
ATT_BLOCK = 512
ATT_LOOKAHEAD = 1
ATT_WEIGHTS = (3, 2)
OUT_TM = 256
VMEM_LIMIT = V7X_VMEM_BYTES * 3 // 4
MASK_VALUE = -1e30
ONES_ROWS = 16
Q_SCALE = HEAD ** -0.5 * math.log2(math.e)


def _dot(a, b):
    return jnp.dot(a, b, preferred_element_type=F32)


def _dot_nt(a, b):
    return lax.dot_general(a, b, (((1,), (1,)), ((), ())), preferred_element_type=F32)


def _dot_tn(a, b):
    return lax.dot_general(a, b, (((0,), (0,)), ((), ())), preferred_element_type=F32)


def _split2(x):
    hi = x.astype(BF16)
    lo = (x - hi.astype(F32)).astype(BF16)
    return hi, lo


def _sigmoid(x):
    return 1.0 / (1.0 + jnp.exp(-x))


def _silu(x):
    return x * _sigmoid(x)


def _interleave(*weighted_generators):
    active = [list(gw) for gw in weighted_generators]
    while active:
        for item in list(active):
            gen, weight = item
            for _ in range(weight):
                if next(gen, "done") == "done":
                    active.remove(item)
                    break


def _rotate_half(blk, cos, sin):
    lane = lax.broadcasted_iota(jnp.int32, blk.shape, 1)
    first_half = (lane % HEAD) < (HEAD // 2)
    swapped = jnp.where(first_half, pltpu.roll(blk, LANES - HEAD // 2, 1), pltpu.roll(blk, HEAD // 2, 1))
    return blk * cos + swapped * sin


def _wprep_kernel(w_ref, meta_ref, g_ref, cos_ref, sin_ref, main_ref, lora_ref, pm_ref, lm_ref,
                  acc_ref, lacc_ref, ssq_ref):
    i = pl.program_id(0)
    rkv_end = 3 * RWKV_WIDTH
    lora_end = rkv_end + 2 * LORA
    w_rkv = w_ref[:, 0:rkv_end].astype(BF16)
    w_lora = w_ref[:, rkv_end:lora_end].astype(BF16)
    w_rest = w_ref[:, lora_end:lora_end + P_COLS - rkv_end].astype(BF16)
    main_ref[:, 0:rkv_end] = w_rkv
    main_ref[:, rkv_end:P_COLS] = w_rest
    lora_ref[...] = w_lora

    @pl.when(i == 0)
    def _():
        acc_ref[...] = jnp.zeros_like(acc_ref)
        lacc_ref[...] = jnp.zeros_like(lacc_ref)
        ssq_ref[...] = jnp.zeros_like(ssq_ref)

    xm = meta_ref[0]
    hn = (xm * g_ref[0]).astype(BF16)
    ssq_ref[...] += jnp.broadcast_to(jnp.sum(xm * xm, axis=-1, keepdims=True), ssq_ref.shape)
    acc_ref[:, 0:rkv_end] += _dot(hn, w_rkv)
    acc_ref[:, rkv_end:P_COLS] += _dot(hn, w_rest)
    lacc_ref[...] += _dot(hn, w_lora)

    @pl.when(i == pl.num_programs(0) - 1)
    def _():
        rs = lax.rsqrt(ssq_ref[...] * (1.0 / D_MODEL) + NORM_EPS)
        lm_ref[...] = lacc_ref[...] * rs
        for c in range(P_COLS // LANES):
            cb = slice(c * LANES, (c + 1) * LANES)
            blk = acc_ref[:, cb]
            if COL_Q <= c * LANES < COL_K:
                blk = _rotate_half(blk, cos_ref[...], sin_ref[...]) * Q_SCALE
            elif COL_K <= c * LANES < COL_V:
                blk = _rotate_half(blk, cos_ref[...], sin_ref[...])
            pm_ref[:, cb] = (blk * rs).astype(pm_ref.dtype)


def _wprep(w, meta_tokens, g, cos, sin):
    rows, cols = w.shape
    assert cols == P_COLS + 2 * LORA and rows % WPREP_ROWS == 0
    n_steps = rows // WPREP_ROWS
    meta_k = meta_tokens.reshape(N_META, n_steps, WPREP_ROWS).transpose(1, 0, 2)
    g_k = g.reshape(n_steps, 1, WPREP_ROWS)
    const = lambda i: (0, 0)
    return pl.pallas_call(
        _wprep_kernel,
        out_shape=(jax.ShapeDtypeStruct((rows, P_COLS), BF16), jax.ShapeDtypeStruct((rows, 2 * LORA), BF16),
                   jax.ShapeDtypeStruct((N_META, P_COLS), BF16), jax.ShapeDtypeStruct((N_META, LANES), F32)),
        grid=(n_steps,),
        in_specs=[
            pl.BlockSpec((WPREP_ROWS, cols), lambda i: (i, 0)),
            pl.BlockSpec((1, N_META, WPREP_ROWS), lambda i: (i, 0, 0)),
            pl.BlockSpec((1, 1, WPREP_ROWS), lambda i: (i, 0, 0)),
            pl.BlockSpec((N_META, LANES), const),
            pl.BlockSpec((N_META, LANES), const),
        ],
        out_specs=(pl.BlockSpec((WPREP_ROWS, P_COLS), lambda i: (i, 0)),
                   pl.BlockSpec((WPREP_ROWS, 2 * LORA), lambda i: (i, 0)),
                   pl.BlockSpec((N_META, P_COLS), const),
                   pl.BlockSpec((N_META, LANES), const)),
        scratch_shapes=[pltpu.VMEM((N_META, P_COLS), F32), pltpu.VMEM((N_META, LANES), F32),
                        pltpu.VMEM((N_META, LANES), F32)],
        compiler_params=pltpu.CompilerParams(dimension_semantics=("arbitrary",), vmem_limit_bytes=VMEM_LIMIT),
        name="wprep",
    )(w, meta_k, g_k, cos, sin)


def _inproj_kernel(*refs, tn):
    n_k = D_MODEL // PROJ_KC
    x_refs = refs[:n_k]
    g_ref, w_ref, wl_ref, cos_ref, sin_ref, o_ref, ol_ref, hn_ref, rs_ref = refs[n_k:]
    j = pl.program_id(1)
    q_tile = COL_Q // tn
    k_tile = COL_K // tn
    assert 0 < q_tile < k_tile
    is_rope = jnp.logical_or(j == q_tile, j == k_tile)
    col_blocks = [slice(c * LANES, (c + 1) * LANES) for c in range(tn // LANES)]

    @pl.when(j == 0)
    def _():
        ssq = None
        acc = None
        for k in range(n_k):
            sl = slice(k * PROJ_KC, (k + 1) * PROJ_KC)
            xc = x_refs[k][...]
            part = jnp.sum(xc * xc, axis=-1, keepdims=True)
            ssq = part if ssq is None else ssq + part
            hn = (xc * g_ref[:, sl]).astype(BF16)
            hn_ref[:, sl] = hn
            d = _dot(hn, w_ref[sl, :])
            acc = d if acc is None else acc + d
        rs = jnp.broadcast_to(lax.rsqrt(ssq * (1.0 / D_MODEL) + NORM_EPS), rs_ref.shape)
        rs_ref[...] = rs
        ol_ref[...] = _dot(hn_ref[...], wl_ref[...]) * rs
        for cb in col_blocks:
            o_ref[:, cb] = (acc[:, cb] * rs).astype(o_ref.dtype)

    @pl.when(is_rope)
    def _():
        acc = _dot(hn_ref[...], w_ref[...])
        cos = cos_ref[...]
        sin = sin_ref[...]
        scale = rs_ref[...] * jnp.where(j == q_tile, Q_SCALE, 1.0).astype(F32)
        for cb in col_blocks:
            o_ref[:, cb] = (_rotate_half(acc[:, cb], cos, sin) * scale).astype(o_ref.dtype)

    @pl.when(jnp.logical_and(j > 0, jnp.logical_not(is_rope)))
    def _():
        acc = _dot(hn_ref[...], w_ref[...])
        rs = rs_ref[...]
        for cb in col_blocks:
            o_ref[:, cb] = (acc[:, cb] * rs).astype(o_ref.dtype)


def _inproj(x2, g, w_main, w_lora, cos, sin):
    m = x2.shape[0]
    tm = PROJ_TM
    tn = PROJ_TN
    n_pos_tiles = cos.shape[0] // tm
    n_row_tiles = m // tm
    n_col_tiles = P_COLS // tn
    n_k = D_MODEL // PROJ_KC
    assert n_k <= n_col_tiles

    def x_chunk(k):
        first_step = n_col_tiles - n_k + k
        return pl.BlockSpec(
            (tm, PROJ_KC), lambda i, j: (jnp.minimum(i + (j >= first_step).astype(jnp.int32), n_row_tiles - 1), k))

    return pl.pallas_call(
        functools.partial(_inproj_kernel, tn=tn),
        out_shape=(jax.ShapeDtypeStruct((m, P_COLS), BF16), jax.ShapeDtypeStruct((m, LANES), F32)),
        grid=(n_row_tiles, n_col_tiles),
        in_specs=[x_chunk(k) for k in range(n_k)] + [
            pl.BlockSpec((1, D_MODEL), lambda i, j: (0, 0)),
            pl.BlockSpec((D_MODEL, tn), lambda i, j: (0, j)),
            pl.BlockSpec((D_MODEL, LANES), lambda i, j: (0, 0)),
            pl.BlockSpec((tm, LANES), lambda i, j: (i % n_pos_tiles, 0)),
            pl.BlockSpec((tm, LANES), lambda i, j: (i % n_pos_tiles, 0)),
        ],
        out_specs=(
            pl.BlockSpec((tm, tn), lambda i, j: (i, j)),
            pl.BlockSpec((tm, LANES), lambda i, j: (i, 0)),
        ),
        scratch_shapes=[pltpu.VMEM((tm, D_MODEL), BF16), pltpu.VMEM((tm, LANES), F32)],
        compiler_params=pltpu.CompilerParams(
            dimension_semantics=("arbitrary", "arbitrary"), vmem_limit_bytes=VMEM_LIMIT),
        name="inproj",
    )(*([x2] * n_k), g, w_main, w_lora, cos, sin)


def _rwkv_kernel(rkv_ref, g_ref, lora_ref, pinit_rkv_ref, pinit_lora_ref, hinit_ref,
                 mu_rkv_ref, mu_lora_ref, w0_ref, a0_ref, wl_ref,
                 kk_ref, ka_ref, rk_ref, gnw_ref, gnb_ref,
                 y_ref, hout_ref,
                 h_scr, prev_rkv, prev_lora,
                 st_lhs, st_rhs, st_vs, st_v, st_kh, st_bh, st_gamma, st_bonus,
                 st_inv, st_arb, st_axv, *, n_chunks):
    C = RWKV_CHUNK
    S = 2 * C
    assert S == LANES
    b_id = pl.program_id(0)
    c = pl.program_id(1)
    pairs = range(N_PAIRS)
    staging = (st_lhs, st_rhs, st_vs, st_v, st_kh, st_bh, st_gamma, st_bonus, st_inv, st_arb, st_axv)

    @pl.when(c == 0)
    def _():
        prev_rkv[...] = pinit_rkv_ref[...]
        prev_lora[...] = pinit_lora_ref[...]

    @pl.when(c <= RWKV_SLOTS - 1)
    def _():
        h_scr[...] = hinit_ref[0]

    @pl.when(jnp.logical_and(b_id == 0, c == 0))
    def _():
        for ref in staging:
            for slot in range(1, RWKV_SLOTS):
                ref[slot] = jnp.zeros(ref.shape[1:], ref.dtype)

    def cols(p, base=0):
        return slice(base + p * LANES, base + (p + 1) * LANES)

    def make_stack():
        head0 = lax.broadcasted_iota(jnp.int32, (C, LANES), 1) < HEAD

        def stack(t):
            return jnp.concatenate([jnp.where(head0, t, 0.0), jnp.where(head0, 0.0, t)], axis=0).astype(BF16)
        return head0, stack

    def make_head_sum():
        li = lax.broadcasted_iota(jnp.int32, (HEADSUM_LANES, HEADSUM_LANES), 0)
        lj = lax.broadcasted_iota(jnp.int32, (HEADSUM_LANES, HEADSUM_LANES), 1)
        ones_bd = jnp.where((li // HEAD) == (lj // HEAD), 1.0, 0.0).astype(BF16)

        def head_sum(t):
            return jnp.concatenate(
                [_dot(t[:, g:g + HEADSUM_LANES].astype(BF16), ones_bd) for g in range(0, RWKV_WIDTH, HEADSUM_LANES)],
                axis=1)
        return head_sum

    def prepare(slot):
        head0, stack = make_stack()
        head_sum = make_head_sum()
        row = lax.broadcasted_iota(jnp.int32, (C, 1), 0)

        def token_shift(x, prev_tile, mu):
            xp = jnp.where(row == 0, prev_tile[SUBLANES - 1:SUBLANES, :], pltpu.roll(x, 1, 0))
            return x + (xp - x) * mu

        xl = lora_ref[...]
        ul = token_shift(xl, prev_lora[...], mu_lora_ref[...])
        prev_lora[...] = xl[C - SUBLANES:C]
        lo_out = _dot(jnp.where(head0, jnp.tanh(ul), ul).astype(BF16), wl_ref[...])
        yield

        def shifted(base):
            sl = slice(base, base + RWKV_WIDTH)
            xs = rkv_ref[:, sl].astype(F32)
            out = token_shift(xs, prev_rkv[:, sl], mu_rkv_ref[:, sl])
            prev_rkv[:, sl] = xs[C - SUBLANES:C]
            return out

        r = shifted(0)
        yield
        k = shifted(RWKV_WIDTH)
        yield
        v = shifted(2 * RWKV_WIDTH)
        yield

        logdec = -math.exp(-0.5) * _sigmoid(w0_ref[...] + lo_out[:, 0:RWKV_WIDTH])
        yield
        a_lr = _sigmoid(a0_ref[...] + lo_out[:, RWKV_WIDTH:2 * RWKV_WIDTH])
        yield

        ti = lax.broadcasted_iota(jnp.int32, (C, C), 0)
        si = lax.broadcasted_iota(jnp.int32, (C, C), 1)
        tri = jnp.where(ti >= si, 1.0, 0.0).astype(BF16)
        ld_hi, ld_lo = _split2(logdec)
        cum = _dot(tri, ld_hi) + _dot(tri, ld_lo)
        cum_last = cum[C - 1:C, :]
        yield
        e_excl = jnp.exp(cum - logdec)
        e_incl = jnp.exp(cum)
        yield
        e_neg = jnp.exp(-cum)
        e_hat = jnp.exp(cum_last - cum)
        yield

        kk = k * kk_ref[...]
        kk = kk * lax.rsqrt(jnp.maximum(head_sum(kk * kk), 1e-24))
        yield
        k2 = k * (1.0 + (a_lr - 1.0) * ka_ref[...])
        b = kk * a_lr
        yield
        xa_all = -kk * e_excl
        xr_all = r * e_incl
        yield
        yb_all = b * e_neg
        yk_all = k2 * e_neg
        yield

        st_gamma[slot] = jnp.exp(cum_last)
        st_bonus[slot] = head_sum(r * k2 * rk_ref[...]) * v
        yield
        st_v[slot] = v.astype(BF16)
        st_kh[slot] = (k2 * e_hat).astype(BF16)
        st_bh[slot] = (b * e_hat).astype(BF16)
        yield
        for p in pairs:
            sl = cols(p)
            st_lhs[slot, p] = jnp.concatenate([xa_all[:, sl], xr_all[:, sl]], axis=0).astype(BF16)
            st_rhs[slot, p] = jnp.concatenate([stack(yb_all[:, sl]), stack(yk_all[:, sl])], axis=0)
            st_vs[slot, p] = stack(v[:, sl])
            yield

    def stage(fn):
        out = []
        for p in pairs:
            out.append(fn(p))
            yield
        return out

    def invert(slot):
        _, stack = make_stack()
        ri = lax.broadcasted_iota(jnp.int32, (S, 2 * S), 0)
        ci = lax.broadcasted_iota(jnp.int32, (S, 2 * S), 1)
        a_mask = (ri % C) >= (ci % C) + jnp.where(ri < C, 1, 0)
        ei = lax.broadcasted_iota(jnp.int32, (C, S), 0)
        ej = lax.broadcasted_iota(jnp.int32, (C, S), 1)
        eye = jnp.where(ei == ej % C, 1.0, 0.0).astype(F32)

        a_all = yield from stage(lambda p: jnp.where(a_mask, _dot_nt(st_lhs[slot, p], st_rhs[slot, p]), 0.0))
        a_ab = [a_all[p][0:C, 0:S] for p in pairs]

        def values(p):
            st_arb[slot, p] = a_all[p][C:S, 0:S].astype(BF16)
            st_axv[slot, p] = _dot(a_all[p][:, S:2 * S].astype(BF16), st_vs[slot, p])

        yield from stage(values)

        inv = [eye + a_ab[p] for p in pairs]
        apow = yield from stage(lambda p: _dot(a_ab[p].astype(BF16), stack(a_ab[p])))
        n = 2
        while 2 * n < C:
            res = yield from stage(
                lambda p: _dot(jnp.concatenate([inv[p], apow[p]], axis=0).astype(BF16), stack(apow[p])))
            inv = [inv[p] + res[p][0:C] for p in pairs]
            apow = [res[p][C:S] for p in pairs]
            n *= 2

        def last(p):
            st_inv[slot, p] = (inv[p] + _dot(inv[p].astype(BF16), stack(apow[p]))).astype(BF16)

        yield from stage(last)

    def state(slot):
        _, stack = make_stack()
        head_sum = make_head_sum()
        bi = lax.broadcasted_iota(jnp.int32, (LANES, LANES), 0)
        bj = lax.broadcasted_iota(jnp.int32, (LANES, LANES), 1)
        same_head = (bi // HEAD) == (bj // HEAD)

        s_old = [h_scr[p] for p in pairs]
        x_s = yield from stage(lambda p: _dot_nt(st_lhs[slot, p], s_old[p].astype(BF16)))
        us = yield from stage(lambda p: _dot(st_inv[slot, p], stack(x_s[p][0:C] + st_axv[slot, p, 0:C])))
        ys = yield from stage(
            lambda p: x_s[p][C:S] + st_axv[slot, p, C:S] + _dot(st_arb[slot, p], stack(us[p])))

        def update(p):
            sl = cols(p)
            vu = jnp.concatenate([st_v[slot, :, sl], us[p].astype(BF16)], axis=0)
            khb = jnp.concatenate([st_kh[slot, :, sl], st_bh[slot, :, sl]], axis=0)
            h_scr[p] = s_old[p] * st_gamma[slot, :, sl] + jnp.where(same_head, _dot_tn(vu, khb), 0.0)

        yield from stage(update)

        y = jnp.concatenate(ys, axis=1)
        yc = y - head_sum(y) * (1.0 / HEAD)
        yield
        var = head_sum(yc * yc) * (1.0 / HEAD)
        y = yc * lax.rsqrt(var + GN_EPS) * gnw_ref[...] + gnb_ref[...] + st_bonus[slot]
        yield
        y_ref[...] = (y * _silu(g_ref[...].astype(F32))).astype(y_ref.dtype)

    phase = lax.rem(c, RWKV_SLOTS)
    for r in range(RWKV_SLOTS):
        @pl.when(phase == r)
        def _():
            _interleave((invert((r - 1) % RWKV_SLOTS), RWKV_WEIGHTS[0]),
                        (state((r - 2) % RWKV_SLOTS), RWKV_WEIGHTS[1]),
                        (prepare(r), RWKV_WEIGHTS[2]))

    @pl.when(c == n_chunks + RWKV_SLOTS - 2)
    def _():
        hout_ref[0] = h_scr[...]


def _rwkv(p_arr, lora_arr, pinit_rkv, pinit_lora, hinit, params, *, batch):
    C = RWKV_CHUNK
    rows = p_arr.shape[0]
    n_chunks = rows // (batch * C)
    const2 = lambda b, c: (0, 0)
    vec = lambda width: pl.BlockSpec((1, width), const2)
    prep_chunk = lambda b, c: b * n_chunks + jnp.minimum(c, n_chunks - 1)
    recur_chunk = lambda b, c: b * n_chunks + jnp.maximum(c - (RWKV_SLOTS - 1), 0)
    slots = lambda shape, dtype: pltpu.VMEM((RWKV_SLOTS,) + shape, dtype)
    return pl.pallas_call(
        functools.partial(_rwkv_kernel, n_chunks=n_chunks),
        out_shape=(jax.ShapeDtypeStruct((rows, RWKV_WIDTH), BF16),
                   jax.ShapeDtypeStruct((batch, N_PAIRS, LANES, LANES), F32)),
        grid=(batch, n_chunks + RWKV_SLOTS - 1),
        in_specs=[
            pl.BlockSpec((C, 3 * RWKV_WIDTH), lambda b, c: (prep_chunk(b, c), COL_RKV // (3 * RWKV_WIDTH))),
            pl.BlockSpec((C, RWKV_WIDTH), lambda b, c: (recur_chunk(b, c), COL_GR // RWKV_WIDTH)),
            pl.BlockSpec((C, LANES), lambda b, c: (prep_chunk(b, c), 0)),
            pl.BlockSpec((SUBLANES, 3 * RWKV_WIDTH), const2),
            pl.BlockSpec((SUBLANES, LANES), const2),
            pl.BlockSpec((1, N_PAIRS, LANES, LANES), lambda b, c: (0, 0, 0, 0)),
            vec(3 * RWKV_WIDTH), vec(LANES), vec(RWKV_WIDTH), vec(RWKV_WIDTH),
            pl.BlockSpec((LANES, 2 * RWKV_WIDTH), const2),
            vec(RWKV_WIDTH), vec(RWKV_WIDTH), vec(RWKV_WIDTH), vec(RWKV_WIDTH), vec(RWKV_WIDTH),
        ],
        out_specs=(
            pl.BlockSpec((C, RWKV_WIDTH), lambda b, c: (recur_chunk(b, c), 0)),
            pl.BlockSpec((1, N_PAIRS, LANES, LANES), lambda b, c: (b, 0, 0, 0)),
        ),
        scratch_shapes=[
            pltpu.VMEM((N_PAIRS, LANES, LANES), F32),
            pltpu.VMEM((SUBLANES, 3 * RWKV_WIDTH), F32),
            pltpu.VMEM((SUBLANES, LANES), F32),
            slots((N_PAIRS, 2 * C, LANES), BF16),
            slots((N_PAIRS, 4 * C, LANES), BF16),
            slots((N_PAIRS, 2 * C, LANES), BF16),
            slots((C, RWKV_WIDTH), BF16),
            slots((C, RWKV_WIDTH), BF16),
            slots((C, RWKV_WIDTH), BF16),
            slots((1, RWKV_WIDTH), F32),
            slots((C, RWKV_WIDTH), F32),
            slots((N_PAIRS, C, 2 * C), BF16),
            slots((N_PAIRS, C, 2 * C), BF16),
            slots((N_PAIRS, 2 * C, LANES), F32),
        ],
        compiler_params=pltpu.CompilerParams(
            dimension_semantics=("arbitrary", "arbitrary"), vmem_limit_bytes=VMEM_LIMIT),
        name="rwkv7_chunk",
    )(p_arr, p_arr, lora_arr, pinit_rkv, pinit_lora, hinit, *params)


def _attn_kernel(q_ref, k_ref, v_ref, g_ref, km_ref, vmt_ref, lq1_ref, lk1_ref, lq2_ref, lk2_ref, sw_ref,
                 o_ref, vt_scr, qst_scr, m_scr, acc_scr, s_scr, *, t, n_blk):
    g_id = pl.program_id(0)

    @pl.when(g_id == 0)
    def _():
        for ref in (vt_scr, qst_scr, m_scr, acc_scr):
            ref[1] = jnp.ones(ref.shape[1:], ref.dtype)

    def setup(slot):
        comp0 = lax.broadcasted_iota(jnp.int32, (LANES, t // 2), 0) < HEAD
        for c in range(n_blk):
            vt_scr[slot, c, 0:LANES, :] = v_ref[c * t:(c + 1) * t, :].astype(F32).T.astype(BF16)
            vt_scr[slot, c, LANES:LANES + ONES_ROWS, :] = jnp.ones((ONES_ROWS, t), BF16)
            yield
        for i in range(n_blk):
            qt = q_ref[i * t:(i + 1) * t, :].astype(F32).T
            parts = []
            for half in (qt[:, 0:t // 2], qt[:, t // 2:t]):
                parts += [jnp.where(comp0, half, 0.0), jnp.where(comp0, 0.0, half)]
            qst = jnp.concatenate(parts, axis=1).astype(BF16)
            qst_scr[slot, i] = qst
            yield
            s = _dot(km_ref[...], qst)
            m = jnp.max(s, axis=0, keepdims=True)
            m_scr[slot, i] = m
            acc_scr[slot, i] = _dot(vmt_ref[0], jnp.exp2(s - m).astype(BF16))
            yield

    def blocks(slot):
        h = t // 2
        full_rows, lo_rows, hi_rows = slice(0, t), slice(0, h), slice(h, t)
        all_cols, hi_cols = slice(0, 2 * t), slice(t, 2 * t)
        units = []
        for i, j in sorted(((i, j) for i in range(n_blk) for j in range(i + 1)), key=lambda ij: (ij[1], ij[0])):
            if i == j:
                units += [(i, j, lo_rows, all_cols, True), (i, j, hi_rows, hi_cols, True)]
            else:
                units.append((i, j, full_rows, all_cols, False))
        n_slots = s_scr.shape[0]
        m = [m_scr[slot, i] for i in range(n_blk)]

        def extent(sl):
            return sl.stop - sl.start

        def issue(n):
            i, j, rows, cols, _ = units[n]
            keys = k_ref[j * t + rows.start:j * t + rows.stop, :]
            s_scr[n % n_slots, 0:extent(rows), 0:extent(cols)] = _dot(keys, qst_scr[slot, i, :, cols])

        def visible(n_cols):
            kpos = lax.broadcasted_iota(jnp.int32, (h, n_cols), 0)
            col = lax.broadcasted_iota(jnp.int32, (h, n_cols), 1)
            return kpos <= col % h + jnp.where(col >= t, h, 0)

        visible = {n_cols: visible(n_cols) for n_cols in (t, 2 * t)}
        for n in range(min(ATT_LOOKAHEAD, len(units))):
            issue(n)
        for n, (i, j, rows, cols, masked) in enumerate(units):
            if n + ATT_LOOKAHEAD < len(units):
                issue(n + ATT_LOOKAHEAD)
            s = s_scr[n % n_slots, 0:extent(rows), 0:extent(cols)]
            if masked:
                s = jnp.where(visible[extent(cols)], s, MASK_VALUE)
            m_old = m[i][:, cols]
            m_new = jnp.maximum(m_old, jnp.max(s, axis=0, keepdims=True))
            alpha = jnp.exp2(m_old - m_new)
            m[i] = m_new if extent(cols) == 2 * t else jnp.concatenate([m[i][:, 0:cols.start], m_new], axis=1)
            acc_scr[slot, i, :, cols] = (alpha * acc_scr[slot, i, :, cols]
                                         + _dot(vt_scr[slot, j, :, rows], jnp.exp2(s - m_new).astype(BF16)))
            yield

        lam = (jnp.exp(jnp.sum(lq1_ref[...] * lk1_ref[...], axis=-1, keepdims=True))
               - jnp.exp(jnp.sum(lq2_ref[...] * lk2_ref[...], axis=-1, keepdims=True)) + LAMBDA_INIT)
        for i in range(n_blk):
            on = acc_scr[slot, i, 0:LANES, :] / acc_scr[slot, i, LANES:LANES + 1, :]
            comp = lambda c: jnp.concatenate([on[:, c * h:(c + 1) * h], on[:, t + c * h:t + (c + 1) * h]], axis=1)
            ot = comp(0) - lam * comp(1)
            ot = ot * lax.rsqrt(jnp.mean(ot * ot, axis=0, keepdims=True) + SUBLN_EPS)
            g = g_ref[i * t:(i + 1) * t, :].astype(F32)
            o_ref[i * t:(i + 1) * t, :] = (ot.T * sw_ref[...] * (1.0 - LAMBDA_INIT) * _silu(g)).astype(o_ref.dtype)
            yield

    odd = jnp.bitwise_and(g_id, 1) == 1

    @pl.when(jnp.logical_not(odd))
    def _():
        _interleave((blocks(1), ATT_WEIGHTS[0]), (setup(0), ATT_WEIGHTS[1]))

    @pl.when(odd)
    def _():
        _interleave((blocks(0), ATT_WEIGHTS[0]), (setup(1), ATT_WEIGHTS[1]))


def _attention(p_arr, p_meta, vm_t, lam_vecs, subln_w, *, batch, seq):
    t = ATT_BLOCK
    n_blk = seq // t
    n_heads_total = batch * DIFF_HEADS
    lanes_blk = lambda col: col // LANES
    small = pl.BlockSpec((1, HEAD), lambda g: (0, 0))
    setup_head = lambda g: jnp.minimum(g, n_heads_total - 1)
    block_head = lambda g: jnp.maximum(g - 1, 0)

    def head_cols(col, head_of):
        def index(g):
            h = head_of(g)
            return (h // DIFF_HEADS, lanes_blk(col) + h % DIFF_HEADS)
        return pl.BlockSpec((seq, LANES), index)

    slots = lambda shape, dtype: pltpu.VMEM((2,) + shape, dtype)
    return pl.pallas_call(
        functools.partial(_attn_kernel, t=t, n_blk=n_blk),
        out_shape=jax.ShapeDtypeStruct((batch * seq, DIFF_WIDTH), BF16),
        grid=(n_heads_total + 1,),
        in_specs=[
            head_cols(COL_Q, setup_head), head_cols(COL_K, block_head), head_cols(COL_V, setup_head),
            head_cols(COL_GD, block_head),
            pl.BlockSpec((N_META, LANES), lambda g: (0, lanes_blk(COL_K) + setup_head(g) % DIFF_HEADS)),
            pl.BlockSpec((1, LANES + ONES_ROWS, N_META), lambda g: (setup_head(g) % DIFF_HEADS, 0, 0)),
            small, small, small, small,
            pl.BlockSpec((1, LANES), lambda g: (0, 0)),
        ],
        out_specs=pl.BlockSpec((seq, LANES), lambda g: (block_head(g) // DIFF_HEADS, block_head(g) % DIFF_HEADS)),
        scratch_shapes=[
            slots((n_blk, LANES + ONES_ROWS, t), BF16),
            slots((n_blk, LANES, 2 * t), BF16),
            slots((n_blk, 1, 2 * t), F32),
            slots((n_blk, LANES + ONES_ROWS, 2 * t), F32),
            pltpu.VMEM((ATT_LOOKAHEAD + 1, t, 2 * t), F32),
        ],
        compiler_params=pltpu.CompilerParams(
            dimension_semantics=("arbitrary",), vmem_limit_bytes=VMEM_LIMIT),
        name="diff_attn",
    )(p_arr, p_arr, p_arr, p_arr, p_meta, vm_t, *lam_vecs, subln_w)


def _outproj_kernel(yr_ref, yd_ref, w1_ref, w2_ref, x_ref, g_ref, o_ref):
    y = _dot(yr_ref[...], w1_ref[...]) + _dot(yd_ref[...], w2_ref[...])
    ms = jnp.mean(y * y, axis=-1, keepdims=True)
    o_ref[...] = x_ref[...] + y * lax.rsqrt(ms + NORM_EPS) * g_ref[...]


def _outproj(y_r, y_d, w, x2, g):
    m = x2.shape[0]
    tm = OUT_TM
    assert RWKV_WIDTH == DIFF_WIDTH and w.shape == (RWKV_WIDTH + DIFF_WIDTH, D_MODEL)
    return pl.pallas_call(
        _outproj_kernel,
        out_shape=jax.ShapeDtypeStruct((m, D_MODEL), F32),
        grid=(m // tm,),
        in_specs=[
            pl.BlockSpec((tm, RWKV_WIDTH), lambda i: (i, 0)),
            pl.BlockSpec((tm, DIFF_WIDTH), lambda i: (i, 0)),
            pl.BlockSpec((RWKV_WIDTH, D_MODEL), lambda i: (0, 0)),
            pl.BlockSpec((DIFF_WIDTH, D_MODEL), lambda i: (1, 0)),
            pl.BlockSpec((tm, D_MODEL), lambda i: (i, 0)),
            pl.BlockSpec((1, D_MODEL), lambda i: (0, 0)),
        ],
        out_specs=pl.BlockSpec((tm, D_MODEL), lambda i: (i, 0)),
        compiler_params=pltpu.CompilerParams(
            dimension_semantics=("arbitrary",), vmem_limit_bytes=VMEM_LIMIT),
        name="outproj",
    )(y_r, y_d, w, w, x2, g)


def _rope_tables(first_pos, n_pos):
    pos = jnp.arange(first_pos, first_pos + n_pos, dtype=F32)
    inv_freq = ROPE_THETA ** (-jnp.arange(0, HEAD, 2, dtype=F32) / HEAD)
    ang = pos[:, None] * inv_freq[None, :]
    cos = jnp.cos(ang)
    sin = jnp.sin(ang)
    cos = jnp.concatenate([cos, cos, cos, cos], axis=-1)
    sin = jnp.concatenate([-sin, sin, -sin, sin], axis=-1)
    return cos, sin


def kernel(x, meta_tokens, pre_norm_w, w_in, rwkv_mu, rwkv_w0, rwkv_w_up, rwkv_a0, rwkv_a_up, rwkv_k_k, rwkv_k_a, rwkv_r_k, rwkv_gn_w, rwkv_gn_b, diff_lam_q1, diff_lam_k1, diff_lam_q2, diff_lam_k2, diff_subln_w, w_out, post_norm_w):
    batch, seq, d = x.shape
    assert d == D_MODEL and meta_tokens.shape == (N_META, D_MODEL)
    assert seq % RWKV_CHUNK == 0 and seq % ATT_BLOCK == 0
    assert (batch * seq) % PROJ_TM == 0 and seq % PROJ_TM == 0
    layer = 0
    x2 = x.reshape(batch * seq, D_MODEL)

    w = w_in[layer]
    rkv_end = 3 * RWKV_WIDTH
    lora_end = rkv_end + 2 * LORA
    g_pre = pre_norm_w[layer].reshape(1, D_MODEL)
    cos_m, sin_m = _rope_tables(0, N_META)
    cos_x, sin_x = _rope_tables(N_META, seq)
    w_main, w_lora_in, p_meta, lora_meta = _wprep(w, meta_tokens.astype(x.dtype), g_pre, cos_m, sin_m)
    p_x, lora_x = _inproj(x2, g_pre, w_main, w_lora_in, cos_x, sin_x)

    mu = rwkv_mu[layer]
    zeros = jnp.zeros((LORA, RWKV_WIDTH), F32)
    w_lora = jnp.concatenate([
        jnp.concatenate([rwkv_w_up[layer], zeros], axis=1),
        jnp.concatenate([zeros, rwkv_a_up[layer]], axis=1)], axis=0)
    row = lambda t, n: t.reshape(1, n)
    rwkv_params = (
        row(mu[:rkv_end], rkv_end), row(mu[rkv_end:lora_end], LANES),
        row(rwkv_w0[layer], RWKV_WIDTH), row(rwkv_a0[layer], RWKV_WIDTH),
        w_lora.astype(BF16),
        row(rwkv_k_k[layer], RWKV_WIDTH), row(rwkv_k_a[layer], RWKV_WIDTH),
        row(rwkv_r_k[layer], RWKV_WIDTH), row(rwkv_gn_w[layer], RWKV_WIDTH), row(rwkv_gn_b[layer], RWKV_WIDTH),
    )
    pad = RWKV_CHUNK - N_META
    _, h_meta = _rwkv(jnp.pad(p_meta, ((pad, 0), (0, 0))), jnp.pad(lora_meta, ((pad, 0), (0, 0))),
                      jnp.zeros((SUBLANES, rkv_end), F32), jnp.zeros((SUBLANES, LANES), F32),
                      jnp.zeros((1, N_PAIRS, LANES, LANES), F32), rwkv_params, batch=1)
    y_r, _ = _rwkv(p_x, lora_x, p_meta[N_META - SUBLANES:, :rkv_end].astype(F32), lora_meta[N_META - SUBLANES:],
                   h_meta, rwkv_params, batch=batch)

    lam_vecs = tuple(t[layer].reshape(1, HEAD) for t in (diff_lam_q1, diff_lam_k1, diff_lam_q2, diff_lam_k2))
    vm_t = p_meta[:, COL_V:COL_V + DIFF_WIDTH].reshape(N_META, DIFF_HEADS, LANES).transpose(1, 2, 0)
    vm_t = jnp.concatenate([vm_t, jnp.ones((DIFF_HEADS, ONES_ROWS, N_META), BF16)], axis=1)
    y_d = _attention(p_x, p_meta, vm_t, lam_vecs, diff_subln_w[layer].reshape(1, LANES), batch=batch, seq=seq)

    out = _outproj(y_r, y_d, w_out[layer].astype(BF16), x2, post_norm_w[layer].reshape(1, D_MODEL))
    return out.reshape(batch, seq, D_MODEL)
```

```python
import functools
import math

import jax
import jax.numpy as jnp
from jax import lax
from jax.experimental import pallas as pl
from jax.experimental.pallas import tpu as pltpu

F32 = jnp.float32
BF16 = jnp.bfloat16

D_MODEL = 2048
N_META = 16
HEAD = 64
LANES = 128
SUBLANES = 8
V7X_VMEM_BYTES = 64 * 1024 * 1024
RWKV_WIDTH = 1024
N_PAIRS = RWKV_WIDTH // LANES
DIFF_WIDTH = 1024
DIFF_HEADS = DIFF_WIDTH // LANES
LORA = 64
ROPE_THETA = 10000.0
NORM_EPS = 1e-6
GN_EPS = 64e-5
SUBLN_EPS = 1e-5
LAMBDA_INIT = 0.8 - 0.6 * math.exp(-0.3 * 0)

COL_RKV = 0
COL_GR = 3 * RWKV_WIDTH
COL_Q = 4 * RWKV_WIDTH
COL_K = COL_Q + DIFF_WIDTH
COL_V = COL_K + DIFF_WIDTH
COL_GD = COL_V + DIFF_WIDTH
P_COLS = COL_GD + DIFF_WIDTH

WPREP_ROWS = 256
PROJ_TM = 1024
PROJ_TN = 1024
PROJ_KC = 512
RWKV_CHUNK = 64
RWKV_SLOTS = 3
RWKV_WEIGHTS = (4, 3, 1)
ATT_BLOCK = 512
ATT_LOOKAHEAD = 1
ATT_WEIGHTS = (3, 2)
OUT_TM = 256
VMEM_LIMIT = V7X_VMEM_BYTES * 3 // 4
MASK_VALUE = -1e30
ONES_ROWS = 16
Q_SCALE = HEAD ** -0.5 * math.log2(math.e)


def _dot(a, b):
    return jnp.dot(a, b, preferred_element_type=F32)


def _dot_nt(a, b):
    return lax.dot_general(a, b, (((1,), (1,)), ((), ())), preferred_element_type=F32)


def _dot_tn(a, b):
    return lax.dot_general(a, b, (((0,), (0,)), ((), ())), preferred_element_type=F32)


def _split2(x):
    hi = x.astype(BF16)
    lo = (x - hi.astype(F32)).astype(BF16)
    return hi, lo


def _sigmoid(x):
    return 1.0 / (1.0 + jnp.exp(-x))


def _silu(x):
    return x * _sigmoid(x)


def _interleave(*weighted_generators):
    active = [list(gw) for gw in weighted_generators]
    while active:
        for item in list(active):
            gen, weight = item
            for _ in range(weight):
                if next(gen, "done") == "done":
                    active.remove(item)
                    break


def _rotate_half(blk, cos, sin):
    lane = lax.broadcasted_iota(jnp.int32, blk.shape, 1)
    first_half = (lane % HEAD) < (HEAD // 2)
    swapped = jnp.where(first_half, pltpu.roll(blk, LANES - HEAD // 2, 1), pltpu.roll(blk, HEAD // 2, 1))
    return blk * cos + swapped * sin


def _wprep_kernel(w_ref, meta_ref, g_ref, cos_ref, sin_ref, main_ref, lora_ref, pm_ref, lm_ref,
                  acc_ref, lacc_ref, ssq_ref):
    i = pl.program_id(0)
    rkv_end = 3 * RWKV_WIDTH
    lora_end = rkv_end + 2 * LORA
    w_rkv = w_ref[:, 0:rkv_end].astype(BF16)
    w_lora = w_ref[:, rkv_end:lora_end].astype(BF16)
    w_rest = w_ref[:, lora_end:lora_end + P_COLS - rkv_end].astype(BF16)
    main_ref[:, 0:rkv_end] = w_rkv
    main_ref[:, rkv_end:P_COLS] = w_rest
    lora_ref[...] = w_lora

    @pl.when(i == 0)
    def _():
        acc_ref[...] = jnp.zeros_like(acc_ref)
        lacc_ref[...] = jnp.zeros_like(lacc_ref)
        ssq_ref[...] = jnp.zeros_like(ssq_ref)

    xm = meta_ref[0]
    hn = (xm * g_ref[0]).astype(BF16)
    ssq_ref[...] += jnp.broadcast_to(jnp.sum(xm * xm, axis=-1, keepdims=True), ssq_ref.shape)
    acc_ref[:, 0:rkv_end] += _dot(hn, w_rkv)
    acc_ref[:, rkv_end:P_COLS] += _dot(hn, w_rest)
    lacc_ref[...] += _dot(hn, w_lora)

    @pl.when(i == pl.num_programs(0) - 1)
    def _():
        rs = lax.rsqrt(ssq_ref[...] * (1.0 / D_MODEL) + NORM_EPS)
        lm_ref[...] = lacc_ref[...] * rs
        for c in range(P_COLS // LANES):
            cb = slice(c * LANES, (c + 1) * LANES)
            blk = acc_ref[:, cb]
            if COL_Q <= c * LANES < COL_K:
                blk = _rotate_half(blk, cos_ref[...], sin_ref[...]) * Q_SCALE
            elif COL_K <= c * LANES < COL_V:
                blk = _rotate_half(blk, cos_ref[...], sin_ref[...])
            pm_ref[:, cb] = (blk * rs).astype(pm_ref.dtype)


def _wprep(w, meta_tokens, g, cos, sin):
    rows, cols = w.shape
    assert cols == P_COLS + 2 * LORA and rows % WPREP_ROWS == 0
    n_steps = rows // WPREP_ROWS
    meta_k = meta_tokens.reshape(N_META, n_steps, WPREP_ROWS).transpose(1, 0, 2)
    g_k = g.reshape(n_steps, 1, WPREP_ROWS)
    const = lambda i: (0, 0)
    return pl.pallas_call(
        _wprep_kernel,
        out_shape=(jax.ShapeDtypeStruct((rows, P_COLS), BF16), jax.ShapeDtypeStruct((rows, 2 * LORA), BF16),
                   jax.ShapeDtypeStruct((N_META, P_COLS), BF16), jax.ShapeDtypeStruct((N_META, LANES), F32)),
        grid=(n_steps,),
        in_specs=[
            pl.BlockSpec((WPREP_ROWS, cols), lambda i: (i, 0)),
            pl.BlockSpec((1, N_META, WPREP_ROWS), lambda i: (i, 0, 0)),
            pl.BlockSpec((1, 1, WPREP_ROWS), lambda i: (i, 0, 0)),
            pl.BlockSpec((N_META, LANES), const),
            pl.BlockSpec((N_META, LANES), const),
        ],
        out_specs=(pl.BlockSpec((WPREP_ROWS, P_COLS), lambda i: (i, 0)),
                   pl.BlockSpec((WPREP_ROWS, 2 * LORA), lambda i: (i, 0)),
                   pl.BlockSpec((N_META, P_COLS), const),
                   pl.BlockSpec((N_META, LANES), const)),
        scratch_shapes=[pltpu.VMEM((N_META, P_COLS), F32), pltpu.VMEM((N_META, LANES), F32),
                        pltpu.VMEM((N_META, LANES), F32)],
        compiler_params=pltpu.CompilerParams(dimension_semantics=("arbitrary",), vmem_limit_bytes=VMEM_LIMIT),
        name="wprep",
    )(w, meta_k, g_k, cos, sin)


def _inproj_kernel(*refs, tn):
    n_k = D_MODEL // PROJ_KC
    x_refs = refs[:n_k]
    g_ref, w_ref, wl_ref, cos_ref, sin_ref, o_ref, ol_ref, hn_ref, rs_ref = refs[n_k:]
    j = pl.program_id(1)
    q_tile = COL_Q // tn
    k_tile = COL_K // tn
    assert 0 < q_tile < k_tile
    is_rope = jnp.logical_or(j == q_tile, j == k_tile)
    col_blocks = [slice(c * LANES, (c + 1) * LANES) for c in range(tn // LANES)]

    @pl.when(j == 0)
    def _():
        ssq = None
        acc = None
        for k in range(n_k):
            sl = slice(k * PROJ_KC, (k + 1) * PROJ_KC)
            xc = x_refs[k][...]
            part = jnp.sum(xc * xc, axis=-1, keepdims=True)
            ssq = part if ssq is None else ssq + part
            hn = (xc * g_ref[:, sl]).astype(BF16)
            hn_ref[:, sl] = hn
            d = _dot(hn, w_ref[sl, :])
            acc = d if acc is None else acc + d
        rs = jnp.broadcast_to(lax.rsqrt(ssq * (1.0 / D_MODEL) + NORM_EPS), rs_ref.shape)
        rs_ref[...] = rs
        ol_ref[...] = _dot(hn_ref[...], wl_ref[...]) * rs
        for cb in col_blocks:
            o_ref[:, cb] = (acc[:, cb] * rs).astype(o_ref.dtype)

    @pl.when(is_rope)
    def _():
        acc = _dot(hn_ref[...], w_ref[...])
        cos = cos_ref[...]
        sin = sin_ref[...]
        scale = rs_ref[...] * jnp.where(j == q_tile, Q_SCALE, 1.0).astype(F32)
        for cb in col_blocks:
            o_ref[:, cb] = (_rotate_half(acc[:, cb], cos, sin) * scale).astype(o_ref.dtype)

    @pl.when(jnp.logical_and(j > 0, jnp.logical_not(is_rope)))
    def _():
        acc = _dot(hn_ref[...], w_ref[...])
        rs = rs_ref[...]
        for cb in col_blocks:
            o_ref[:, cb] = (acc[:, cb] * rs).astype(o_ref.dtype)


def _inproj(x2, g, w_main, w_lora, cos, sin):
    m = x2.shape[0]
    tm = PROJ_TM
    tn = PROJ_TN
    n_pos_tiles = cos.shape[0] // tm
    n_row_tiles = m // tm
    n_col_tiles = P_COLS // tn
    n_k = D_MODEL // PROJ_KC
    assert n_k <= n_col_tiles

    def x_chunk(k):
        first_step = n_col_tiles - n_k + k
        return pl.BlockSpec(
            (tm, PROJ_KC), lambda i, j: (jnp.minimum(i + (j >= first_step).astype(jnp.int32), n_row_tiles - 1), k))

    return pl.pallas_call(
        functools.partial(_inproj_kernel, tn=tn),
        out_shape=(jax.ShapeDtypeStruct((m, P_COLS), BF16), jax.ShapeDtypeStruct((m, LANES), F32)),
        grid=(n_row_tiles, n_col_tiles),
        in_specs=[x_chunk(k) for k in range(n_k)] + [
            pl.BlockSpec((1, D_MODEL), lambda i, j: (0, 0)),
            pl.BlockSpec((D_MODEL, tn), lambda i, j: (0, j)),
            pl.BlockSpec((D_MODEL, LANES), lambda i, j: (0, 0)),
            pl.BlockSpec((tm, LANES), lambda i, j: (i % n_pos_tiles, 0)),
            pl.BlockSpec((tm, LANES), lambda i, j: (i % n_pos_tiles, 0)),
        ],
        out_specs=(
            pl.BlockSpec((tm, tn), lambda i, j: (i, j)),
            pl.BlockSpec((tm, LANES), lambda i, j: (i, 0)),
        ),
        scratch_shapes=[pltpu.VMEM((tm, D_MODEL), BF16), pltpu.VMEM((tm, LANES), F32)],
        compiler_params=pltpu.CompilerParams(
            dimension_semantics=("arbitrary", "arbitrary"), vmem_limit_bytes=VMEM_LIMIT),
        name="inproj",
    )(*([x2] * n_k), g, w_main, w_lora, cos, sin)


def _rwkv_kernel(rkv_ref, g_ref, lora_ref, pinit_rkv_ref, pinit_lora_ref, hinit_ref,
                 mu_rkv_ref, mu_lora_ref, w0_ref, a0_ref, wl_ref,
                 kk_ref, ka_ref, rk_ref, gnw_ref, gnb_ref,
                 y_ref, hout_ref,
                 h_scr, prev_rkv, prev_lora,
                 st_lhs, st_rhs, st_vs, st_v, st_kh, st_bh, st_gamma, st_bonus,
                 st_inv, st_arb, st_axv, *, n_chunks):
    C = RWKV_CHUNK
    S = 2 * C
    assert S == LANES
    b_id = pl.program_id(0)
    c = pl.program_id(1)
    pairs = range(N_PAIRS)
    staging = (st_lhs, st_rhs, st_vs, st_v, st_kh, st_bh, st_gamma, st_bonus, st_inv, st_arb, st_axv)

    @pl.when(c == 0)
    def _():
        prev_rkv[...] = pinit_rkv_ref[...]
        prev_lora[...] = pinit_lora_ref[...]

    @pl.when(c <= RWKV_SLOTS - 1)
    def _():
        h_scr[...] = hinit_ref[0]

    @pl.when(jnp.logical_and(b_id == 0, c == 0))
    def _():
        for ref in staging:
            for slot in range(1, RWKV_SLOTS):
                ref[slot] = jnp.zeros(ref.shape[1:], ref.dtype)

    def cols(p, base=0):
        return slice(base + p * LANES, base + (p + 1) * LANES)

    def make_stack():
        head0 = lax.broadcasted_iota(jnp.int32, (C, LANES), 1) < HEAD

        def stack(t):
            return jnp.concatenate([jnp.where(head0, t, 0.0), jnp.where(head0, 0.0, t)], axis=0).astype(BF16)
        return head0, stack

    def make_head_sum():
        li = lax.broadcasted_iota(jnp.int32, (2 * LANES, LANES), 0)
        lj = lax.broadcasted_iota(jnp.int32, (2 * LANES, LANES), 1)
        ones_bd2 = jnp.where(((li % LANES) // HEAD) == (lj // HEAD), 1.0, 0.0).astype(BF16)

        def head_sum(t):
            rows = jnp.concatenate([t[:, cols(p)] for p in pairs], axis=0)
            hi, lo = _split2(rows)
            s = _dot(jnp.concatenate([hi, lo], axis=1), ones_bd2)
            return jnp.concatenate([s[p * C:(p + 1) * C] for p in pairs], axis=1)
        return head_sum

    def prepare(slot):
        head0, stack = make_stack()
        head_sum = make_head_sum()
        row = lax.broadcasted_iota(jnp.int32, (C, 1), 0)

        def token_shift(x, prev_tile, mu):
            xp = jnp.where(row == 0, prev_tile[SUBLANES - 1:SUBLANES, :], pltpu.roll(x, 1, 0))
            return x + (xp - x) * mu

        xl = lora_ref[...]
        ul = token_shift(xl, prev_lora[...], mu_lora_ref[...])
        prev_lora[...] = xl[C - SUBLANES:C]
        lo_out = _dot(jnp.where(head0, jnp.tanh(ul), ul).astype(BF16), wl_ref[...])
        yield

        def shifted(base):
            sl = slice(base, base + RWKV_WIDTH)
            xs = rkv_ref[:, sl].astype(F32)
            out = token_shift(xs, prev_rkv[:, sl], mu_rkv_ref[:, sl])
            prev_rkv[:, sl] = xs[C - SUBLANES:C]
            return out

        r = shifted(0)
        yield
        k = shifted(RWKV_WIDTH)
        yield
        v = shifted(2 * RWKV_WIDTH)
        yield

        logdec = -math.exp(-0.5) * _sigmoid(w0_ref[...] + lo_out[:, 0:RWKV_WIDTH])
        yield
        a_lr = _sigmoid(a0_ref[...] + lo_out[:, RWKV_WIDTH:2 * RWKV_WIDTH])
        yield

        ti = lax.broadcasted_iota(jnp.int32, (C, C), 0)
        si = lax.broadcasted_iota(jnp.int32, (C, C), 1)
        tri = jnp.where(ti >= si, 1.0, 0.0).astype(BF16)
        ld_hi, ld_lo = _split2(logdec)
        cum = _dot(tri, ld_hi) + _dot(tri, ld_lo)
        cum_last = cum[C - 1:C, :]
        yield
        e_excl = jnp.exp(cum - logdec)
        e_incl = jnp.exp(cum)
        yield
        e_neg = jnp.exp(-cum)
        e_hat = jnp.exp(cum_last - cum)
        yield

        kk = k * kk_ref[...]
        kk = kk * lax.rsqrt(jnp.maximum(head_sum(kk * kk), 1e-24))
        yield
        k2 = k * (1.0 + (a_lr - 1.0) * ka_ref[...])
        b = kk * a_lr
        yield
        xa_all = -kk * e_excl
        xr_all = r * e_incl
        yield
        yb_all = b * e_neg
        yk_all = k2 * e_neg
        yield

        st_gamma[slot] = jnp.exp(cum_last)
        st_bonus[slot] = head_sum(r * k2 * rk_ref[...]) * v
        yield
        st_v[slot] = v.astype(BF16)
        st_kh[slot] = (k2 * e_hat).astype(BF16)
        st_bh[slot] = (b * e_hat).astype(BF16)
        yield
        for p in pairs:
            sl = cols(p)
            st_lhs[slot, p] = jnp.concatenate([xa_all[:, sl], xr_all[:, sl]], axis=0).astype(BF16)
            st_rhs[slot, p] = jnp.concatenate([stack(yb_all[:, sl]), stack(yk_all[:, sl])], axis=0)
            st_vs[slot, p] = stack(v[:, sl])
            yield

    def stage(fn):
        out = []
        for p in pairs:
            out.append(fn(p))
            yield
        return out

    def invert(slot):
        _, stack = make_stack()
        ri = lax.broadcasted_iota(jnp.int32, (S, 2 * S), 0)
        ci = lax.broadcasted_iota(jnp.int32, (S, 2 * S), 1)
        a_mask = (ri % C) >= (ci % C) + jnp.where(ri < C, 1, 0)
        ei = lax.broadcasted_iota(jnp.int32, (C, S), 0)
        ej = lax.broadcasted_iota(jnp.int32, (C, S), 1)
        eye = jnp.where(ei == ej % C, 1.0, 0.0).astype(F32)

        a_all = yield from stage(lambda p: jnp.where(a_mask, _dot_nt(st_lhs[slot, p], st_rhs[slot, p]), 0.0))
        a_ab = [a_all[p][0:C, 0:S] for p in pairs]

        def values(p):
            st_arb[slot, p] = a_all[p][C:S, 0:S].astype(BF16)
            st_axv[slot, p] = _dot(a_all[p][:, S:2 * S].astype(BF16), st_vs[slot, p])

        yield from stage(values)

        inv = [eye + a_ab[p] for p in pairs]
        apow = yield from stage(lambda p: _dot(a_ab[p].astype(BF16), stack(a_ab[p])))
        n = 2
        while 2 * n < C:
            res = yield from stage(
                lambda p: _dot(jnp.concatenate([inv[p], apow[p]], axis=0).astype(BF16), stack(apow[p])))
            inv = [inv[p] + res[p][0:C] for p in pairs]
            apow = [res[p][C:S] for p in pairs]
            n *= 2

        def last(p):
            st_inv[slot, p] = (inv[p] + _dot(inv[p].astype(BF16), stack(apow[p]))).astype(BF16)

        yield from stage(last)

    def state(slot):
        _, stack = make_stack()
        head_sum = make_head_sum()
        bi = lax.broadcasted_iota(jnp.int32, (LANES, LANES), 0)
        bj = lax.broadcasted_iota(jnp.int32, (LANES, LANES), 1)
        same_head = (bi // HEAD) == (bj // HEAD)

        s_old = [h_scr[p] for p in pairs]
        x_s = yield from stage(lambda p: _dot_nt(st_lhs[slot, p], s_old[p].astype(BF16)))
        us = yield from stage(lambda p: _dot(st_inv[slot, p], stack(x_s[p][0:C] + st_axv[slot, p, 0:C])))
        ys = yield from stage(
            lambda p: x_s[p][C:S] + st_axv[slot, p, C:S] + _dot(st_arb[slot, p], stack(us[p])))

        def update(p):
            sl = cols(p)
            vu = jnp.concatenate([st_v[slot, :, sl], us[p].astype(BF16)], axis=0)
            khb = jnp.concatenate([st_kh[slot, :, sl], st_bh[slot, :, sl]], axis=0)
            h_scr[p] = s_old[p] * st_gamma[slot, :, sl] + jnp.where(same_head, _dot_tn(vu, khb), 0.0)

        yield from stage(update)

        y = jnp.concatenate(ys, axis=1)
        yc = y - head_sum(y) * (1.0 / HEAD)
        yield
        var = head_sum(yc * yc) * (1.0 / HEAD)
        y = yc * lax.rsqrt(var + GN_EPS) * gnw_ref[...] + gnb_ref[...] + st_bonus[slot]
        yield
        y_ref[...] = (y * _silu(g_ref[...].astype(F32))).astype(y_ref.dtype)

    phase = lax.rem(c, RWKV_SLOTS)
    for r in range(RWKV_SLOTS):
        @pl.when(phase == r)
        def _():
            _interleave((invert((r - 1) % RWKV_SLOTS), RWKV_WEIGHTS[0]),
                        (state((r - 2) % RWKV_SLOTS), RWKV_WEIGHTS[1]),
                        (prepare(r), RWKV_WEIGHTS[2]))

    @pl.when(c == n_chunks + RWKV_SLOTS - 2)
    def _():
        hout_ref[0] = h_scr[...]


def _rwkv(p_arr, lora_arr, pinit_rkv, pinit_lora, hinit, params, *, batch):
    C = RWKV_CHUNK
    rows = p_arr.shape[0]
    n_chunks = rows // (batch * C)
    const2 = lambda b, c: (0, 0)
    vec = lambda width: pl.BlockSpec((1, width), const2)
    prep_chunk = lambda b, c: b * n_chunks + jnp.minimum(c, n_chunks - 1)
    recur_chunk = lambda b, c: b * n_chunks + jnp.maximum(c - (RWKV_SLOTS - 1), 0)
    slots = lambda shape, dtype: pltpu.VMEM((RWKV_SLOTS,) + shape, dtype)
    return pl.pallas_call(
        functools.partial(_rwkv_kernel, n_chunks=n_chunks),
        out_shape=(jax.ShapeDtypeStruct((rows, RWKV_WIDTH), BF16),
                   jax.ShapeDtypeStruct((batch, N_PAIRS, LANES, LANES), F32)),
        grid=(batch, n_chunks + RWKV_SLOTS - 1),
        in_specs=[
            pl.BlockSpec((C, 3 * RWKV_WIDTH), lambda b, c: (prep_chunk(b, c), COL_RKV // (3 * RWKV_WIDTH))),
            pl.BlockSpec((C, RWKV_WIDTH), lambda b, c: (recur_chunk(b, c), COL_GR // RWKV_WIDTH)),
            pl.BlockSpec((C, LANES), lambda b, c: (prep_chunk(b, c), 0)),
            pl.BlockSpec((SUBLANES, 3 * RWKV_WIDTH), const2),
            pl.BlockSpec((SUBLANES, LANES), const2),
            pl.BlockSpec((1, N_PAIRS, LANES, LANES), lambda b, c: (0, 0, 0, 0)),
            vec(3 * RWKV_WIDTH), vec(LANES), vec(RWKV_WIDTH), vec(RWKV_WIDTH),
            pl.BlockSpec((LANES, 2 * RWKV_WIDTH), const2),
            vec(RWKV_WIDTH), vec(RWKV_WIDTH), vec(RWKV_WIDTH), vec(RWKV_WIDTH), vec(RWKV_WIDTH),
        ],
        out_specs=(
            pl.BlockSpec((C, RWKV_WIDTH), lambda b, c: (recur_chunk(b, c), 0)),
            pl.BlockSpec((1, N_PAIRS, LANES, LANES), lambda b, c: (b, 0, 0, 0)),
        ),
        scratch_shapes=[
            pltpu.VMEM((N_PAIRS, LANES, LANES), F32),
            pltpu.VMEM((SUBLANES, 3 * RWKV_WIDTH), F32),
            pltpu.VMEM((SUBLANES, LANES), F32),
            slots((N_PAIRS, 2 * C, LANES), BF16),
            slots((N_PAIRS, 4 * C, LANES), BF16),
            slots((N_PAIRS, 2 * C, LANES), BF16),
            slots((C, RWKV_WIDTH), BF16),
            slots((C, RWKV_WIDTH), BF16),
            slots((C, RWKV_WIDTH), BF16),
            slots((1, RWKV_WIDTH), F32),
            slots((C, RWKV_WIDTH), F32),
            slots((N_PAIRS, C, 2 * C), BF16),
            slots((N_PAIRS, C, 2 * C), BF16),
            slots((N_PAIRS, 2 * C, LANES), F32),
        ],
        compiler_params=pltpu.CompilerParams(
            dimension_semantics=("arbitrary", "arbitrary"), vmem_limit_bytes=VMEM_LIMIT),
        name="rwkv7_chunk",
    )(p_arr, p_arr, lora_arr, pinit_rkv, pinit_lora, hinit, *params)


def _attn_kernel(q_ref, k_ref, v_ref, g_ref, km_ref, vmt_ref, lq1_ref, lk1_ref, lq2_ref, lk2_ref, sw_ref,
                 o_ref, vt_scr, qst_scr, m_scr, acc_scr, s_scr, *, t, n_blk):
    g_id = pl.program_id(0)

    @pl.when(g_id == 0)
    def _():
        for ref in (vt_scr, qst_scr, m_scr, acc_scr):
            ref[1] = jnp.ones(ref.shape[1:], ref.dtype)

    def setup(slot):
        comp0 = lax.broadcasted_iota(jnp.int32, (LANES, t // 2), 0) < HEAD
        for c in range(n_blk):
            vt_scr[slot, c, 0:LANES, :] = v_ref[c * t:(c + 1) * t, :].astype(F32).T.astype(BF16)
            vt_scr[slot, c, LANES:LANES + ONES_ROWS, :] = jnp.ones((ONES_ROWS, t), BF16)
            yield
        for i in range(n_blk):
            qt = q_ref[i * t:(i + 1) * t, :].astype(F32).T
            parts = []
            for half in (qt[:, 0:t // 2], qt[:, t // 2:t]):
                parts += [jnp.where(comp0, half, 0.0), jnp.where(comp0, 0.0, half)]
            qst = jnp.concatenate(parts, axis=1).astype(BF16)
            qst_scr[slot, i] = qst
            yield
            s = _dot(km_ref[...], qst)
            m = jnp.max(s, axis=0, keepdims=True)
            m_scr[slot, i] = m
            acc_scr[slot, i] = _dot(vmt_ref[0], jnp.exp2(s - m).astype(BF16))
            yield

    def blocks(slot):
        h = t // 2
        full_rows, lo_rows, hi_rows = slice(0, t), slice(0, h), slice(h, t)
        all_cols, hi_cols = slice(0, 2 * t), slice(t, 2 * t)
        units = []
        for i, j in sorted(((i, j) for i in range(n_blk) for j in range(i + 1)), key=lambda ij: (ij[1], ij[0])):
            if i == j:
                units += [(i, j, lo_rows, all_cols, True), (i, j, hi_rows, hi_cols, True)]
            else:
                units.append((i, j, full_rows, all_cols, False))
        n_slots = s_scr.shape[0]
        m = [m_scr[slot, i] for i in range(n_blk)]

        def extent(sl):
            return sl.stop - sl.start

        def issue(n):
            i, j, rows, cols, _ = units[n]
            keys = k_ref[j * t + rows.start:j * t + rows.stop, :]
            s_scr[n % n_slots, 0:extent(rows), 0:extent(cols)] = _dot(keys, qst_scr[slot, i, :, cols])

        def visible(n_cols):
            kpos = lax.broadcasted_iota(jnp.int32, (h, n_cols), 0)
            col = lax.broadcasted_iota(jnp.int32, (h, n_cols), 1)
            return kpos <= col % h + jnp.where(col >= t, h, 0)

        visible = {n_cols: visible(n_cols) for n_cols in (t, 2 * t)}
        for n in range(min(ATT_LOOKAHEAD, len(units))):
            issue(n)
        for n, (i, j, rows, cols, masked) in enumerate(units):
            if n + ATT_LOOKAHEAD < len(units):
                issue(n + ATT_LOOKAHEAD)
            s = s_scr[n % n_slots, 0:extent(rows), 0:extent(cols)]
            if masked:
                s = jnp.where(visible[extent(cols)], s, MASK_VALUE)
            m_old = m[i][:, cols]
            m_new = jnp.maximum(m_old, jnp.max(s, axis=0, keepdims=True))
            alpha = jnp.exp2(m_old - m_new)
            m[i] = m_new if extent(cols) == 2 * t else jnp.concatenate([m[i][:, 0:cols.start], m_new], axis=1)
            acc_scr[slot, i, :, cols] = (alpha * acc_scr[slot, i, :, cols]
                                         + _dot(vt_scr[slot, j, :, rows], jnp.exp2(s - m_new).astype(BF16)))
            yield

        lam = (jnp.exp(jnp.sum(lq1_ref[...] * lk1_ref[...], axis=-1, keepdims=True))
               - jnp.exp(jnp.sum(lq2_ref[...] * lk2_ref[...], axis=-1, keepdims=True)) + LAMBDA_INIT)
        for i in range(n_blk):
            on = acc_scr[slot, i, 0:LANES, :] / acc_scr[slot, i, LANES:LANES + 1, :]
            comp = lambda c: jnp.concatenate([on[:, c * h:(c + 1) * h], on[:, t + c * h:t + (c + 1) * h]], axis=1)
            ot = comp(0) - lam * comp(1)
            ot = ot * lax.rsqrt(jnp.mean(ot * ot, axis=0, keepdims=True) + SUBLN_EPS)
            g = g_ref[i * t:(i + 1) * t, :].astype(F32)
            o_ref[i * t:(i + 1) * t, :] = (ot.T * sw_ref[...] * (1.0 - LAMBDA_INIT) * _silu(g)).astype(o_ref.dtype)
            yield

    odd = jnp.bitwise_and(g_id, 1) == 1

    @pl.when(jnp.logical_not(odd))
    def _():
        _interleave((blocks(1), ATT_WEIGHTS[0]), (setup(0), ATT_WEIGHTS[1]))

    @pl.when(odd)
    def _():
        _interleave((blocks(0), ATT_WEIGHTS[0]), (setup(1), ATT_WEIGHTS[1]))


def _attention(p_arr, p_meta, vm_t, lam_vecs, subln_w, *, batch, seq):
    t = ATT_BLOCK
    n_blk = seq // t
    n_heads_total = batch * DIFF_HEADS
    lanes_blk = lambda col: col // LANES
    small = pl.BlockSpec((1, HEAD), lambda g: (0, 0))
    setup_head = lambda g: jnp.minimum(g, n_heads_total - 1)
    block_head = lambda g: jnp.maximum(g - 1, 0)

    def head_cols(col, head_of):
        def index(g):
            h = head_of(g)
            return (h // DIFF_HEADS, lanes_blk(col) + h % DIFF_HEADS)
        return pl.BlockSpec((seq, LANES), index)

    slots = lambda shape, dtype: pltpu.VMEM((2,) + shape, dtype)
    return pl.pallas_call(
        functools.partial(_attn_kernel, t=t, n_blk=n_blk),
        out_shape=jax.ShapeDtypeStruct((batch * seq, DIFF_WIDTH), BF16),
        grid=(n_heads_total + 1,),
        in_specs=[
            head_cols(COL_Q, setup_head), head_cols(COL_K, block_head), head_cols(COL_V, setup_head),
            head_cols(COL_GD, block_head),
            pl.BlockSpec((N_META, LANES), lambda g: (0, lanes_blk(COL_K) + setup_head(g) % DIFF_HEADS)),
            pl.BlockSpec((1, LANES + ONES_ROWS, N_META), lambda g: (setup_head(g) % DIFF_HEADS, 0, 0)),
            small, small, small, small,
            pl.BlockSpec((1, LANES), lambda g: (0, 0)),
        ],
        out_specs=pl.BlockSpec((seq, LANES), lambda g: (block_head(g) // DIFF_HEADS, block_head(g) % DIFF_HEADS)),
        scratch_shapes=[
            slots((n_blk, LANES + ONES_ROWS, t), BF16),
            slots((n_blk, LANES, 2 * t), BF16),
            slots((n_blk, 1, 2 * t), F32),
            slots((n_blk, LANES + ONES_ROWS, 2 * t), F32),
            pltpu.VMEM((ATT_LOOKAHEAD + 1, t, 2 * t), F32),
        ],
        compiler_params=pltpu.CompilerParams(
            dimension_semantics=("arbitrary",), vmem_limit_bytes=VMEM_LIMIT),
        name="diff_attn",
    )(p_arr, p_arr, p_arr, p_arr, p_meta, vm_t, *lam_vecs, subln_w)


def _outproj_kernel(yr_ref, yd_ref, w1_ref, w2_ref, x_ref, g_ref, o_ref):
    y = _dot(yr_ref[...], w1_ref[...]) + _dot(yd_ref[...], w2_ref[...])
    ms = jnp.mean(y * y, axis=-1, keepdims=True)
    o_ref[...] = x_ref[...] + y * lax.rsqrt(ms + NORM_EPS) * g_ref[...]


def _outproj(y_r, y_d, w, x2, g):
    m = x2.shape[0]
    tm = OUT_TM
    assert RWKV_WIDTH == DIFF_WIDTH and w.shape == (RWKV_WIDTH + DIFF_WIDTH, D_MODEL)
    return pl.pallas_call(
        _outproj_kernel,
        out_shape=jax.ShapeDtypeStruct((m, D_MODEL), F32),
        grid=(m // tm,),
        in_specs=[
            pl.BlockSpec((tm, RWKV_WIDTH), lambda i: (i, 0)),
            pl.BlockSpec((tm, DIFF_WIDTH), lambda i: (i, 0)),
            pl.BlockSpec((RWKV_WIDTH, D_MODEL), lambda i: (0, 0)),
            pl.BlockSpec((DIFF_WIDTH, D_MODEL), lambda i: (1, 0)),
            pl.BlockSpec((tm, D_MODEL), lambda i: (i, 0)),
            pl.BlockSpec((1, D_MODEL), lambda i: (0, 0)),
        ],
        out_specs=pl.BlockSpec((tm, D_MODEL), lambda i: (i, 0)),
        compiler_params=pltpu.CompilerParams(
            dimension_semantics=("arbitrary",), vmem_limit_bytes=VMEM_LIMIT),
        name="outproj",
    )(y_r, y_d, w, w, x2, g)


def _rope_tables(first_pos, n_pos):
    pos = jnp.arange(first_pos, first_pos + n_pos, dtype=F32)
    inv_freq = ROPE_THETA ** (-jnp.arange(0, HEAD, 2, dtype=F32) / HEAD)
    ang = pos[:, None] * inv_freq[None, :]
    cos = jnp.cos(ang)
    sin = jnp.sin(ang)
    cos = jnp.concatenate([cos, cos, cos, cos], axis=-1)
    sin = jnp.concatenate([-sin, sin, -sin, sin], axis=-1)
    return cos, sin


def kernel(x, meta_tokens, pre_norm_w, w_in, rwkv_mu, rwkv_w0, rwkv_w_up, rwkv_a0, rwkv_a_up, rwkv_k_k, rwkv_k_a, rwkv_r_k, rwkv_gn_w, rwkv_gn_b, diff_lam_q1, diff_lam_k1, diff_lam_q2, diff_lam_k2, diff_subln_w, w_out, post_norm_w):
    batch, seq, d = x.shape
    assert d == D_MODEL and meta_tokens.shape == (N_META, D_MODEL)
    assert seq % RWKV_CHUNK == 0 and seq % ATT_BLOCK == 0
    assert (batch * seq) % PROJ_TM == 0 and seq % PROJ_TM == 0
    layer = 0
    x2 = x.reshape(batch * seq, D_MODEL)

    w = w_in[layer]
    rkv_end = 3 * RWKV_WIDTH
    lora_end = rkv_end + 2 * LORA
    g_pre = pre_norm_w[layer].reshape(1, D_MODEL)
    cos_m, sin_m = _rope_tables(0, N_META)
    cos_x, sin_x = _rope_tables(N_META, seq)
    w_main, w_lora_in, p_meta, lora_meta = _wprep(w, meta_tokens.astype(x.dtype), g_pre, cos_m, sin_m)
    p_x, lora_x = _inproj(x2, g_pre, w_main, w_lora_in, cos_x, sin_x)

    mu = rwkv_mu[layer]
    zeros = jnp.zeros((LORA, RWKV_WIDTH), F32)
    w_lora = jnp.concatenate([
        jnp.concatenate([rwkv_w_up[layer], zeros], axis=1),
        jnp.concatenate([zeros, rwkv_a_up[layer]], axis=1)], axis=0)
    row = lambda t, n: t.reshape(1, n)
    rwkv_params = (
        row(mu[:rkv_end], rkv_end), row(mu[rkv_end:lora_end], LANES),
        row(rwkv_w0[layer], RWKV_WIDTH), row(rwkv_a0[layer], RWKV_WIDTH),
        w_lora.astype(BF16),
        row(rwkv_k_k[layer], RWKV_WIDTH), row(rwkv_k_a[layer], RWKV_WIDTH),
        row(rwkv_r_k[layer], RWKV_WIDTH), row(rwkv_gn_w[layer], RWKV_WIDTH), row(rwkv_gn_b[layer], RWKV_WIDTH),
    )
    pad = RWKV_CHUNK - N_META
    _, h_meta = _rwkv(jnp.pad(p_meta, ((pad, 0), (0, 0))), jnp.pad(lora_meta, ((pad, 0), (0, 0))),
                      jnp.zeros((SUBLANES, rkv_end), F32), jnp.zeros((SUBLANES, LANES), F32),
                      jnp.zeros((1, N_PAIRS, LANES, LANES), F32), rwkv_params, batch=1)
    y_r, _ = _rwkv(p_x, lora_x, p_meta[N_META - SUBLANES:, :rkv_end].astype(F32), lora_meta[N_META - SUBLANES:],
                   h_meta, rwkv_params, batch=batch)

    lam_vecs = tuple(t[layer].reshape(1, HEAD) for t in (diff_lam_q1, diff_lam_k1, diff_lam_q2, diff_lam_k2))
    vm_t = p_meta[:, COL_V:COL_V + DIFF_WIDTH].reshape(N_META, DIFF_HEADS, LANES).transpose(1, 2, 0)
    vm_t = jnp.concatenate([vm_t, jnp.ones((DIFF_HEADS, ONES_ROWS, N_META), BF16)], axis=1)
    y_d = _attention(p_x, p_meta, vm_t, lam_vecs, diff_subln_w[layer].reshape(1, LANES), batch=batch, seq=seq)

    out = _outproj(y_r, y_d, w_out[layer].astype(BF16), x2, post_norm_w[layer].reshape(1, D_MODEL))
    return out.reshape(batch, seq, D_MODEL)
```

```python
import functools
import math

import jax
import jax.numpy as jnp
from jax import lax
from jax.experimental import pallas as pl
from jax.experimental.pallas import tpu as pltpu

F32 = jnp.float32
BF16 = jnp.bfloat16

D_MODEL = 2048
N_META = 16
HEAD = 64
LANES = 128
SUBLANES = 8
V7X_VMEM_BYTES = 64 * 1024 * 1024
RWKV_WIDTH = 1024
N_PAIRS = RWKV_WIDTH // LANES
DIFF_WIDTH = 1024
DIFF_HEADS = DIFF_WIDTH // LANES
LORA = 64
ROPE_THETA = 10000.0
NORM_EPS = 1e-6
GN_EPS = 64e-5
SUBLN_EPS = 1e-5
LAMBDA_INIT = 0.8 - 0.6 * math.exp(-0.3 * 0)

COL_RKV = 0
COL_GR = 3 * RWKV_WIDTH
COL_Q = 4 * RWKV_WIDTH
COL_K = COL_Q + DIFF_WIDTH
COL_V = COL_K + DIFF_WIDTH
COL_GD = COL_V + DIFF_WIDTH
P_COLS = COL_GD + DIFF_WIDTH

WPREP_ROWS = 256
PROJ_TM = 1024
PROJ_TN = 1024
PROJ_KC = 512
RWKV_CHUNK = 64
RWKV_SLOTS = 3
RWKV_WEIGHTS = (4, 3, 1)
ATT_BLOCK = 512
ATT_LOOKAHEAD = 1
ATT_WEIGHTS = (7, 3)
OUT_TM = 256
VMEM_LIMIT = V7X_VMEM_BYTES * 3 // 4
MASK_VALUE = -1e30
ONES_ROWS = 16
Q_SCALE = HEAD ** -0.5 * math.log2(math.e)


def _dot(a, b):
    return jnp.dot(a, b, preferred_element_type=F32)


def _dot_nt(a, b):
    return lax.dot_general(a, b, (((1,), (1,)), ((), ())), preferred_element_type=F32)


def _dot_tn(a, b):
    return lax.dot_general(a, b, (((0,), (0,)), ((), ())), preferred_element_type=F32)


def _split2(x):
    hi = x.astype(BF16)
    lo = (x - hi.astype(F32)).astype(BF16)
    return hi, lo


def _sigmoid(x):
    return 1.0 / (1.0 + jnp.exp(-x))


def _silu(x):
    return x * _sigmoid(x)


def _interleave(*weighted_generators):
    active = [list(gw) for gw in weighted_generators]
    while active:
        for item in list(active):
            gen, weight = item
            for _ in range(weight):
                if next(gen, "done") == "done":
                    active.remove(item)
                    break


def _rotate_half(blk, cos, sin):
    lane = lax.broadcasted_iota(jnp.int32, blk.shape, 1)
    first_half = (lane % HEAD) < (HEAD // 2)
    swapped = jnp.where(first_half, pltpu.roll(blk, LANES - HEAD // 2, 1), pltpu.roll(blk, HEAD // 2, 1))
    return blk * cos + swapped * sin


def _wprep_kernel(w_ref, meta_ref, g_ref, cos_ref, sin_ref, main_ref, lora_ref, pm_ref, lm_ref,
                  acc_ref, lacc_ref, ssq_ref):
    i = pl.program_id(0)
    rkv_end = 3 * RWKV_WIDTH
    lora_end = rkv_end + 2 * LORA
    w_rkv = w_ref[:, 0:rkv_end].astype(BF16)
    w_lora = w_ref[:, rkv_end:lora_end].astype(BF16)
    w_rest = w_ref[:, lora_end:lora_end + P_COLS - rkv_end].astype(BF16)
    main_ref[:, 0:rkv_end] = w_rkv
    main_ref[:, rkv_end:P_COLS] = w_rest
    lora_ref[...] = w_lora

    @pl.when(i == 0)
    def _():
        acc_ref[...] = jnp.zeros_like(acc_ref)
        lacc_ref[...] = jnp.zeros_like(lacc_ref)
        ssq_ref[...] = jnp.zeros_like(ssq_ref)

    xm = meta_ref[0]
    hn = (xm * g_ref[0]).astype(BF16)
    ssq_ref[...] += jnp.broadcast_to(jnp.sum(xm * xm, axis=-1, keepdims=True), ssq_ref.shape)
    acc_ref[:, 0:rkv_end] += _dot(hn, w_rkv)
    acc_ref[:, rkv_end:P_COLS] += _dot(hn, w_rest)
    lacc_ref[...] += _dot(hn, w_lora)

    @pl.when(i == pl.num_programs(0) - 1)
    def _():
        rs = lax.rsqrt(ssq_ref[...] * (1.0 / D_MODEL) + NORM_EPS)
        lm_ref[...] = lacc_ref[...] * rs
        for c in range(P_COLS // LANES):
            cb = slice(c * LANES, (c + 1) * LANES)
            blk = acc_ref[:, cb]
            if COL_Q <= c * LANES < COL_K:
                blk = _rotate_half(blk, cos_ref[...], sin_ref[...]) * Q_SCALE
            elif COL_K <= c * LANES < COL_V:
                blk = _rotate_half(blk, cos_ref[...], sin_ref[...])
            pm_ref[:, cb] = (blk * rs).astype(pm_ref.dtype)


def _wprep(w, meta_tokens, g, cos, sin):
    rows, cols = w.shape
    assert cols == P_COLS + 2 * LORA and rows % WPREP_ROWS == 0
    n_steps = rows // WPREP_ROWS
    meta_k = meta_tokens.reshape(N_META, n_steps, WPREP_ROWS).transpose(1, 0, 2)
    g_k = g.reshape(n_steps, 1, WPREP_ROWS)
    const = lambda i: (0, 0)
    return pl.pallas_call(
        _wprep_kernel,
        out_shape=(jax.ShapeDtypeStruct((rows, P_COLS), BF16), jax.ShapeDtypeStruct((rows, 2 * LORA), BF16),
                   jax.ShapeDtypeStruct((N_META, P_COLS), BF16), jax.ShapeDtypeStruct((N_META, LANES), F32)),
        grid=(n_steps,),
        in_specs=[
            pl.BlockSpec((WPREP_ROWS, cols), lambda i: (i, 0)),
            pl.BlockSpec((1, N_META, WPREP_ROWS), lambda i: (i, 0, 0)),
            pl.BlockSpec((1, 1, WPREP_ROWS), lambda i: (i, 0, 0)),
            pl.BlockSpec((N_META, LANES), const),
            pl.BlockSpec((N_META, LANES), const),
        ],
        out_specs=(pl.BlockSpec((WPREP_ROWS, P_COLS), lambda i: (i, 0)),
                   pl.BlockSpec((WPREP_ROWS, 2 * LORA), lambda i: (i, 0)),
                   pl.BlockSpec((N_META, P_COLS), const),
                   pl.BlockSpec((N_META, LANES), const)),
        scratch_shapes=[pltpu.VMEM((N_META, P_COLS), F32), pltpu.VMEM((N_META, LANES), F32),
                        pltpu.VMEM((N_META, LANES), F32)],
        compiler_params=pltpu.CompilerParams(dimension_semantics=("arbitrary",), vmem_limit_bytes=VMEM_LIMIT),
        name="wprep",
    )(w, meta_k, g_k, cos, sin)


def _inproj_kernel(*refs, tn):
    n_k = D_MODEL // PROJ_KC
    x_refs = refs[:n_k]
    g_ref, w_ref, wl_ref, cos_ref, sin_ref, o_ref, ol_ref, hn_ref, rs_ref = refs[n_k:]
    j = pl.program_id(1)
    q_tile = COL_Q // tn
    k_tile = COL_K // tn
    assert 0 < q_tile < k_tile
    is_rope = jnp.logical_or(j == q_tile, j == k_tile)
    col_blocks = [slice(c * LANES, (c + 1) * LANES) for c in range(tn // LANES)]

    @pl.when(j == 0)
    def _():
        ssq = None
        acc = None
        for k in range(n_k):
            sl = slice(k * PROJ_KC, (k + 1) * PROJ_KC)
            xc = x_refs[k][...]
            part = jnp.sum(xc * xc, axis=-1, keepdims=True)
            ssq = part if ssq is None else ssq + part
            hn = (xc * g_ref[:, sl]).astype(BF16)
            hn_ref[:, sl] = hn
            d = _dot(hn, w_ref[sl, :])
            acc = d if acc is None else acc + d
        rs = jnp.broadcast_to(lax.rsqrt(ssq * (1.0 / D_MODEL) + NORM_EPS), rs_ref.shape)
        rs_ref[...] = rs
        ol_ref[...] = _dot(hn_ref[...], wl_ref[...]) * rs
        for cb in col_blocks:
            o_ref[:, cb] = (acc[:, cb] * rs).astype(o_ref.dtype)

    @pl.when(is_rope)
    def _():
        acc = _dot(hn_ref[...], w_ref[...])
        cos = cos_ref[...]
        sin = sin_ref[...]
        scale = rs_ref[...] * jnp.where(j == q_tile, Q_SCALE, 1.0).astype(F32)
        for cb in col_blocks:
            o_ref[:, cb] = (_rotate_half(acc[:, cb], cos, sin) * scale).astype(o_ref.dtype)

    @pl.when(jnp.logical_and(j > 0, jnp.logical_not(is_rope)))
    def _():
        acc = _dot(hn_ref[...], w_ref[...])
        rs = rs_ref[...]
        for cb in col_blocks:
            o_ref[:, cb] = (acc[:, cb] * rs).astype(o_ref.dtype)


def _inproj(x2, g, w_main, w_lora, cos, sin):
    m = x2.shape[0]
    tm = PROJ_TM
    tn = PROJ_TN
    n_pos_tiles = cos.shape[0] // tm
    n_row_tiles = m // tm
    n_col_tiles = P_COLS // tn
    n_k = D_MODEL // PROJ_KC
    assert n_k <= n_col_tiles

    def x_chunk(k):
        first_step = n_col_tiles - n_k + k
        return pl.BlockSpec(
            (tm, PROJ_KC), lambda i, j: (jnp.minimum(i + (j >= first_step).astype(jnp.int32), n_row_tiles - 1), k))

    return pl.pallas_call(
        functools.partial(_inproj_kernel, tn=tn),
        out_shape=(jax.ShapeDtypeStruct((m, P_COLS), BF16), jax.ShapeDtypeStruct((m, LANES), F32)),
        grid=(n_row_tiles, n_col_tiles),
        in_specs=[x_chunk(k) for k in range(n_k)] + [
            pl.BlockSpec((1, D_MODEL), lambda i, j: (0, 0)),
            pl.BlockSpec((D_MODEL, tn), lambda i, j: (0, j)),
            pl.BlockSpec((D_MODEL, LANES), lambda i, j: (0, 0)),
            pl.BlockSpec((tm, LANES), lambda i, j: (i % n_pos_tiles, 0)),
            pl.BlockSpec((tm, LANES), lambda i, j: (i % n_pos_tiles, 0)),
        ],
        out_specs=(
            pl.BlockSpec((tm, tn), lambda i, j: (i, j)),
            pl.BlockSpec((tm, LANES), lambda i, j: (i, 0)),
        ),
        scratch_shapes=[pltpu.VMEM((tm, D_MODEL), BF16), pltpu.VMEM((tm, LANES), F32)],
        compiler_params=pltpu.CompilerParams(
            dimension_semantics=("arbitrary", "arbitrary"), vmem_limit_bytes=VMEM_LIMIT),
        name="inproj",
    )(*([x2] * n_k), g, w_main, w_lora, cos, sin)


def _rwkv_kernel(rkv_ref, g_ref, lora_ref, pinit_rkv_ref, pinit_lora_ref, hinit_ref,
                 mu_rkv_ref, mu_lora_ref, w0_ref, a0_ref, wl_ref,
                 kk_ref, ka_ref, rk_ref, gnw_ref, gnb_ref,
                 y_ref, hout_ref,
                 h_scr, prev_rkv, prev_lora,
                 st_lhs, st_rhs, st_vs, st_v, st_kh, st_bh, st_gamma, st_bonus,
                 st_inv, st_arb, st_axv, *, n_chunks):
    C = RWKV_CHUNK
    S = 2 * C
    assert S == LANES
    b_id = pl.program_id(0)
    c = pl.program_id(1)
    pairs = range(N_PAIRS)
    staging = (st_lhs, st_rhs, st_vs, st_v, st_kh, st_bh, st_gamma, st_bonus, st_inv, st_arb, st_axv)

    @pl.when(c == 0)
    def _():
        prev_rkv[...] = pinit_rkv_ref[...]
        prev_lora[...] = pinit_lora_ref[...]

    @pl.when(c <= RWKV_SLOTS - 1)
    def _():
        h_scr[...] = hinit_ref[0]

    @pl.when(jnp.logical_and(b_id == 0, c == 0))
    def _():
        for ref in staging:
            for slot in range(1, RWKV_SLOTS):
                ref[slot] = jnp.zeros(ref.shape[1:], ref.dtype)

    def cols(p, base=0):
        return slice(base + p * LANES, base + (p + 1) * LANES)

    def make_stack():
        head0 = lax.broadcasted_iota(jnp.int32, (C, LANES), 1) < HEAD

        def stack(t):
            return jnp.concatenate([jnp.where(head0, t, 0.0), jnp.where(head0, 0.0, t)], axis=0).astype(BF16)
        return head0, stack

    def make_head_sum():
        li = lax.broadcasted_iota(jnp.int32, (2 * LANES, LANES), 0)
        lj = lax.broadcasted_iota(jnp.int32, (2 * LANES, LANES), 1)
        ones_bd2 = jnp.where(((li % LANES) // HEAD) == (lj // HEAD), 1.0, 0.0).astype(BF16)

        def head_sum(t):
            rows = jnp.concatenate([t[:, cols(p)] for p in pairs], axis=0)
            hi, lo = _split2(rows)
            s = _dot(jnp.concatenate([hi, lo], axis=1), ones_bd2)
            return jnp.concatenate([s[p * C:(p + 1) * C] for p in pairs], axis=1)
        return head_sum

    def prepare(slot):
        head0, stack = make_stack()
        head_sum = make_head_sum()
        row = lax.broadcasted_iota(jnp.int32, (C, 1), 0)

        def token_shift(x, prev_tile, mu):
            xp = jnp.where(row == 0, prev_tile[SUBLANES - 1:SUBLANES, :], pltpu.roll(x, 1, 0))
            return x + (xp - x) * mu

        xl = lora_ref[...]
        ul = token_shift(xl, prev_lora[...], mu_lora_ref[...])
        prev_lora[...] = xl[C - SUBLANES:C]
        lo_out = _dot(jnp.where(head0, jnp.tanh(ul), ul).astype(BF16), wl_ref[...])
        yield

        def shifted(base):
            sl = slice(base, base + RWKV_WIDTH)
            xs = rkv_ref[:, sl].astype(F32)
            out = token_shift(xs, prev_rkv[:, sl], mu_rkv_ref[:, sl])
            prev_rkv[:, sl] = xs[C - SUBLANES:C]
            return out

        r = shifted(0)
        yield
        k = shifted(RWKV_WIDTH)
        yield
        v = shifted(2 * RWKV_WIDTH)
        yield

        logdec = -math.exp(-0.5) * _sigmoid(w0_ref[...] + lo_out[:, 0:RWKV_WIDTH])
        yield
        a_lr = _sigmoid(a0_ref[...] + lo_out[:, RWKV_WIDTH:2 * RWKV_WIDTH])
        yield

        ti = lax.broadcasted_iota(jnp.int32, (C, C), 0)
        si = lax.broadcasted_iota(jnp.int32, (C, C), 1)
        tri = jnp.where(ti >= si, 1.0, 0.0).astype(BF16)
        ld_hi, ld_lo = _split2(logdec)
        cum = _dot(tri, ld_hi) + _dot(tri, ld_lo)
        cum_last = cum[C - 1:C, :]
        yield
        e_excl = jnp.exp(cum - logdec)
        e_incl = jnp.exp(cum)
        yield
        e_neg = jnp.exp(-cum)
        e_hat = jnp.exp(cum_last - cum)
        yield

        kk = k * kk_ref[...]
        kk = kk * lax.rsqrt(jnp.maximum(head_sum(kk * kk), 1e-24))
        yield
        k2 = k * (1.0 + (a_lr - 1.0) * ka_ref[...])
        b = kk * a_lr
        yield
        xa_all = -kk * e_excl
        xr_all = r * e_incl
        yield
        yb_all = b * e_neg
        yk_all = k2 * e_neg
        yield

        st_gamma[slot] = jnp.exp(cum_last)
        st_bonus[slot] = head_sum(r * k2 * rk_ref[...]) * v
        yield
        st_v[slot] = v.astype(BF16)
        st_kh[slot] = (k2 * e_hat).astype(BF16)
        st_bh[slot] = (b * e_hat).astype(BF16)
        yield
        for p in pairs:
            sl = cols(p)
            st_lhs[slot, p] = jnp.concatenate([xa_all[:, sl], xr_all[:, sl]], axis=0).astype(BF16)
            st_rhs[slot, p] = jnp.concatenate([stack(yb_all[:, sl]), stack(yk_all[:, sl])], axis=0)
            st_vs[slot, p] = stack(v[:, sl])
            yield

    def stage(fn):
        out = []
        for p in pairs:
            out.append(fn(p))
            yield
        return out

    def invert(slot):
        _, stack = make_stack()
        ri = lax.broadcasted_iota(jnp.int32, (S, 2 * S), 0)
        ci = lax.broadcasted_iota(jnp.int32, (S, 2 * S), 1)
        a_mask = (ri % C) >= (ci % C) + jnp.where(ri < C, 1, 0)
        ei = lax.broadcasted_iota(jnp.int32, (C, S), 0)
        ej = lax.broadcasted_iota(jnp.int32, (C, S), 1)
        eye = jnp.where(ei == ej % C, 1.0, 0.0).astype(F32)

        a_all = yield from stage(lambda p: jnp.where(a_mask, _dot_nt(st_lhs[slot, p], st_rhs[slot, p]), 0.0))
        a_ab = [a_all[p][0:C, 0:S] for p in pairs]

        def values(p):
            st_arb[slot, p] = a_all[p][C:S, 0:S].astype(BF16)
            st_axv[slot, p] = _dot(a_all[p][:, S:2 * S].astype(BF16), st_vs[slot, p])

        yield from stage(values)

        inv = [eye + a_ab[p] for p in pairs]
        apow = yield from stage(lambda p: _dot(a_ab[p].astype(BF16), stack(a_ab[p])))
        n = 2
        while 2 * n < C:
            res = yield from stage(
                lambda p: _dot(jnp.concatenate([inv[p], apow[p]], axis=0).astype(BF16), stack(apow[p])))
            inv = [inv[p] + res[p][0:C] for p in pairs]
            apow = [res[p][C:S] for p in pairs]
            n *= 2

        def last(p):
            st_inv[slot, p] = (inv[p] + _dot(inv[p].astype(BF16), stack(apow[p]))).astype(BF16)

        yield from stage(last)

    def state(slot):
        _, stack = make_stack()
        head_sum = make_head_sum()
        bi = lax.broadcasted_iota(jnp.int32, (LANES, LANES), 0)
        bj = lax.broadcasted_iota(jnp.int32, (LANES, LANES), 1)
        same_head = (bi // HEAD) == (bj // HEAD)

        s_old = [h_scr[p] for p in pairs]
        x_s = yield from stage(lambda p: _dot_nt(st_lhs[slot, p], s_old[p].astype(BF16)))
        us = yield from stage(lambda p: _dot(st_inv[slot, p], stack(x_s[p][0:C] + st_axv[slot, p, 0:C])))
        ys = yield from stage(
            lambda p: x_s[p][C:S] + st_axv[slot, p, C:S] + _dot(st_arb[slot, p], stack(us[p])))

        def update(p):
            sl = cols(p)
            vu = jnp.concatenate([st_v[slot, :, sl], us[p].astype(BF16)], axis=0)
            khb = jnp.concatenate([st_kh[slot, :, sl], st_bh[slot, :, sl]], axis=0)
            h_scr[p] = s_old[p] * st_gamma[slot, :, sl] + jnp.where(same_head, _dot_tn(vu, khb), 0.0)

        yield from stage(update)

        y = jnp.concatenate(ys, axis=1)
        yc = y - head_sum(y) * (1.0 / HEAD)
        yield
        var = head_sum(yc * yc) * (1.0 / HEAD)
        y = yc * lax.rsqrt(var + GN_EPS) * gnw_ref[...] + gnb_ref[...] + st_bonus[slot]
        yield
        y_ref[...] = (y * _silu(g_ref[...].astype(F32))).astype(y_ref.dtype)

    phase = lax.rem(c, RWKV_SLOTS)
    for r in range(RWKV_SLOTS):
        @pl.when(phase == r)
        def _():
            _interleave((invert((r - 1) % RWKV_SLOTS), RWKV_WEIGHTS[0]),
                        (state((r - 2) % RWKV_SLOTS), RWKV_WEIGHTS[1]),
                        (prepare(r), RWKV_WEIGHTS[2]))

    @pl.when(c == n_chunks + RWKV_SLOTS - 2)
    def _():
        hout_ref[0] = h_scr[...]


def _rwkv(p_arr, lora_arr, pinit_rkv, pinit_lora, hinit, params, *, batch):
    C = RWKV_CHUNK
    rows = p_arr.shape[0]
    n_chunks = rows // (batch * C)
    const2 = lambda b, c: (0, 0)
    vec = lambda width: pl.BlockSpec((1, width), const2)
    prep_chunk = lambda b, c: b * n_chunks + jnp.minimum(c, n_chunks - 1)
    recur_chunk = lambda b, c: b * n_chunks + jnp.maximum(c - (RWKV_SLOTS - 1), 0)
    slots = lambda shape, dtype: pltpu.VMEM((RWKV_SLOTS,) + shape, dtype)
    return pl.pallas_call(
        functools.partial(_rwkv_kernel, n_chunks=n_chunks),
        out_shape=(jax.ShapeDtypeStruct((rows, RWKV_WIDTH), BF16),
                   jax.ShapeDtypeStruct((batch, N_PAIRS, LANES, LANES), F32)),
        grid=(batch, n_chunks + RWKV_SLOTS - 1),
        in_specs=[
            pl.BlockSpec((C, 3 * RWKV_WIDTH), lambda b, c: (prep_chunk(b, c), COL_RKV // (3 * RWKV_WIDTH))),
            pl.BlockSpec((C, RWKV_WIDTH), lambda b, c: (recur_chunk(b, c), COL_GR // RWKV_WIDTH)),
            pl.BlockSpec((C, LANES), lambda b, c: (prep_chunk(b, c), 0)),
            pl.BlockSpec((SUBLANES, 3 * RWKV_WIDTH), const2),
            pl.BlockSpec((SUBLANES, LANES), const2),
            pl.BlockSpec((1, N_PAIRS, LANES, LANES), lambda b, c: (0, 0, 0, 0)),
            vec(3 * RWKV_WIDTH), vec(LANES), vec(RWKV_WIDTH), vec(RWKV_WIDTH),
            pl.BlockSpec((LANES, 2 * RWKV_WIDTH), const2),
            vec(RWKV_WIDTH), vec(RWKV_WIDTH), vec(RWKV_WIDTH), vec(RWKV_WIDTH), vec(RWKV_WIDTH),
        ],
        out_specs=(
            pl.BlockSpec((C, RWKV_WIDTH), lambda b, c: (recur_chunk(b, c), 0)),
            pl.BlockSpec((1, N_PAIRS, LANES, LANES), lambda b, c: (b, 0, 0, 0)),
        ),
        scratch_shapes=[
            pltpu.VMEM((N_PAIRS, LANES, LANES), F32),
            pltpu.VMEM((SUBLANES, 3 * RWKV_WIDTH), F32),
            pltpu.VMEM((SUBLANES, LANES), F32),
            slots((N_PAIRS, 2 * C, LANES), BF16),
            slots((N_PAIRS, 4 * C, LANES), BF16),
            slots((N_PAIRS, 2 * C, LANES), BF16),
            slots((C, RWKV_WIDTH), BF16),
            slots((C, RWKV_WIDTH), BF16),
            slots((C, RWKV_WIDTH), BF16),
            slots((1, RWKV_WIDTH), F32),
            slots((C, RWKV_WIDTH), F32),
            slots((N_PAIRS, C, 2 * C), BF16),
            slots((N_PAIRS, C, 2 * C), BF16),
            slots((N_PAIRS, 2 * C, LANES), F32),
        ],
        compiler_params=pltpu.CompilerParams(
            dimension_semantics=("arbitrary", "arbitrary"), vmem_limit_bytes=VMEM_LIMIT),
        name="rwkv7_chunk",
    )(p_arr, p_arr, lora_arr, pinit_rkv, pinit_lora, hinit, *params)


def _attn_kernel(q_ref, k_ref, v_ref, g_ref, km_ref, vmt_ref, lq1_ref, lk1_ref, lq2_ref, lk2_ref, sw_ref,
                 o_ref, vt_scr, qst_scr, m_scr, acc_scr, s_scr, *, t, n_blk):
    g_id = pl.program_id(0)

    @pl.when(g_id == 0)
    def _():
        for ref in (vt_scr, qst_scr, m_scr, acc_scr):
            ref[1] = jnp.ones(ref.shape[1:], ref.dtype)

    def setup(slot):
        comp0 = lax.broadcasted_iota(jnp.int32, (LANES, t // 2), 0) < HEAD
        for c in range(n_blk):
            vt_scr[slot, c, 0:LANES, :] = v_ref[c * t:(c + 1) * t, :].astype(F32).T.astype(BF16)
            vt_scr[slot, c, LANES:LANES + ONES_ROWS, :] = jnp.ones((ONES_ROWS, t), BF16)
            yield
        for i in range(n_blk):
            qt = q_ref[i * t:(i + 1) * t, :].astype(F32).T
            parts = []
            for half in (qt[:, 0:t // 2], qt[:, t // 2:t]):
                parts += [jnp.where(comp0, half, 0.0), jnp.where(comp0, 0.0, half)]
            qst = jnp.concatenate(parts, axis=1).astype(BF16)
            qst_scr[slot, i] = qst
            yield
            s = _dot(km_ref[...], qst)
            m = jnp.max(s, axis=0, keepdims=True)
            m_scr[slot, i] = m
            acc_scr[slot, i] = _dot(vmt_ref[0], jnp.exp2(s - m).astype(BF16))
            yield

    def blocks(slot):
        h = t // 2
        full_rows, lo_rows, hi_rows = slice(0, t), slice(0, h), slice(h, t)
        all_cols, hi_cols = slice(0, 2 * t), slice(t, 2 * t)
        units = []
        for i, j in sorted(((i, j) for i in range(n_blk) for j in range(i + 1)), key=lambda ij: (ij[1], ij[0])):
            if i == j:
                units += [(i, j, lo_rows, all_cols, True), (i, j, hi_rows, hi_cols, True)]
            else:
                units.append((i, j, full_rows, all_cols, False))
        n_slots = s_scr.shape[0]
        m = [m_scr[slot, i] for i in range(n_blk)]

        def extent(sl):
            return sl.stop - sl.start

        def issue(n):
            i, j, rows, cols, _ = units[n]
            keys = k_ref[j * t + rows.start:j * t + rows.stop, :]
            s_scr[n % n_slots, 0:extent(rows), 0:extent(cols)] = _dot(keys, qst_scr[slot, i, :, cols])

        def visible(n_cols):
            kpos = lax.broadcasted_iota(jnp.int32, (h, n_cols), 0)
            col = lax.broadcasted_iota(jnp.int32, (h, n_cols), 1)
            return kpos <= col % h + jnp.where(col >= t, h, 0)

        visible = {n_cols: visible(n_cols) for n_cols in (t, 2 * t)}
        for n in range(min(ATT_LOOKAHEAD, len(units))):
            issue(n)
        for n, (i, j, rows, cols, masked) in enumerate(units):
            if n + ATT_LOOKAHEAD < len(units):
                issue(n + ATT_LOOKAHEAD)
            s = s_scr[n % n_slots, 0:extent(rows), 0:extent(cols)]
            if masked:
                s = jnp.where(visible[extent(cols)], s, MASK_VALUE)
            m_old = m[i][:, cols]
            m_new = jnp.maximum(m_old, jnp.max(s, axis=0, keepdims=True))
            alpha = jnp.exp2(m_old - m_new)
            m[i] = m_new if extent(cols) == 2 * t else jnp.concatenate([m[i][:, 0:cols.start], m_new], axis=1)
            yield
            acc_scr[slot, i, :, cols] = (alpha * acc_scr[slot, i, :, cols]
                                         + _dot(vt_scr[slot, j, :, rows], jnp.exp2(s - m_new).astype(BF16)))
            yield

        lam = (jnp.exp(jnp.sum(lq1_ref[...] * lk1_ref[...], axis=-1, keepdims=True))
               - jnp.exp(jnp.sum(lq2_ref[...] * lk2_ref[...], axis=-1, keepdims=True)) + LAMBDA_INIT)
        for i in range(n_blk):
            on = acc_scr[slot, i, 0:LANES, :] / acc_scr[slot, i, LANES:LANES + 1, :]
            comp = lambda c: jnp.concatenate([on[:, c * h:(c + 1) * h], on[:, t + c * h:t + (c + 1) * h]], axis=1)
            ot = comp(0) - lam * comp(1)
            ot = ot * lax.rsqrt(jnp.mean(ot * ot, axis=0, keepdims=True) + SUBLN_EPS)
            g = g_ref[i * t:(i + 1) * t, :].astype(F32)
            o_ref[i * t:(i + 1) * t, :] = (ot.T * sw_ref[...] * (1.0 - LAMBDA_INIT) * _silu(g)).astype(o_ref.dtype)
            yield

    odd = jnp.bitwise_and(g_id, 1) == 1

    @pl.when(jnp.logical_not(odd))
    def _():
        _interleave((blocks(1), ATT_WEIGHTS[0]), (setup(0), ATT_WEIGHTS[1]))

    @pl.when(odd)
    def _():
        _interleave((blocks(0), ATT_WEIGHTS[0]), (setup(1), ATT_WEIGHTS[1]))


def _attention(p_arr, p_meta, vm_t, lam_vecs, subln_w, *, batch, seq):
    t = ATT_BLOCK
    n_blk = seq // t
    n_heads_total = batch * DIFF_HEADS
    lanes_blk = lambda col: col // LANES
    small = pl.BlockSpec((1, HEAD), lambda g: (0, 0))
    setup_head = lambda g: jnp.minimum(g, n_heads_total - 1)
    block_head = lambda g: jnp.maximum(g - 1, 0)

    def head_cols(col, head_of):
        def index(g):
            h = head_of(g)
            return (h // DIFF_HEADS, lanes_blk(col) + h % DIFF_HEADS)
        return pl.BlockSpec((seq, LANES), index)

    slots = lambda shape, dtype: pltpu.VMEM((2,) + shape, dtype)
    return pl.pallas_call(
        functools.partial(_attn_kernel, t=t, n_blk=n_blk),
        out_shape=jax.ShapeDtypeStruct((batch * seq, DIFF_WIDTH), BF16),
        grid=(n_heads_total + 1,),
        in_specs=[
            head_cols(COL_Q, setup_head), head_cols(COL_K, block_head), head_cols(COL_V, setup_head),
            head_cols(COL_GD, block_head),
            pl.BlockSpec((N_META, LANES), lambda g: (0, lanes_blk(COL_K) + setup_head(g) % DIFF_HEADS)),
            pl.BlockSpec((1, LANES + ONES_ROWS, N_META), lambda g: (setup_head(g) % DIFF_HEADS, 0, 0)),
            small, small, small, small,
            pl.BlockSpec((1, LANES), lambda g: (0, 0)),
        ],
        out_specs=pl.BlockSpec((seq, LANES), lambda g: (block_head(g) // DIFF_HEADS, block_head(g) % DIFF_HEADS)),
        scratch_shapes=[
            slots((n_blk, LANES + ONES_ROWS, t), BF16),
            slots((n_blk, LANES, 2 * t), BF16),
            slots((n_blk, 1, 2 * t), F32),
            slots((n_blk, LANES + ONES_ROWS, 2 * t), F32),
            pltpu.VMEM((ATT_LOOKAHEAD + 1, t, 2 * t), F32),
        ],
        compiler_params=pltpu.CompilerParams(
            dimension_semantics=("arbitrary",), vmem_limit_bytes=VMEM_LIMIT),
        name="diff_attn",
    )(p_arr, p_arr, p_arr, p_arr, p_meta, vm_t, *lam_vecs, subln_w)


def _outproj_kernel(yr_ref, yd_ref, w1_ref, w2_ref, x_ref, g_ref, o_ref):
    y = _dot(yr_ref[...], w1_ref[...]) + _dot(yd_ref[...], w2_ref[...])
    ms = jnp.mean(y * y, axis=-1, keepdims=True)
    o_ref[...] = x_ref[...] + y * lax.rsqrt(ms + NORM_EPS) * g_ref[...]


def _outproj(y_r, y_d, w, x2, g):
    m = x2.shape[0]
    tm = OUT_TM
    assert RWKV_WIDTH == DIFF_WIDTH and w.shape == (RWKV_WIDTH + DIFF_WIDTH, D_MODEL)
    return pl.pallas_call(
        _outproj_kernel,
        out_shape=jax.ShapeDtypeStruct((m, D_MODEL), F32),
        grid=(m // tm,),
        in_specs=[
            pl.BlockSpec((tm, RWKV_WIDTH), lambda i: (i, 0)),
            pl.BlockSpec((tm, DIFF_WIDTH), lambda i: (i, 0)),
            pl.BlockSpec((RWKV_WIDTH, D_MODEL), lambda i: (0, 0)),
            pl.BlockSpec((DIFF_WIDTH, D_MODEL), lambda i: (1, 0)),
            pl.BlockSpec((tm, D_MODEL), lambda i: (i, 0)),
            pl.BlockSpec((1, D_MODEL), lambda i: (0, 0)),
        ],
        out_specs=pl.BlockSpec((tm, D_MODEL), lambda i: (i, 0)),
        compiler_params=pltpu.CompilerParams(
            dimension_semantics=("arbitrary",), vmem_limit_bytes=VMEM_LIMIT),
        name="outproj",
    )(y_r, y_d, w, w, x2, g)


def _rope_tables(first_pos, n_pos):
    pos = jnp.arange(first_pos, first_pos + n_pos, dtype=F32)
    inv_freq = ROPE_THETA ** (-jnp.arange(0, HEAD, 2, dtype=F32) / HEAD)
    ang = pos[:, None] * inv_freq[None, :]
    cos = jnp.cos(ang)
    sin = jnp.sin(ang)
    cos = jnp.concatenate([cos, cos, cos, cos], axis=-1)
    sin = jnp.concatenate([-sin, sin, -sin, sin], axis=-1)
    return cos, sin


def kernel(x, meta_tokens, pre_norm_w, w_in, rwkv_mu, rwkv_w0, rwkv_w_up, rwkv_a0, rwkv_a_up, rwkv_k_k, rwkv_k_a, rwkv_r_k, rwkv_gn_w, rwkv_gn_b, diff_lam_q1, diff_lam_k1, diff_lam_q2, diff_lam_k2, diff_subln_w, w_out, post_norm_w):
    batch, seq, d = x.shape
    assert d == D_MODEL and meta_tokens.shape == (N_META, D_MODEL)
    assert seq % RWKV_CHUNK == 0 and seq % ATT_BLOCK == 0
    assert (batch * seq) % PROJ_TM == 0 and seq % PROJ_TM == 0
    layer = 0
    x2 = x.reshape(batch * seq, D_MODEL)

    w = w_in[layer]
    rkv_end = 3 * RWKV_WIDTH
    lora_end = rkv_end + 2 * LORA
    g_pre = pre_norm_w[layer].reshape(1, D_MODEL)
    cos_m, sin_m = _rope_tables(0, N_META)
    cos_x, sin_x = _rope_tables(N_META, seq)
    w_main, w_lora_in, p_meta, lora_meta = _wprep(w, meta_tokens.astype(x.dtype), g_pre, cos_m, sin_m)
    p_x, lora_x = _inproj(x2, g_pre, w_main, w_lora_in, cos_x, sin_x)

    mu = rwkv_mu[layer]
    zeros = jnp.zeros((LORA, RWKV_WIDTH), F32)
    w_lora = jnp.concatenate([
        jnp.concatenate([rwkv_w_up[layer], zeros], axis=1),
        jnp.concatenate([zeros, rwkv_a_up[layer]], axis=1)], axis=0)
    row = lambda t, n: t.reshape(1, n)
    rwkv_params = (
        row(mu[:rkv_end], rkv_end), row(mu[rkv_end:lora_end], LANES),
        row(rwkv_w0[layer], RWKV_WIDTH), row(rwkv_a0[layer], RWKV_WIDTH),
        w_lora.astype(BF16),
        row(rwkv_k_k[layer], RWKV_WIDTH), row(rwkv_k_a[layer], RWKV_WIDTH),
        row(rwkv_r_k[layer], RWKV_WIDTH), row(rwkv_gn_w[layer], RWKV_WIDTH), row(rwkv_gn_b[layer], RWKV_WIDTH),
    )
    pad = RWKV_CHUNK - N_META
    _, h_meta = _rwkv(jnp.pad(p_meta, ((pad, 0), (0, 0))), jnp.pad(lora_meta, ((pad, 0), (0, 0))),
                      jnp.zeros((SUBLANES, rkv_end), F32), jnp.zeros((SUBLANES, LANES), F32),
                      jnp.zeros((1, N_PAIRS, LANES, LANES), F32), rwkv_params, batch=1)
    y_r, _ = _rwkv(p_x, lora_x, p_meta[N_META - SUBLANES:, :rkv_end].astype(F32), lora_meta[N_META - SUBLANES:],
                   h_meta, rwkv_params, batch=batch)

    lam_vecs = tuple(t[layer].reshape(1, HEAD) for t in (diff_lam_q1, diff_lam_k1, diff_lam_q2, diff_lam_k2))
    vm_t = p_meta[:, COL_V:COL_V + DIFF_WIDTH].reshape(N_META, DIFF_HEADS, LANES).transpose(1, 2, 0)
    vm_t = jnp.concatenate([vm_t, jnp.ones((DIFF_HEADS, ONES_ROWS, N_META), BF16)], axis=1)
    y_d = _attention(p_x, p_meta, vm_t, lam_vecs, diff_subln_w[layer].reshape(1, LANES), batch=batch, seq=seq)

    out = _outproj(y_r, y_d, w_out[layer].astype(BF16), x2, post_norm_w[layer].reshape(1, D_MODEL))
    return out.reshape(batch, seq, D_MODEL)
```

```python
import functools
import math

import jax
import jax.numpy as jnp
from jax import lax
from jax.experimental import pallas as pl
from jax.experimental.pallas import tpu as pltpu

F32 = jnp.float32
BF16 = jnp.bfloat16

D_MODEL = 2048
N_META = 16
HEAD = 64
LANES = 128
SUBLANES = 8
V7X_VMEM_BYTES = 64 * 1024 * 1024
RWKV_WIDTH = 1024
N_PAIRS = RWKV_WIDTH // LANES
DIFF_WIDTH = 1024
DIFF_HEADS = DIFF_WIDTH // LANES
LORA = 64
ROPE_THETA = 10000.0
NORM_EPS = 1e-6
GN_EPS = 64e-5
SUBLN_EPS = 1e-5
LAMBDA_INIT = 0.8 - 0.6 * math.exp(-0.3 * 0)

COL_RKV = 0
COL_GR = 3 * RWKV_WIDTH
COL_Q = 4 * RWKV_WIDTH
COL_K = COL_Q + DIFF_WIDTH
COL_V = COL_K + DIFF_WIDTH
COL_GD = COL_V + DIFF_WIDTH
P_COLS = COL_GD + DIFF_WIDTH

WPREP_ROWS = 256
PROJ_TM = 1024
PROJ_TN = 1024
PROJ_KC = 512
RWKV_CHUNK = 64
RWKV_SLOTS = 3
RWKV_WEIGHTS = (4, 3, 1)
ATT_BLOCK = 512
ATT_LOOKAHEAD = 1
ATT_WEIGHTS = (7, 3)
OUT_TM = 256
VMEM_LIMIT = V7X_VMEM_BYTES * 3 // 4
MASK_VALUE = -1e30
ONES_ROWS = 16
Q_SCALE = HEAD ** -0.5 * math.log2(math.e)


def _dot(a, b):
    return jnp.dot(a, b, preferred_element_type=F32)


def _dot_nt(a, b):
    return lax.dot_general(a, b, (((1,), (1,)), ((), ())), preferred_element_type=F32)


def _dot_tn(a, b):
    return lax.dot_general(a, b, (((0,), (0,)), ((), ())), preferred_element_type=F32)


def _split2(x):
    hi = x.astype(BF16)
    lo = (x - hi.astype(F32)).astype(BF16)
    return hi, lo


def _sigmoid(x):
    return 1.0 / (1.0 + jnp.exp(-x))


def _silu(x):
    return x * _sigmoid(x)


def _interleave(*weighted_generators):
    active = [list(gw) for gw in weighted_generators]
    while active:
        for item in list(active):
            gen, weight = item
            for _ in range(weight):
                if next(gen, "done") == "done":
                    active.remove(item)
                    break


def _rotate_half(blk, cos, sin):
    lane = lax.broadcasted_iota(jnp.int32, blk.shape, 1)
    first_half = (lane % HEAD) < (HEAD // 2)
    swapped = jnp.where(first_half, pltpu.roll(blk, LANES - HEAD // 2, 1), pltpu.roll(blk, HEAD // 2, 1))
    return blk * cos + swapped * sin


def _wprep_kernel(w_ref, meta_ref, g_ref, cos_ref, sin_ref, main_ref, lora_ref, pm_ref, lm_ref,
                  acc_ref, lacc_ref, ssq_ref):
    i = pl.program_id(0)
    rkv_end = 3 * RWKV_WIDTH
    lora_end = rkv_end + 2 * LORA
    w_rkv = w_ref[:, 0:rkv_end].astype(BF16)
    w_lora = w_ref[:, rkv_end:lora_end].astype(BF16)
    w_rest = w_ref[:, lora_end:lora_end + P_COLS - rkv_end].astype(BF16)
    main_ref[:, 0:rkv_end] = w_rkv
    main_ref[:, rkv_end:P_COLS] = w_rest
    lora_ref[...] = w_lora

    @pl.when(i == 0)
    def _():
        acc_ref[...] = jnp.zeros_like(acc_ref)
        lacc_ref[...] = jnp.zeros_like(lacc_ref)
        ssq_ref[...] = jnp.zeros_like(ssq_ref)

    xm = meta_ref[0]
    hn = (xm * g_ref[0]).astype(BF16)
    ssq_ref[...] += jnp.broadcast_to(jnp.sum(xm * xm, axis=-1, keepdims=True), ssq_ref.shape)
    acc_ref[:, 0:rkv_end] += _dot(hn, w_rkv)
    acc_ref[:, rkv_end:P_COLS] += _dot(hn, w_rest)
    lacc_ref[...] += _dot(hn, w_lora)

    @pl.when(i == pl.num_programs(0) - 1)
    def _():
        rs = lax.rsqrt(ssq_ref[...] * (1.0 / D_MODEL) + NORM_EPS)
        lm_ref[...] = lacc_ref[...] * rs
        for c in range(P_COLS // LANES):
            cb = slice(c * LANES, (c + 1) * LANES)
            blk = acc_ref[:, cb]
            if COL_Q <= c * LANES < COL_K:
                blk = _rotate_half(blk, cos_ref[...], sin_ref[...]) * Q_SCALE
            elif COL_K <= c * LANES < COL_V:
                blk = _rotate_half(blk, cos_ref[...], sin_ref[...])
            pm_ref[:, cb] = (blk * rs).astype(pm_ref.dtype)


def _wprep(w, meta_tokens, g, cos, sin):
    rows, cols = w.shape
    assert cols == P_COLS + 2 * LORA and rows % WPREP_ROWS == 0
    n_steps = rows // WPREP_ROWS
    meta_k = meta_tokens.reshape(N_META, n_steps, WPREP_ROWS).transpose(1, 0, 2)
    g_k = g.reshape(n_steps, 1, WPREP_ROWS)
    const = lambda i: (0, 0)
    return pl.pallas_call(
        _wprep_kernel,
        out_shape=(jax.ShapeDtypeStruct((rows, P_COLS), BF16), jax.ShapeDtypeStruct((rows, 2 * LORA), BF16),
                   jax.ShapeDtypeStruct((N_META, P_COLS), BF16), jax.ShapeDtypeStruct((N_META, LANES), F32)),
        grid=(n_steps,),
        in_specs=[
            pl.BlockSpec((WPREP_ROWS, cols), lambda i: (i, 0)),
            pl.BlockSpec((1, N_META, WPREP_ROWS), lambda i: (i, 0, 0)),
            pl.BlockSpec((1, 1, WPREP_ROWS), lambda i: (i, 0, 0)),
            pl.BlockSpec((N_META, LANES), const),
            pl.BlockSpec((N_META, LANES), const),
        ],
        out_specs=(pl.BlockSpec((WPREP_ROWS, P_COLS), lambda i: (i, 0)),
                   pl.BlockSpec((WPREP_ROWS, 2 * LORA), lambda i: (i, 0)),
                   pl.BlockSpec((N_META, P_COLS), const),
                   pl.BlockSpec((N_META, LANES), const)),
        scratch_shapes=[pltpu.VMEM((N_META, P_COLS), F32), pltpu.VMEM((N_META, LANES), F32),
                        pltpu.VMEM((N_META, LANES), F32)],
        compiler_params=pltpu.CompilerParams(dimension_semantics=("arbitrary",), vmem_limit_bytes=VMEM_LIMIT),
        name="wprep",
    )(w, meta_k, g_k, cos, sin)


def _inproj_kernel(*refs, tn):
    n_k = D_MODEL // PROJ_KC
    x_refs = refs[:n_k]
    g_ref, w_ref, wl_ref, cos_ref, sin_ref, o_ref, ol_ref, hn_ref, rs_ref = refs[n_k:]
    j = pl.program_id(1)
    q_tile = COL_Q // tn
    k_tile = COL_K // tn
    assert 0 < q_tile < k_tile
    is_rope = jnp.logical_or(j == q_tile, j == k_tile)
    col_blocks = [slice(c * LANES, (c + 1) * LANES) for c in range(tn // LANES)]

    @pl.when(j == 0)
    def _():
        ssq = None
        acc = None
        for k in range(n_k):
            sl = slice(k * PROJ_KC, (k + 1) * PROJ_KC)
            xc = x_refs[k][...]
            part = jnp.sum(xc * xc, axis=-1, keepdims=True)
            ssq = part if ssq is None else ssq + part
            hn = (xc * g_ref[:, sl]).astype(BF16)
            hn_ref[:, sl] = hn
            d = _dot(hn, w_ref[sl, :])
            acc = d if acc is None else acc + d
        rs = jnp.broadcast_to(lax.rsqrt(ssq * (1.0 / D_MODEL) + NORM_EPS), rs_ref.shape)
        rs_ref[...] = rs
        ol_ref[...] = _dot(hn_ref[...], wl_ref[...]) * rs
        for cb in col_blocks:
            o_ref[:, cb] = (acc[:, cb] * rs).astype(o_ref.dtype)

    @pl.when(is_rope)
    def _():
        acc = _dot(hn_ref[...], w_ref[...])
        cos = cos_ref[...]
        sin = sin_ref[...]
        scale = rs_ref[...] * jnp.where(j == q_tile, Q_SCALE, 1.0).astype(F32)
        for cb in col_blocks:
            o_ref[:, cb] = (_rotate_half(acc[:, cb], cos, sin) * scale).astype(o_ref.dtype)

    @pl.when(jnp.logical_and(j > 0, jnp.logical_not(is_rope)))
    def _():
        acc = _dot(hn_ref[...], w_ref[...])
        rs = rs_ref[...]
        for cb in col_blocks:
            o_ref[:, cb] = (acc[:, cb] * rs).astype(o_ref.dtype)


def _inproj(x2, g, w_main, w_lora, cos, sin):
    m = x2.shape[0]
    tm = PROJ_TM
    tn = PROJ_TN
    n_pos_tiles = cos.shape[0] // tm
    n_row_tiles = m // tm
    n_col_tiles = P_COLS // tn
    n_k = D_MODEL // PROJ_KC
    assert n_k <= n_col_tiles

    def x_chunk(k):
        first_step = n_col_tiles - n_k + k
        return pl.BlockSpec(
            (tm, PROJ_KC), lambda i, j: (jnp.minimum(i + (j >= first_step).astype(jnp.int32), n_row_tiles - 1), k))

    return pl.pallas_call(
        functools.partial(_inproj_kernel, tn=tn),
        out_shape=(jax.ShapeDtypeStruct((m, P_COLS), BF16), jax.ShapeDtypeStruct((m, LANES), F32)),
        grid=(n_row_tiles, n_col_tiles),
        in_specs=[x_chunk(k) for k in range(n_k)] + [
            pl.BlockSpec((1, D_MODEL), lambda i, j: (0, 0)),
            pl.BlockSpec((D_MODEL, tn), lambda i, j: (0, j)),
            pl.BlockSpec((D_MODEL, LANES), lambda i, j: (0, 0)),
            pl.BlockSpec((tm, LANES), lambda i, j: (i % n_pos_tiles, 0)),
            pl.BlockSpec((tm, LANES), lambda i, j: (i % n_pos_tiles, 0)),
        ],
        out_specs=(
            pl.BlockSpec((tm, tn), lambda i, j: (i, j)),
            pl.BlockSpec((tm, LANES), lambda i, j: (i, 0)),
        ),
        scratch_shapes=[pltpu.VMEM((tm, D_MODEL), BF16), pltpu.VMEM((tm, LANES), F32)],
        compiler_params=pltpu.CompilerParams(
            dimension_semantics=("arbitrary", "arbitrary"), vmem_limit_bytes=VMEM_LIMIT),
        name="inproj",
    )(*([x2] * n_k), g, w_main, w_lora, cos, sin)


def _rwkv_kernel(rkv_ref, g_ref, lora_ref, pinit_rkv_ref, pinit_lora_ref, hinit_ref,
                 mu_rkv_ref, mu_lora_ref, w0_ref, a0_ref, wl_ref,
                 kk_ref, ka_ref, rk_ref, gnw_ref, gnb_ref,
                 y_ref, hout_ref,
                 h_scr, prev_rkv, prev_lora,
                 st_lhs, st_rhs, st_vs, st_v, st_kh, st_bh, st_gamma, st_bonus,
                 st_inv, st_arb, st_axv, *, n_chunks):
    C = RWKV_CHUNK
    S = 2 * C
    assert S == LANES
    b_id = pl.program_id(0)
    c = pl.program_id(1)
    pairs = range(N_PAIRS)
    staging = (st_lhs, st_rhs, st_vs, st_v, st_kh, st_bh, st_gamma, st_bonus, st_inv, st_arb, st_axv)

    @pl.when(c == 0)
    def _():
        prev_rkv[...] = pinit_rkv_ref[...]
        prev_lora[...] = pinit_lora_ref[...]

    @pl.when(c <= RWKV_SLOTS - 1)
    def _():
        h_scr[...] = hinit_ref[0]

    @pl.when(jnp.logical_and(b_id == 0, c == 0))
    def _():
        for ref in staging:
            for slot in range(1, RWKV_SLOTS):
                ref[slot] = jnp.zeros(ref.shape[1:], ref.dtype)

    def cols(p, base=0):
        return slice(base + p * LANES, base + (p + 1) * LANES)

    def make_stack():
        head0 = lax.broadcasted_iota(jnp.int32, (C, LANES), 1) < HEAD

        def stack(t):
            return jnp.concatenate([jnp.where(head0, t, 0.0), jnp.where(head0, 0.0, t)], axis=0).astype(BF16)
        return head0, stack

    def make_head_sum():
        li = lax.broadcasted_iota(jnp.int32, (2 * LANES, LANES), 0)
        lj = lax.broadcasted_iota(jnp.int32, (2 * LANES, LANES), 1)
        ones_bd2 = jnp.where(((li % LANES) // HEAD) == (lj // HEAD), 1.0, 0.0).astype(BF16)

        def head_sum(t):
            rows = jnp.concatenate([t[:, cols(p)] for p in pairs], axis=0)
            hi, lo = _split2(rows)
            s = _dot(jnp.concatenate([hi, lo], axis=1), ones_bd2)
            return jnp.concatenate([s[p * C:(p + 1) * C] for p in pairs], axis=1)
        return head_sum

    def prepare(slot):
        head0, stack = make_stack()
        head_sum = make_head_sum()
        row = lax.broadcasted_iota(jnp.int32, (C, 1), 0)

        def token_shift(x, prev_tile, mu):
            xp = jnp.where(row == 0, prev_tile[SUBLANES - 1:SUBLANES, :], pltpu.roll(x, 1, 0))
            return x + (xp - x) * mu

        xl = lora_ref[...]
        ul = token_shift(xl, prev_lora[...], mu_lora_ref[...])
        prev_lora[...] = xl[C - SUBLANES:C]
        lo_out = _dot(jnp.where(head0, jnp.tanh(ul), ul).astype(BF16), wl_ref[...])
        yield

        def shifted(base):
            sl = slice(base, base + RWKV_WIDTH)
            xs = rkv_ref[:, sl].astype(F32)
            out = token_shift(xs, prev_rkv[:, sl], mu_rkv_ref[:, sl])
            prev_rkv[:, sl] = xs[C - SUBLANES:C]
            return out

        r = shifted(0)
        yield
        k = shifted(RWKV_WIDTH)
        yield
        v = shifted(2 * RWKV_WIDTH)
        yield

        logdec = -math.exp(-0.5) * _sigmoid(w0_ref[...] + lo_out[:, 0:RWKV_WIDTH])
        yield
        a_lr = _sigmoid(a0_ref[...] + lo_out[:, RWKV_WIDTH:2 * RWKV_WIDTH])
        yield

        ti = lax.broadcasted_iota(jnp.int32, (C, C), 0)
        si = lax.broadcasted_iota(jnp.int32, (C, C), 1)
        tri = jnp.where(ti >= si, 1.0, 0.0).astype(BF16)
        ld_hi, ld_lo = _split2(logdec)
        cum = _dot(tri, ld_hi) + _dot(tri, ld_lo)
        cum_last = cum[C - 1:C, :]
        yield
        e_excl = jnp.exp(cum - logdec)
        e_incl = jnp.exp(cum)
        yield
        e_neg = jnp.exp(-cum)
        e_hat = jnp.exp(cum_last - cum)
        yield

        kk = k * kk_ref[...]
        kk = kk * lax.rsqrt(jnp.maximum(head_sum(kk * kk), 1e-24))
        yield
        k2 = k * (1.0 + (a_lr - 1.0) * ka_ref[...])
        b = kk * a_lr
        yield
        xa_all = -kk * e_excl
        xr_all = r * e_incl
        yield
        yb_all = b * e_neg
        yk_all = k2 * e_neg
        yield

        st_gamma[slot] = jnp.exp(cum_last)
        st_bonus[slot] = head_sum(r * k2 * rk_ref[...]) * v
        yield
        st_v[slot] = v.astype(BF16)
        st_kh[slot] = (k2 * e_hat).astype(BF16)
        st_bh[slot] = (b * e_hat).astype(BF16)
        yield
        for p in pairs:
            sl = cols(p)
            st_lhs[slot, p] = jnp.concatenate([xa_all[:, sl], xr_all[:, sl]], axis=0).astype(BF16)
            st_rhs[slot, p] = jnp.concatenate([stack(yb_all[:, sl]), stack(yk_all[:, sl])], axis=0)
            st_vs[slot, p] = stack(v[:, sl])
            yield

    def stage(fn):
        out = []
        for p in pairs:
            out.append(fn(p))
            yield
        return out

    def invert(slot):
        _, stack = make_stack()
        ri = lax.broadcasted_iota(jnp.int32, (S, 2 * S), 0)
        ci = lax.broadcasted_iota(jnp.int32, (S, 2 * S), 1)
        a_mask = (ri % C) >= (ci % C) + jnp.where(ri < C, 1, 0)
        ei = lax.broadcasted_iota(jnp.int32, (C, S), 0)
        ej = lax.broadcasted_iota(jnp.int32, (C, S), 1)
        eye = jnp.where(ei == ej % C, 1.0, 0.0).astype(F32)

        a_all = yield from stage(lambda p: jnp.where(a_mask, _dot_nt(st_lhs[slot, p], st_rhs[slot, p]), 0.0))
        a_ab = [a_all[p][0:C, 0:S] for p in pairs]

        def values(p):
            st_arb[slot, p] = a_all[p][C:S, 0:S].astype(BF16)
            st_axv[slot, p] = _dot(a_all[p][:, S:2 * S].astype(BF16), st_vs[slot, p])

        yield from stage(values)

        inv = [eye + a_ab[p] for p in pairs]
        apow = yield from stage(lambda p: _dot(a_ab[p].astype(BF16), stack(a_ab[p])))
        n = 2
        while 2 * n < C:
            res = yield from stage(
                lambda p: _dot(jnp.concatenate([inv[p], apow[p]], axis=0).astype(BF16), stack(apow[p])))
            inv = [inv[p] + res[p][0:C] for p in pairs]
            apow = [res[p][C:S] for p in pairs]
            n *= 2

        def last(p):
            st_inv[slot, p] = (inv[p] + _dot(inv[p].astype(BF16), stack(apow[p]))).astype(BF16)

        yield from stage(last)

    def state(slot):
        _, stack = make_stack()
        head_sum = make_head_sum()
        bi = lax.broadcasted_iota(jnp.int32, (LANES, LANES), 0)
        bj = lax.broadcasted_iota(jnp.int32, (LANES, LANES), 1)
        same_head = (bi // HEAD) == (bj // HEAD)

        s_old = [h_scr[p] for p in pairs]
        x_s = yield from stage(lambda p: _dot_nt(st_lhs[slot, p], s_old[p].astype(BF16)))
        us = yield from stage(lambda p: _dot(st_inv[slot, p], stack(x_s[p][0:C] + st_axv[slot, p, 0:C])))
        ys = yield from stage(
            lambda p: x_s[p][C:S] + st_axv[slot, p, C:S] + _dot(st_arb[slot, p], stack(us[p])))

        def update(p):
            sl = cols(p)
            vu = jnp.concatenate([st_v[slot, :, sl], us[p].astype(BF16)], axis=0)
            khb = jnp.concatenate([st_kh[slot, :, sl], st_bh[slot, :, sl]], axis=0)
            h_scr[p] = s_old[p] * st_gamma[slot, :, sl] + jnp.where(same_head, _dot_tn(vu, khb), 0.0)

        yield from stage(update)

        y = jnp.concatenate(ys, axis=1)
        yc = y - head_sum(y) * (1.0 / HEAD)
        yield
        var = head_sum(yc * yc) * (1.0 / HEAD)
        y = yc * lax.rsqrt(var + GN_EPS) * gnw_ref[...] + gnb_ref[...] + st_bonus[slot]
        yield
        y_ref[...] = (y * _silu(g_ref[...].astype(F32))).astype(y_ref.dtype)

    phase = lax.rem(c, RWKV_SLOTS)
    _interleave((invert(lax.rem(phase + RWKV_SLOTS - 1, RWKV_SLOTS)), RWKV_WEIGHTS[0]),
                (state(lax.rem(phase + RWKV_SLOTS - 2, RWKV_SLOTS)), RWKV_WEIGHTS[1]),
                (prepare(phase), RWKV_WEIGHTS[2]))

    @pl.when(c == n_chunks + RWKV_SLOTS - 2)
    def _():
        hout_ref[0] = h_scr[...]


def _rwkv(p_arr, lora_arr, pinit_rkv, pinit_lora, hinit, params, *, batch):
    C = RWKV_CHUNK
    rows = p_arr.shape[0]
    n_chunks = rows // (batch * C)
    const2 = lambda b, c: (0, 0)
    vec = lambda width: pl.BlockSpec((1, width), const2)
    prep_chunk = lambda b, c: b * n_chunks + jnp.minimum(c, n_chunks - 1)
    recur_chunk = lambda b, c: b * n_chunks + jnp.maximum(c - (RWKV_SLOTS - 1), 0)
    slots = lambda shape, dtype: pltpu.VMEM((RWKV_SLOTS,) + shape, dtype)
    return pl.pallas_call(
        functools.partial(_rwkv_kernel, n_chunks=n_chunks),
        out_shape=(jax.ShapeDtypeStruct((rows, RWKV_WIDTH), BF16),
                   jax.ShapeDtypeStruct((batch, N_PAIRS, LANES, LANES), F32)),
        grid=(batch, n_chunks + RWKV_SLOTS - 1),
        in_specs=[
            pl.BlockSpec((C, 3 * RWKV_WIDTH), lambda b, c: (prep_chunk(b, c), COL_RKV // (3 * RWKV_WIDTH))),
            pl.BlockSpec((C, RWKV_WIDTH), lambda b, c: (recur_chunk(b, c), COL_GR // RWKV_WIDTH)),
            pl.BlockSpec((C, LANES), lambda b, c: (prep_chunk(b, c), 0)),
            pl.BlockSpec((SUBLANES, 3 * RWKV_WIDTH), const2),
            pl.BlockSpec((SUBLANES, LANES), const2),
            pl.BlockSpec((1, N_PAIRS, LANES, LANES), lambda b, c: (0, 0, 0, 0)),
            vec(3 * RWKV_WIDTH), vec(LANES), vec(RWKV_WIDTH), vec(RWKV_WIDTH),
            pl.BlockSpec((LANES, 2 * RWKV_WIDTH), const2),
            vec(RWKV_WIDTH), vec(RWKV_WIDTH), vec(RWKV_WIDTH), vec(RWKV_WIDTH), vec(RWKV_WIDTH),
        ],
        out_specs=(
            pl.BlockSpec((C, RWKV_WIDTH), lambda b, c: (recur_chunk(b, c), 0)),
            pl.BlockSpec((1, N_PAIRS, LANES, LANES), lambda b, c: (b, 0, 0, 0)),
        ),
        scratch_shapes=[
            pltpu.VMEM((N_PAIRS, LANES, LANES), F32),
            pltpu.VMEM((SUBLANES, 3 * RWKV_WIDTH), F32),
            pltpu.VMEM((SUBLANES, LANES), F32),
            slots((N_PAIRS, 2 * C, LANES), BF16),
            slots((N_PAIRS, 4 * C, LANES), BF16),
            slots((N_PAIRS, 2 * C, LANES), BF16),
            slots((C, RWKV_WIDTH), BF16),
            slots((C, RWKV_WIDTH), BF16),
            slots((C, RWKV_WIDTH), BF16),
            slots((1, RWKV_WIDTH), F32),
            slots((C, RWKV_WIDTH), F32),
            slots((N_PAIRS, C, 2 * C), BF16),
            slots((N_PAIRS, C, 2 * C), BF16),
            slots((N_PAIRS, 2 * C, LANES), F32),
        ],
        compiler_params=pltpu.CompilerParams(
            dimension_semantics=("arbitrary", "arbitrary"), vmem_limit_bytes=VMEM_LIMIT),
        name="rwkv7_chunk",
    )(p_arr, p_arr, lora_arr, pinit_rkv, pinit_lora, hinit, *params)


def _attn_kernel(q_ref, k_ref, v_ref, g_ref, km_ref, vmt_ref, lq1_ref, lk1_ref, lq2_ref, lk2_ref, sw_ref,
                 o_ref, vt_scr, qst_scr, m_scr, acc_scr, s_scr, *, t, n_blk):
    g_id = pl.program_id(0)

    @pl.when(g_id == 0)
    def _():
        for ref in (vt_scr, qst_scr, m_scr, acc_scr):
            ref[1] = jnp.ones(ref.shape[1:], ref.dtype)

    def setup(slot):
        comp0 = lax.broadcasted_iota(jnp.int32, (LANES, t // 2), 0) < HEAD
        for c in range(n_blk):
            vt_scr[slot, c, 0:LANES, :] = v_ref[c * t:(c + 1) * t, :].astype(F32).T.astype(BF16)
            vt_scr[slot, c, LANES:LANES + ONES_ROWS, :] = jnp.ones((ONES_ROWS, t), BF16)
            yield
        for i in range(n_blk):
            qt = q_ref[i * t:(i + 1) * t, :].astype(F32).T
            parts = []
            for half in (qt[:, 0:t // 2], qt[:, t // 2:t]):
                parts += [jnp.where(comp0, half, 0.0), jnp.where(comp0, 0.0, half)]
            qst = jnp.concatenate(parts, axis=1).astype(BF16)
            qst_scr[slot, i] = qst
            yield
            s = _dot(km_ref[...], qst)
            m = jnp.max(s, axis=0, keepdims=True)
            m_scr[slot, i] = m
            acc_scr[slot, i] = _dot(vmt_ref[0], jnp.exp2(s - m).astype(BF16))
            yield

    def blocks(slot):
        h = t // 2
        full_rows, lo_rows, hi_rows = slice(0, t), slice(0, h), slice(h, t)
        all_cols, hi_cols = slice(0, 2 * t), slice(t, 2 * t)
        units = []
        for i, j in sorted(((i, j) for i in range(n_blk) for j in range(i + 1)), key=lambda ij: (ij[1], ij[0])):
            if i == j:
                units += [(i, j, lo_rows, all_cols, True), (i, j, hi_rows, hi_cols, True)]
            else:
                units.append((i, j, full_rows, all_cols, False))
        n_slots = s_scr.shape[0]
        m = [m_scr[slot, i] for i in range(n_blk)]

        def extent(sl):
            return sl.stop - sl.start

        def issue(n):
            i, j, rows, cols, _ = units[n]
            keys = k_ref[j * t + rows.start:j * t + rows.stop, :]
            s_scr[n % n_slots, 0:extent(rows), 0:extent(cols)] = _dot(keys, qst_scr[slot, i, :, cols])

        def visible(n_cols):
            kpos = lax.broadcasted_iota(jnp.int32, (h, n_cols), 0)
            col = lax.broadcasted_iota(jnp.int32, (h, n_cols), 1)
            return kpos <= col % h + jnp.where(col >= t, h, 0)

        visible = {n_cols: visible(n_cols) for n_cols in (t, 2 * t)}
        for n in range(min(ATT_LOOKAHEAD, len(units))):
            issue(n)
        for n, (i, j, rows, cols, masked) in enumerate(units):
            if n + ATT_LOOKAHEAD < len(units):
                issue(n + ATT_LOOKAHEAD)
            s = s_scr[n % n_slots, 0:extent(rows), 0:extent(cols)]
            if masked:
                s = jnp.where(visible[extent(cols)], s, MASK_VALUE)
            m_old = m[i][:, cols]
            m_new = jnp.maximum(m_old, jnp.max(s, axis=0, keepdims=True))
            alpha = jnp.exp2(m_old - m_new)
            m[i] = m_new if extent(cols) == 2 * t else jnp.concatenate([m[i][:, 0:cols.start], m_new], axis=1)
            yield
            acc_scr[slot, i, :, cols] = (alpha * acc_scr[slot, i, :, cols]
                                         + _dot(vt_scr[slot, j, :, rows], jnp.exp2(s - m_new).astype(BF16)))
            yield

        lam = (jnp.exp(jnp.sum(lq1_ref[...] * lk1_ref[...], axis=-1, keepdims=True))
               - jnp.exp(jnp.sum(lq2_ref[...] * lk2_ref[...], axis=-1, keepdims=True)) + LAMBDA_INIT)
        for i in range(n_blk):
            on = acc_scr[slot, i, 0:LANES, :] / acc_scr[slot, i, LANES:LANES + 1, :]
            comp = lambda c: jnp.concatenate([on[:, c * h:(c + 1) * h], on[:, t + c * h:t + (c + 1) * h]], axis=1)
            ot = comp(0) - lam * comp(1)
            ot = ot * lax.rsqrt(jnp.mean(ot * ot, axis=0, keepdims=True) + SUBLN_EPS)
            g = g_ref[i * t:(i + 1) * t, :].astype(F32)
            o_ref[i * t:(i + 1) * t, :] = (ot.T * sw_ref[...] * (1.0 - LAMBDA_INIT) * _silu(g)).astype(o_ref.dtype)
            yield

    odd = jnp.bitwise_and(g_id, 1) == 1

    @pl.when(jnp.logical_not(odd))
    def _():
        _interleave((blocks(1), ATT_WEIGHTS[0]), (setup(0), ATT_WEIGHTS[1]))

    @pl.when(odd)
    def _():
        _interleave((blocks(0), ATT_WEIGHTS[0]), (setup(1), ATT_WEIGHTS[1]))


def _attention(p_arr, p_meta, vm_t, lam_vecs, subln_w, *, batch, seq):
    t = ATT_BLOCK
    n_blk = seq // t
    n_heads_total = batch * DIFF_HEADS
    lanes_blk = lambda col: col // LANES
    small = pl.BlockSpec((1, HEAD), lambda g: (0, 0))
    setup_head = lambda g: jnp.minimum(g, n_heads_total - 1)
    block_head = lambda g: jnp.maximum(g - 1, 0)

    def head_cols(col, head_of):
        def index(g):
            h = head_of(g)
            return (h // DIFF_HEADS, lanes_blk(col) + h % DIFF_HEADS)
        return pl.BlockSpec((seq, LANES), index)

    slots = lambda shape, dtype: pltpu.VMEM((2,) + shape, dtype)
    return pl.pallas_call(
        functools.partial(_attn_kernel, t=t, n_blk=n_blk),
        out_shape=jax.ShapeDtypeStruct((batch * seq, DIFF_WIDTH), BF16),
        grid=(n_heads_total + 1,),
        in_specs=[
            head_cols(COL_Q, setup_head), head_cols(COL_K, block_head), head_cols(COL_V, setup_head),
            head_cols(COL_GD, block_head),
            pl.BlockSpec((N_META, LANES), lambda g: (0, lanes_blk(COL_K) + setup_head(g) % DIFF_HEADS)),
            pl.BlockSpec((1, LANES + ONES_ROWS, N_META), lambda g: (setup_head(g) % DIFF_HEADS, 0, 0)),
            small, small, small, small,
            pl.BlockSpec((1, LANES), lambda g: (0, 0)),
        ],
        out_specs=pl.BlockSpec((seq, LANES), lambda g: (block_head(g) // DIFF_HEADS, block_head(g) % DIFF_HEADS)),
        scratch_shapes=[
            slots((n_blk, LANES + ONES_ROWS, t), BF16),
            slots((n_blk, LANES, 2 * t), BF16),
            slots((n_blk, 1, 2 * t), F32),
            slots((n_blk, LANES + ONES_ROWS, 2 * t), F32),
            pltpu.VMEM((ATT_LOOKAHEAD + 1, t, 2 * t), F32),
        ],
        compiler_params=pltpu.CompilerParams(
            dimension_semantics=("arbitrary",), vmem_limit_bytes=VMEM_LIMIT),
        name="diff_attn",
    )(p_arr, p_arr, p_arr, p_arr, p_meta, vm_t, *lam_vecs, subln_w)


def _outproj_kernel(yr_ref, yd_ref, w1_ref, w2_ref, x_ref, g_ref, o_ref):
    y = _dot(yr_ref[...], w1_ref[...]) + _dot(yd_ref[...], w2_ref[...])
    ms = jnp.mean(y * y, axis=-1, keepdims=True)
    o_ref[...] = x_ref[...] + y * lax.rsqrt(ms + NORM_EPS) * g_ref[...]


def _outproj(y_r, y_d, w, x2, g):
    m = x2.shape[0]
    tm = OUT_TM
    assert RWKV_WIDTH == DIFF_WIDTH and w.shape == (RWKV_WIDTH + DIFF_WIDTH, D_MODEL)
    return pl.pallas_call(
        _outproj_kernel,
        out_shape=jax.ShapeDtypeStruct((m, D_MODEL), F32),
        grid=(m // tm,),
        in_specs=[
            pl.BlockSpec((tm, RWKV_WIDTH), lambda i: (i, 0)),
            pl.BlockSpec((tm, DIFF_WIDTH), lambda i: (i, 0)),
            pl.BlockSpec((RWKV_WIDTH, D_MODEL), lambda i: (0, 0)),
            pl.BlockSpec((DIFF_WIDTH, D_MODEL), lambda i: (1, 0)),
            pl.BlockSpec((tm, D_MODEL), lambda i: (i, 0)),
            pl.BlockSpec((1, D_MODEL), lambda i: (0, 0)),
        ],
        out_specs=pl.BlockSpec((tm, D_MODEL), lambda i: (i, 0)),
        compiler_params=pltpu.CompilerParams(
            dimension_semantics=("arbitrary",), vmem_limit_bytes=VMEM_LIMIT),
        name="outproj",
    )(y_r, y_d, w, w, x2, g)


def _rope_tables(first_pos, n_pos):
    pos = jnp.arange(first_pos, first_pos + n_pos, dtype=F32)
    inv_freq = ROPE_THETA ** (-jnp.arange(0, HEAD, 2, dtype=F32) / HEAD)
    ang = pos[:, None] * inv_freq[None, :]
    cos = jnp.cos(ang)
    sin = jnp.sin(ang)
    cos = jnp.concatenate([cos, cos, cos, cos], axis=-1)
    sin = jnp.concatenate([-sin, sin, -sin, sin], axis=-1)
    return cos, sin


def kernel(x, meta_tokens, pre_norm_w, w_in, rwkv_mu, rwkv_w0, rwkv_w_up, rwkv_a0, rwkv_a_up, rwkv_k_k, rwkv_k_a, rwkv_r_k, rwkv_gn_w, rwkv_gn_b, diff_lam_q1, diff_lam_k1, diff_lam_q2, diff_lam_k2, diff_subln_w, w_out, post_norm_w):
    batch, seq, d = x.shape
    assert d == D_MODEL and meta_tokens.shape == (N_META, D_MODEL)
    assert seq % RWKV_CHUNK == 0 and seq % ATT_BLOCK == 0
    assert (batch * seq) % PROJ_TM == 0 and seq % PROJ_TM == 0
    layer = 0
    x2 = x.reshape(batch * seq, D_MODEL)

    w = w_in[layer]
    rkv_end = 3 * RWKV_WIDTH
    lora_end = rkv_end + 2 * LORA
    g_pre = pre_norm_w[layer].reshape(1, D_MODEL)
    cos_m, sin_m = _rope_tables(0, N_META)
    cos_x, sin_x = _rope_tables(N_META, seq)
    w_main, w_lora_in, p_meta, lora_meta = _wprep(w, meta_tokens.astype(x.dtype), g_pre, cos_m, sin_m)
    p_x, lora_x = _inproj(x2, g_pre, w_main, w_lora_in, cos_x, sin_x)

    mu = rwkv_mu[layer]
    zeros = jnp.zeros((LORA, RWKV_WIDTH), F32)
    w_lora = jnp.concatenate([
        jnp.concatenate([rwkv_w_up[layer], zeros], axis=1),
        jnp.concatenate([zeros, rwkv_a_up[layer]], axis=1)], axis=0)
    row = lambda t, n: t.reshape(1, n)
    rwkv_params = (
        row(mu[:rkv_end], rkv_end), row(mu[rkv_end:lora_end], LANES),
        row(rwkv_w0[layer], RWKV_WIDTH), row(rwkv_a0[layer], RWKV_WIDTH),
        w_lora.astype(BF16),
        row(rwkv_k_k[layer], RWKV_WIDTH), row(rwkv_k_a[layer], RWKV_WIDTH),
        row(rwkv_r_k[layer], RWKV_WIDTH), row(rwkv_gn_w[layer], RWKV_WIDTH), row(rwkv_gn_b[layer], RWKV_WIDTH),
    )
    pad = RWKV_CHUNK - N_META
    _, h_meta = _rwkv(jnp.pad(p_meta, ((pad, 0), (0, 0))), jnp.pad(lora_meta, ((pad, 0), (0, 0))),
                      jnp.zeros((SUBLANES, rkv_end), F32), jnp.zeros((SUBLANES, LANES), F32),
                      jnp.zeros((1, N_PAIRS, LANES, LANES), F32), rwkv_params, batch=1)
    y_r, _ = _rwkv(p_x, lora_x, p_meta[N_META - SUBLANES:, :rkv_end].astype(F32), lora_meta[N_META - SUBLANES:],
                   h_meta, rwkv_params, batch=batch)

    lam_vecs = tuple(t[layer].reshape(1, HEAD) for t in (diff_lam_q1, diff_lam_k1, diff_lam_q2, diff_lam_k2))
    vm_t = p_meta[:, COL_V:COL_V + DIFF_WIDTH].reshape(N_META, DIFF_HEADS, LANES).transpose(1, 2, 0)
    vm_t = jnp.concatenate([vm_t, jnp.ones((DIFF_HEADS, ONES_ROWS, N_META), BF16)], axis=1)
    y_d = _attention(p_x, p_meta, vm_t, lam_vecs, diff_subln_w[layer].reshape(1, LANES), batch=batch, seq=seq)

    out = _outproj(y_r, y_d, w_out[layer].astype(BF16), x2, post_norm_w[layer].reshape(1, D_MODEL))
    return out.reshape(batch, seq, D_MODEL)
```

```python
import functools
import math

import jax
import jax.numpy as jnp
from jax import lax
from jax.experimental import pallas as pl
from jax.experimental.pallas import tpu as pltpu

F32 = jnp.float32
BF16 = jnp.bfloat16

D_MODEL = 2048
N_META = 16
HEAD = 64
LANES = 128
SUBLANES = 8
V7X_VMEM_BYTES = 64 * 1024 * 1024
RWKV_WIDTH = 1024
N_PAIRS = RWKV_WIDTH // LANES
DIFF_WIDTH = 1024
DIFF_HEADS = DIFF_WIDTH // LANES
LORA = 64
ROPE_THETA = 10000.0
NORM_EPS = 1e-6
GN_EPS = 64e-5
SUBLN_EPS = 1e-5
LAMBDA_INIT = 0.8 - 0.6 * math.exp(-0.3 * 0)

COL_RKV = 0
COL_GR = 3 * RWKV_WIDTH
COL_Q = 4 * RWKV_WIDTH
COL_K = COL_Q + DIFF_WIDTH
COL_V = COL_K + DIFF_WIDTH
COL_GD = COL_V + DIFF_WIDTH
P_COLS = COL_GD + DIFF_WIDTH

WPREP_ROWS = 256
PROJ_TM = 1024
PROJ_TN = 1024
PROJ_KC = 512
RWKV_CHUNK = 64
RWKV_SLOTS = 3
RWKV_WEIGHTS = (4, 3, 1)
ATT_BLOCK = 512
ATT_LOOKAHEAD = 1
ATT_WEIGHTS = (7, 3)
OUT_TM = 256
VMEM_LIMIT = V7X_VMEM_BYTES * 3 // 4
MASK_VALUE = -1e30
ONES_ROWS = 16
Q_SCALE = HEAD ** -0.5 * math.log2(math.e)


def _dot(a, b):
    return jnp.dot(a, b, preferred_element_type=F32)


def _dot_nt(a, b):
    return lax.dot_general(a, b, (((1,), (1,)), ((), ())), preferred_element_type=F32)


def _dot_tn(a, b):
    return lax.dot_general(a, b, (((0,), (0,)), ((), ())), preferred_element_type=F32)


def _split2(x):
    hi = x.astype(BF16)
    lo = (x - hi.astype(F32)).astype(BF16)
    return hi, lo


def _sigmoid(x):
    return 1.0 / (1.0 + jnp.exp(-x))


def _silu(x):
    return x * _sigmoid(x)


def _interleave(*weighted_generators):
    active = [list(gw) for gw in weighted_generators]
    while active:
        for item in list(active):
            gen, weight = item
            for _ in range(weight):
                if next(gen, "done") == "done":
                    active.remove(item)
                    break


def _rotate_half(blk, cos, sin):
    lane = lax.broadcasted_iota(jnp.int32, blk.shape, 1)
    first_half = (lane % HEAD) < (HEAD // 2)
    swapped = jnp.where(first_half, pltpu.roll(blk, LANES - HEAD // 2, 1), pltpu.roll(blk, HEAD // 2, 1))
    return blk * cos + swapped * sin


def _wprep_kernel(w_ref, meta_ref, g_ref, cos_ref, sin_ref, main_ref, lora_ref, pm_ref, lm_ref,
                  acc_ref, lacc_ref, ssq_ref):
    i = pl.program_id(0)
    rkv_end = 3 * RWKV_WIDTH
    lora_end = rkv_end + 2 * LORA
    w_rkv = w_ref[:, 0:rkv_end].astype(BF16)
    w_lora = w_ref[:, rkv_end:lora_end].astype(BF16)
    w_rest = w_ref[:, lora_end:lora_end + P_COLS - rkv_end].astype(BF16)
    main_ref[:, 0:rkv_end] = w_rkv
    main_ref[:, rkv_end:P_COLS] = w_rest
    lora_ref[...] = w_lora

    @pl.when(i == 0)
    def _():
        acc_ref[...] = jnp.zeros_like(acc_ref)
        lacc_ref[...] = jnp.zeros_like(lacc_ref)
        ssq_ref[...] = jnp.zeros_like(ssq_ref)

    xm = meta_ref[0]
    hn = (xm * g_ref[0]).astype(BF16)
    ssq_ref[...] += jnp.broadcast_to(jnp.sum(xm * xm, axis=-1, keepdims=True), ssq_ref.shape)
    acc_ref[:, 0:rkv_end] += _dot(hn, w_rkv)
    acc_ref[:, rkv_end:P_COLS] += _dot(hn, w_rest)
    lacc_ref[...] += _dot(hn, w_lora)

    @pl.when(i == pl.num_programs(0) - 1)
    def _():
        rs = lax.rsqrt(ssq_ref[...] * (1.0 / D_MODEL) + NORM_EPS)
        lm_ref[...] = lacc_ref[...] * rs
        for c in range(P_COLS // LANES):
            cb = slice(c * LANES, (c + 1) * LANES)
            blk = acc_ref[:, cb]
            if COL_Q <= c * LANES < COL_K:
                blk = _rotate_half(blk, cos_ref[...], sin_ref[...]) * Q_SCALE
            elif COL_K <= c * LANES < COL_V:
                blk = _rotate_half(blk, cos_ref[...], sin_ref[...])
            pm_ref[:, cb] = (blk * rs).astype(pm_ref.dtype)


def _wprep(w, meta_tokens, g, cos, sin):
    rows, cols = w.shape
    assert cols == P_COLS + 2 * LORA and rows % WPREP_ROWS == 0
    n_steps = rows // WPREP_ROWS
    meta_k = meta_tokens.reshape(N_META, n_steps, WPREP_ROWS).transpose(1, 0, 2)
    g_k = g.reshape(n_steps, 1, WPREP_ROWS)
    const = lambda i: (0, 0)
    return pl.pallas_call(
        _wprep_kernel,
        out_shape=(jax.ShapeDtypeStruct((rows, P_COLS), BF16), jax.ShapeDtypeStruct((rows, 2 * LORA), BF16),
                   jax.ShapeDtypeStruct((N_META, P_COLS), BF16), jax.ShapeDtypeStruct((N_META, LANES), F32)),
        grid=(n_steps,),
        in_specs=[
            pl.BlockSpec((WPREP_ROWS, cols), lambda i: (i, 0)),
            pl.BlockSpec((1, N_META, WPREP_ROWS), lambda i: (i, 0, 0)),
            pl.BlockSpec((1, 1, WPREP_ROWS), lambda i: (i, 0, 0)),
            pl.BlockSpec((N_META, LANES), const),
            pl.BlockSpec((N_META, LANES), const),
        ],
        out_specs=(pl.BlockSpec((WPREP_ROWS, P_COLS), lambda i: (i, 0)),
                   pl.BlockSpec((WPREP_ROWS, 2 * LORA), lambda i: (i, 0)),
                   pl.BlockSpec((N_META, P_COLS), const),
                   pl.BlockSpec((N_META, LANES), const)),
        scratch_shapes=[pltpu.VMEM((N_META, P_COLS), F32), pltpu.VMEM((N_META, LANES), F32),
                        pltpu.VMEM((N_META, LANES), F32)],
        compiler_params=pltpu.CompilerParams(dimension_semantics=("arbitrary",), vmem_limit_bytes=VMEM_LIMIT),
        name="wprep",
    )(w, meta_k, g_k, cos, sin)


def _inproj_kernel(*refs, tn):
    n_k = D_MODEL // PROJ_KC
    x_refs = refs[:n_k]
    g_ref, w_ref, wl_ref, cos_ref, sin_ref, o_ref, ol_ref, hn_ref, rs_ref = refs[n_k:]
    j = pl.program_id(1)
    q_tile = COL_Q // tn
    k_tile = COL_K // tn
    assert 0 < q_tile < k_tile
    is_rope = jnp.logical_or(j == q_tile, j == k_tile)
    col_blocks = [slice(c * LANES, (c + 1) * LANES) for c in range(tn // LANES)]

    @pl.when(j == 0)
    def _():
        ssq = None
        acc = None
        for k in range(n_k):
            sl = slice(k * PROJ_KC, (k + 1) * PROJ_KC)
            xc = x_refs[k][...]
            part = jnp.sum(xc * xc, axis=-1, keepdims=True)
            ssq = part if ssq is None else ssq + part
            hn = (xc * g_ref[:, sl]).astype(BF16)
            hn_ref[:, sl] = hn
            d = _dot(hn, w_ref[sl, :])
            acc = d if acc is None else acc + d
        rs = jnp.broadcast_to(lax.rsqrt(ssq * (1.0 / D_MODEL) + NORM_EPS), rs_ref.shape)
        rs_ref[...] = rs
        ol_ref[...] = _dot(hn_ref[...], wl_ref[...]) * rs
        for cb in col_blocks:
            o_ref[:, cb] = (acc[:, cb] * rs).astype(o_ref.dtype)

    @pl.when(is_rope)
    def _():
        acc = _dot(hn_ref[...], w_ref[...])
        cos = cos_ref[...]
        sin = sin_ref[...]
        scale = rs_ref[...] * jnp.where(j == q_tile, Q_SCALE, 1.0).astype(F32)
        for cb in col_blocks:
            o_ref[:, cb] = (_rotate_half(acc[:, cb], cos, sin) * scale).astype(o_ref.dtype)

    @pl.when(jnp.logical_and(j > 0, jnp.logical_not(is_rope)))
    def _():
        acc = _dot(hn_ref[...], w_ref[...])
        rs = rs_ref[...]
        for cb in col_blocks:
            o_ref[:, cb] = (acc[:, cb] * rs).astype(o_ref.dtype)


def _inproj(x2, g, w_main, w_lora, cos, sin):
    m = x2.shape[0]
    tm = PROJ_TM
    tn = PROJ_TN
    n_pos_tiles = cos.shape[0] // tm
    n_row_tiles = m // tm
    n_col_tiles = P_COLS // tn
    n_k = D_MODEL // PROJ_KC
    assert n_k <= n_col_tiles

    def x_chunk(k):
        first_step = n_col_tiles - n_k + k
        return pl.BlockSpec(
            (tm, PROJ_KC), lambda i, j: (jnp.minimum(i + (j >= first_step).astype(jnp.int32), n_row_tiles - 1), k))

    return pl.pallas_call(
        functools.partial(_inproj_kernel, tn=tn),
        out_shape=(jax.ShapeDtypeStruct((m, P_COLS), BF16), jax.ShapeDtypeStruct((m, LANES), F32)),
        grid=(n_row_tiles, n_col_tiles),
        in_specs=[x_chunk(k) for k in range(n_k)] + [
            pl.BlockSpec((1, D_MODEL), lambda i, j: (0, 0)),
            pl.BlockSpec((D_MODEL, tn), lambda i, j: (0, j)),
            pl.BlockSpec((D_MODEL, LANES), lambda i, j: (0, 0)),
            pl.BlockSpec((tm, LANES), lambda i, j: (i % n_pos_tiles, 0)),
            pl.BlockSpec((tm, LANES), lambda i, j: (i % n_pos_tiles, 0)),
        ],
        out_specs=(
            pl.BlockSpec((tm, tn), lambda i, j: (i, j)),
            pl.BlockSpec((tm, LANES), lambda i, j: (i, 0)),
        ),
        scratch_shapes=[pltpu.VMEM((tm, D_MODEL), BF16), pltpu.VMEM((tm, LANES), F32)],
        compiler_params=pltpu.CompilerParams(
            dimension_semantics=("arbitrary", "arbitrary"), vmem_limit_bytes=VMEM_LIMIT),
        name="inproj",
    )(*([x2] * n_k), g, w_main, w_lora, cos, sin)


def _rwkv_kernel(rkv_ref, g_ref, lora_ref, pinit_rkv_ref, pinit_lora_ref, hinit_ref,
                 mu_rkv_ref, mu_lora_ref, w0_ref, a0_ref, wl_ref,
                 kk_ref, ka_ref, rk_ref, gnw_ref, gnb_ref,
                 y_ref, hout_ref,
                 h_scr, prev_rkv, prev_lora,
                 st_lhs, st_rhs, st_vs, st_v, st_kh, st_bh, st_gamma, st_bonus,
                 st_inv, st_arb, st_axv, *, n_chunks):
    C = RWKV_CHUNK
    S = 2 * C
    assert S == LANES
    f = pl.program_id(0)
    lag = RWKV_SLOTS - 1
    state_chunk = lax.rem(jnp.maximum(f - lag, 0), n_chunks)
    pairs = range(N_PAIRS)
    staging = (st_lhs, st_rhs, st_vs, st_v, st_kh, st_bh, st_gamma, st_bonus, st_inv, st_arb, st_axv)

    @pl.when(lax.rem(f, n_chunks) == 0)
    def _():
        prev_rkv[...] = pinit_rkv_ref[...]
        prev_lora[...] = pinit_lora_ref[...]

    @pl.when(jnp.logical_or(f < lag, state_chunk == 0))
    def _():
        h_scr[...] = hinit_ref[0]

    @pl.when(f == 0)
    def _():
        for ref in staging:
            for slot in range(1, RWKV_SLOTS):
                ref[slot] = jnp.zeros(ref.shape[1:], ref.dtype)

    def cols(p, base=0):
        return slice(base + p * LANES, base + (p + 1) * LANES)

    def make_stack():
        head0 = lax.broadcasted_iota(jnp.int32, (C, LANES), 1) < HEAD

        def stack(t):
            return jnp.concatenate([jnp.where(head0, t, 0.0), jnp.where(head0, 0.0, t)], axis=0).astype(BF16)
        return head0, stack

    def make_head_sum():
        li = lax.broadcasted_iota(jnp.int32, (2 * LANES, LANES), 0)
        lj = lax.broadcasted_iota(jnp.int32, (2 * LANES, LANES), 1)
        ones_bd2 = jnp.where(((li % LANES) // HEAD) == (lj // HEAD), 1.0, 0.0).astype(BF16)

        def head_sum(t):
            rows = jnp.concatenate([t[:, cols(p)] for p in pairs], axis=0)
            hi, lo = _split2(rows)
            s = _dot(jnp.concatenate([hi, lo], axis=1), ones_bd2)
            return jnp.concatenate([s[p * C:(p + 1) * C] for p in pairs], axis=1)
        return head_sum

    def prepare(slot):
        head0, stack = make_stack()
        head_sum = make_head_sum()
        row = lax.broadcasted_iota(jnp.int32, (C, 1), 0)

        def token_shift(x, prev_tile, mu):
            xp = jnp.where(row == 0, prev_tile[SUBLANES - 1:SUBLANES, :], pltpu.roll(x, 1, 0))
            return x + (xp - x) * mu

        xl = lora_ref[...]
        ul = token_shift(xl, prev_lora[...], mu_lora_ref[...])
        prev_lora[...] = xl[C - SUBLANES:C]
        lo_out = _dot(jnp.where(head0, jnp.tanh(ul), ul).astype(BF16), wl_ref[...])
        yield

        def shifted(base):
            sl = slice(base, base + RWKV_WIDTH)
            xs = rkv_ref[:, sl].astype(F32)
            out = token_shift(xs, prev_rkv[:, sl], mu_rkv_ref[:, sl])
            prev_rkv[:, sl] = xs[C - SUBLANES:C]
            return out

        r = shifted(0)
        yield
        k = shifted(RWKV_WIDTH)
        yield
        v = shifted(2 * RWKV_WIDTH)
        yield

        logdec = -math.exp(-0.5) * _sigmoid(w0_ref[...] + lo_out[:, 0:RWKV_WIDTH])
        yield
        a_lr = _sigmoid(a0_ref[...] + lo_out[:, RWKV_WIDTH:2 * RWKV_WIDTH])
        yield

        ti = lax.broadcasted_iota(jnp.int32, (C, C), 0)
        si = lax.broadcasted_iota(jnp.int32, (C, C), 1)
        tri = jnp.where(ti >= si, 1.0, 0.0).astype(BF16)
        ld_hi, ld_lo = _split2(logdec)
        cum = _dot(tri, ld_hi) + _dot(tri, ld_lo)
        cum_last = cum[C - 1:C, :]
        yield
        e_excl = jnp.exp(cum - logdec)
        e_incl = jnp.exp(cum)
        yield
        e_neg = jnp.exp(-cum)
        e_hat = jnp.exp(cum_last - cum)
        yield

        kk = k * kk_ref[...]
        kk = kk * lax.rsqrt(jnp.maximum(head_sum(kk * kk), 1e-24))
        yield
        k2 = k * (1.0 + (a_lr - 1.0) * ka_ref[...])
        b = kk * a_lr
        yield
        xa_all = -kk * e_excl
        xr_all = r * e_incl
        yield
        yb_all = b * e_neg
        yk_all = k2 * e_neg
        yield

        st_gamma[slot] = jnp.exp(cum_last)
        st_bonus[slot] = head_sum(r * k2 * rk_ref[...]) * v
        yield
        st_v[slot] = v.astype(BF16)
        st_kh[slot] = (k2 * e_hat).astype(BF16)
        st_bh[slot] = (b * e_hat).astype(BF16)
        yield
        for p in pairs:
            sl = cols(p)
            st_lhs[slot, p] = jnp.concatenate([xa_all[:, sl], xr_all[:, sl]], axis=0).astype(BF16)
            st_rhs[slot, p] = jnp.concatenate([stack(yb_all[:, sl]), stack(yk_all[:, sl])], axis=0)
            st_vs[slot, p] = stack(v[:, sl])
            yield

    def stage(fn):
        out = []
        for p in pairs:
            out.append(fn(p))
            yield
        return out

    def invert(slot):
        _, stack = make_stack()
        ri = lax.broadcasted_iota(jnp.int32, (S, 2 * S), 0)
        ci = lax.broadcasted_iota(jnp.int32, (S, 2 * S), 1)
        a_mask = (ri % C) >= (ci % C) + jnp.where(ri < C, 1, 0)
        ei = lax.broadcasted_iota(jnp.int32, (C, S), 0)
        ej = lax.broadcasted_iota(jnp.int32, (C, S), 1)
        eye = jnp.where(ei == ej % C, 1.0, 0.0).astype(F32)

        a_all = yield from stage(lambda p: jnp.where(a_mask, _dot_nt(st_lhs[slot, p], st_rhs[slot, p]), 0.0))
        a_ab = [a_all[p][0:C, 0:S] for p in pairs]

        def values(p):
            st_arb[slot, p] = a_all[p][C:S, 0:S].astype(BF16)
            st_axv[slot, p] = _dot(a_all[p][:, S:2 * S].astype(BF16), st_vs[slot, p])

        yield from stage(values)

        inv = [eye + a_ab[p] for p in pairs]
        apow = yield from stage(lambda p: _dot(a_ab[p].astype(BF16), stack(a_ab[p])))
        n = 2
        while 2 * n < C:
            res = yield from stage(
                lambda p: _dot(jnp.concatenate([inv[p], apow[p]], axis=0).astype(BF16), stack(apow[p])))
            inv = [inv[p] + res[p][0:C] for p in pairs]
            apow = [res[p][C:S] for p in pairs]
            n *= 2

        def last(p):
            st_inv[slot, p] = (inv[p] + _dot(inv[p].astype(BF16), stack(apow[p]))).astype(BF16)

        yield from stage(last)

    def state(slot):
        _, stack = make_stack()
        head_sum = make_head_sum()
        bi = lax.broadcasted_iota(jnp.int32, (LANES, LANES), 0)
        bj = lax.broadcasted_iota(jnp.int32, (LANES, LANES), 1)
        same_head = (bi // HEAD) == (bj // HEAD)

        s_old = [h_scr[p] for p in pairs]
        x_s = yield from stage(lambda p: _dot_nt(st_lhs[slot, p], s_old[p].astype(BF16)))
        us = yield from stage(lambda p: _dot(st_inv[slot, p], stack(x_s[p][0:C] + st_axv[slot, p, 0:C])))
        ys = yield from stage(
            lambda p: x_s[p][C:S] + st_axv[slot, p, C:S] + _dot(st_arb[slot, p], stack(us[p])))

        def update(p):
            sl = cols(p)
            vu = jnp.concatenate([st_v[slot, :, sl], us[p].astype(BF16)], axis=0)
            khb = jnp.concatenate([st_kh[slot, :, sl], st_bh[slot, :, sl]], axis=0)
            h_scr[p] = s_old[p] * st_gamma[slot, :, sl] + jnp.where(same_head, _dot_tn(vu, khb), 0.0)

        yield from stage(update)

        y = jnp.concatenate(ys, axis=1)
        yc = y - head_sum(y) * (1.0 / HEAD)
        yield
        var = head_sum(yc * yc) * (1.0 / HEAD)
        y = yc * lax.rsqrt(var + GN_EPS) * gnw_ref[...] + gnb_ref[...] + st_bonus[slot]
        yield
        y_ref[...] = (y * _silu(g_ref[...].astype(F32))).astype(y_ref.dtype)

    phase = lax.rem(f, RWKV_SLOTS)
    for r in range(RWKV_SLOTS):
        @pl.when(phase == r)
        def _():
            _interleave((invert((r - 1) % RWKV_SLOTS), RWKV_WEIGHTS[0]),
                        (state((r - 2) % RWKV_SLOTS), RWKV_WEIGHTS[1]),
                        (prepare(r), RWKV_WEIGHTS[2]))

    @pl.when(jnp.logical_and(f >= lag, state_chunk == n_chunks - 1))
    def _():
        hout_ref[0] = h_scr[...]


def _rwkv(p_arr, lora_arr, pinit_rkv, pinit_lora, hinit, params, *, batch):
    C = RWKV_CHUNK
    rows = p_arr.shape[0]
    n_chunks = rows // (batch * C)
    total = batch * n_chunks
    lag = RWKV_SLOTS - 1
    const2 = lambda f: (0, 0)
    vec = lambda width: pl.BlockSpec((1, width), const2)
    prep_chunk = lambda f: jnp.minimum(f, total - 1)
    state_chunk = lambda f: jnp.maximum(f - lag, 0)
    slots = lambda shape, dtype: pltpu.VMEM((RWKV_SLOTS,) + shape, dtype)
    return pl.pallas_call(
        functools.partial(_rwkv_kernel, n_chunks=n_chunks),
        out_shape=(jax.ShapeDtypeStruct((rows, RWKV_WIDTH), BF16),
                   jax.ShapeDtypeStruct((batch, N_PAIRS, LANES, LANES), F32)),
        grid=(total + lag,),
        in_specs=[
            pl.BlockSpec((C, 3 * RWKV_WIDTH), lambda f: (prep_chunk(f), COL_RKV // (3 * RWKV_WIDTH))),
            pl.BlockSpec((C, RWKV_WIDTH), lambda f: (state_chunk(f), COL_GR // RWKV_WIDTH)),
            pl.BlockSpec((C, LANES), lambda f: (prep_chunk(f), 0)),
            pl.BlockSpec((SUBLANES, 3 * RWKV_WIDTH), const2),
            pl.BlockSpec((SUBLANES, LANES), const2),
            pl.BlockSpec((1, N_PAIRS, LANES, LANES), lambda f: (0, 0, 0, 0)),
            vec(3 * RWKV_WIDTH), vec(LANES), vec(RWKV_WIDTH), vec(RWKV_WIDTH),
            pl.BlockSpec((LANES, 2 * RWKV_WIDTH), const2),
            vec(RWKV_WIDTH), vec(RWKV_WIDTH), vec(RWKV_WIDTH), vec(RWKV_WIDTH), vec(RWKV_WIDTH),
        ],
        out_specs=(
            pl.BlockSpec((C, RWKV_WIDTH), lambda f: (state_chunk(f), 0)),
            pl.BlockSpec((1, N_PAIRS, LANES, LANES), lambda f: (state_chunk(f) // n_chunks, 0, 0, 0)),
        ),
        scratch_shapes=[
            pltpu.VMEM((N_PAIRS, LANES, LANES), F32),
            pltpu.VMEM((SUBLANES, 3 * RWKV_WIDTH), F32),
            pltpu.VMEM((SUBLANES, LANES), F32),
            slots((N_PAIRS, 2 * C, LANES), BF16),
            slots((N_PAIRS, 4 * C, LANES), BF16),
            slots((N_PAIRS, 2 * C, LANES), BF16),
            slots((C, RWKV_WIDTH), BF16),
            slots((C, RWKV_WIDTH), BF16),
            slots((C, RWKV_WIDTH), BF16),
            slots((1, RWKV_WIDTH), F32),
            slots((C, RWKV_WIDTH), F32),
            slots((N_PAIRS, C, 2 * C), BF16),
            slots((N_PAIRS, C, 2 * C), BF16),
            slots((N_PAIRS, 2 * C, LANES), F32),
        ],
        compiler_params=pltpu.CompilerParams(
            dimension_semantics=("arbitrary",), vmem_limit_bytes=VMEM_LIMIT),
        name="rwkv7_chunk",
    )(p_arr, p_arr, lora_arr, pinit_rkv, pinit_lora, hinit, *params)


def _attn_kernel(q_ref, k_ref, v_ref, g_ref, km_ref, vmt_ref, lq1_ref, lk1_ref, lq2_ref, lk2_ref, sw_ref,
                 o_ref, vt_scr, qst_scr, m_scr, acc_scr, s_scr, *, t, n_blk):
    g_id = pl.program_id(0)

    @pl.when(g_id == 0)
    def _():
        for ref in (vt_scr, qst_scr, m_scr, acc_scr):
            ref[1] = jnp.ones(ref.shape[1:], ref.dtype)

    def setup(slot):
        comp0 = lax.broadcasted_iota(jnp.int32, (LANES, t // 2), 0) < HEAD
        for c in range(n_blk):
            vt_scr[slot, c, 0:LANES, :] = v_ref[c * t:(c + 1) * t, :].astype(F32).T.astype(BF16)
            vt_scr[slot, c, LANES:LANES + ONES_ROWS, :] = jnp.ones((ONES_ROWS, t), BF16)
            yield
        for i in range(n_blk):
            qt = q_ref[i * t:(i + 1) * t, :].astype(F32).T
            parts = []
            for half in (qt[:, 0:t // 2], qt[:, t // 2:t]):
                parts += [jnp.where(comp0, half, 0.0), jnp.where(comp0, 0.0, half)]
            qst = jnp.concatenate(parts, axis=1).astype(BF16)
            qst_scr[slot, i] = qst
            yield
            s = _dot(km_ref[...], qst)
            m = jnp.max(s, axis=0, keepdims=True)
            m_scr[slot, i] = m
            acc_scr[slot, i] = _dot(vmt_ref[0], jnp.exp2(s - m).astype(BF16))
            yield

    def blocks(slot):
        h = t // 2
        full_rows, lo_rows, hi_rows = slice(0, t), slice(0, h), slice(h, t)
        all_cols, hi_cols = slice(0, 2 * t), slice(t, 2 * t)
        units = []
        for i, j in sorted(((i, j) for i in range(n_blk) for j in range(i + 1)), key=lambda ij: (ij[1], ij[0])):
            if i == j:
                units += [(i, j, lo_rows, all_cols, True), (i, j, hi_rows, hi_cols, True)]
            else:
                units.append((i, j, full_rows, all_cols, False))
        n_slots = s_scr.shape[0]
        m = [m_scr[slot, i] for i in range(n_blk)]

        def extent(sl):
            return sl.stop - sl.start

        def issue(n):
            i, j, rows, cols, _ = units[n]
            keys = k_ref[j * t + rows.start:j * t + rows.stop, :]
            s_scr[n % n_slots, 0:extent(rows), 0:extent(cols)] = _dot(keys, qst_scr[slot, i, :, cols])

        def visible(n_cols):
            kpos = lax.broadcasted_iota(jnp.int32, (h, n_cols), 0)
            col = lax.broadcasted_iota(jnp.int32, (h, n_cols), 1)
            return kpos <= col % h + jnp.where(col >= t, h, 0)

        visible = {n_cols: visible(n_cols) for n_cols in (t, 2 * t)}
        for n in range(min(ATT_LOOKAHEAD, len(units))):
            issue(n)
        for n, (i, j, rows, cols, masked) in enumerate(units):
            if n + ATT_LOOKAHEAD < len(units):
                issue(n + ATT_LOOKAHEAD)
            s = s_scr[n % n_slots, 0:extent(rows), 0:extent(cols)]
            if masked:
                s = jnp.where(visible[extent(cols)], s, MASK_VALUE)
            m_old = m[i][:, cols]
            m_new = jnp.maximum(m_old, jnp.max(s, axis=0, keepdims=True))
            alpha = jnp.exp2(m_old - m_new)
            m[i] = m_new if extent(cols) == 2 * t else jnp.concatenate([m[i][:, 0:cols.start], m_new], axis=1)
            yield
            acc_scr[slot, i, :, cols] = (alpha * acc_scr[slot, i, :, cols]
                                         + _dot(vt_scr[slot, j, :, rows], jnp.exp2(s - m_new).astype(BF16)))
            yield

        lam = (jnp.exp(jnp.sum(lq1_ref[...] * lk1_ref[...], axis=-1, keepdims=True))
               - jnp.exp(jnp.sum(lq2_ref[...] * lk2_ref[...], axis=-1, keepdims=True)) + LAMBDA_INIT)
        for i in range(n_blk):
            on = acc_scr[slot, i, 0:LANES, :] / acc_scr[slot, i, LANES:LANES + 1, :]
            comp = lambda c: jnp.concatenate([on[:, c * h:(c + 1) * h], on[:, t + c * h:t + (c + 1) * h]], axis=1)
            ot = comp(0) - lam * comp(1)
            ot = ot * lax.rsqrt(jnp.mean(ot * ot, axis=0, keepdims=True) + SUBLN_EPS)
            g = g_ref[i * t:(i + 1) * t, :].astype(F32)
            o_ref[i * t:(i + 1) * t, :] = (ot.T * sw_ref[...] * (1.0 - LAMBDA_INIT) * _silu(g)).astype(o_ref.dtype)
            yield

    odd = jnp.bitwise_and(g_id, 1) == 1

    @pl.when(jnp.logical_not(odd))
    def _():
        _interleave((blocks(1), ATT_WEIGHTS[0]), (setup(0), ATT_WEIGHTS[1]))

    @pl.when(odd)
    def _():
        _interleave((blocks(0), ATT_WEIGHTS[0]), (setup(1), ATT_WEIGHTS[1]))


def _attention(p_arr, p_meta, vm_t, lam_vecs, subln_w, *, batch, seq):
    t = ATT_BLOCK
    n_blk = seq // t
    n_heads_total = batch * DIFF_HEADS
    lanes_blk = lambda col: col // LANES
    small = pl.BlockSpec((1, HEAD), lambda g: (0, 0))
    setup_head = lambda g: jnp.minimum(g, n_heads_total - 1)
    block_head = lambda g: jnp.maximum(g - 1, 0)

    def head_cols(col, head_of):
        def index(g):
            h = head_of(g)
            return (h // DIFF_HEADS, lanes_blk(col) + h % DIFF_HEADS)
        return pl.BlockSpec((seq, LANES), index)

    slots = lambda shape, dtype: pltpu.VMEM((2,) + shape, dtype)
    return pl.pallas_call(
        functools.partial(_attn_kernel, t=t, n_blk=n_blk),
        out_shape=jax.ShapeDtypeStruct((batch * seq, DIFF_WIDTH), BF16),
        grid=(n_heads_total + 1,),
        in_specs=[
            head_cols(COL_Q, setup_head), head_cols(COL_K, block_head), head_cols(COL_V, setup_head),
            head_cols(COL_GD, block_head),
            pl.BlockSpec((N_META, LANES), lambda g: (0, lanes_blk(COL_K) + setup_head(g) % DIFF_HEADS)),
            pl.BlockSpec((1, LANES + ONES_ROWS, N_META), lambda g: (setup_head(g) % DIFF_HEADS, 0, 0)),
            small, small, small, small,
            pl.BlockSpec((1, LANES), lambda g: (0, 0)),
        ],
        out_specs=pl.BlockSpec((seq, LANES), lambda g: (block_head(g) // DIFF_HEADS, block_head(g) % DIFF_HEADS)),
        scratch_shapes=[
            slots((n_blk, LANES + ONES_ROWS, t), BF16),
            slots((n_blk, LANES, 2 * t), BF16),
            slots((n_blk, 1, 2 * t), F32),
            slots((n_blk, LANES + ONES_ROWS, 2 * t), F32),
            pltpu.VMEM((ATT_LOOKAHEAD + 1, t, 2 * t), F32),
        ],
        compiler_params=pltpu.CompilerParams(
            dimension_semantics=("arbitrary",), vmem_limit_bytes=VMEM_LIMIT),
        name="diff_attn",
    )(p_arr, p_arr, p_arr, p_arr, p_meta, vm_t, *lam_vecs, subln_w)


def _outproj_kernel(yr_ref, yd_ref, w1_ref, w2_ref, x_ref, g_ref, o_ref):
    y = _dot(yr_ref[...], w1_ref[...]) + _dot(yd_ref[...], w2_ref[...])
    ms = jnp.mean(y * y, axis=-1, keepdims=True)
    o_ref[...] = x_ref[...] + y * lax.rsqrt(ms + NORM_EPS) * g_ref[...]


def _outproj(y_r, y_d, w, x2, g):
    m = x2.shape[0]
    tm = OUT_TM
    assert RWKV_WIDTH == DIFF_WIDTH and w.shape == (RWKV_WIDTH + DIFF_WIDTH, D_MODEL)
    return pl.pallas_call(
        _outproj_kernel,
        out_shape=jax.ShapeDtypeStruct((m, D_MODEL), F32),
        grid=(m // tm,),
        in_specs=[
            pl.BlockSpec((tm, RWKV_WIDTH), lambda i: (i, 0)),
            pl.BlockSpec((tm, DIFF_WIDTH), lambda i: (i, 0)),
            pl.BlockSpec((RWKV_WIDTH, D_MODEL), lambda i: (0, 0)),
            pl.BlockSpec((DIFF_WIDTH, D_MODEL), lambda i: (1, 0)),
            pl.BlockSpec((tm, D_MODEL), lambda i: (i, 0)),
            pl.BlockSpec((1, D_MODEL), lambda i: (0, 0)),
        ],
        out_specs=pl.BlockSpec((tm, D_MODEL), lambda i: (i, 0)),
        compiler_params=pltpu.CompilerParams(
            dimension_semantics=("arbitrary",), vmem_limit_bytes=VMEM_LIMIT),
        name="outproj",
    )(y_r, y_d, w, w, x2, g)


def _rope_tables(first_pos, n_pos):
    pos = jnp.arange(first_pos, first_pos + n_pos, dtype=F32)
    inv_freq = ROPE_THETA ** (-jnp.arange(0, HEAD, 2, dtype=F32) / HEAD)
    ang = pos[:, None] * inv_freq[None, :]
    cos = jnp.cos(ang)
    sin = jnp.sin(ang)
    cos = jnp.concatenate([cos, cos, cos, cos], axis=-1)
    sin = jnp.concatenate([-sin, sin, -sin, sin], axis=-1)
    return cos, sin


def kernel(x, meta_tokens, pre_norm_w, w_in, rwkv_mu, rwkv_w0, rwkv_w_up, rwkv_a0, rwkv_a_up, rwkv_k_k, rwkv_k_a, rwkv_r_k, rwkv_gn_w, rwkv_gn_b, diff_lam_q1, diff_lam_k1, diff_lam_q2, diff_lam_k2, diff_subln_w, w_out, post_norm_w):
    batch, seq, d = x.shape
    assert d == D_MODEL and meta_tokens.shape == (N_META, D_MODEL)
    assert seq % RWKV_CHUNK == 0 and seq % ATT_BLOCK == 0
    assert (batch * seq) % PROJ_TM == 0 and seq % PROJ_TM == 0
    layer = 0
    x2 = x.reshape(batch * seq, D_MODEL)

    w = w_in[layer]
    rkv_end = 3 * RWKV_WIDTH
    lora_end = rkv_end + 2 * LORA
    g_pre = pre_norm_w[layer].reshape(1, D_MODEL)
    cos_m, sin_m = _rope_tables(0, N_META)
    cos_x, sin_x = _rope_tables(N_META, seq)
    w_main, w_lora_in, p_meta, lora_meta = _wprep(w, meta_tokens.astype(x.dtype), g_pre, cos_m, sin_m)
    p_x, lora_x = _inproj(x2, g_pre, w_main, w_lora_in, cos_x, sin_x)

    mu = rwkv_mu[layer]
    zeros = jnp.zeros((LORA, RWKV_WIDTH), F32)
    w_lora = jnp.concatenate([
        jnp.concatenate([rwkv_w_up[layer], zeros], axis=1),
        jnp.concatenate([zeros, rwkv_a_up[layer]], axis=1)], axis=0)
    row = lambda t, n: t.reshape(1, n)
    rwkv_params = (
        row(mu[:rkv_end], rkv_end), row(mu[rkv_end:lora_end], LANES),
        row(rwkv_w0[layer], RWKV_WIDTH), row(rwkv_a0[layer], RWKV_WIDTH),
        w_lora.astype(BF16),
        row(rwkv_k_k[layer], RWKV_WIDTH), row(rwkv_k_a[layer], RWKV_WIDTH),
        row(rwkv_r_k[layer], RWKV_WIDTH), row(rwkv_gn_w[layer], RWKV_WIDTH), row(rwkv_gn_b[layer], RWKV_WIDTH),
    )
    pad = RWKV_CHUNK - N_META
    _, h_meta = _rwkv(jnp.pad(p_meta, ((pad, 0), (0, 0))), jnp.pad(lora_meta, ((pad, 0), (0, 0))),
                      jnp.zeros((SUBLANES, rkv_end), F32), jnp.zeros((SUBLANES, LANES), F32),
                      jnp.zeros((1, N_PAIRS, LANES, LANES), F32), rwkv_params, batch=1)
    y_r, _ = _rwkv(p_x, lora_x, p_meta[N_META - SUBLANES:, :rkv_end].astype(F32), lora_meta[N_META - SUBLANES:],
                   h_meta, rwkv_params, batch=batch)

    lam_vecs = tuple(t[layer].reshape(1, HEAD) for t in (diff_lam_q1, diff_lam_k1, diff_lam_q2, diff_lam_k2))
    vm_t = p_meta[:, COL_V:COL_V + DIFF_WIDTH].reshape(N_META, DIFF_HEADS, LANES).transpose(1, 2, 0)
    vm_t = jnp.concatenate([vm_t, jnp.ones((DIFF_HEADS, ONES_ROWS, N_META), BF16)], axis=1)
    y_d = _attention(p_x, p_meta, vm_t, lam_vecs, diff_subln_w[layer].reshape(1, LANES), batch=batch, seq=seq)

    out = _outproj(y_r, y_d, w_out[layer].astype(BF16), x2, post_norm_w[layer].reshape(1, D_MODEL))
    return out.reshape(batch, seq, D_MODEL)
```
